```python
import jax
import jax.numpy as jnp
from jax import lax
import numpy as np

D_MODEL = 1024
BATCH = 8
SEQ = 4096
DEPTH = 2

GRID_W = 64
CTX_LEN = 256

POOL_GROUPS = 4
POOL_WINDOWS = (2, 4, 8, 16)
POOL_WIDTH = D_MODEL // 2
POOL_GDIM = POOL_WIDTH // POOL_GROUPS

HG_HEADS = 4
HG_DK = 128
HG_DV = D_MODEL // 2 // HG_HEADS
HG_KW = HG_HEADS * HG_DK
HG_VW = HG_HEADS * HG_DV
HG_CHUNK = 16

NA_HEADS = 8
NA_HD = D_MODEL // 2 // NA_HEADS
NA_W = NA_HEADS * NA_HD
NA_ROWS = 8
NA_COLS = 16

N_BRANCH = 3

N_GROUPS = 4
EXP_PER_GROUP = 8
N_EXPERTS = N_GROUPS * EXP_PER_GROUP
TOP_K = 2
D_EXPERT = D_MODEL // 2

LN_EPS = 1e-5
RMS_EPS = 1e-6

OFF_A = 0
OFF_Q = OFF_A + POOL_WIDTH
OFF_FF = OFF_Q + HG_KW
OFF_FB = OFF_FF + HG_KW
OFF_I = OFF_FB + HG_KW
OFF_G = OFF_I + HG_VW
OFF_NA = OFF_G + HG_VW
OFF_GATE = OFF_NA + 3 * NA_W
D_IN = OFF_GATE + N_BRANCH * D_MODEL

kernel_name = "hybrid_pool_hgrn2_natten_hmoe_prefix_dit"


def _layernorm(x, g=None, b=None):
    xf = x.astype(jnp.float32)
    mu = jnp.mean(xf, axis=-1, keepdims=True)
    var = jnp.mean(jnp.square(xf - mu), axis=-1, keepdims=True)
    y = (xf - mu) * lax.rsqrt(var + LN_EPS)
    if g is not None:
        y = y * g.astype(jnp.float32) + b.astype(jnp.float32)
    return y.astype(x.dtype)


def _modulate(x, shift, scale):
    return _layernorm(x) * (1 + scale) + shift


def _heads(a, n):
    bsz, t_len, _ = a.shape
    return a.reshape(bsz, t_len, n, -1).transpose(0, 2, 1, 3)


def _merge_heads(a):
    bsz, nh, t_len, d = a.shape
    return a.transpose(0, 2, 1, 3).reshape(bsz, t_len, nh * d)


def _split_in(z):
    return (z[..., OFF_A:OFF_Q], z[..., OFF_Q:OFF_FF], z[..., OFF_FF:OFF_FB], z[..., OFF_FB:OFF_I],
            z[..., OFF_I:OFF_G], z[..., OFF_G:OFF_NA], z[..., OFF_NA:OFF_GATE], z[..., OFF_GATE:])


def _pool_mix(u, w_pool, pool_scale):
    bsz, t_len, _ = u.shape
    uf = u.reshape(bsz, t_len, POOL_GROUPS, POOL_GDIM).astype(jnp.float32)
    cs = jnp.concatenate([jnp.zeros_like(uf[:, :1]), lax.cumsum(uf, axis=1)], axis=1)
    pos = np.arange(t_len)
    means = []
    for g, win in enumerate(POOL_WINDOWS):
        lo = np.clip(pos - win // 2, 0, t_len)
        hi = np.clip(pos - win // 2 + win, 0, t_len)
        cnt = (hi - lo).astype(np.float32)[None, :, None]
        means.append((cs[:, hi, g] - cs[:, lo, g]) / cnt)
    y = (jnp.stack(means, axis=2) - uf).astype(u.dtype)
    y = jnp.einsum("btgc,gcd->btgd", y, w_pool)
    return y.reshape(bsz, t_len, POOL_WIDTH) * pool_scale


def _lower_bounds(logits):
    p = jax.nn.softmax(logits.astype(jnp.float32), axis=0)
    return jnp.cumsum(p, axis=0) - p[:1]


def _hgrn_inputs(zq, zf, zi, lb):
    zf = zf.astype(jnp.float32)
    sig = jax.nn.sigmoid(zf)
    log_f = jnp.log(lb + (1 - lb) * sig)
    k = (1 - lb) * jax.nn.sigmoid(-zf)
    q = jax.nn.silu(zq.astype(jnp.float32))
    return (_heads(q, HG_HEADS), _heads(k, HG_HEADS), _heads(zi.astype(jnp.float32), HG_HEADS), _heads(log_f, HG_HEADS))


def _hgrn_scan(q, k, v, log_f, s0):
    bsz, nh, t_len, _ = q.shape
    nc = t_len // HG_CHUNK

    def to_chunks(a):
        return jnp.moveaxis(a.reshape(bsz, nh, nc, HG_CHUNK, a.shape[-1]), 2, 0)

    mask = jnp.tril(jnp.ones((HG_CHUNK, HG_CHUNK), dtype=bool))

    def step(state, inp):
        qc, kc, vc, lfc = inp
        b = jnp.cumsum(lfc, axis=2)
        q_dec = qc * jnp.exp(b)
        a_mat = jnp.einsum("bhtk,bhsk->bhts", q_dec, kc * jnp.exp(-b))
        a_mat = jnp.where(mask, a_mat, 0.0)
        o = jnp.einsum("bhts,bhsv->bhtv", a_mat, vc) + jnp.einsum("bhtk,bhkv->bhtv", q_dec, state)
        b_last = b[:, :, -1:]
        state = (jnp.exp(b_last[:, :, 0])[..., None] * state
                 + jnp.einsum("bhsk,bhsv->bhkv", kc * jnp.exp(b_last - b), vc))
        return state, o

    s_fin, o = lax.scan(step, s0, (to_chunks(q), to_chunks(k), to_chunks(v), to_chunks(log_f)))
    return jnp.moveaxis(o, 0, 2).reshape(bsz, nh, t_len, -1), s_fin


def _hgrn_direction(lat, ctx, reverse):
    if reverse:
        lat = tuple(jnp.flip(a, axis=2) for a in lat)
        ctx = tuple(jnp.flip(a, axis=2) for a in ctx)
    s0 = jnp.zeros((lat[0].shape[0], HG_HEADS, HG_DK, HG_DV), jnp.float32)
    o_ctx, s_ctx = _hgrn_scan(*ctx, s0)
    o_lat, _ = _hgrn_scan(*lat, s_ctx)
    if reverse:
        o_lat = jnp.flip(o_lat, axis=2)
        o_ctx = jnp.flip(o_ctx, axis=2)
    return o_lat, o_ctx


def _hgrn_readout(o, zg, gain):
    o = o * lax.rsqrt(jnp.mean(o * o, axis=-1, keepdims=True) + RMS_EPS) * gain[None, :, None, :].astype(jnp.float32)
    return _merge_heads(o).astype(zg.dtype) * jax.nn.silu(zg)


def _na_latent(q, k, v, kc, vc, rpb):
    bsz, nh, t_len, hd = q.shape
    rows = t_len // GRID_W
    kr = min(NA_ROWS, rows)
    scale = hd ** -0.5
    qg = q.reshape(bsz, nh, rows, GRID_W, hd)
    kg = k.reshape(bsz, nh, rows, GRID_W, hd)
    vg = v.reshape(bsz, nh, rows, GRID_W, hd)
    col = np.arange(GRID_W)
    c0 = np.clip(col - NA_COLS // 2, 0, GRID_W - NA_COLS)
    col_idx = c0[:, None] + np.arange(NA_COLS)[None, :]
    dc = col_idx - col[:, None] + (NA_COLS - 1)
    rpb_c = rpb[:, :, dc]
    n_loc = kr * NA_COLS

    def row_block(r):
        r0 = jnp.clip(r - kr // 2, 0, rows - kr)
        qr = lax.dynamic_index_in_dim(qg, r, axis=2, keepdims=False)
        kn = lax.dynamic_slice_in_dim(kg, r0, kr, axis=2)[:, :, :, col_idx]
        vn = lax.dynamic_slice_in_dim(vg, r0, kr, axis=2)[:, :, :, col_idx]
        dr = r0 + jnp.arange(kr) - r + (NA_ROWS - 1)
        bias = jnp.transpose(rpb_c[:, dr], (0, 2, 1, 3))
        s_loc = jnp.einsum("bhqd,bhrqcd->bhqrc", qr, kn) * scale + bias[None]
        s_ctx = jnp.einsum("bhqd,bhld->bhql", qr, kc) * scale
        s = jnp.concatenate([s_loc.reshape(bsz, nh, GRID_W, n_loc), s_ctx], axis=-1)
        p = jax.nn.softmax(s.astype(jnp.float32), axis=-1).astype(v.dtype)
        p_loc = p[..., :n_loc].reshape(bsz, nh, GRID_W, kr, NA_COLS)
        return (jnp.einsum("bhqrc,bhrqcd->bhqd", p_loc, vn)
                + jnp.einsum("bhql,bhld->bhqd", p[..., n_loc:], vc))

    out = lax.map(row_block, jnp.arange(rows))
    return jnp.moveaxis(out, 0, 2).reshape(bsz, nh, t_len, hd)


def _ctx_attn(qc, kc, vc):
    s = jnp.einsum("bhqd,bhkd->bhqk", qc, kc) * (qc.shape[-1] ** -0.5)
    p = jax.nn.softmax(s.astype(jnp.float32), axis=-1).astype(vc.dtype)
    return jnp.einsum("bhqk,bhkd->bhqd", p, vc)


def _merge_branches(ya, yb, yc, zgate, w_br_a, w_br_b, w_br_c, w_out):
    gates = jax.nn.sigmoid(zgate.reshape(zgate.shape[:-1] + (N_BRANCH, D_MODEL)))
    m = (gates[..., 0, :] * (ya @ w_br_a) + gates[..., 1, :] * (yb @ w_br_b)
         + gates[..., 2, :] * (yc @ w_br_c))
    return m @ w_out


def _token_mixer(h, hc, w_in, w_pool, pool_scale, lb_f, lb_b, hg_gain, rpb,
                 w_br_a, w_br_b, w_br_c, w_out, with_ctx_out):
    za, zq, zff, zfb, zi, zg, zqkv, zgate = _split_in(h @ w_in)
    cza, czq, czff, czfb, czi, czg, czqkv, czgate = _split_in(hc @ w_in)
    o_f, oc_f = _hgrn_direction(_hgrn_inputs(zq, zff, zi, lb_f), _hgrn_inputs(czq, czff, czi, lb_f), reverse=False)
    o_b, oc_b = _hgrn_direction(_hgrn_inputs(zq, zfb, zi, lb_b), _hgrn_inputs(czq, czfb, czi, lb_b), reverse=True)
    yb = _hgrn_readout(o_f + o_b, zg, hg_gain)
    q, k, v = [_heads(a, NA_HEADS) for a in jnp.split(zqkv, 3, axis=-1)]
    qc, kc, vc = [_heads(a, NA_HEADS) for a in jnp.split(czqkv, 3, axis=-1)]
    yc = _merge_heads(_na_latent(q, k, v, kc, vc, rpb))
    ya = _pool_mix(za, w_pool, pool_scale)
    mix = _merge_branches(ya, yb, yc, zgate, w_br_a, w_br_b, w_br_c, w_out)
    if not with_ctx_out:
        return mix, None
    yac = _pool_mix(cza, w_pool, pool_scale)
    ybc = _hgrn_readout(oc_f + oc_b, czg, hg_gain)
    ycc = _merge_heads(_ctx_attn(qc, kc, vc))
    mixc = _merge_branches(yac, ybc, ycc, czgate, w_br_a, w_br_b, w_br_c, w_out)
    return mix, mixc


def _hier_moe(h, w_rg, b_rg, w_re, b_re, w_gate, w_up, w_down):
    shape = h.shape
    hf = h.reshape(-1, D_MODEL)
    n_tok = hf.shape[0]
    lg = (hf @ w_rg + b_rg).astype(jnp.float32)
    p_grp, grp = lax.top_k(jax.nn.softmax(lg, axis=-1), 1)
    le = (hf @ w_re + b_re).astype(jnp.float32).reshape(n_tok, N_GROUPS, EXP_PER_GROUP)
    le_sel = jnp.einsum("nge,ng->ne", le, jax.nn.one_hot(grp[:, 0], N_GROUPS, dtype=jnp.float32))
    p_top, e_top = lax.top_k(jax.nn.softmax(le_sel, axis=-1), TOP_K)
    w_tok = p_grp * p_top / jnp.sum(p_top, axis=-1, keepdims=True)
    expert_id = grp * EXP_PER_GROUP + e_top
    combine = jnp.sum(jax.nn.one_hot(expert_id, N_EXPERTS, dtype=jnp.float32) * w_tok[..., None], axis=1)
    out = jnp.zeros((n_tok, D_MODEL), jnp.float32)
    for e in range(N_EXPERTS):
        a = jax.nn.silu(hf @ w_gate[e]) * (hf @ w_up[e])
        out = out + combine[:, e:e + 1] * (a @ w_down[e])
    return out.astype(h.dtype).reshape(shape)


def setup_inputs(seed: int = 0) -> dict:
    key = jax.random.key(seed)
    ks = jax.random.split(key, 32)
    f32 = jnp.float32
    beta = (8.0 * DEPTH) ** -0.25

    def nrm(k, shape, s):
        return jax.random.normal(k, shape, f32) * s

    dm = D_MODEL
    return {
        "x": nrm(ks[0], (BATCH, SEQ, dm), 1.0),
        "c": nrm(ks[1], (BATCH, dm), 1.0),
        "ctx": nrm(ks[2], (BATCH, CTX_LEN, dm), 1.0),
        "c_ctx": nrm(ks[3], (dm,), 1.0),
        "w_ada": nrm(ks[4], (DEPTH, dm, 6 * dm), dm ** -0.5),
        "b_ada": nrm(ks[5], (DEPTH, 6 * dm), 0.02),
        "w_in": nrm(ks[6], (DEPTH, dm, D_IN), dm ** -0.5),
        "w_pool": nrm(ks[7], (DEPTH, POOL_GROUPS, POOL_GDIM, POOL_GDIM), POOL_GDIM ** -0.5),
        "pool_scale": 1.0 + nrm(ks[8], (DEPTH, POOL_WIDTH), 0.1),
        "lb_logits_fwd": nrm(ks[9], (DEPTH, HG_KW), 0.5),
        "lb_logits_bwd": nrm(ks[10], (DEPTH, HG_KW), 0.5),
        "hg_gain": 1.0 + nrm(ks[11], (DEPTH, HG_HEADS, HG_DV), 0.02),
        "rpb": nrm(ks[12], (DEPTH, NA_HEADS, 2 * NA_ROWS - 1, 2 * NA_COLS - 1), 0.1),
        "w_br_a": nrm(ks[13], (DEPTH, POOL_WIDTH, dm), POOL_WIDTH ** -0.5),
        "w_br_b": nrm(ks[14], (DEPTH, HG_VW, dm), HG_VW ** -0.5),
        "w_br_c": nrm(ks[15], (DEPTH, NA_W, dm), NA_W ** -0.5),
        "w_out": nrm(ks[16], (DEPTH, dm, dm), dm ** -0.5 * beta),
        "ln1_g": 1.0 + nrm(ks[17], (DEPTH, dm), 0.02),
        "ln1_b": nrm(ks[18], (DEPTH, dm), 0.02),
        "w_rg": nrm(ks[19], (DEPTH, dm, N_GROUPS), dm ** -0.5),
        "b_rg": nrm(ks[20], (DEPTH, N_GROUPS), 0.01),
        "w_re": nrm(ks[21], (DEPTH, dm, N_EXPERTS), dm ** -0.5),
        "b_re": nrm(ks[22], (DEPTH, N_EXPERTS), 0.01),
        "w_gate": nrm(ks[23], (DEPTH, N_EXPERTS, dm, D_EXPERT), dm ** -0.5),
        "w_up": nrm(ks[24], (DEPTH, N_EXPERTS, dm, D_EXPERT), dm ** -0.5),
        "w_down": nrm(ks[25], (DEPTH, N_EXPERTS, D_EXPERT, dm), D_EXPERT ** -0.5 * beta),
        "ln2_g": 1.0 + nrm(ks[26], (DEPTH, dm), 0.02),
        "ln2_b": nrm(ks[27], (DEPTH, dm), 0.02),
    }


def reference(x, c, ctx, c_ctx, w_ada, b_ada, w_in, w_pool, pool_scale, lb_logits_fwd, lb_logits_bwd,
              hg_gain, rpb, w_br_a, w_br_b, w_br_c, w_out, ln1_g, ln1_b, w_rg, b_rg, w_re, b_re,
              w_gate, w_up, w_down, ln2_g, ln2_b):
    alpha = (2.0 * DEPTH) ** 0.25
    lb_f_all = _lower_bounds(lb_logits_fwd)
    lb_b_all = _lower_bounds(lb_logits_bwd)
    silu_c = jax.nn.silu(c)
    silu_cc = jax.nn.silu(c_ctx)
    xc = ctx
    for l in range(DEPTH):
        last = l == DEPTH - 1
        ada = silu_c @ w_ada[l] + b_ada[l]
        adac = silu_cc @ w_ada[l] + b_ada[l]
        sh1, sc1, g1, sh2, sc2, g2 = [a[:, None, :] for a in jnp.split(ada, 6, axis=-1)]
        sh1c, sc1c, g1c, sh2c, sc2c, g2c = jnp.split(adac, 6, axis=-1)
        h = _modulate(x, sh1, sc1)
        hc = _modulate(xc, sh1c, sc1c)
        mix, mixc = _token_mixer(h, hc, w_in[l], w_pool[l], pool_scale[l], lb_f_all[l], lb_b_all[l],
                                 hg_gain[l], rpb[l], w_br_a[l], w_br_b[l], w_br_c[l], w_out[l],
                                 with_ctx_out=not last)
        x = _layernorm(alpha * x + g1 * mix, ln1_g[l], ln1_b[l])
        h = _modulate(x, sh2, sc2)
        moe = _hier_moe(h, w_rg[l], b_rg[l], w_re[l], b_re[l], w_gate[l], w_up[l], w_down[l])
        x = _layernorm(alpha * x + g2 * moe, ln2_g[l], ln2_b[l])
        if not last:
            xc = _layernorm(alpha * xc + g1c * mixc, ln1_g[l], ln1_b[l])
            hc = _modulate(xc, sh2c, sc2c)
            moec = _hier_moe(hc, w_rg[l], b_rg[l], w_re[l], b_re[l], w_gate[l], w_up[l], w_down[l])
            xc = _layernorm(alpha * xc + g2c * moec, ln2_g[l], ln2_b[l])
    return x
```

```python
import functools

import numpy as np
import jax
import jax.numpy as jnp
from jax import lax
from jax.experimental import pallas as pl
from jax.experimental.pallas import tpu as pltpu

F32 = jnp.float32
BF16 = jnp.bfloat16
HIGHEST = lax.Precision.HIGHEST

GRID_W = 64
POOL_WINDOWS = (2, 4, 8, 16)
POOL_GDIM = 128
HG_HEADS = 4
HG_DK = 128
HG_BLOCK = 16
NA_HEADS = 8
NA_HD = 64
NA_ROWS = 8
NA_COLS = 16
NA_QROWS = 4
NA_KROWS = 12
N_GROUPS = 4
EXP_PER_GROUP = 8
N_EXPERTS = N_GROUPS * EXP_PER_GROUP
LN_EPS = 1e-5
RMS_EPS = 1e-6
NEG_BIG = -1e30
SEQ_TILE = 256
VMEM_LIMIT = 56 * 1024 * 1024

CB_A, CB_Q, CB_FF, CB_FB, CB_I, CB_G, CB_NQ, CB_NK, CB_NV, CB_GATE = 0, 4, 8, 12, 16, 20, 24, 28, 32, 36


def _cparams(sem):
    return pltpu.CompilerParams(dimension_semantics=sem, vmem_limit_bytes=VMEM_LIMIT)


def _ln(x):
    mu = jnp.mean(x, axis=-1, keepdims=True)
    xc = x - mu
    var = jnp.mean(xc * xc, axis=-1, keepdims=True)
    return xc * lax.rsqrt(var + LN_EPS)


def _sigmoid(x):
    return 1.0 / (1.0 + jnp.exp(-x))


def _dot(a, b):
    return jnp.dot(a, b, preferred_element_type=F32)


def _dot_nt(a, b):
    return lax.dot_general(a, b, (((1,), (1,)), ((), ())), preferred_element_type=F32)


def _dot_tn(a, b):
    return lax.dot_general(a, b, (((0,), (0,)), ((), ())), preferred_element_type=F32)


def _dot01(m01, x):
    x1 = x.astype(BF16)
    r1 = x - x1.astype(F32)
    x2 = r1.astype(BF16)
    x3 = (r1 - x2.astype(F32)).astype(BF16)
    return _dot(m01, x1) + _dot(m01, x2) + _dot(m01, x3)


def _ada_body(c_ref, w_ref, b_ref, o_ref):
    cs = c_ref[...]
    s = cs * _sigmoid(cs)
    o_ref[0] = jnp.dot(s, w_ref[0], preferred_element_type=F32, precision=HIGHEST) + b_ref[0]


def _ada_call(cc, w_ada, b_ada):
    depth, d, n6 = w_ada.shape
    rows = cc.shape[0]
    return pl.pallas_call(
        _ada_body,
        grid=(depth, n6 // d),
        in_specs=[pl.BlockSpec((rows, d), lambda l, j: (0, 0)),
                  pl.BlockSpec((1, d, d), lambda l, j: (l, 0, j)),
                  pl.BlockSpec((1, 1, d), lambda l, j: (l, 0, j))],
        out_specs=pl.BlockSpec((1, rows, d), lambda l, j: (l, 0, j)),
        out_shape=jax.ShapeDtypeStruct((depth, rows, n6), F32),
        compiler_params=_cparams(("parallel", "parallel")),
        name="ada",
    )(cc, w_ada, b_ada.reshape(depth, 1, n6))


def _inproj_body(x_ref, mod_ref, w_ref, z_ref, h_scr):
    @pl.when(pl.program_id(1) == 0)
    def _():
        h = _ln(x_ref[...]) * (1.0 + mod_ref[0, 1:2, :]) + mod_ref[0, 0:1, :]
        h_scr[...] = h.astype(BF16)

    z_ref[...] = _dot(h_scr[...], w_ref[...]).astype(z_ref.dtype)


def _inproj_call(xa, mod, w_in, tm, modrow):
    na, d = xa.shape
    d_in = w_in.shape[1]
    tn = 512
    return pl.pallas_call(
        _inproj_body,
        grid=(na // tm, d_in // tn),
        in_specs=[pl.BlockSpec((tm, d), lambda i, j: (i, 0)),
                  pl.BlockSpec((1, 6, d), lambda i, j: (modrow(i), 0, 0)),
                  pl.BlockSpec((d, tn), lambda i, j: (0, j))],
        out_specs=pl.BlockSpec((tm, tn), lambda i, j: (i, j)),
        out_shape=jax.ShapeDtypeStruct((na, d_in), BF16),
        scratch_shapes=[pltpu.VMEM((tm, d), BF16)],
        compiler_params=_cparams(("parallel", "arbitrary")),
        name="inproj",
    )(xa, mod, w_in)


def _pool_consts():
    n = SEQ_TILE
    t = np.arange(n)[:, None]
    bc = np.zeros((4, n, n), np.float32)
    bp = np.zeros((4, n, 16), np.float32)
    bn = np.zeros((4, n, 16), np.float32)
    for g, win in enumerate(POOL_WINDOWS):
        lo, hi = t - win // 2, t + win // 2 - 1
        s = np.arange(n)[None, :]
        bc[g] = (s >= lo) & (s <= hi)
        s = np.arange(16)[None, :] - 16
        bp[g] = (s >= lo) & (s <= hi)
        s = np.arange(16)[None, :] + n
        bn[g] = (s >= lo) & (s <= hi)
    cnt = np.stack([bc.sum(-1), bp.sum(-1), bn.sum(-1)], axis=1)
    cnt = np.broadcast_to(cnt[..., None], (4, 3, n, 128)).astype(np.float32)
    return (jnp.asarray(bc, BF16), jnp.asarray(bp, BF16), jnp.asarray(bn, BF16), jnp.asarray(cnt))


def _pool_body(prev_ref, cur_ref, next_ref, bc_ref, bp_ref, bn_ref, cnt_ref, wp_ref, ps_ref, o_ref,
               *, nt, n_lat_tiles):
    i = pl.program_id(0)
    k = i % nt
    is_lat = i < n_lat_tiles
    has_prev = jnp.where(jnp.logical_and(is_lat, k != 0), 1.0, 0.0).astype(F32)
    has_next = jnp.where(jnp.logical_and(is_lat, k != nt - 1), 1.0, 0.0).astype(F32)
    for g in range(len(POOL_WINDOWS)):
        sl = slice(g * POOL_GDIM, (g + 1) * POOL_GDIM)
        u = cur_ref[:, sl]
        ssum = (_dot(bc_ref[g], u) + has_prev * _dot(bp_ref[g], prev_ref[:, sl])
                + has_next * _dot(bn_ref[g], next_ref[:, sl]))
        cnt = cnt_ref[g, 0] + has_prev * cnt_ref[g, 1] + has_next * cnt_ref[g, 2]
        dlt = ssum / cnt - u.astype(F32)
        y = _dot(dlt.astype(BF16), wp_ref[g]) * ps_ref[:, sl]
        o_ref[:, sl] = y.astype(o_ref.dtype)


def _pool_call(z, consts, w_pool, pool_scale, nt, n_lat_tiles):
    na = z.shape[0]
    n = SEQ_TILE
    hb = n // 16
    last16 = na // 16 - 1
    bc, bp, bn, cnt = consts
    full = lambda a: pl.BlockSpec(a.shape, lambda i: (0,) * a.ndim)
    return pl.pallas_call(
        functools.partial(_pool_body, nt=nt, n_lat_tiles=n_lat_tiles),
        grid=(na // n,),
        in_specs=[pl.BlockSpec((16, 512), lambda i: (jnp.maximum(i * hb - 1, 0), CB_A // 4)),
                  pl.BlockSpec((n, 512), lambda i: (i, CB_A // 4)),
                  pl.BlockSpec((16, 512), lambda i: (jnp.minimum((i + 1) * hb, last16), CB_A // 4)),
                  full(bc), full(bp), full(bn), full(cnt), full(w_pool), full(pool_scale)],
        out_specs=pl.BlockSpec((n, 512), lambda i: (i, 0)),
        out_shape=jax.ShapeDtypeStruct((na, 512), BF16),
        compiler_params=_cparams(("parallel",)),
        name="pool",
    )(z, z, z, bc, bp, bn, cnt, w_pool, pool_scale)


def _hgrn_consts(reverse):
    n, bs = SEQ_TILE, HG_BLOCK
    nb = n // bs
    t = np.arange(n)
    o = (n - 1 - t) if reverse else t
    blk = t // bs
    jb = np.arange(nb)
    ob = (nb - 1 - jb) if reverse else jb
    cum = ((blk[:, None] == blk[None, :]) & (o[None, :] <= o[:, None])).astype(np.float32)
    bsum = (jb[:, None] == blk[None, :]).astype(np.float32)
    widths = [2 ** l for l in range(1, int(np.log2(nb)) + 1)]
    lvl = np.full((n, n), -1, np.int32)
    obt = ob[blk]
    same = blk[:, None] == blk[None, :]
    lvl[same & (o[None, :] <= o[:, None])] = 0
    for li, w in reversed(list(enumerate(widths, start=1))):
        m = (obt[:, None] // w == obt[None, :] // w) & (obt[None, :] < obt[:, None]) & ~same
        lvl[m] = li
    mats = []
    for w in widths:
        mid = (ob // w) * w + w // 2
        mats.append((mid[:, None] <= ob[None, :]) & (ob[None, :] < ob[:, None]))
    for w in widths:
        mid = (ob // w) * w + w // 2
        mats.append((ob[:, None] < ob[None, :]) & (ob[None, :] < mid[:, None]))
    mats.append(ob[None, :] < ob[:, None])
    mats.append(ob[None, :] > ob[:, None])
    mats.append(np.ones((nb, nb), bool))
    tsm = np.concatenate(mats, axis=0).astype(np.float32)
    return (jnp.asarray(cum, BF16), jnp.asarray(bsum, BF16), jnp.asarray(lvl), jnp.asarray(tsm, BF16),
            len(widths))


def _expand_blocks(c, n):
    nb, lanes = c.shape
    return jnp.broadcast_to(c[:, None, :], (nb, n // nb, lanes)).reshape(n, lanes)


def _hgrn_body(*refs, n_levels, final):
    if final:
        (zq_ref, zf_ref, zi_ref, lb_ref, cum_ref, bsum_ref, lvl_ref, tsm_ref,
         of_ref, zg_ref, gain_ref, o_ref, st_scr) = refs
    else:
        zq_ref, zf_ref, zi_ref, lb_ref, cum_ref, bsum_ref, lvl_ref, tsm_ref, o_ref, st_scr = refs
    n = zq_ref.shape[0]
    nb = n // HG_BLOCK

    @pl.when(pl.program_id(2) == 0)
    def _():
        st_scr[...] = jnp.zeros_like(st_scr)

    zq = zq_ref[...].astype(F32)
    zf = zf_ref[...].astype(F32)
    lb = lb_ref[0]
    lf = jnp.log(lb + (1.0 - lb) * _sigmoid(zf))
    k = (1.0 - lb) * _sigmoid(-zf)
    q = zq * _sigmoid(zq)
    v = zi_ref[...]

    b = _dot01(cum_ref[...], lf)
    tot = _dot01(bsum_ref[...], lf)
    coef = _dot01(tsm_ref[...], tot)
    crow = lambda idx: coef[idx * nb:(idx + 1) * nb]

    qd = q * jnp.exp(b)
    kd = (k * jnp.exp(-b)).astype(BF16)
    ks = k * jnp.exp(_expand_blocks(tot, n) - b)

    lvl = lvl_ref[...]
    a = jnp.where(lvl == 0, _dot_nt(qd.astype(BF16), kd), 0.0)
    for li in range(1, n_levels + 1):
        qw = (qd * _expand_blocks(jnp.exp(crow(li - 1)), n)).astype(BF16)
        kw = (ks * _expand_blocks(jnp.exp(crow(n_levels + li - 1)), n)).astype(BF16)
        a = jnp.where(lvl == li, _dot_nt(qw, kw), a)
    o = _dot(a.astype(BF16), v)

    st = st_scr[...]
    qs = (qd * _expand_blocks(jnp.exp(crow(2 * n_levels)), n)).astype(BF16)
    o = o + _dot_nt(qs, st.astype(BF16))
    kn = (ks * _expand_blocks(jnp.exp(crow(2 * n_levels + 1)), n)).astype(BF16)
    dec = jnp.exp(coef[(2 * n_levels + 2) * nb:(2 * n_levels + 2) * nb + 1])
    st_scr[...] = st * dec + _dot_tn(v, kn)

    if final:
        o = o + of_ref[...]
        o = o * lax.rsqrt(jnp.mean(o * o, axis=-1, keepdims=True) + RMS_EPS) * gain_ref[0]
        zg = zg_ref[...].astype(F32)
        o_ref[...] = (o * (zg * _sigmoid(zg))).astype(o_ref.dtype)
    else:
        o_ref[...] = o


def _hgrn_call(z, lb, consts, nbatch, nt, reverse, o_fwd=None, gain=None):
    na = z.shape[0]
    n = SEQ_TILE
    cum, bsum, lvl, tsm, n_levels = consts
    ctx_base = nbatch * nt
    final = o_fwd is not None

    def tile(b, s):
        lat = (b * nt + nt - s) if reverse else (b * nt + s - 1)
        return jnp.where(s == 0, ctx_base + b, lat)

    def col(cb):
        return pl.BlockSpec((n, HG_DK), lambda b, h, s: (tile(b, s), cb + h))

    full = lambda a: pl.BlockSpec(a.shape, lambda b, h, s: (0,) * a.ndim)
    per_head = pl.BlockSpec((1, 1, HG_DK), lambda b, h, s: (h, 0, 0))
    in_specs = [col(CB_Q), col(CB_FB if reverse else CB_FF), col(CB_I), per_head,
                full(cum), full(bsum), full(lvl), full(tsm)]
    args = [z, z, z, lb, cum, bsum, lvl, tsm]
    if final:
        in_specs += [col(0), col(CB_G), per_head]
        args += [o_fwd, z, gain]
    return pl.pallas_call(
        functools.partial(_hgrn_body, n_levels=n_levels, final=final),
        grid=(nbatch, HG_HEADS, nt + 1),
        in_specs=in_specs,
        out_specs=col(0),
        out_shape=jax.ShapeDtypeStruct((na, HG_HEADS * HG_DK), BF16 if final else F32),
        scratch_shapes=[pltpu.VMEM((HG_DK, HG_DK), F32)],
        compiler_params=_cparams(("parallel", "parallel", "arbitrary")),
        name="hgrn_bwd" if final else "hgrn_fwd",
    )(*args)


def _na_bias_table(rpb, rows):
    nrb = rows // NA_QROWS
    assert nrb >= 3 and rows >= NA_KROWS
    kr = min(NA_ROWS, rows)
    qc = np.arange(GRID_W)
    c0 = np.clip(qc - NA_COLS // 2, 0, GRID_W - NA_COLS)
    kc = np.arange(GRID_W)
    col_ok = (kc[None, :] >= c0[:, None]) & (kc[None, :] < c0[:, None] + NA_COLS)
    dc = np.clip(kc[None, :] - qc[:, None] + NA_COLS - 1, 0, 2 * NA_COLS - 2)
    pats = []
    for rb in (0, 1, nrb - 1):
        start = int(np.clip(NA_QROWS * rb - 4, 0, rows - NA_KROWS))
        r = NA_QROWS * rb + np.arange(NA_QROWS)
        r0 = np.clip(r - kr // 2, 0, rows - kr)
        keyrow = start + np.arange(NA_KROWS)
        row_ok = (keyrow[None, :] >= r0[:, None]) & (keyrow[None, :] < r0[:, None] + kr)
        dr = np.clip(keyrow[None, :] - r[:, None] + NA_ROWS - 1, 0, 2 * NA_ROWS - 2)
        ok = row_ok[:, None, :, None] & col_ok[None, :, None, :]
        dr_i = np.broadcast_to(dr[:, None, :, None], ok.shape)
        dc_i = np.broadcast_to(dc[None, :, None, :], ok.shape)
        bias = rpb[:, dr_i, dc_i]
        bias = jnp.where(ok[None], bias, NEG_BIG)
        pats.append(bias.reshape(NA_HEADS, NA_QROWS * GRID_W, NA_KROWS * GRID_W))
    pats.append(jnp.full_like(pats[0], NEG_BIG))
    return jnp.stack(pats).astype(F32)


def _na_body(q_ref, k_ref, v_ref, kc_ref, vc_ref, bias_ref, o_ref, *, rows):
    rb = pl.program_id(1)
    nk = NA_KROWS * GRID_W
    start_row = jnp.clip(NA_QROWS * rb - 4, 0, rows - NA_KROWS)
    start = pl.multiple_of(start_row * GRID_W, GRID_W)
    nq = q_ref.shape[0]
    lane = lax.broadcasted_iota(jnp.int32, (nq, 128), 1)
    scale = NA_HD ** -0.5
    for p in range(NA_HEADS // 2):
        sl = slice(128 * p, 128 * (p + 1))
        qp = q_ref[:, sl] * scale
        kp = k_ref[pl.ds(start, nk), sl]
        vp = v_ref[pl.ds(start, nk), sl]
        kcp = kc_ref[:, sl]
        vcp = vc_ref[:, sl]
        outs = []
        for hh in range(2):
            sel = (lane < NA_HD) if hh == 0 else (lane >= NA_HD)
            qh = jnp.where(sel, qp, jnp.zeros_like(qp))
            s_loc = _dot_nt(qh, kp) + bias_ref[0, 2 * p + hh]
            s_ctx = _dot_nt(qh, kcp)
            m = jnp.maximum(jnp.max(s_loc, axis=-1, keepdims=True), jnp.max(s_ctx, axis=-1, keepdims=True))
            p_loc = jnp.exp(s_loc - m)
            p_ctx = jnp.exp(s_ctx - m)
            den = jnp.sum(p_loc, axis=-1, keepdims=True) + jnp.sum(p_ctx, axis=-1, keepdims=True)
            o = _dot(p_loc.astype(BF16), vp) + _dot(p_ctx.astype(BF16), vcp)
            outs.append(o / den)
        o_ref[:, sl] = jnp.where(lane < NA_HD, outs[0], outs[1]).astype(o_ref.dtype)


def _na_call(z, bias, nbatch, t_len, c_len, with_ctx):
    na = z.shape[0]
    rows = t_len // GRID_W
    nrb = rows // NA_QROWS
    nq = NA_QROWS * GRID_W
    assert nq == c_len
    ctx_base = nbatch * nrb
    steps = nrb + 1 if with_ctx else nrb

    def qtile(b, r):
        return jnp.where(r < nrb, b * nrb + r, ctx_base + b)

    def pattern(b, r):
        return jnp.where(r == 0, 0, jnp.where(r == nrb - 1, 2, jnp.where(r == nrb, 3, 1)))

    return pl.pallas_call(
        functools.partial(_na_body, rows=rows),
        grid=(nbatch, steps),
        in_specs=[pl.BlockSpec((nq, 512), lambda b, r: (qtile(b, r), CB_NQ // 4)),
                  pl.BlockSpec((t_len, 512), lambda b, r: (b, CB_NK // 4)),
                  pl.BlockSpec((t_len, 512), lambda b, r: (b, CB_NV // 4)),
                  pl.BlockSpec((c_len, 512), lambda b, r: (ctx_base + b, CB_NK // 4)),
                  pl.BlockSpec((c_len, 512), lambda b, r: (ctx_base + b, CB_NV // 4)),
                  pl.BlockSpec((1,) + bias.shape[1:], lambda b, r: (pattern(b, r), 0, 0, 0))],
        out_specs=pl.BlockSpec((nq, 512), lambda b, r: (qtile(b, r), 0)),
        out_shape=jax.ShapeDtypeStruct((na, 512), BF16),
        compiler_params=_cparams(("parallel", "arbitrary")),
        name="natten",
    )(z, z, z, z, z, bias)


def _merge_body(x_ref, ya_ref, yb_ref, yc_ref, g0, g1, g2, g3, g4, g5, wa_ref, wb_ref, wc_ref, wo_ref,
                lng_ref, lnb_ref, mod_ref, wr_ref, br_ref, x1_ref, h2_ref, comb_ref, *, alpha):
    gates = ((g0, g1), (g2, g3), (g4, g5))
    ys = (ya_ref[...], yb_ref[...], yc_ref[...])
    ws = (wa_ref, wb_ref, wc_ref)
    half = wa_ref.shape[1] // 2
    mix = None
    for n in range(2):
        m = None
        for kbr in range(3):
            pr = _dot(ys[kbr], ws[kbr][:, n * half:(n + 1) * half])
            term = _sigmoid(gates[kbr][n][...].astype(F32)) * pr
            m = term if m is None else m + term
        part = _dot(m.astype(BF16), wo_ref[n * half:(n + 1) * half, :])
        mix = part if mix is None else mix + part
    r = alpha * x_ref[...] + mod_ref[0, 2:3, :] * mix
    x1 = _ln(r) * lng_ref[...] + lnb_ref[...]
    x1_ref[...] = x1
    h2 = _ln(x1) * (1.0 + mod_ref[0, 4:5, :]) + mod_ref[0, 3:4, :]
    h2_ref[...] = h2.astype(h2_ref.dtype)

    logits = jnp.dot(h2, wr_ref[...], preferred_element_type=F32, precision=HIGHEST) + br_ref[...]
    lane = lax.broadcasted_iota(jnp.int32, logits.shape, 1).astype(F32)
    is_grp = jnp.where(lane >= N_EXPERTS, jnp.where(lane < N_EXPERTS + N_GROUPS, 1.0, 0.0), 0.0) > 0.5
    lgm = jnp.where(is_grp, logits, NEG_BIG)
    mg = jnp.max(lgm, axis=-1, keepdims=True)
    p_grp = 1.0 / jnp.sum(jnp.exp(lgm - mg), axis=-1, keepdims=True)
    grp = jnp.min(jnp.where(lgm == mg, lane, 1e9), axis=-1, keepdims=True) - N_EXPERTS
    lo = grp * EXP_PER_GROUP
    in_grp = jnp.where(lane >= lo, jnp.where(lane < lo + EXP_PER_GROUP, 1.0, 0.0), 0.0) > 0.5
    lem = jnp.where(in_grp, logits, NEG_BIG)
    m1 = jnp.max(lem, axis=-1, keepdims=True)
    id1 = jnp.min(jnp.where(lem == m1, lane, 1e9), axis=-1, keepdims=True)
    lem2 = jnp.where(lane == id1, NEG_BIG, lem)
    m2 = jnp.max(lem2, axis=-1, keepdims=True)
    id2 = jnp.min(jnp.where(lem2 == m2, lane, 1e9), axis=-1, keepdims=True)
    u2 = jnp.exp(m2 - m1)
    w1 = p_grp / (1.0 + u2)
    w2 = p_grp * u2 / (1.0 + u2)
    comb_ref[...] = jnp.where(lane == id1, w1, 0.0) + jnp.where(lane == id2, w2, 0.0)


def _merge_call(xa, ya, yb, yc, z, mod, wa, wb, wc, wo, lng, lnb, wr, br, tm, modrow, alpha):
    na, d = xa.shape
    row = lambda w: pl.BlockSpec((tm, w), lambda i: (i, 0))
    gate = lambda cb: pl.BlockSpec((tm, 512), lambda i: (i, cb))
    full = lambda a: pl.BlockSpec(a.shape, lambda i: (0,) * a.ndim)
    g0 = CB_GATE // 4
    return pl.pallas_call(
        functools.partial(_merge_body, alpha=alpha),
        grid=(na // tm,),
        in_specs=[row(d), row(512), row(512), row(512)] + [gate(g0 + j) for j in range(6)]
                 + [full(wa), full(wb), full(wc), full(wo), full(lng), full(lnb),
                    pl.BlockSpec((1, 6, d), lambda i: (modrow(i), 0, 0)), full(wr), full(br)],
        out_specs=[row(d), row(d), row(128)],
        out_shape=[jax.ShapeDtypeStruct((na, d), F32), jax.ShapeDtypeStruct((na, d), BF16),
                   jax.ShapeDtypeStruct((na, 128), F32)],
        compiler_params=_cparams(("parallel",)),
        name="merge",
    )(xa, ya, yb, yc, z, z, z, z, z, z, wa, wb, wc, wo, lng, lnb, mod, wr, br)


def _moe_body(x1_ref, h2_ref, comb_ref, wg_ref, wu_ref, wd_ref, lng_ref, lnb_ref, mod_ref, o_ref, acc_scr,
              *, alpha):
    e = pl.program_id(1)

    @pl.when(e == 0)
    def _():
        acc_scr[...] = jnp.zeros_like(acc_scr)

    h = h2_ref[...]
    gt = _dot(h, wg_ref[0])
    a = gt * _sigmoid(gt) * _dot(h, wu_ref[0])
    y = _dot(a.astype(BF16), wd_ref[0])
    comb = comb_ref[...]
    lane = lax.broadcasted_iota(jnp.int32, comb.shape, 1)
    w = jnp.sum(jnp.where(lane == e, comb, 0.0), axis=-1, keepdims=True)
    acc_scr[...] += w * y

    @pl.when(e == pl.num_programs(1) - 1)
    def _():
        r = alpha * x1_ref[...] + mod_ref[0, 5:6, :] * acc_scr[...]
        o_ref[...] = _ln(r) * lng_ref[...] + lnb_ref[...]


def _moe_call(x1, h2, comb, wg, wu, wd, lng, lnb, mod, tm, modrow, alpha):
    na, d = x1.shape
    ne, _, de = wg.shape
    full = lambda a: pl.BlockSpec(a.shape, lambda i, e: (0,) * a.ndim)
    return pl.pallas_call(
        functools.partial(_moe_body, alpha=alpha),
        grid=(na // tm, ne),
        in_specs=[pl.BlockSpec((tm, d), lambda i, e: (i, 0)),
                  pl.BlockSpec((tm, d), lambda i, e: (i, 0)),
                  pl.BlockSpec((tm, 128), lambda i, e: (i, 0)),
                  pl.BlockSpec((1, d, de), lambda i, e: (e, 0, 0)),
                  pl.BlockSpec((1, d, de), lambda i, e: (e, 0, 0)),
                  pl.BlockSpec((1, de, d), lambda i, e: (e, 0, 0)),
                  full(lng), full(lnb),
                  pl.BlockSpec((1, 6, d), lambda i, e: (modrow(i), 0, 0))],
        out_specs=pl.BlockSpec((tm, d), lambda i, e: (i, 0)),
        out_shape=jax.ShapeDtypeStruct((na, d), F32),
        scratch_shapes=[pltpu.VMEM((tm, d), F32)],
        compiler_params=_cparams(("parallel", "arbitrary")),
        name="moe",
    )(x1, h2, comb, wg, wu, wd, lng, lnb, mod)


def _lower_bounds(logits):
    p = jax.nn.softmax(logits.astype(F32), axis=0)
    return jnp.cumsum(p, axis=0) - p[:1]


def _row_tile(limit, *sizes):
    tm = limit
    while any(s % tm for s in sizes):
        tm //= 2
    return tm


def kernel(x, c, ctx, c_ctx, w_ada, b_ada, w_in, w_pool, pool_scale, lb_logits_fwd, lb_logits_bwd, hg_gain, rpb, w_br_a, w_br_b, w_br_c, w_out, ln1_g, ln1_b, w_rg, b_rg, w_re, b_re, w_gate, w_up, w_down, ln2_g, ln2_b):
    nbatch, t_len, d = x.shape
    c_len = ctx.shape[1]
    depth = w_ada.shape[0]
    assert c_len == SEQ_TILE and t_len % SEQ_TILE == 0 and t_len % GRID_W == 0
    alpha = (2.0 * depth) ** 0.25
    n_lat = nbatch * t_len
    nt = t_len // SEQ_TILE

    xa = jnp.concatenate([x.reshape(n_lat, d), ctx.reshape(nbatch * c_len, d)], axis=0)

    mod_rows = -(-(nbatch + 1) // 8) * 8
    cc = jnp.zeros((mod_rows, d), F32).at[:nbatch].set(c).at[nbatch].set(c_ctx)
    ada = _ada_call(cc, w_ada, b_ada)

    lb_f = _lower_bounds(lb_logits_fwd).reshape(depth, HG_HEADS, 1, HG_DK)
    lb_b = _lower_bounds(lb_logits_bwd).reshape(depth, HG_HEADS, 1, HG_DK)
    pool_consts = _pool_consts()
    hg_f = _hgrn_consts(False)
    hg_b = _hgrn_consts(True)

    tm_big = _row_tile(1024, t_len, nbatch * c_len)
    tm_mid = _row_tile(512, t_len, nbatch * c_len)

    def modrow_for(tm):
        return lambda i: jnp.where(i * tm < n_lat, (i * tm) // t_len, nbatch)

    for l in range(depth):
        last = l == depth - 1
        mod = ada[l].reshape(mod_rows, 6, d)
        z = _inproj_call(xa, mod, w_in[l].astype(BF16), tm_big, modrow_for(tm_big))
        ya = _pool_call(z, pool_consts, w_pool[l].astype(BF16), pool_scale[l].reshape(1, -1),
                        nt, nbatch * nt)
        o_f = _hgrn_call(z, lb_f[l], hg_f, nbatch, nt, reverse=False)
        yb = _hgrn_call(z, lb_b[l], hg_b, nbatch, nt, reverse=True, o_fwd=o_f,
                        gain=hg_gain[l].reshape(HG_HEADS, 1, HG_DK))
        bias = _na_bias_table(rpb[l], t_len // GRID_W)
        yc = _na_call(z, bias, nbatch, t_len, c_len, with_ctx=not last)
        wr = jnp.zeros((d, 128), F32).at[:, :N_EXPERTS].set(w_re[l]).at[:, N_EXPERTS:N_EXPERTS + N_GROUPS].set(w_rg[l])
        br = jnp.zeros((1, 128), F32).at[0, :N_EXPERTS].set(b_re[l]).at[0, N_EXPERTS:N_EXPERTS + N_GROUPS].set(b_rg[l])
        x1, h2, comb = _merge_call(xa, ya, yb, yc, z, mod, w_br_a[l].astype(BF16), w_br_b[l].astype(BF16),
                                   w_br_c[l].astype(BF16), w_out[l].astype(BF16), ln1_g[l].reshape(1, d),
                                   ln1_b[l].reshape(1, d), wr, br, tm_mid, modrow_for(tm_mid), alpha)
        xa = _moe_call(x1, h2, comb, w_gate[l].astype(BF16), w_up[l].astype(BF16), w_down[l].astype(BF16),
                       ln2_g[l].reshape(1, d), ln2_b[l].reshape(1, d), mod, tm_big, modrow_for(tm_big), alpha)
    return xa[:n_lat].reshape(nbatch, t_len, d)
```

```python
import functools

import numpy as np
import jax
import jax.numpy as jnp
from jax import lax
from jax.experimental import pallas as pl
from jax.experimental.pallas import tpu as pltpu

F32 = jnp.float32
BF16 = jnp.bfloat16
HIGHEST = lax.Precision.HIGHEST

GRID_W = 64
POOL_WINDOWS = (2, 4, 8, 16)
POOL_GDIM = 128
HG_HEADS = 4
HG_DK = 128
HG_BLOCK = 16
NA_HEADS = 8
NA_HD = 64
NA_ROWS = 8
NA_COLS = 16
NA_QROWS = 4
NA_KROWS = 12
N_GROUPS = 4
EXP_PER_GROUP = 8
N_EXPERTS = N_GROUPS * EXP_PER_GROUP
LN_EPS = 1e-5
RMS_EPS = 1e-6
NEG_BIG = -1e30
SEQ_TILE = 256
EXPERT_TILE = 512
VMEM_LIMIT = 56 * 1024 * 1024

CB_A, CB_Q, CB_FF, CB_FB, CB_I, CB_G, CB_NQ, CB_NK, CB_NV, CB_GATE = 0, 4, 8, 12, 16, 20, 24, 28, 32, 36


def _cparams(sem):
    return pltpu.CompilerParams(dimension_semantics=sem, vmem_limit_bytes=VMEM_LIMIT)


def _ln(x):
    mu = jnp.mean(x, axis=-1, keepdims=True)
    xc = x - mu
    var = jnp.mean(xc * xc, axis=-1, keepdims=True)
    return xc * lax.rsqrt(var + LN_EPS)


def _sigmoid(x):
    return 1.0 / (1.0 + jnp.exp(-x))


def _dot(a, b):
    return jnp.dot(a, b, preferred_element_type=F32)


def _dot_nt(a, b):
    return lax.dot_general(a, b, (((1,), (1,)), ((), ())), preferred_element_type=F32)


def _dot_tn(a, b):
    return lax.dot_general(a, b, (((0,), (0,)), ((), ())), preferred_element_type=F32)


def _dot01(m01, x):
    x1 = x.astype(BF16)
    r1 = x - x1.astype(F32)
    x2 = r1.astype(BF16)
    x3 = (r1 - x2.astype(F32)).astype(BF16)
    return _dot(m01, x1) + _dot(m01, x2) + _dot(m01, x3)


def _ada_body(c_ref, w_ref, b_ref, o_ref):
    cs = c_ref[...]
    s = cs * _sigmoid(cs)
    o_ref[0] = jnp.dot(s, w_ref[0], preferred_element_type=F32, precision=HIGHEST) + b_ref[0]


def _ada_call(cc, w_ada, b_ada):
    depth, d, n6 = w_ada.shape
    rows = cc.shape[0]
    return pl.pallas_call(
        _ada_body,
        grid=(depth, n6 // d),
        in_specs=[pl.BlockSpec((rows, d), lambda l, j: (0, 0)),
                  pl.BlockSpec((1, d, d), lambda l, j: (l, 0, j)),
                  pl.BlockSpec((1, 1, d), lambda l, j: (l, 0, j))],
        out_specs=pl.BlockSpec((1, rows, d), lambda l, j: (l, 0, j)),
        out_shape=jax.ShapeDtypeStruct((depth, rows, n6), F32),
        compiler_params=_cparams(("parallel", "parallel")),
        name="ada",
    )(cc, w_ada, b_ada.reshape(depth, 1, n6))


def _inproj_body(x_ref, mod_ref, w_ref, z_ref, h_scr):
    @pl.when(pl.program_id(1) == 0)
    def _():
        h = _ln(x_ref[...]) * (1.0 + mod_ref[0, 1:2, :]) + mod_ref[0, 0:1, :]
        h_scr[...] = h.astype(BF16)

    z_ref[...] = _dot(h_scr[...], w_ref[...]).astype(z_ref.dtype)


def _inproj_call(xa, mod, w_in, tm, modrow):
    na, d = xa.shape
    d_in = w_in.shape[1]
    tn = 512
    return pl.pallas_call(
        _inproj_body,
        grid=(na // tm, d_in // tn),
        in_specs=[pl.BlockSpec((tm, d), lambda i, j: (i, 0)),
                  pl.BlockSpec((1, 6, d), lambda i, j: (modrow(i), 0, 0)),
                  pl.BlockSpec((d, tn), lambda i, j: (0, j))],
        out_specs=pl.BlockSpec((tm, tn), lambda i, j: (i, j)),
        out_shape=jax.ShapeDtypeStruct((na, d_in), BF16),
        scratch_shapes=[pltpu.VMEM((tm, d), BF16)],
        compiler_params=_cparams(("parallel", "arbitrary")),
        name="inproj",
    )(xa, mod, w_in)


def _pool_consts():
    n = SEQ_TILE
    t = np.arange(n)[:, None]
    bc = np.zeros((4, n, n), np.float32)
    bp = np.zeros((4, n, 16), np.float32)
    bn = np.zeros((4, n, 16), np.float32)
    for g, win in enumerate(POOL_WINDOWS):
        lo, hi = t - win // 2, t + win // 2 - 1
        s = np.arange(n)[None, :]
        bc[g] = (s >= lo) & (s <= hi)
        s = np.arange(16)[None, :] - 16
        bp[g] = (s >= lo) & (s <= hi)
        s = np.arange(16)[None, :] + n
        bn[g] = (s >= lo) & (s <= hi)
    cnt = np.stack([bc.sum(-1), bp.sum(-1), bn.sum(-1)], axis=1)
    cnt = np.broadcast_to(cnt[..., None], (4, 3, n, 128)).astype(np.float32)
    return (jnp.asarray(bc, BF16), jnp.asarray(bp, BF16), jnp.asarray(bn, BF16), jnp.asarray(cnt))


def _pool_body(prev_ref, cur_ref, next_ref, bc_ref, bp_ref, bn_ref, cnt_ref, wp_ref, ps_ref, o_ref,
               *, nt, n_lat_tiles):
    i = pl.program_id(0)
    k = i % nt
    is_lat = i < n_lat_tiles
    has_prev = jnp.where(jnp.logical_and(is_lat, k != 0), 1.0, 0.0).astype(F32)
    has_next = jnp.where(jnp.logical_and(is_lat, k != nt - 1), 1.0, 0.0).astype(F32)
    for g in range(len(POOL_WINDOWS)):
        sl = slice(g * POOL_GDIM, (g + 1) * POOL_GDIM)
        u = cur_ref[:, sl]
        ssum = (_dot(bc_ref[g], u) + has_prev * _dot(bp_ref[g], prev_ref[:, sl])
                + has_next * _dot(bn_ref[g], next_ref[:, sl]))
        cnt = cnt_ref[g, 0] + has_prev * cnt_ref[g, 1] + has_next * cnt_ref[g, 2]
        dlt = ssum / cnt - u.astype(F32)
        y = _dot(dlt.astype(BF16), wp_ref[g]) * ps_ref[:, sl]
        o_ref[:, sl] = y.astype(o_ref.dtype)


def _pool_call(z, consts, w_pool, pool_scale, nt, n_lat_tiles):
    na = z.shape[0]
    n = SEQ_TILE
    hb = n // 16
    last16 = na // 16 - 1
    bc, bp, bn, cnt = consts
    full = lambda a: pl.BlockSpec(a.shape, lambda i: (0,) * a.ndim)
    return pl.pallas_call(
        functools.partial(_pool_body, nt=nt, n_lat_tiles=n_lat_tiles),
        grid=(na // n,),
        in_specs=[pl.BlockSpec((16, 512), lambda i: (jnp.maximum(i * hb - 1, 0), CB_A // 4)),
                  pl.BlockSpec((n, 512), lambda i: (i, CB_A // 4)),
                  pl.BlockSpec((16, 512), lambda i: (jnp.minimum((i + 1) * hb, last16), CB_A // 4)),
                  full(bc), full(bp), full(bn), full(cnt), full(w_pool), full(pool_scale)],
        out_specs=pl.BlockSpec((n, 512), lambda i: (i, 0)),
        out_shape=jax.ShapeDtypeStruct((na, 512), BF16),
        compiler_params=_cparams(("parallel",)),
        name="pool",
    )(z, z, z, bc, bp, bn, cnt, w_pool, pool_scale)


def _hgrn_consts(reverse):
    n, bs = SEQ_TILE, HG_BLOCK
    nb = n // bs
    t = np.arange(n)
    o = (n - 1 - t) if reverse else t
    blk = t // bs
    jb = np.arange(nb)
    ob = (nb - 1 - jb) if reverse else jb
    cum = ((blk[:, None] == blk[None, :]) & (o[None, :] <= o[:, None])).astype(np.float32)
    bsum = (jb[:, None] == blk[None, :]).astype(np.float32)
    widths = [2 ** l for l in range(1, int(np.log2(nb)) + 1)]
    lvl = np.full((n, n), -1, np.int32)
    obt = ob[blk]
    same = blk[:, None] == blk[None, :]
    lvl[same & (o[None, :] <= o[:, None])] = 0
    for li, w in reversed(list(enumerate(widths, start=1))):
        m = (obt[:, None] // w == obt[None, :] // w) & (obt[None, :] < obt[:, None]) & ~same
        lvl[m] = li
    mats = []
    for w in widths:
        mid = (ob // w) * w + w // 2
        mats.append((mid[:, None] <= ob[None, :]) & (ob[None, :] < ob[:, None]))
    for w in widths:
        mid = (ob // w) * w + w // 2
        mats.append((ob[:, None] < ob[None, :]) & (ob[None, :] < mid[:, None]))
    mats.append(ob[None, :] < ob[:, None])
    mats.append(ob[None, :] > ob[:, None])
    mats.append(np.ones((nb, nb), bool))
    tsm = np.concatenate(mats, axis=0).astype(np.float32)
    return (jnp.asarray(cum, BF16), jnp.asarray(bsum, BF16), jnp.asarray(lvl), jnp.asarray(tsm, BF16),
            len(widths))


def _expand_blocks(c, n):
    nb, lanes = c.shape
    return jnp.broadcast_to(c[:, None, :], (nb, n // nb, lanes)).reshape(n, lanes)


def _hgrn_body(*refs, n_levels, final):
    if final:
        (zq_ref, zf_ref, zi_ref, lb_ref, cum_ref, bsum_ref, lvl_ref, tsm_ref,
         of_ref, zg_ref, gain_ref, o_ref, st_scr) = refs
    else:
        zq_ref, zf_ref, zi_ref, lb_ref, cum_ref, bsum_ref, lvl_ref, tsm_ref, o_ref, st_scr = refs
    n = zq_ref.shape[0]
    nb = n // HG_BLOCK

    @pl.when(pl.program_id(1) == 0)
    def _():
        st_scr[...] = jnp.zeros_like(st_scr)

    for h in range(HG_HEADS):
        sl = slice(h * HG_DK, (h + 1) * HG_DK)
        zq = zq_ref[:, sl].astype(F32)
        zf = zf_ref[:, sl].astype(F32)
        lb = lb_ref[h]
        lf = jnp.log(lb + (1.0 - lb) * _sigmoid(zf))
        k = (1.0 - lb) * _sigmoid(-zf)
        q = zq * _sigmoid(zq)
        v = zi_ref[:, sl]

        b = _dot01(cum_ref[...], lf)
        tot = _dot01(bsum_ref[...], lf)
        coef = _dot01(tsm_ref[...], tot)
        crow = lambda idx: coef[idx * nb:(idx + 1) * nb]

        qd = q * jnp.exp(b)
        kd = (k * jnp.exp(-b)).astype(BF16)
        ks = k * jnp.exp(_expand_blocks(tot, n) - b)

        lvl = lvl_ref[...]
        a = jnp.where(lvl == 0, _dot_nt(qd.astype(BF16), kd), 0.0)
        for li in range(1, n_levels + 1):
            qw = (qd * _expand_blocks(jnp.exp(crow(li - 1)), n)).astype(BF16)
            kw = (ks * _expand_blocks(jnp.exp(crow(n_levels + li - 1)), n)).astype(BF16)
            a = jnp.where(lvl == li, _dot_nt(qw, kw), a)
        o = _dot(a.astype(BF16), v)

        st = st_scr[h]
        qs = (qd * _expand_blocks(jnp.exp(crow(2 * n_levels)), n)).astype(BF16)
        o = o + _dot_nt(qs, st.astype(BF16))
        kn = (ks * _expand_blocks(jnp.exp(crow(2 * n_levels + 1)), n)).astype(BF16)
        dec = jnp.exp(coef[(2 * n_levels + 2) * nb:(2 * n_levels + 2) * nb + 1])
        st_scr[h] = st * dec + _dot_tn(v, kn)

        if final:
            o = o + of_ref[:, sl]
            o = o * lax.rsqrt(jnp.mean(o * o, axis=-1, keepdims=True) + RMS_EPS) * gain_ref[h]
            zg = zg_ref[:, sl].astype(F32)
            o_ref[:, sl] = (o * (zg * _sigmoid(zg))).astype(o_ref.dtype)
        else:
            o_ref[:, sl] = o


def _hgrn_call(z, lb, consts, nbatch, nt, reverse, o_fwd=None, gain=None):
    na = z.shape[0]
    n = SEQ_TILE
    cum, bsum, lvl, tsm, n_levels = consts
    ctx_base = nbatch * nt
    final = o_fwd is not None

    def tile(b, s):
        lat = (b * nt + nt - s) if reverse else (b * nt + s - 1)
        return jnp.where(s == 0, ctx_base + b, lat)

    width = HG_HEADS * HG_DK

    def col(cb):
        return pl.BlockSpec((n, width), lambda b, s: (tile(b, s), cb // HG_HEADS))

    full = lambda a: pl.BlockSpec(a.shape, lambda b, s: (0,) * a.ndim)
    in_specs = [col(CB_Q), col(CB_FB if reverse else CB_FF), col(CB_I), full(lb),
                full(cum), full(bsum), full(lvl), full(tsm)]
    args = [z, z, z, lb, cum, bsum, lvl, tsm]
    if final:
        in_specs += [col(0), col(CB_G), full(gain)]
        args += [o_fwd, z, gain]
    return pl.pallas_call(
        functools.partial(_hgrn_body, n_levels=n_levels, final=final),
        grid=(nbatch, nt + 1),
        in_specs=in_specs,
        out_specs=col(0),
        out_shape=jax.ShapeDtypeStruct((na, width), BF16 if final else F32),
        scratch_shapes=[pltpu.VMEM((HG_HEADS, HG_DK, HG_DK), F32)],
        compiler_params=_cparams(("parallel", "arbitrary")),
        name="hgrn_bwd" if final else "hgrn_fwd",
    )(*args)


def _na_bias_table(rpb, rows):
    nrb = rows // NA_QROWS
    assert nrb >= 3 and rows >= NA_KROWS
    kr = min(NA_ROWS, rows)
    qc = np.arange(GRID_W)
    c0 = np.clip(qc - NA_COLS // 2, 0, GRID_W - NA_COLS)
    kc = np.arange(GRID_W)
    col_ok = (kc[None, :] >= c0[:, None]) & (kc[None, :] < c0[:, None] + NA_COLS)
    dc = np.clip(kc[None, :] - qc[:, None] + NA_COLS - 1, 0, 2 * NA_COLS - 2)
    sel_c = (dc[..., None] == np.arange(2 * NA_COLS - 1)).astype(np.float32)
    sel_r, oks = [], []
    for rb in (0, 1, nrb - 1):
        start = int(np.clip(NA_QROWS * rb - 4, 0, rows - NA_KROWS))
        r = NA_QROWS * rb + np.arange(NA_QROWS)
        r0 = np.clip(r - kr // 2, 0, rows - kr)
        keyrow = start + np.arange(NA_KROWS)
        row_ok = (keyrow[None, :] >= r0[:, None]) & (keyrow[None, :] < r0[:, None] + kr)
        dr = np.clip(keyrow[None, :] - r[:, None] + NA_ROWS - 1, 0, 2 * NA_ROWS - 2)
        sel_r.append((dr[..., None] == np.arange(2 * NA_ROWS - 1)).astype(np.float32))
        oks.append(row_ok[:, None, :, None] & col_ok[None, :, None, :])
    bias = jnp.einsum("hij,paki,cdj->phackd", rpb.astype(F32), jnp.asarray(np.stack(sel_r)),
                      jnp.asarray(sel_c), precision=HIGHEST)
    bias = jnp.where(jnp.asarray(np.stack(oks))[:, None], bias, NEG_BIG)
    bias = bias.reshape(3, NA_HEADS, NA_QROWS * GRID_W, NA_KROWS * GRID_W)
    return jnp.concatenate([bias, jnp.full_like(bias[:1], NEG_BIG)], axis=0)


def _na_body(q_ref, k_ref, v_ref, kc_ref, vc_ref, bias_ref, o_ref, *, rows):
    rb = pl.program_id(1)
    nk = NA_KROWS * GRID_W
    start_row = jnp.clip(NA_QROWS * rb - 4, 0, rows - NA_KROWS)
    start = pl.multiple_of(start_row * GRID_W, GRID_W)
    nq = q_ref.shape[0]
    lane = lax.broadcasted_iota(jnp.int32, (nq, 128), 1)
    scale = NA_HD ** -0.5
    for p in range(NA_HEADS // 2):
        sl = slice(128 * p, 128 * (p + 1))
        qp = q_ref[:, sl] * scale
        kp = k_ref[pl.ds(start, nk), sl]
        vp = v_ref[pl.ds(start, nk), sl]
        kcp = kc_ref[:, sl]
        vcp = vc_ref[:, sl]
        outs = []
        for hh in range(2):
            sel = (lane < NA_HD) if hh == 0 else (lane >= NA_HD)
            qh = jnp.where(sel, qp, jnp.zeros_like(qp))
            s_loc = _dot_nt(qh, kp) + bias_ref[0, 2 * p + hh]
            s_ctx = _dot_nt(qh, kcp)
            m = jnp.maximum(jnp.max(s_loc, axis=-1, keepdims=True), jnp.max(s_ctx, axis=-1, keepdims=True))
            p_loc = jnp.exp(s_loc - m)
            p_ctx = jnp.exp(s_ctx - m)
            den = jnp.sum(p_loc, axis=-1, keepdims=True) + jnp.sum(p_ctx, axis=-1, keepdims=True)
            o = _dot(p_loc.astype(BF16), vp) + _dot(p_ctx.astype(BF16), vcp)
            outs.append(o / den)
        o_ref[:, sl] = jnp.where(lane < NA_HD, outs[0], outs[1]).astype(o_ref.dtype)


def _na_call(z, bias, nbatch, t_len, c_len, with_ctx):
    na = z.shape[0]
    rows = t_len // GRID_W
    nrb = rows // NA_QROWS
    nq = NA_QROWS * GRID_W
    assert nq == c_len
    ctx_base = nbatch * nrb
    steps = nrb + 1 if with_ctx else nrb

    def qtile(b, r):
        return jnp.where(r < nrb, b * nrb + r, ctx_base + b)

    def pattern(b, r):
        return jnp.where(r == 0, 0, jnp.where(r == nrb - 1, 2, jnp.where(r == nrb, 3, 1)))

    return pl.pallas_call(
        functools.partial(_na_body, rows=rows),
        grid=(nbatch, steps),
        in_specs=[pl.BlockSpec((nq, 512), lambda b, r: (qtile(b, r), CB_NQ // 4)),
                  pl.BlockSpec((t_len, 512), lambda b, r: (b, CB_NK // 4)),
                  pl.BlockSpec((t_len, 512), lambda b, r: (b, CB_NV // 4)),
                  pl.BlockSpec((c_len, 512), lambda b, r: (ctx_base + b, CB_NK // 4)),
                  pl.BlockSpec((c_len, 512), lambda b, r: (ctx_base + b, CB_NV // 4)),
                  pl.BlockSpec((1,) + bias.shape[1:], lambda b, r: (pattern(b, r), 0, 0, 0))],
        out_specs=pl.BlockSpec((nq, 512), lambda b, r: (qtile(b, r), 0)),
        out_shape=jax.ShapeDtypeStruct((na if with_ctx else nbatch * t_len, 512), BF16),
        compiler_params=_cparams(("parallel", "arbitrary")),
        name="natten",
    )(z, z, z, z, z, bias)


def _pack_bf16_pairs(x):
    w = x.shape[1] // 2

    def bits(v):
        u = lax.bitcast_convert_type(v, jnp.uint32)
        return u + jnp.uint32(0x7FFF) + ((u >> 16) & jnp.uint32(1))

    return (bits(x[:, w:]) & jnp.uint32(0xFFFF0000)) | (bits(x[:, :w]) >> 16)


def _unpack_bf16_pairs(p):
    lo = lax.bitcast_convert_type(p << 16, F32).astype(BF16)
    hi = lax.bitcast_convert_type(p & jnp.uint32(0xFFFF0000), F32).astype(BF16)
    return lo, hi


def _merge_body(x_ref, ya_ref, yb_ref, yc_ref, g0, g1, g2, g3, g4, g5, wa_ref, wb_ref, wc_ref, wo_ref,
                lng_ref, lnb_ref, mod_ref, wr_ref, br_ref, tril_ref, x1_ref, h2_ref, route_ref, cnt_ref,
                cnt_scr, *, alpha):
    @pl.when(pl.program_id(0) == 0)
    def _():
        cnt_scr[...] = jnp.zeros_like(cnt_scr)

    gates = ((g0, g1), (g2, g3), (g4, g5))
    ys = (ya_ref[...], yb_ref[...], yc_ref[...])
    ws = (wa_ref, wb_ref, wc_ref)
    half = wa_ref.shape[1] // 2
    mix = None
    for n in range(2):
        m = None
        for kbr in range(3):
            pr = _dot(ys[kbr], ws[kbr][:, n * half:(n + 1) * half])
            term = _sigmoid(gates[kbr][n][...].astype(F32)) * pr
            m = term if m is None else m + term
        part = _dot(m.astype(BF16), wo_ref[n * half:(n + 1) * half, :])
        mix = part if mix is None else mix + part
    r = alpha * x_ref[...] + mod_ref[0, 2:3, :] * mix
    x1 = _ln(r) * lng_ref[...] + lnb_ref[...]
    x1_ref[...] = x1
    h2 = _ln(x1) * (1.0 + mod_ref[0, 4:5, :]) + mod_ref[0, 3:4, :]
    h2_ref[...] = _pack_bf16_pairs(h2)

    logits = jnp.dot(h2, wr_ref[...], preferred_element_type=F32, precision=HIGHEST) + br_ref[...]
    lane = lax.broadcasted_iota(jnp.int32, logits.shape, 1).astype(F32)
    is_grp = jnp.where(lane >= N_EXPERTS, jnp.where(lane < N_EXPERTS + N_GROUPS, 1.0, 0.0), 0.0) > 0.5
    lgm = jnp.where(is_grp, logits, NEG_BIG)
    mg = jnp.max(lgm, axis=-1, keepdims=True)
    p_grp = 1.0 / jnp.sum(jnp.exp(lgm - mg), axis=-1, keepdims=True)
    grp = jnp.min(jnp.where(lgm == mg, lane, 1e9), axis=-1, keepdims=True) - N_EXPERTS
    lo = grp * EXP_PER_GROUP
    in_grp = jnp.where(lane >= lo, jnp.where(lane < lo + EXP_PER_GROUP, 1.0, 0.0), 0.0) > 0.5
    lem = jnp.where(in_grp, logits, NEG_BIG)
    m1 = jnp.max(lem, axis=-1, keepdims=True)
    id1 = jnp.min(jnp.where(lem == m1, lane, 1e9), axis=-1, keepdims=True)
    lem2 = jnp.where(lane == id1, NEG_BIG, lem)
    m2 = jnp.max(lem2, axis=-1, keepdims=True)
    id2 = jnp.min(jnp.where(lem2 == m2, lane, 1e9), axis=-1, keepdims=True)
    u2 = jnp.exp(m2 - m1)
    w1 = p_grp / (1.0 + u2)
    w2 = p_grp * u2 / (1.0 + u2)
    oh1 = jnp.where(lane == id1, 1.0, 0.0)
    oh2 = jnp.where(lane == id2, 1.0, 0.0)
    oh = oh1 + oh2
    before = _dot(tril_ref[...], oh.astype(BF16)) + cnt_scr[...]
    rank1 = jnp.sum(before * oh1, axis=-1, keepdims=True)
    rank2 = jnp.sum(before * oh2, axis=-1, keepdims=True)
    cnt_scr[...] += jnp.sum(oh, axis=0, keepdims=True)
    cnt_ref[...] = jnp.broadcast_to(cnt_scr[...], cnt_ref.shape)
    route = jnp.zeros_like(logits)
    for ln, val in enumerate((w1, w2, id1, id2, rank1, rank2)):
        route = jnp.where(lane == ln, val, route)
    route_ref[...] = route


def _merge_call(xa, ya, yb, yc, z, mod, wa, wb, wc, wo, lng, lnb, wr, br, tm, modrow, alpha):
    na, d = yc.shape[0], xa.shape[1]
    row = lambda w: pl.BlockSpec((tm, w), lambda i: (i, 0))
    gate = lambda cb: pl.BlockSpec((tm, 512), lambda i: (i, cb))
    full = lambda a: pl.BlockSpec(a.shape, lambda i: (0,) * a.ndim)
    g0 = CB_GATE // 4
    tril = jnp.asarray(np.tril(np.ones((tm, tm), np.float32), -1), BF16)
    return pl.pallas_call(
        functools.partial(_merge_body, alpha=alpha),
        grid=(na // tm,),
        in_specs=[row(d), row(512), row(512), row(512)] + [gate(g0 + j) for j in range(6)]
                 + [full(wa), full(wb), full(wc), full(wo), full(lng), full(lnb),
                    pl.BlockSpec((1, 6, d), lambda i: (modrow(i), 0, 0)), full(wr), full(br), full(tril)],
        out_specs=[row(d), row(d // 2), row(128), pl.BlockSpec((8, 128), lambda i: (0, 0))],
        out_shape=[jax.ShapeDtypeStruct((na, d), F32), jax.ShapeDtypeStruct((na, d // 2), jnp.uint32),
                   jax.ShapeDtypeStruct((na, 128), F32), jax.ShapeDtypeStruct((8, 128), F32)],
        scratch_shapes=[pltpu.VMEM((1, 128), F32)],
        compiler_params=_cparams(("arbitrary",)),
        name="merge",
    )(xa, ya, yb, yc, z, z, z, z, z, z, wa, wb, wc, wo, lng, lnb, mod, wr, br, tril)


ROW_DMA_UNROLL = 8


def _row_copy(src_hbm, row, dst_ref, r, sem):
    return pltpu.make_async_copy(src_hbm.at[pl.ds(row, 1), :], dst_ref.at[pl.ds(r, 1), :], sem)


def _gather_rows(idx_ref, src_hbm, dst_ref, sem):
    n = dst_ref.shape[0]

    def issue(i, carry):
        for u in range(ROW_DMA_UNROLL):
            r = i * ROW_DMA_UNROLL + u
            _row_copy(src_hbm, idx_ref[0, 0, r], dst_ref, r, sem).start(priority=u % 2)
        return carry

    lax.fori_loop(0, n // ROW_DMA_UNROLL, issue, 0)

    def wait(r, carry):
        _row_copy(src_hbm, 0, dst_ref, r, sem).wait()
        return carry

    lax.fori_loop(0, n, wait, 0)


def _dispatch_body(src_ref, h_hbm, o_ref, sem):
    _gather_rows(src_ref, h_hbm, o_ref, sem)


def _dispatch_call(h2p, src, tr):
    n_tiles = src.shape[0]
    w = h2p.shape[1]
    return pl.pallas_call(
        _dispatch_body,
        grid=(n_tiles,),
        in_specs=[pl.BlockSpec((1, 1, tr), lambda j: (j, 0, 0), memory_space=pltpu.SMEM),
                  pl.BlockSpec(memory_space=pl.ANY)],
        out_specs=pl.BlockSpec((tr, w), lambda j: (j, 0)),
        out_shape=jax.ShapeDtypeStruct((n_tiles * tr, w), h2p.dtype),
        scratch_shapes=[pltpu.SemaphoreType.DMA(())],
        compiler_params=_cparams(("arbitrary",)),
        name="dispatch",
    )(src, h2p)


def _experts_body(te_ref, nu_ref, xs_ref, wg_ref, wu_ref, wd_ref, y_ref):
    @pl.when(pl.program_id(0) >= nu_ref[0])
    def _():
        y_ref[...] = jnp.zeros_like(y_ref)

    @pl.when(pl.program_id(0) < nu_ref[0])
    def _():
        lo, hi = _unpack_bf16_pairs(xs_ref[...])
        half = lo.shape[1]
        gt = _dot(lo, wg_ref[0, :half, :]) + _dot(hi, wg_ref[0, half:, :])
        up = _dot(lo, wu_ref[0, :half, :]) + _dot(hi, wu_ref[0, half:, :])
        a = gt * _sigmoid(gt) * up
        y_ref[...] = _pack_bf16_pairs(_dot(a.astype(BF16), wd_ref[0]))


def _experts_call(tile_expert, n_used, xs, wg, wu, wd, tr):
    rows, w = xs.shape
    ne, d, de = wg.shape
    used = lambda j, te, nu: jnp.minimum(j, nu[0] - 1)
    return pl.pallas_call(
        _experts_body,
        grid_spec=pltpu.PrefetchScalarGridSpec(
            num_scalar_prefetch=2,
            grid=(rows // tr,),
            in_specs=[pl.BlockSpec((tr, w), lambda j, te, nu: (used(j, te, nu), 0)),
                      pl.BlockSpec((1, d, de), lambda j, te, nu: (te[used(j, te, nu)], 0, 0)),
                      pl.BlockSpec((1, d, de), lambda j, te, nu: (te[used(j, te, nu)], 0, 0)),
                      pl.BlockSpec((1, de, d), lambda j, te, nu: (te[used(j, te, nu)], 0, 0))],
            out_specs=pl.BlockSpec((tr, w), lambda j, te, nu: (j, 0))),
        out_shape=jax.ShapeDtypeStruct((rows, w), jnp.uint32),
        compiler_params=_cparams(("arbitrary",)),
        name="experts",
    )(tile_expert, n_used, xs, wg, wu, wd)


def _combine_body(d1_ref, d2_ref, y_hbm, x1_ref, route_ref, lng_ref, lnb_ref, mod_ref, o_ref,
                  b1_scr, b2_scr, sem, *, alpha):
    _gather_rows(d1_ref, y_hbm, b1_scr, sem)
    _gather_rows(d2_ref, y_hbm, b2_scr, sem)
    w1 = route_ref[:, 0:1]
    w2 = route_ref[:, 1:2]
    half = b1_scr.shape[1]
    g2 = mod_ref[0, 5:6, :]
    x1 = x1_ref[...]
    parts = []
    for hf, (a1, a2) in enumerate(zip(_unpack_bf16_pairs(b1_scr[...]), _unpack_bf16_pairs(b2_scr[...]))):
        sl = slice(hf * half, (hf + 1) * half)
        moe = w1 * a1.astype(F32) + w2 * a2.astype(F32)
        parts.append(alpha * x1[:, sl] + g2[:, sl] * moe)
    r = jnp.concatenate(parts, axis=1)
    o_ref[...] = _ln(r) * lng_ref[...] + lnb_ref[...]


def _combine_call(dest1, dest2, y, x1, route, lng, lnb, mod, tm, modrow, alpha):
    na, d = x1.shape
    w = y.shape[1]
    full = lambda a: pl.BlockSpec(a.shape, lambda i: (0,) * a.ndim)
    smem = pl.BlockSpec((1, 1, tm), lambda i: (i, 0, 0), memory_space=pltpu.SMEM)
    return pl.pallas_call(
        functools.partial(_combine_body, alpha=alpha),
        grid=(na // tm,),
        in_specs=[smem, smem, pl.BlockSpec(memory_space=pl.ANY),
                  pl.BlockSpec((tm, d), lambda i: (i, 0)),
                  pl.BlockSpec((tm, 128), lambda i: (i, 0)),
                  full(lng), full(lnb),
                  pl.BlockSpec((1, 6, d), lambda i: (modrow(i), 0, 0))],
        out_specs=pl.BlockSpec((tm, d), lambda i: (i, 0)),
        out_shape=jax.ShapeDtypeStruct((na, d), F32),
        scratch_shapes=[pltpu.VMEM((tm, w), jnp.uint32), pltpu.VMEM((tm, w), jnp.uint32),
                        pltpu.SemaphoreType.DMA(())],
        compiler_params=_cparams(("arbitrary",)),
        name="combine",
    )(dest1.reshape(na // tm, 1, tm), dest2.reshape(na // tm, 1, tm), y, x1, route, lng, lnb, mod)


def _routing_tables(route, counts, na, tr):
    cnt = counts[0, :N_EXPERTS].astype(jnp.int32)
    ntile = (cnt + tr - 1) // tr
    tile_start = jnp.cumsum(ntile) - ntile
    n_used = jnp.sum(ntile)
    n_tiles = -(-2 * na // tr) + N_EXPERTS
    id1 = route[:, 2].astype(jnp.int32)
    id2 = route[:, 3].astype(jnp.int32)
    dest1 = tile_start[id1] * tr + route[:, 4].astype(jnp.int32)
    dest2 = tile_start[id2] * tr + route[:, 5].astype(jnp.int32)
    tok = jnp.arange(na, dtype=jnp.int32)
    src = jnp.zeros((n_tiles * tr,), jnp.int32).at[dest1].set(tok).at[dest2].set(tok)
    tile_expert = jnp.sum(jnp.arange(n_tiles, dtype=jnp.int32)[:, None] >= tile_start[None, :], axis=1) - 1
    return dest1, dest2, src.reshape(n_tiles, 1, tr), tile_expert.astype(jnp.int32), n_used.reshape(1).astype(jnp.int32)


def _lower_bounds(logits):
    p = jax.nn.softmax(logits.astype(F32), axis=0)
    return jnp.cumsum(p, axis=0) - p[:1]


def _row_tile(limit, *sizes):
    tm = limit
    while any(s % tm for s in sizes):
        tm //= 2
    return tm


def kernel(x, c, ctx, c_ctx, w_ada, b_ada, w_in, w_pool, pool_scale, lb_logits_fwd, lb_logits_bwd, hg_gain, rpb, w_br_a, w_br_b, w_br_c, w_out, ln1_g, ln1_b, w_rg, b_rg, w_re, b_re, w_gate, w_up, w_down, ln2_g, ln2_b):
    nbatch, t_len, d = x.shape
    c_len = ctx.shape[1]
    depth = w_ada.shape[0]
    assert c_len == SEQ_TILE and t_len % SEQ_TILE == 0 and t_len % GRID_W == 0
    alpha = (2.0 * depth) ** 0.25
    n_lat = nbatch * t_len
    nt = t_len // SEQ_TILE

    xa = jnp.concatenate([x.reshape(n_lat, d), ctx.reshape(nbatch * c_len, d)], axis=0)

    mod_rows = -(-(nbatch + 1) // 8) * 8
    cc = jnp.zeros((mod_rows, d), F32).at[:nbatch].set(c).at[nbatch].set(c_ctx)
    ada = _ada_call(cc, w_ada, b_ada)

    lb_f = _lower_bounds(lb_logits_fwd).reshape(depth, HG_HEADS, 1, HG_DK)
    lb_b = _lower_bounds(lb_logits_bwd).reshape(depth, HG_HEADS, 1, HG_DK)
    pool_consts = _pool_consts()
    hg_f = _hgrn_consts(False)
    hg_b = _hgrn_consts(True)

    tm_big = _row_tile(1024, t_len, nbatch * c_len)
    tm_mid = _row_tile(512, t_len, nbatch * c_len)

    def modrow_for(tm):
        return lambda i: jnp.where(i * tm < n_lat, (i * tm) // t_len, nbatch)

    for l in range(depth):
        last = l == depth - 1
        mod = ada[l].reshape(mod_rows, 6, d)
        z = _inproj_call(xa, mod, w_in[l].astype(BF16), tm_big, modrow_for(tm_big))
        ya = _pool_call(z, pool_consts, w_pool[l].astype(BF16), pool_scale[l].reshape(1, -1),
                        nt, nbatch * nt)
        o_f = _hgrn_call(z, lb_f[l], hg_f, nbatch, nt, reverse=False)
        yb = _hgrn_call(z, lb_b[l], hg_b, nbatch, nt, reverse=True, o_fwd=o_f,
                        gain=hg_gain[l].reshape(HG_HEADS, 1, HG_DK))
        bias = _na_bias_table(rpb[l], t_len // GRID_W)
        yc = _na_call(z, bias, nbatch, t_len, c_len, with_ctx=not last)
        wr = jnp.zeros((d, 128), F32).at[:, :N_EXPERTS].set(w_re[l]).at[:, N_EXPERTS:N_EXPERTS + N_GROUPS].set(w_rg[l])
        br = jnp.zeros((1, 128), F32).at[0, :N_EXPERTS].set(b_re[l]).at[0, N_EXPERTS:N_EXPERTS + N_GROUPS].set(b_rg[l])
        x1, h2p, route, counts = _merge_call(
            xa, ya, yb, yc, z, mod, w_br_a[l].astype(BF16), w_br_b[l].astype(BF16), w_br_c[l].astype(BF16),
            w_out[l].astype(BF16), ln1_g[l].reshape(1, d), ln1_b[l].reshape(1, d), wr, br, tm_mid,
            modrow_for(tm_mid), alpha)
        dest1, dest2, src, tile_expert, n_used = _routing_tables(route, counts, route.shape[0], EXPERT_TILE)
        xs = _dispatch_call(h2p, src, EXPERT_TILE)
        ys = _experts_call(tile_expert, n_used, xs, w_gate[l].astype(BF16), w_up[l].astype(BF16),
                           w_down[l].astype(BF16), EXPERT_TILE)
        xa = _combine_call(dest1, dest2, ys, x1, route, ln2_g[l].reshape(1, d), ln2_b[l].reshape(1, d), mod,
                           tm_mid, modrow_for(tm_mid), alpha)
    return xa.reshape(nbatch, t_len, d)
```

```python
import functools

import numpy as np
import jax
import jax.numpy as jnp
from jax import lax
from jax.experimental import pallas as pl
from jax.experimental.pallas import tpu as pltpu

F32 = jnp.float32
BF16 = jnp.bfloat16
HIGHEST = lax.Precision.HIGHEST

GRID_W = 64
POOL_WINDOWS = (2, 4, 8, 16)
POOL_GDIM = 128
HG_HEADS = 4
HG_DK = 128
HG_BLOCK = 16
NA_HEADS = 8
NA_HD = 64
NA_ROWS = 8
NA_COLS = 16
NA_QROWS = 4
NA_KROWS = 12
N_GROUPS = 4
EXP_PER_GROUP = 8
N_EXPERTS = N_GROUPS * EXP_PER_GROUP
LN_EPS = 1e-5
RMS_EPS = 1e-6
NEG_BIG = -1e30
SEQ_TILE = 256
EXPERT_TILE = 512
VMEM_LIMIT = 56 * 1024 * 1024

CB_A, CB_Q, CB_FF, CB_FB, CB_I, CB_G, CB_NQ, CB_NK, CB_NV, CB_GATE = 0, 4, 8, 12, 16, 20, 24, 28, 32, 36


def _cparams(sem):
    return pltpu.CompilerParams(dimension_semantics=sem, vmem_limit_bytes=VMEM_LIMIT)


def _ln(x):
    mu = jnp.mean(x, axis=-1, keepdims=True)
    xc = x - mu
    var = jnp.mean(xc * xc, axis=-1, keepdims=True)
    return xc * lax.rsqrt(var + LN_EPS)


def _sigmoid(x):
    return 1.0 / (1.0 + jnp.exp(-x))


def _dot(a, b):
    return jnp.dot(a, b, preferred_element_type=F32)


def _dot_nt(a, b):
    return lax.dot_general(a, b, (((1,), (1,)), ((), ())), preferred_element_type=F32)


def _dot_tn(a, b):
    return lax.dot_general(a, b, (((0,), (0,)), ((), ())), preferred_element_type=F32)


def _dot01(m01, x):
    x1 = x.astype(BF16)
    r1 = x - x1.astype(F32)
    x2 = r1.astype(BF16)
    x3 = (r1 - x2.astype(F32)).astype(BF16)
    return _dot(m01, x1) + _dot(m01, x2) + _dot(m01, x3)


def _ada_body(c_ref, w_ref, b_ref, o_ref):
    cs = c_ref[...]
    s = cs * _sigmoid(cs)
    o_ref[0] = jnp.dot(s, w_ref[0], preferred_element_type=F32, precision=HIGHEST) + b_ref[0]


def _ada_call(cc, w_ada, b_ada):
    depth, d, n6 = w_ada.shape
    rows = cc.shape[0]
    return pl.pallas_call(
        _ada_body,
        grid=(depth, n6 // d),
        in_specs=[pl.BlockSpec((rows, d), lambda l, j: (0, 0)),
                  pl.BlockSpec((1, d, d), lambda l, j: (l, 0, j)),
                  pl.BlockSpec((1, 1, d), lambda l, j: (l, 0, j))],
        out_specs=pl.BlockSpec((1, rows, d), lambda l, j: (l, 0, j)),
        out_shape=jax.ShapeDtypeStruct((depth, rows, n6), F32),
        compiler_params=_cparams(("parallel", "parallel")),
        name="ada",
    )(cc, w_ada, b_ada.reshape(depth, 1, n6))


def _inproj_body(x_ref, mod_ref, w_ref, z_ref, h_scr):
    @pl.when(pl.program_id(1) == 0)
    def _():
        h = _ln(x_ref[...]) * (1.0 + mod_ref[0, 1:2, :]) + mod_ref[0, 0:1, :]
        h_scr[...] = h.astype(BF16)

    z_ref[...] = _dot(h_scr[...], w_ref[...]).astype(z_ref.dtype)


def _inproj_call(xa, mod, w_in, tm, modrow):
    na, d = xa.shape
    d_in = w_in.shape[1]
    tn = 512
    return pl.pallas_call(
        _inproj_body,
        grid=(na // tm, d_in // tn),
        in_specs=[pl.BlockSpec((tm, d), lambda i, j: (i, 0)),
                  pl.BlockSpec((1, 6, d), lambda i, j: (modrow(i), 0, 0)),
                  pl.BlockSpec((d, tn), lambda i, j: (0, j))],
        out_specs=pl.BlockSpec((tm, tn), lambda i, j: (i, j)),
        out_shape=jax.ShapeDtypeStruct((na, d_in), BF16),
        scratch_shapes=[pltpu.VMEM((tm, d), BF16)],
        compiler_params=_cparams(("parallel", "arbitrary")),
        name="inproj",
    )(xa, mod, w_in)


def _pool_consts():
    n = SEQ_TILE
    t = np.arange(n)[:, None]
    bc = np.zeros((4, n, n), np.float32)
    bp = np.zeros((4, n, 16), np.float32)
    bn = np.zeros((4, n, 16), np.float32)
    for g, win in enumerate(POOL_WINDOWS):
        lo, hi = t - win // 2, t + win // 2 - 1
        s = np.arange(n)[None, :]
        bc[g] = (s >= lo) & (s <= hi)
        s = np.arange(16)[None, :] - 16
        bp[g] = (s >= lo) & (s <= hi)
        s = np.arange(16)[None, :] + n
        bn[g] = (s >= lo) & (s <= hi)
    cnt = np.stack([bc.sum(-1), bp.sum(-1), bn.sum(-1)], axis=1)
    cnt = np.broadcast_to(cnt[..., None], (4, 3, n, 128)).astype(np.float32)
    return (jnp.asarray(bc, BF16), jnp.asarray(bp, BF16), jnp.asarray(bn, BF16), jnp.asarray(cnt))


def _pool_body(prev_ref, cur_ref, next_ref, bc_ref, bp_ref, bn_ref, cnt_ref, wp_ref, ps_ref, o_ref,
               *, nt, n_lat_tiles):
    i = pl.program_id(0)
    k = i % nt
    is_lat = i < n_lat_tiles
    has_prev = jnp.where(jnp.logical_and(is_lat, k != 0), 1.0, 0.0).astype(F32)
    has_next = jnp.where(jnp.logical_and(is_lat, k != nt - 1), 1.0, 0.0).astype(F32)
    for g in range(len(POOL_WINDOWS)):
        sl = slice(g * POOL_GDIM, (g + 1) * POOL_GDIM)
        u = cur_ref[:, sl]
        ssum = (_dot(bc_ref[g], u) + has_prev * _dot(bp_ref[g], prev_ref[:, sl])
                + has_next * _dot(bn_ref[g], next_ref[:, sl]))
        cnt = cnt_ref[g, 0] + has_prev * cnt_ref[g, 1] + has_next * cnt_ref[g, 2]
        dlt = ssum / cnt - u.astype(F32)
        y = _dot(dlt.astype(BF16), wp_ref[g]) * ps_ref[:, sl]
        o_ref[:, sl] = y.astype(o_ref.dtype)


def _pool_call(z, consts, w_pool, pool_scale, nt, n_lat_tiles):
    na = z.shape[0]
    n = SEQ_TILE
    hb = n // 16
    last16 = na // 16 - 1
    bc, bp, bn, cnt = consts
    full = lambda a: pl.BlockSpec(a.shape, lambda i: (0,) * a.ndim)
    return pl.pallas_call(
        functools.partial(_pool_body, nt=nt, n_lat_tiles=n_lat_tiles),
        grid=(na // n,),
        in_specs=[pl.BlockSpec((16, 512), lambda i: (jnp.maximum(i * hb - 1, 0), CB_A // 4)),
                  pl.BlockSpec((n, 512), lambda i: (i, CB_A // 4)),
                  pl.BlockSpec((16, 512), lambda i: (jnp.minimum((i + 1) * hb, last16), CB_A // 4)),
                  full(bc), full(bp), full(bn), full(cnt), full(w_pool), full(pool_scale)],
        out_specs=pl.BlockSpec((n, 512), lambda i: (i, 0)),
        out_shape=jax.ShapeDtypeStruct((na, 512), BF16),
        compiler_params=_cparams(("parallel",)),
        name="pool",
    )(z, z, z, bc, bp, bn, cnt, w_pool, pool_scale)


def _hgrn_consts(reverse):
    n, bs = SEQ_TILE, HG_BLOCK
    nb = n // bs
    t = np.arange(n)
    o = (n - 1 - t) if reverse else t
    blk = t // bs
    jb = np.arange(nb)
    ob = (nb - 1 - jb) if reverse else jb
    cum = ((blk[:, None] == blk[None, :]) & (o[None, :] <= o[:, None])).astype(np.float32)
    bsum = (jb[:, None] == blk[None, :]).astype(np.float32)
    widths = [2 ** l for l in range(1, int(np.log2(nb)) + 1)]
    lvl = np.full((n, n), -1, np.int32)
    obt = ob[blk]
    same = blk[:, None] == blk[None, :]
    lvl[same & (o[None, :] <= o[:, None])] = 0
    for li, w in reversed(list(enumerate(widths, start=1))):
        m = (obt[:, None] // w == obt[None, :] // w) & (obt[None, :] < obt[:, None]) & ~same
        lvl[m] = li
    mats = []
    for w in widths:
        mid = (ob // w) * w + w // 2
        mats.append((mid[:, None] <= ob[None, :]) & (ob[None, :] < ob[:, None]))
    for w in widths:
        mid = (ob // w) * w + w // 2
        mats.append((ob[:, None] < ob[None, :]) & (ob[None, :] < mid[:, None]))
    mats.append(ob[None, :] < ob[:, None])
    mats.append(ob[None, :] > ob[:, None])
    mats.append(np.ones((nb, nb), bool))
    tsm = np.concatenate(mats, axis=0).astype(np.float32)
    return (jnp.asarray(cum, BF16), jnp.asarray(bsum, BF16), jnp.asarray(lvl), jnp.asarray(tsm, BF16),
            len(widths))


def _expand_blocks(c, n):
    nb, lanes = c.shape
    return jnp.broadcast_to(c[:, None, :], (nb, n // nb, lanes)).reshape(n, lanes)


def _hgrn_body(*refs, n_levels, final):
    if final:
        (zq_ref, zf_ref, zi_ref, lb_ref, cum_ref, bsum_ref, lvl_ref, tsm_ref,
         of_ref, zg_ref, gain_ref, o_ref, st_scr) = refs
    else:
        zq_ref, zf_ref, zi_ref, lb_ref, cum_ref, bsum_ref, lvl_ref, tsm_ref, o_ref, st_scr = refs
    n = zq_ref.shape[0]
    nb = n // HG_BLOCK

    @pl.when(pl.program_id(1) == 0)
    def _():
        st_scr[...] = jnp.zeros_like(st_scr)

    for h in range(HG_HEADS):
        sl = slice(h * HG_DK, (h + 1) * HG_DK)
        zq = zq_ref[:, sl].astype(F32)
        zf = zf_ref[:, sl].astype(F32)
        lb = lb_ref[h]
        lf = jnp.log(lb + (1.0 - lb) * _sigmoid(zf))
        k = (1.0 - lb) * _sigmoid(-zf)
        q = zq * _sigmoid(zq)
        v = zi_ref[:, sl]

        b = _dot01(cum_ref[...], lf)
        tot = _dot01(bsum_ref[...], lf)
        coef = _dot01(tsm_ref[...], tot)
        crow = lambda idx: coef[idx * nb:(idx + 1) * nb]

        qd = q * jnp.exp(b)
        kd = (k * jnp.exp(-b)).astype(BF16)
        ks = k * jnp.exp(_expand_blocks(tot, n) - b)

        lvl = lvl_ref[...]
        a = jnp.where(lvl == 0, _dot_nt(qd.astype(BF16), kd), 0.0)
        for li in range(1, n_levels + 1):
            qw = (qd * _expand_blocks(jnp.exp(crow(li - 1)), n)).astype(BF16)
            kw = (ks * _expand_blocks(jnp.exp(crow(n_levels + li - 1)), n)).astype(BF16)
            a = jnp.where(lvl == li, _dot_nt(qw, kw), a)
        o = _dot(a.astype(BF16), v)

        st = st_scr[h]
        qs = (qd * _expand_blocks(jnp.exp(crow(2 * n_levels)), n)).astype(BF16)
        o = o + _dot_nt(qs, st.astype(BF16))
        kn = (ks * _expand_blocks(jnp.exp(crow(2 * n_levels + 1)), n)).astype(BF16)
        dec = jnp.exp(coef[(2 * n_levels + 2) * nb:(2 * n_levels + 2) * nb + 1])
        st_scr[h] = st * dec + _dot_tn(v, kn)

        if final:
            o = o + of_ref[:, sl]
            o = o * lax.rsqrt(jnp.mean(o * o, axis=-1, keepdims=True) + RMS_EPS) * gain_ref[h]
            zg = zg_ref[:, sl].astype(F32)
            o_ref[:, sl] = (o * (zg * _sigmoid(zg))).astype(o_ref.dtype)
        else:
            o_ref[:, sl] = o


def _hgrn_call(z, lb, consts, nbatch, nt, reverse, o_fwd=None, gain=None):
    na = z.shape[0]
    n = SEQ_TILE
    cum, bsum, lvl, tsm, n_levels = consts
    ctx_base = nbatch * nt
    final = o_fwd is not None

    def tile(b, s):
        lat = (b * nt + nt - s) if reverse else (b * nt + s - 1)
        return jnp.where(s == 0, ctx_base + b, lat)

    width = HG_HEADS * HG_DK

    def col(cb):
        return pl.BlockSpec((n, width), lambda b, s: (tile(b, s), cb // HG_HEADS))

    full = lambda a: pl.BlockSpec(a.shape, lambda b, s: (0,) * a.ndim)
    in_specs = [col(CB_Q), col(CB_FB if reverse else CB_FF), col(CB_I), full(lb),
                full(cum), full(bsum), full(lvl), full(tsm)]
    args = [z, z, z, lb, cum, bsum, lvl, tsm]
    if final:
        in_specs += [col(0), col(CB_G), full(gain)]
        args += [o_fwd, z, gain]
    return pl.pallas_call(
        functools.partial(_hgrn_body, n_levels=n_levels, final=final),
        grid=(nbatch, nt + 1),
        in_specs=in_specs,
        out_specs=col(0),
        out_shape=jax.ShapeDtypeStruct((na, width), BF16 if final else F32),
        scratch_shapes=[pltpu.VMEM((HG_HEADS, HG_DK, HG_DK), F32)],
        compiler_params=_cparams(("parallel", "arbitrary")),
        name="hgrn_bwd" if final else "hgrn_fwd",
    )(*args)


def _na_bias_table(rpb, rows):
    nrb = rows // NA_QROWS
    assert nrb >= 3 and rows >= NA_KROWS
    kr = min(NA_ROWS, rows)
    qc = np.arange(GRID_W)
    c0 = np.clip(qc - NA_COLS // 2, 0, GRID_W - NA_COLS)
    kc = np.arange(GRID_W)
    col_ok = (kc[None, :] >= c0[:, None]) & (kc[None, :] < c0[:, None] + NA_COLS)
    dc = np.clip(kc[None, :] - qc[:, None] + NA_COLS - 1, 0, 2 * NA_COLS - 2)
    sel_c = (dc[..., None] == np.arange(2 * NA_COLS - 1)).astype(np.float32)
    sel_r, oks = [], []
    for rb in (0, 1, nrb - 1):
        start = int(np.clip(NA_QROWS * rb - 4, 0, rows - NA_KROWS))
        r = NA_QROWS * rb + np.arange(NA_QROWS)
        r0 = np.clip(r - kr // 2, 0, rows - kr)
        keyrow = start + np.arange(NA_KROWS)
        row_ok = (keyrow[None, :] >= r0[:, None]) & (keyrow[None, :] < r0[:, None] + kr)
        dr = np.clip(keyrow[None, :] - r[:, None] + NA_ROWS - 1, 0, 2 * NA_ROWS - 2)
        sel_r.append((dr[..., None] == np.arange(2 * NA_ROWS - 1)).astype(np.float32))
        oks.append(row_ok[:, None, :, None] & col_ok[None, :, None, :])
    bias = jnp.einsum("hij,paki,cdj->phackd", rpb.astype(F32), jnp.asarray(np.stack(sel_r)),
                      jnp.asarray(sel_c), precision=HIGHEST)
    bias = jnp.where(jnp.asarray(np.stack(oks))[:, None], bias, NEG_BIG)
    bias = bias.reshape(3, NA_HEADS, NA_QROWS * GRID_W, NA_KROWS * GRID_W)
    return jnp.concatenate([bias, jnp.full_like(bias[:1], NEG_BIG)], axis=0)


def _na_body(q_ref, k_ref, v_ref, kc_ref, vc_ref, bias_ref, o_ref, *, rows):
    rb = pl.program_id(1)
    nk = NA_KROWS * GRID_W
    start_row = jnp.clip(NA_QROWS * rb - 4, 0, rows - NA_KROWS)
    start = pl.multiple_of(start_row * GRID_W, GRID_W)
    nq = q_ref.shape[0]
    lane = lax.broadcasted_iota(jnp.int32, (nq, 128), 1)
    scale = NA_HD ** -0.5
    for p in range(NA_HEADS // 2):
        sl = slice(128 * p, 128 * (p + 1))
        qp = q_ref[:, sl] * scale
        kp = k_ref[pl.ds(start, nk), sl]
        vp = v_ref[pl.ds(start, nk), sl]
        kcp = kc_ref[:, sl]
        vcp = vc_ref[:, sl]
        outs = []
        for hh in range(2):
            sel = (lane < NA_HD) if hh == 0 else (lane >= NA_HD)
            qh = jnp.where(sel, qp, jnp.zeros_like(qp))
            s_loc = _dot_nt(qh, kp) + bias_ref[0, 2 * p + hh]
            s_ctx = _dot_nt(qh, kcp)
            m = jnp.maximum(jnp.max(s_loc, axis=-1, keepdims=True), jnp.max(s_ctx, axis=-1, keepdims=True))
            p_loc = jnp.exp(s_loc - m)
            p_ctx = jnp.exp(s_ctx - m)
            den = jnp.sum(p_loc, axis=-1, keepdims=True) + jnp.sum(p_ctx, axis=-1, keepdims=True)
            o = _dot(p_loc.astype(BF16), vp) + _dot(p_ctx.astype(BF16), vcp)
            outs.append(o / den)
        o_ref[:, sl] = jnp.where(lane < NA_HD, outs[0], outs[1]).astype(o_ref.dtype)


def _na_call(z, bias, nbatch, t_len, c_len, with_ctx):
    na = z.shape[0]
    rows = t_len // GRID_W
    nrb = rows // NA_QROWS
    nq = NA_QROWS * GRID_W
    assert nq == c_len
    ctx_base = nbatch * nrb
    steps = nrb + 1 if with_ctx else nrb

    def qtile(b, r):
        return jnp.where(r < nrb, b * nrb + r, ctx_base + b)

    def pattern(b, r):
        return jnp.where(r == 0, 0, jnp.where(r == nrb - 1, 2, jnp.where(r == nrb, 3, 1)))

    return pl.pallas_call(
        functools.partial(_na_body, rows=rows),
        grid=(nbatch, steps),
        in_specs=[pl.BlockSpec((nq, 512), lambda b, r: (qtile(b, r), CB_NQ // 4)),
                  pl.BlockSpec((t_len, 512), lambda b, r: (b, CB_NK // 4)),
                  pl.BlockSpec((t_len, 512), lambda b, r: (b, CB_NV // 4)),
                  pl.BlockSpec((c_len, 512), lambda b, r: (ctx_base + b, CB_NK // 4)),
                  pl.BlockSpec((c_len, 512), lambda b, r: (ctx_base + b, CB_NV // 4)),
                  pl.BlockSpec((1,) + bias.shape[1:], lambda b, r: (pattern(b, r), 0, 0, 0))],
        out_specs=pl.BlockSpec((nq, 512), lambda b, r: (qtile(b, r), 0)),
        out_shape=jax.ShapeDtypeStruct((na if with_ctx else nbatch * t_len, 512), BF16),
        compiler_params=_cparams(("parallel", "arbitrary")),
        name="natten",
    )(z, z, z, z, z, bias)


ROW_SLABS = 4


def _store_packed_rows(ref, x):
    m = x.shape[0]

    def bits(v):
        u = lax.bitcast_convert_type(v, jnp.uint32)
        return u + jnp.uint32(0x7FFF) + ((u >> 16) & jnp.uint32(1))

    for s in range(ROW_SLABS):
        lo = x[:, 128 * s:128 * (s + 1)]
        hi = x[:, 512 + 128 * s:512 + 128 * (s + 1)]
        ref[pl.ds(s, m, stride=ROW_SLABS), :] = (bits(hi) & jnp.uint32(0xFFFF0000)) | (bits(lo) >> 16)


def _load_packed_rows(ref):
    m = ref.shape[0] // ROW_SLABS
    los, his = [], []
    for s in range(ROW_SLABS):
        p = ref[pl.ds(s, m, stride=ROW_SLABS), :]
        los.append(lax.bitcast_convert_type(p << 16, F32))
        his.append(lax.bitcast_convert_type(p & jnp.uint32(0xFFFF0000), F32))
    return los + his


def _merge_body(x_ref, ya_ref, yb_ref, yc_ref, g0, g1, g2, g3, g4, g5, wa_ref, wb_ref, wc_ref, wo_ref,
                lng_ref, lnb_ref, mod_ref, wr_ref, br_ref, tril_ref, x1_ref, h2_ref, route_ref, cnt_ref,
                cnt_scr, *, alpha):
    @pl.when(pl.program_id(0) == 0)
    def _():
        cnt_scr[...] = jnp.zeros_like(cnt_scr)

    gates = ((g0, g1), (g2, g3), (g4, g5))
    ys = (ya_ref[...], yb_ref[...], yc_ref[...])
    ws = (wa_ref, wb_ref, wc_ref)
    half = wa_ref.shape[1] // 2
    mix = None
    for n in range(2):
        m = None
        for kbr in range(3):
            pr = _dot(ys[kbr], ws[kbr][:, n * half:(n + 1) * half])
            term = _sigmoid(gates[kbr][n][...].astype(F32)) * pr
            m = term if m is None else m + term
        part = _dot(m.astype(BF16), wo_ref[n * half:(n + 1) * half, :])
        mix = part if mix is None else mix + part
    r = alpha * x_ref[...] + mod_ref[0, 2:3, :] * mix
    x1 = _ln(r) * lng_ref[...] + lnb_ref[...]
    x1_ref[...] = x1
    h2 = _ln(x1) * (1.0 + mod_ref[0, 4:5, :]) + mod_ref[0, 3:4, :]
    _store_packed_rows(h2_ref, h2)

    logits = jnp.dot(h2, wr_ref[...], preferred_element_type=F32, precision=HIGHEST) + br_ref[...]
    lane = lax.broadcasted_iota(jnp.int32, logits.shape, 1).astype(F32)
    is_grp = jnp.where(lane >= N_EXPERTS, jnp.where(lane < N_EXPERTS + N_GROUPS, 1.0, 0.0), 0.0) > 0.5
    lgm = jnp.where(is_grp, logits, NEG_BIG)
    mg = jnp.max(lgm, axis=-1, keepdims=True)
    p_grp = 1.0 / jnp.sum(jnp.exp(lgm - mg), axis=-1, keepdims=True)
    grp = jnp.min(jnp.where(lgm == mg, lane, 1e9), axis=-1, keepdims=True) - N_EXPERTS
    lo = grp * EXP_PER_GROUP
    in_grp = jnp.where(lane >= lo, jnp.where(lane < lo + EXP_PER_GROUP, 1.0, 0.0), 0.0) > 0.5
    lem = jnp.where(in_grp, logits, NEG_BIG)
    m1 = jnp.max(lem, axis=-1, keepdims=True)
    id1 = jnp.min(jnp.where(lem == m1, lane, 1e9), axis=-1, keepdims=True)
    lem2 = jnp.where(lane == id1, NEG_BIG, lem)
    m2 = jnp.max(lem2, axis=-1, keepdims=True)
    id2 = jnp.min(jnp.where(lem2 == m2, lane, 1e9), axis=-1, keepdims=True)
    u2 = jnp.exp(m2 - m1)
    w1 = p_grp / (1.0 + u2)
    w2 = p_grp * u2 / (1.0 + u2)
    oh1 = jnp.where(lane == id1, 1.0, 0.0)
    oh2 = jnp.where(lane == id2, 1.0, 0.0)
    oh = oh1 + oh2
    before = _dot(tril_ref[...], oh.astype(BF16)) + cnt_scr[...]
    rank1 = jnp.sum(before * oh1, axis=-1, keepdims=True)
    rank2 = jnp.sum(before * oh2, axis=-1, keepdims=True)
    cnt_scr[...] += jnp.sum(oh, axis=0, keepdims=True)
    cnt_ref[...] = jnp.broadcast_to(cnt_scr[...], cnt_ref.shape)
    route = jnp.zeros_like(logits)
    for ln, val in enumerate((w1, w2, id1, id2, rank1, rank2)):
        route = jnp.where(lane == ln, val, route)
    route_ref[...] = route


def _merge_call(xa, ya, yb, yc, z, mod, wa, wb, wc, wo, lng, lnb, wr, br, tm, modrow, alpha):
    na, d = yc.shape[0], xa.shape[1]
    row = lambda w: pl.BlockSpec((tm, w), lambda i: (i, 0))
    gate = lambda cb: pl.BlockSpec((tm, 512), lambda i: (i, cb))
    full = lambda a: pl.BlockSpec(a.shape, lambda i: (0,) * a.ndim)
    g0 = CB_GATE // 4
    tril = jnp.asarray(np.tril(np.ones((tm, tm), np.float32), -1), BF16)
    return pl.pallas_call(
        functools.partial(_merge_body, alpha=alpha),
        grid=(na // tm,),
        in_specs=[row(d), row(512), row(512), row(512)] + [gate(g0 + j) for j in range(6)]
                 + [full(wa), full(wb), full(wc), full(wo), full(lng), full(lnb),
                    pl.BlockSpec((1, 6, d), lambda i: (modrow(i), 0, 0)), full(wr), full(br), full(tril)],
        out_specs=[row(d), pl.BlockSpec((tm * ROW_SLABS, 128), lambda i: (i, 0)), row(128),
                   pl.BlockSpec((8, 128), lambda i: (0, 0))],
        out_shape=[jax.ShapeDtypeStruct((na, d), F32), jax.ShapeDtypeStruct((na * ROW_SLABS, 128), jnp.uint32),
                   jax.ShapeDtypeStruct((na, 128), F32), jax.ShapeDtypeStruct((8, 128), F32)],
        scratch_shapes=[pltpu.VMEM((1, 128), F32)],
        compiler_params=_cparams(("arbitrary",)),
        name="merge",
    )(xa, ya, yb, yc, z, z, z, z, z, z, wa, wb, wc, wo, lng, lnb, mod, wr, br, tril)


ROW_DMA_UNROLL = 8


def _slab(ref, row):
    start = row * ROW_SLABS
    if not isinstance(row, int):
        start = pl.multiple_of(start, ROW_SLABS)
    return ref.at[pl.ds(start, ROW_SLABS), :]


def _slots_body(route_ref, seg_ref, o_ref):
    route = route_ref[...]
    lane = lax.broadcasted_iota(jnp.int32, route.shape, 1).astype(F32)
    seg = seg_ref[...]
    slots = []
    for k in range(2):
        eid = route[:, 2 + k:3 + k]
        start = jnp.sum(jnp.where(lane == eid, seg, 0.0), axis=-1, keepdims=True)
        slots.append(start + route[:, 4 + k:5 + k])
    both = jnp.where(lane == 0.0, slots[0], jnp.where(lane == 1.0, slots[1], 0.0))
    o_ref[0] = both.T[:8].astype(jnp.int32)


def _slots_call(route, seg, tm):
    na = route.shape[0]
    return pl.pallas_call(
        _slots_body,
        grid=(na // tm,),
        in_specs=[pl.BlockSpec((tm, 128), lambda i: (i, 0)), pl.BlockSpec((1, 128), lambda i: (0, 0))],
        out_specs=pl.BlockSpec((1, 8, tm), lambda i: (i, 0, 0)),
        out_shape=jax.ShapeDtypeStruct((na // tm, 8, tm), jnp.int32),
        compiler_params=_cparams(("parallel",)),
        name="slots",
    )(route, seg)


def _dispatch_body(slot_ref, h_ref, xs_in, xs_out, sem):
    del xs_in
    tm = h_ref.shape[0] // ROW_SLABS

    def issue(i, carry):
        for u in range(ROW_DMA_UNROLL):
            t = i * ROW_DMA_UNROLL + u
            for k in range(2):
                pltpu.make_async_copy(_slab(h_ref, t), _slab(xs_out, slot_ref[0, k, t]), sem).start(priority=k)
        return carry

    lax.fori_loop(0, tm // ROW_DMA_UNROLL, issue, 0)

    def wait(r, carry):
        pltpu.make_async_copy(_slab(h_ref, 0), _slab(xs_out, 0), sem).wait()
        return carry

    lax.fori_loop(0, 2 * tm, wait, 0)


def _dispatch_call(h2p, slots, n_rows, tm):
    xs0 = jnp.zeros((n_rows * ROW_SLABS, 128), h2p.dtype)
    return pl.pallas_call(
        _dispatch_body,
        grid=(slots.shape[0],),
        in_specs=[pl.BlockSpec((1, 8, tm), lambda i: (i, 0, 0), memory_space=pltpu.SMEM),
                  pl.BlockSpec((tm * ROW_SLABS, 128), lambda i: (i, 0)),
                  pl.BlockSpec(memory_space=pl.ANY)],
        out_specs=pl.BlockSpec(memory_space=pl.ANY),
        out_shape=jax.ShapeDtypeStruct(xs0.shape, xs0.dtype),
        input_output_aliases={2: 0},
        scratch_shapes=[pltpu.SemaphoreType.DMA(())],
        compiler_params=_cparams(("arbitrary",)),
        name="dispatch",
    )(slots, h2p, xs0)


def _experts_body(te_ref, nu_ref, xs_ref, wg_ref, wu_ref, wd_ref, y_ref):
    @pl.when(pl.program_id(0) >= nu_ref[0])
    def _():
        y_ref[...] = jnp.zeros_like(y_ref)

    @pl.when(pl.program_id(0) < nu_ref[0])
    def _():
        x = jnp.concatenate([c.astype(BF16) for c in _load_packed_rows(xs_ref)], axis=1)
        gt = _dot(x, wg_ref[0])
        a = gt * _sigmoid(gt) * _dot(x, wu_ref[0])
        _store_packed_rows(y_ref, _dot(a.astype(BF16), wd_ref[0]))


def _experts_call(tile_expert, n_used, xs, wg, wu, wd, tr):
    rows, w = xs.shape
    ne, d, de = wg.shape
    tr = tr * ROW_SLABS
    used = lambda j, te, nu: jnp.minimum(j, nu[0] - 1)
    return pl.pallas_call(
        _experts_body,
        grid_spec=pltpu.PrefetchScalarGridSpec(
            num_scalar_prefetch=2,
            grid=(rows // tr,),
            in_specs=[pl.BlockSpec((tr, w), lambda j, te, nu: (used(j, te, nu), 0)),
                      pl.BlockSpec((1, d, de), lambda j, te, nu: (te[used(j, te, nu)], 0, 0)),
                      pl.BlockSpec((1, d, de), lambda j, te, nu: (te[used(j, te, nu)], 0, 0)),
                      pl.BlockSpec((1, de, d), lambda j, te, nu: (te[used(j, te, nu)], 0, 0))],
            out_specs=pl.BlockSpec((tr, w), lambda j, te, nu: (j, 0))),
        out_shape=jax.ShapeDtypeStruct((rows, w), jnp.uint32),
        compiler_params=_cparams(("arbitrary",)),
        name="experts",
    )(tile_expert, n_used, xs, wg, wu, wd)


def _combine_body(slot_ref, y_hbm, x1_ref, route_ref, lng_ref, lnb_ref, mod_ref, o_ref,
                  b1_scr, b2_scr, sem, *, alpha):
    tm = x1_ref.shape[0]
    bufs = (b1_scr, b2_scr)

    def issue(i, carry):
        for u in range(ROW_DMA_UNROLL):
            t = i * ROW_DMA_UNROLL + u
            for k in range(2):
                pltpu.make_async_copy(_slab(y_hbm, slot_ref[0, k, t]), _slab(bufs[k], t), sem).start(priority=k)
        return carry

    lax.fori_loop(0, tm // ROW_DMA_UNROLL, issue, 0)

    def wait(r, carry):
        pltpu.make_async_copy(_slab(y_hbm, 0), _slab(b1_scr, 0), sem).wait()
        return carry

    lax.fori_loop(0, 2 * tm, wait, 0)

    w1 = route_ref[:, 0:1]
    w2 = route_ref[:, 1:2]
    moe = jnp.concatenate([w1 * a1 + w2 * a2 for a1, a2 in zip(_load_packed_rows(b1_scr), _load_packed_rows(b2_scr))],
                          axis=1)
    r = alpha * x1_ref[...] + mod_ref[0, 5:6, :] * moe
    o_ref[...] = _ln(r) * lng_ref[...] + lnb_ref[...]


def _combine_call(slots, y, x1, route, lng, lnb, mod, tm, modrow, alpha):
    na, d = x1.shape
    full = lambda a: pl.BlockSpec(a.shape, lambda i: (0,) * a.ndim)
    return pl.pallas_call(
        functools.partial(_combine_body, alpha=alpha),
        grid=(na // tm,),
        in_specs=[pl.BlockSpec((1, 8, tm), lambda i: (i, 0, 0), memory_space=pltpu.SMEM),
                  pl.BlockSpec(memory_space=pl.ANY),
                  pl.BlockSpec((tm, d), lambda i: (i, 0)),
                  pl.BlockSpec((tm, 128), lambda i: (i, 0)),
                  full(lng), full(lnb),
                  pl.BlockSpec((1, 6, d), lambda i: (modrow(i), 0, 0))],
        out_specs=pl.BlockSpec((tm, d), lambda i: (i, 0)),
        out_shape=jax.ShapeDtypeStruct((na, d), F32),
        scratch_shapes=[pltpu.VMEM((tm * ROW_SLABS, 128), jnp.uint32),
                        pltpu.VMEM((tm * ROW_SLABS, 128), jnp.uint32),
                        pltpu.SemaphoreType.DMA(())],
        compiler_params=_cparams(("arbitrary",)),
        name="combine",
    )(slots, y, x1, route, lng, lnb, mod)


def _routing_tables(counts, na, tr):
    cnt = counts[0, :N_EXPERTS].astype(jnp.int32)
    ntile = (cnt + tr - 1) // tr
    tile_start = jnp.cumsum(ntile) - ntile
    n_used = jnp.sum(ntile)
    n_tiles = -(-2 * na // tr) + N_EXPERTS
    seg = jnp.zeros((1, 128), F32).at[0, :N_EXPERTS].set((tile_start * tr).astype(F32))
    tile_expert = jnp.sum(jnp.arange(n_tiles, dtype=jnp.int32)[:, None] >= tile_start[None, :], axis=1) - 1
    return seg, n_tiles * tr, tile_expert.astype(jnp.int32), n_used.reshape(1).astype(jnp.int32)


def _lower_bounds(logits):
    p = jax.nn.softmax(logits.astype(F32), axis=0)
    return jnp.cumsum(p, axis=0) - p[:1]


def _row_tile(limit, *sizes):
    tm = limit
    while any(s % tm for s in sizes):
        tm //= 2
    return tm


def kernel(x, c, ctx, c_ctx, w_ada, b_ada, w_in, w_pool, pool_scale, lb_logits_fwd, lb_logits_bwd, hg_gain, rpb, w_br_a, w_br_b, w_br_c, w_out, ln1_g, ln1_b, w_rg, b_rg, w_re, b_re, w_gate, w_up, w_down, ln2_g, ln2_b):
    nbatch, t_len, d = x.shape
    c_len = ctx.shape[1]
    depth = w_ada.shape[0]
    assert c_len == SEQ_TILE and t_len % SEQ_TILE == 0 and t_len % GRID_W == 0
    alpha = (2.0 * depth) ** 0.25
    n_lat = nbatch * t_len
    nt = t_len // SEQ_TILE

    xa = jnp.concatenate([x.reshape(n_lat, d), ctx.reshape(nbatch * c_len, d)], axis=0)

    mod_rows = -(-(nbatch + 1) // 8) * 8
    cc = jnp.zeros((mod_rows, d), F32).at[:nbatch].set(c).at[nbatch].set(c_ctx)
    ada = _ada_call(cc, w_ada, b_ada)

    lb_f = _lower_bounds(lb_logits_fwd).reshape(depth, HG_HEADS, 1, HG_DK)
    lb_b = _lower_bounds(lb_logits_bwd).reshape(depth, HG_HEADS, 1, HG_DK)
    pool_consts = _pool_consts()
    hg_f = _hgrn_consts(False)
    hg_b = _hgrn_consts(True)

    tm_big = _row_tile(1024, t_len, nbatch * c_len)
    tm_mid = _row_tile(512, t_len, nbatch * c_len)

    def modrow_for(tm):
        return lambda i: jnp.where(i * tm < n_lat, (i * tm) // t_len, nbatch)

    for l in range(depth):
        last = l == depth - 1
        mod = ada[l].reshape(mod_rows, 6, d)
        z = _inproj_call(xa, mod, w_in[l].astype(BF16), tm_big, modrow_for(tm_big))
        ya = _pool_call(z, pool_consts, w_pool[l].astype(BF16), pool_scale[l].reshape(1, -1),
                        nt, nbatch * nt)
        o_f = _hgrn_call(z, lb_f[l], hg_f, nbatch, nt, reverse=False)
        yb = _hgrn_call(z, lb_b[l], hg_b, nbatch, nt, reverse=True, o_fwd=o_f,
                        gain=hg_gain[l].reshape(HG_HEADS, 1, HG_DK))
        bias = _na_bias_table(rpb[l], t_len // GRID_W)
        yc = _na_call(z, bias, nbatch, t_len, c_len, with_ctx=not last)
        wr = jnp.zeros((d, 128), F32).at[:, :N_EXPERTS].set(w_re[l]).at[:, N_EXPERTS:N_EXPERTS + N_GROUPS].set(w_rg[l])
        br = jnp.zeros((1, 128), F32).at[0, :N_EXPERTS].set(b_re[l]).at[0, N_EXPERTS:N_EXPERTS + N_GROUPS].set(b_rg[l])
        x1, h2p, route, counts = _merge_call(
            xa, ya, yb, yc, z, mod, w_br_a[l].astype(BF16), w_br_b[l].astype(BF16), w_br_c[l].astype(BF16),
            w_out[l].astype(BF16), ln1_g[l].reshape(1, d), ln1_b[l].reshape(1, d), wr, br, tm_mid,
            modrow_for(tm_mid), alpha)
        seg, n_rows, tile_expert, n_used = _routing_tables(counts, route.shape[0], EXPERT_TILE)
        slots = _slots_call(route, seg, tm_mid)
        xs = _dispatch_call(h2p, slots, n_rows, tm_mid)
        ys = _experts_call(tile_expert, n_used, xs, w_gate[l].astype(BF16), w_up[l].astype(BF16),
                           w_down[l].astype(BF16), EXPERT_TILE)
        xa = _combine_call(slots, ys, x1, route, ln2_g[l].reshape(1, d), ln2_b[l].reshape(1, d), mod,
                           tm_mid, modrow_for(tm_mid), alpha)
    return xa.reshape(nbatch, t_len, d)
```

```python
import functools

import numpy as np
import jax
import jax.numpy as jnp
from jax import lax
from jax.experimental import pallas as pl
from jax.experimental.pallas import tpu as pltpu

F32 = jnp.float32
BF16 = jnp.bfloat16
HIGHEST = lax.Precision.HIGHEST

GRID_W = 64
POOL_WINDOWS = (2, 4, 8, 16)
POOL_GDIM = 128
HG_HEADS = 4
HG_DK = 128
HG_BLOCK = 16
NA_HEADS = 8
NA_HD = 64
NA_ROWS = 8
NA_COLS = 16
NA_QROWS = 4
NA_KROWS = 12
N_GROUPS = 4
EXP_PER_GROUP = 8
N_EXPERTS = N_GROUPS * EXP_PER_GROUP
LN_EPS = 1e-5
RMS_EPS = 1e-6
NEG_BIG = -1e30
SEQ_TILE = 256
EXPERT_TILE = 512
VMEM_LIMIT = 56 * 1024 * 1024

CB_A, CB_Q, CB_FF, CB_FB, CB_I, CB_G, CB_NQ, CB_NK, CB_NV, CB_GATE = 0, 4, 8, 12, 16, 20, 24, 28, 32, 36


def _cparams(sem):
    return pltpu.CompilerParams(dimension_semantics=sem, vmem_limit_bytes=VMEM_LIMIT)


def _ln(x):
    mu = jnp.mean(x, axis=-1, keepdims=True)
    xc = x - mu
    var = jnp.mean(xc * xc, axis=-1, keepdims=True)
    return xc * lax.rsqrt(var + LN_EPS)


def _sigmoid(x):
    return 1.0 / (1.0 + jnp.exp(-x))


def _sigmoid_t(x):
    return 0.5 * jnp.tanh(0.5 * x) + 0.5


def _dot(a, b):
    return jnp.dot(a, b, preferred_element_type=F32)


def _dot_nt(a, b):
    return lax.dot_general(a, b, (((1,), (1,)), ((), ())), preferred_element_type=F32)


def _dot_tn(a, b):
    return lax.dot_general(a, b, (((0,), (0,)), ((), ())), preferred_element_type=F32)


def _dot01(m01, x, pieces=3):
    x1 = x.astype(BF16)
    r1 = x - x1.astype(F32)
    x2 = r1.astype(BF16)
    out = _dot(m01, x1) + _dot(m01, x2)
    if pieces == 3:
        out = out + _dot(m01, (r1 - x2.astype(F32)).astype(BF16))
    return out


def _ada_body(c_ref, w_ref, b_ref, o_ref):
    cs = c_ref[...]
    s = cs * _sigmoid(cs)
    o_ref[0] = jnp.dot(s, w_ref[0], preferred_element_type=F32, precision=HIGHEST) + b_ref[0]


def _ada_call(cc, w_ada, b_ada):
    depth, d, n6 = w_ada.shape
    rows = cc.shape[0]
    return pl.pallas_call(
        _ada_body,
        grid=(depth, n6 // d),
        in_specs=[pl.BlockSpec((rows, d), lambda l, j: (0, 0)),
                  pl.BlockSpec((1, d, d), lambda l, j: (l, 0, j)),
                  pl.BlockSpec((1, 1, d), lambda l, j: (l, 0, j))],
        out_specs=pl.BlockSpec((1, rows, d), lambda l, j: (l, 0, j)),
        out_shape=jax.ShapeDtypeStruct((depth, rows, n6), F32),
        compiler_params=_cparams(("parallel", "parallel")),
        name="ada",
    )(cc, w_ada, b_ada.reshape(depth, 1, n6))


def _inproj_body(x_ref, mod_ref, w_ref, z_ref, h_scr):
    @pl.when(pl.program_id(1) == 0)
    def _():
        h = _ln(x_ref[...]) * (1.0 + mod_ref[0, 1:2, :]) + mod_ref[0, 0:1, :]
        h_scr[...] = h.astype(BF16)

    z_ref[...] = _dot(h_scr[...], w_ref[...]).astype(z_ref.dtype)


def _inproj_call(xa, mod, w_in, tm, modrow):
    na, d = xa.shape
    d_in = w_in.shape[1]
    tn = 1536
    assert d_in % tn == 0
    return pl.pallas_call(
        _inproj_body,
        grid=(na // tm, d_in // tn),
        in_specs=[pl.BlockSpec((tm, d), lambda i, j: (i, 0)),
                  pl.BlockSpec((1, 6, d), lambda i, j: (modrow(i), 0, 0)),
                  pl.BlockSpec((d, tn), lambda i, j: (0, j))],
        out_specs=pl.BlockSpec((tm, tn), lambda i, j: (i, j)),
        out_shape=jax.ShapeDtypeStruct((na, d_in), BF16),
        scratch_shapes=[pltpu.VMEM((tm, d), BF16)],
        compiler_params=_cparams(("parallel", "arbitrary")),
        name="inproj",
    )(xa, mod, w_in)


def _pool_consts():
    n = SEQ_TILE
    t = np.arange(n)[:, None]
    bc = np.zeros((4, n, n), np.float32)
    bp = np.zeros((4, n, 16), np.float32)
    bn = np.zeros((4, n, 16), np.float32)
    for g, win in enumerate(POOL_WINDOWS):
        lo, hi = t - win // 2, t + win // 2 - 1
        s = np.arange(n)[None, :]
        bc[g] = (s >= lo) & (s <= hi)
        s = np.arange(16)[None, :] - 16
        bp[g] = (s >= lo) & (s <= hi)
        s = np.arange(16)[None, :] + n
        bn[g] = (s >= lo) & (s <= hi)
    cnt = np.stack([bc.sum(-1), bp.sum(-1), bn.sum(-1)], axis=1)
    cnt = np.broadcast_to(cnt[..., None], (4, 3, n, 128)).astype(np.float32)
    return (jnp.asarray(bc, BF16), jnp.asarray(bp, BF16), jnp.asarray(bn, BF16), jnp.asarray(cnt))


def _pool_body(prev_ref, cur_ref, next_ref, bc_ref, bp_ref, bn_ref, cnt_ref, wp_ref, ps_ref, o_ref,
               *, nt, n_lat_tiles):
    i = pl.program_id(0)
    k = i % nt
    is_lat = i < n_lat_tiles
    has_prev = jnp.where(jnp.logical_and(is_lat, k != 0), 1.0, 0.0).astype(F32)
    has_next = jnp.where(jnp.logical_and(is_lat, k != nt - 1), 1.0, 0.0).astype(F32)
    for g in range(len(POOL_WINDOWS)):
        sl = slice(g * POOL_GDIM, (g + 1) * POOL_GDIM)
        u = cur_ref[:, sl]
        ssum = (_dot(bc_ref[g], u) + has_prev * _dot(bp_ref[g], prev_ref[:, sl])
                + has_next * _dot(bn_ref[g], next_ref[:, sl]))
        cnt = cnt_ref[g, 0] + has_prev * cnt_ref[g, 1] + has_next * cnt_ref[g, 2]
        dlt = ssum / cnt - u.astype(F32)
        y = _dot(dlt.astype(BF16), wp_ref[g]) * ps_ref[:, sl]
        o_ref[:, sl] = y.astype(o_ref.dtype)


def _pool_call(z, consts, w_pool, pool_scale, nt, n_lat_tiles):
    na = z.shape[0]
    n = SEQ_TILE
    hb = n // 16
    last16 = na // 16 - 1
    bc, bp, bn, cnt = consts
    full = lambda a: pl.BlockSpec(a.shape, lambda i: (0,) * a.ndim)
    return pl.pallas_call(
        functools.partial(_pool_body, nt=nt, n_lat_tiles=n_lat_tiles),
        grid=(na // n,),
        in_specs=[pl.BlockSpec((16, 512), lambda i: (jnp.maximum(i * hb - 1, 0), CB_A // 4)),
                  pl.BlockSpec((n, 512), lambda i: (i, CB_A // 4)),
                  pl.BlockSpec((16, 512), lambda i: (jnp.minimum((i + 1) * hb, last16), CB_A // 4)),
                  full(bc), full(bp), full(bn), full(cnt), full(w_pool), full(pool_scale)],
        out_specs=pl.BlockSpec((n, 512), lambda i: (i, 0)),
        out_shape=jax.ShapeDtypeStruct((na, 512), BF16),
        compiler_params=_cparams(("parallel",)),
        name="pool",
    )(z, z, z, bc, bp, bn, cnt, w_pool, pool_scale)


def _hgrn_consts(reverse):
    n, bs = SEQ_TILE, HG_BLOCK
    nb = n // bs
    t = np.arange(n)
    o = (n - 1 - t) if reverse else t
    blk = t // bs
    jb = np.arange(nb)
    ob = (nb - 1 - jb) if reverse else jb
    cum = ((blk[:, None] == blk[None, :]) & (o[None, :] <= o[:, None])).astype(np.float32)
    bsum = (jb[:, None] == blk[None, :]).astype(np.float32)
    widths = [2 ** l for l in range(1, int(np.log2(nb)) + 1)]
    lvl = np.full((n, n), -1, np.int32)
    obt = ob[blk]
    same = blk[:, None] == blk[None, :]
    lvl[same & (o[None, :] <= o[:, None])] = 0
    for li, w in reversed(list(enumerate(widths, start=1))):
        m = (obt[:, None] // w == obt[None, :] // w) & (obt[None, :] < obt[:, None]) & ~same
        lvl[m] = li
    mats = []
    for w in widths:
        mid = (ob // w) * w + w // 2
        mats.append((mid[:, None] <= ob[None, :]) & (ob[None, :] < ob[:, None]))
    for w in widths:
        mid = (ob // w) * w + w // 2
        mats.append((ob[:, None] < ob[None, :]) & (ob[None, :] < mid[:, None]))
    mats.append(ob[None, :] < ob[:, None])
    mats.append(ob[None, :] > ob[:, None])
    mats.append(np.ones((nb, nb), bool))
    tsm = np.concatenate(mats, axis=0).astype(np.float32)
    return (jnp.asarray(cum, BF16), jnp.asarray(bsum, BF16), jnp.asarray(lvl), jnp.asarray(tsm, BF16),
            len(widths))


def _expand_blocks(c, n):
    nb, lanes = c.shape
    return jnp.broadcast_to(c[:, None, :], (nb, n // nb, lanes)).reshape(n, lanes)


def _hgrn_body(*refs, n_levels, final):
    if final:
        (zq_ref, zf_ref, zi_ref, lb_ref, cum_ref, bsum_ref, lvl_ref, tsm_ref,
         of_ref, zg_ref, gain_ref, o_ref, st_scr) = refs
    else:
        zq_ref, zf_ref, zi_ref, lb_ref, cum_ref, bsum_ref, lvl_ref, tsm_ref, o_ref, st_scr = refs
    n = zq_ref.shape[0]
    nb = n // HG_BLOCK

    @pl.when(pl.program_id(1) == 0)
    def _():
        st_scr[...] = jnp.zeros_like(st_scr)

    for h in range(HG_HEADS):
        sl = slice(h * HG_DK, (h + 1) * HG_DK)
        zq = zq_ref[:, sl].astype(F32)
        zf = zf_ref[:, sl].astype(F32)
        lb = lb_ref[h]
        lf = jnp.log(lb + (1.0 - lb) * _sigmoid(zf))
        k = (1.0 - lb) * _sigmoid(-zf)
        q = zq * _sigmoid_t(zq)
        v = zi_ref[:, sl]

        b = _dot01(cum_ref[...], lf, pieces=2)
        tot = _dot01(bsum_ref[...], lf, pieces=2)
        coef = _dot01(tsm_ref[...], tot)
        crow = lambda idx: coef[idx * nb:(idx + 1) * nb]

        qd = q * jnp.exp(b)
        kd = (k * jnp.exp(-b)).astype(BF16)
        ks = k * jnp.exp(_expand_blocks(tot, n) - b)

        lvl = lvl_ref[...]
        a = jnp.where(lvl == 0, _dot_nt(qd.astype(BF16), kd), 0.0)
        for li in range(1, n_levels + 1):
            qw = (qd * _expand_blocks(jnp.exp(crow(li - 1)), n)).astype(BF16)
            kw = (ks * _expand_blocks(jnp.exp(crow(n_levels + li - 1)), n)).astype(BF16)
            a = jnp.where(lvl == li, _dot_nt(qw, kw), a)
        o = _dot(a.astype(BF16), v)

        st = st_scr[h]
        qs = (qd * _expand_blocks(jnp.exp(crow(2 * n_levels)), n)).astype(BF16)
        o = o + _dot_nt(qs, st.astype(BF16))
        kn = (ks * _expand_blocks(jnp.exp(crow(2 * n_levels + 1)), n)).astype(BF16)
        dec = jnp.exp(coef[(2 * n_levels + 2) * nb:(2 * n_levels + 2) * nb + 1])
        st_scr[h] = st * dec + _dot_tn(v, kn)

        if final:
            o = o + of_ref[:, sl]
            o = o * lax.rsqrt(jnp.mean(o * o, axis=-1, keepdims=True) + RMS_EPS) * gain_ref[h]
            zg = zg_ref[:, sl].astype(F32)
            o_ref[:, sl] = (o * (zg * _sigmoid_t(zg))).astype(o_ref.dtype)
        else:
            o_ref[:, sl] = o


def _hgrn_call(z, lb, consts, nbatch, nt, reverse, o_fwd=None, gain=None):
    na = z.shape[0]
    n = SEQ_TILE
    cum, bsum, lvl, tsm, n_levels = consts
    ctx_base = nbatch * nt
    final = o_fwd is not None

    def tile(b, s):
        lat = (b * nt + nt - s) if reverse else (b * nt + s - 1)
        return jnp.where(s == 0, ctx_base + b, lat)

    width = HG_HEADS * HG_DK

    def col(cb):
        return pl.BlockSpec((n, width), lambda b, s: (tile(b, s), cb // HG_HEADS))

    full = lambda a: pl.BlockSpec(a.shape, lambda b, s: (0,) * a.ndim)
    in_specs = [col(CB_Q), col(CB_FB if reverse else CB_FF), col(CB_I), full(lb),
                full(cum), full(bsum), full(lvl), full(tsm)]
    args = [z, z, z, lb, cum, bsum, lvl, tsm]
    if final:
        in_specs += [col(0), col(CB_G), full(gain)]
        args += [o_fwd, z, gain]
    return pl.pallas_call(
        functools.partial(_hgrn_body, n_levels=n_levels, final=final),
        grid=(nbatch, nt + 1),
        in_specs=in_specs,
        out_specs=col(0),
        out_shape=jax.ShapeDtypeStruct((na, width), BF16 if final else F32),
        scratch_shapes=[pltpu.VMEM((HG_HEADS, HG_DK, HG_DK), F32)],
        compiler_params=_cparams(("parallel", "arbitrary")),
        name="hgrn_bwd" if final else "hgrn_fwd",
    )(*args)


def _na_bias_table(rpb, rows):
    nrb = rows // NA_QROWS
    assert nrb >= 3 and rows >= NA_KROWS
    kr = min(NA_ROWS, rows)
    qc = np.arange(GRID_W)
    c0 = np.clip(qc - NA_COLS // 2, 0, GRID_W - NA_COLS)
    kc = np.arange(GRID_W)
    col_ok = (kc[None, :] >= c0[:, None]) & (kc[None, :] < c0[:, None] + NA_COLS)
    dc = np.clip(kc[None, :] - qc[:, None] + NA_COLS - 1, 0, 2 * NA_COLS - 2)
    sel_c = (dc[..., None] == np.arange(2 * NA_COLS - 1)).astype(np.float32)
    sel_r, oks = [], []
    for rb in (0, 1, nrb - 1):
        start = int(np.clip(NA_QROWS * rb - 4, 0, rows - NA_KROWS))
        r = NA_QROWS * rb + np.arange(NA_QROWS)
        r0 = np.clip(r - kr // 2, 0, rows - kr)
        keyrow = start + np.arange(NA_KROWS)
        row_ok = (keyrow[None, :] >= r0[:, None]) & (keyrow[None, :] < r0[:, None] + kr)
        dr = np.clip(keyrow[None, :] - r[:, None] + NA_ROWS - 1, 0, 2 * NA_ROWS - 2)
        sel_r.append((dr[..., None] == np.arange(2 * NA_ROWS - 1)).astype(np.float32))
        oks.append(row_ok[:, None, :, None] & col_ok[None, :, None, :])
    bias = jnp.einsum("hij,paki,cdj->phackd", rpb.astype(F32), jnp.asarray(np.stack(sel_r)),
                      jnp.asarray(sel_c), precision=HIGHEST)
    bias = jnp.where(jnp.asarray(np.stack(oks))[:, None], bias, NEG_BIG)
    bias = bias.reshape(3, NA_HEADS, NA_QROWS * GRID_W, NA_KROWS * GRID_W)
    return jnp.concatenate([bias, jnp.full_like(bias[:1], NEG_BIG)], axis=0)


def _na_body(q_ref, k_ref, v_ref, kc_ref, vc_ref, bias_ref, o_ref, *, rows):
    rb = pl.program_id(1)
    nk = NA_KROWS * GRID_W
    start_row = jnp.clip(NA_QROWS * rb - 4, 0, rows - NA_KROWS)
    start = pl.multiple_of(start_row * GRID_W, GRID_W)
    nq = q_ref.shape[0]
    lane = lax.broadcasted_iota(jnp.int32, (nq, 128), 1)
    scale = NA_HD ** -0.5
    for p in range(NA_HEADS // 2):
        sl = slice(128 * p, 128 * (p + 1))
        qp = q_ref[:, sl] * scale
        kp = k_ref[pl.ds(start, nk), sl]
        vp = v_ref[pl.ds(start, nk), sl]
        kcp = kc_ref[:, sl]
        vcp = vc_ref[:, sl]
        outs = []
        for hh in range(2):
            sel = (lane < NA_HD) if hh == 0 else (lane >= NA_HD)
            qh = jnp.where(sel, qp, jnp.zeros_like(qp))
            s_loc = _dot_nt(qh, kp) + bias_ref[0, 2 * p + hh]
            s_ctx = _dot_nt(qh, kcp)
            m = jnp.maximum(jnp.max(s_loc, axis=-1, keepdims=True), jnp.max(s_ctx, axis=-1, keepdims=True))
            p_loc = jnp.exp(s_loc - m)
            p_ctx = jnp.exp(s_ctx - m)
            den = jnp.sum(p_loc, axis=-1, keepdims=True) + jnp.sum(p_ctx, axis=-1, keepdims=True)
            o = _dot(p_loc.astype(BF16), vp) + _dot(p_ctx.astype(BF16), vcp)
            outs.append(o / den)
        o_ref[:, sl] = jnp.where(lane < NA_HD, outs[0], outs[1]).astype(o_ref.dtype)


def _na_call(z, bias, nbatch, t_len, c_len, with_ctx):
    na = z.shape[0]
    rows = t_len // GRID_W
    nrb = rows // NA_QROWS
    nq = NA_QROWS * GRID_W
    assert nq == c_len
    ctx_base = nbatch * nrb
    steps = nrb + 1 if with_ctx else nrb

    def qtile(b, r):
        return jnp.where(r < nrb, b * nrb + r, ctx_base + b)

    def pattern(b, r):
        return jnp.where(r == 0, 0, jnp.where(r == nrb - 1, 2, jnp.where(r == nrb, 3, 1)))

    return pl.pallas_call(
        functools.partial(_na_body, rows=rows),
        grid=(nbatch, steps),
        in_specs=[pl.BlockSpec((nq, 512), lambda b, r: (qtile(b, r), CB_NQ // 4)),
                  pl.BlockSpec((t_len, 512), lambda b, r: (b, CB_NK // 4)),
                  pl.BlockSpec((t_len, 512), lambda b, r: (b, CB_NV // 4)),
                  pl.BlockSpec((c_len, 512), lambda b, r: (ctx_base + b, CB_NK // 4)),
                  pl.BlockSpec((c_len, 512), lambda b, r: (ctx_base + b, CB_NV // 4)),
                  pl.BlockSpec((1,) + bias.shape[1:], lambda b, r: (pattern(b, r), 0, 0, 0))],
        out_specs=pl.BlockSpec((nq, 512), lambda b, r: (qtile(b, r), 0)),
        out_shape=jax.ShapeDtypeStruct((na if with_ctx else nbatch * t_len, 512), BF16),
        compiler_params=_cparams(("parallel", "arbitrary")),
        name="natten",
    )(z, z, z, z, z, bias)


ROW_SLABS = 4


def _store_packed_rows(ref, x):
    m = x.shape[0]

    def bits(v):
        u = lax.bitcast_convert_type(v, jnp.uint32)
        return u + jnp.uint32(0x7FFF) + ((u >> 16) & jnp.uint32(1))

    for s in range(ROW_SLABS):
        lo = x[:, 128 * s:128 * (s + 1)]
        hi = x[:, 512 + 128 * s:512 + 128 * (s + 1)]
        ref[pl.ds(s, m, stride=ROW_SLABS), :] = (bits(hi) & jnp.uint32(0xFFFF0000)) | (bits(lo) >> 16)


def _load_packed_rows(ref):
    m = ref.shape[0] // ROW_SLABS
    los, his = [], []
    for s in range(ROW_SLABS):
        p = ref[pl.ds(s, m, stride=ROW_SLABS), :]
        los.append(lax.bitcast_convert_type(p << 16, F32))
        his.append(lax.bitcast_convert_type(p & jnp.uint32(0xFFFF0000), F32))
    return los + his


def _merge_body(x_ref, ya_ref, yb_ref, yc_ref, g0, g1, g2, g3, g4, g5, wa_ref, wb_ref, wc_ref, wo_ref,
                lng_ref, lnb_ref, mod_ref, wrh_ref, wrl_ref, br_ref, tril_ref, x1_ref, h2_ref, route_ref, cnt_ref,
                cnt_scr, *, alpha):
    @pl.when(pl.program_id(0) == 0)
    def _():
        cnt_scr[...] = jnp.zeros_like(cnt_scr)

    gates = ((g0, g1), (g2, g3), (g4, g5))
    ys = (ya_ref[...], yb_ref[...], yc_ref[...])
    ws = (wa_ref, wb_ref, wc_ref)
    half = wa_ref.shape[1] // 2
    mix = None
    for n in range(2):
        m = None
        for kbr in range(3):
            pr = _dot(ys[kbr], ws[kbr][:, n * half:(n + 1) * half])
            term = _sigmoid_t(gates[kbr][n][...].astype(F32)) * pr
            m = term if m is None else m + term
        part = _dot(m.astype(BF16), wo_ref[n * half:(n + 1) * half, :])
        mix = part if mix is None else mix + part
    r = alpha * x_ref[...] + mod_ref[0, 2:3, :] * mix
    x1 = _ln(r) * lng_ref[...] + lnb_ref[...]
    x1_ref[...] = x1
    h2 = _ln(x1) * (1.0 + mod_ref[0, 4:5, :]) + mod_ref[0, 3:4, :]
    _store_packed_rows(h2_ref, h2)

    h2_hi = h2.astype(BF16)
    h2_lo = (h2 - h2_hi.astype(F32)).astype(BF16)
    logits = (_dot(h2_hi, wrh_ref[...]) + _dot(h2_lo, wrh_ref[...]) + _dot(h2_hi, wrl_ref[...])) + br_ref[...]
    lane = lax.broadcasted_iota(jnp.int32, logits.shape, 1).astype(F32)
    is_grp = jnp.where(lane >= N_EXPERTS, jnp.where(lane < N_EXPERTS + N_GROUPS, 1.0, 0.0), 0.0) > 0.5
    lgm = jnp.where(is_grp, logits, NEG_BIG)
    mg = jnp.max(lgm, axis=-1, keepdims=True)
    p_grp = 1.0 / jnp.sum(jnp.exp(lgm - mg), axis=-1, keepdims=True)
    grp = jnp.min(jnp.where(lgm == mg, lane, 1e9), axis=-1, keepdims=True) - N_EXPERTS
    lo = grp * EXP_PER_GROUP
    in_grp = jnp.where(lane >= lo, jnp.where(lane < lo + EXP_PER_GROUP, 1.0, 0.0), 0.0) > 0.5
    lem = jnp.where(in_grp, logits, NEG_BIG)
    m1 = jnp.max(lem, axis=-1, keepdims=True)
    id1 = jnp.min(jnp.where(lem == m1, lane, 1e9), axis=-1, keepdims=True)
    lem2 = jnp.where(lane == id1, NEG_BIG, lem)
    m2 = jnp.max(lem2, axis=-1, keepdims=True)
    id2 = jnp.min(jnp.where(lem2 == m2, lane, 1e9), axis=-1, keepdims=True)
    u2 = jnp.exp(m2 - m1)
    w1 = p_grp / (1.0 + u2)
    w2 = p_grp * u2 / (1.0 + u2)
    oh1 = jnp.where(lane == id1, 1.0, 0.0)
    oh2 = jnp.where(lane == id2, 1.0, 0.0)
    oh = oh1 + oh2
    before = _dot(tril_ref[...], oh.astype(BF16)) + cnt_scr[...]
    rank1 = jnp.sum(before * oh1, axis=-1, keepdims=True)
    rank2 = jnp.sum(before * oh2, axis=-1, keepdims=True)
    cnt_scr[...] += jnp.sum(oh, axis=0, keepdims=True)
    cnt_ref[...] = jnp.broadcast_to(cnt_scr[...], cnt_ref.shape)
    route = jnp.zeros_like(logits)
    for ln, val in enumerate((w1, w2, id1, id2, rank1, rank2)):
        route = jnp.where(lane == ln, val, route)
    route_ref[...] = route


def _merge_call(xa, ya, yb, yc, z, mod, wa, wb, wc, wo, lng, lnb, wr, br, tm, modrow, alpha):
    na, d = yc.shape[0], xa.shape[1]
    row = lambda w: pl.BlockSpec((tm, w), lambda i: (i, 0))
    gate = lambda cb: pl.BlockSpec((tm, 512), lambda i: (i, cb))
    full = lambda a: pl.BlockSpec(a.shape, lambda i: (0,) * a.ndim)
    g0 = CB_GATE // 4
    tril = jnp.asarray(np.tril(np.ones((tm, tm), np.float32), -1), BF16)
    wr_hi = lax.reduce_precision(wr, exponent_bits=8, mantissa_bits=7)
    wr_lo = (wr - wr_hi).astype(BF16)
    wr_hi = wr_hi.astype(BF16)
    return pl.pallas_call(
        functools.partial(_merge_body, alpha=alpha),
        grid=(na // tm,),
        in_specs=[row(d), row(512), row(512), row(512)] + [gate(g0 + j) for j in range(6)]
                 + [full(wa), full(wb), full(wc), full(wo), full(lng), full(lnb),
                    pl.BlockSpec((1, 6, d), lambda i: (modrow(i), 0, 0)), full(wr_hi), full(wr_lo), full(br),
                    full(tril)],
        out_specs=[row(d), pl.BlockSpec((tm * ROW_SLABS, 128), lambda i: (i, 0)), row(128),
                   pl.BlockSpec((8, 128), lambda i: (0, 0))],
        out_shape=[jax.ShapeDtypeStruct((na, d), F32), jax.ShapeDtypeStruct((na * ROW_SLABS, 128), jnp.uint32),
                   jax.ShapeDtypeStruct((na, 128), F32), jax.ShapeDtypeStruct((8, 128), F32)],
        scratch_shapes=[pltpu.VMEM((1, 128), F32)],
        compiler_params=_cparams(("arbitrary",)),
        name="merge",
    )(xa, ya, yb, yc, z, z, z, z, z, z, wa, wb, wc, wo, lng, lnb, mod, wr_hi, wr_lo, br, tril)


ROW_DMA_UNROLL = 8


def _slab(ref, row):
    start = row * ROW_SLABS
    if not isinstance(row, int):
        start = pl.multiple_of(start, ROW_SLABS)
    return ref.at[pl.ds(start, ROW_SLABS), :]


def _slots_body(route_ref, seg_ref, o_ref):
    route = route_ref[...]
    lane = lax.broadcasted_iota(jnp.int32, route.shape, 1).astype(F32)
    seg = seg_ref[...]
    slots = []
    for k in range(2):
        eid = route[:, 2 + k:3 + k]
        start = jnp.sum(jnp.where(lane == eid, seg, 0.0), axis=-1, keepdims=True)
        slots.append(start + route[:, 4 + k:5 + k])
    both = jnp.where(lane == 0.0, slots[0], jnp.where(lane == 1.0, slots[1], 0.0))
    o_ref[0] = both.T[:8].astype(jnp.int32)


def _slots_call(route, seg, tm):
    na = route.shape[0]
    return pl.pallas_call(
        _slots_body,
        grid=(na // tm,),
        in_specs=[pl.BlockSpec((tm, 128), lambda i: (i, 0)), pl.BlockSpec((1, 128), lambda i: (0, 0))],
        out_specs=pl.BlockSpec((1, 8, tm), lambda i: (i, 0, 0)),
        out_shape=jax.ShapeDtypeStruct((na // tm, 8, tm), jnp.int32),
        compiler_params=_cparams(("parallel",)),
        name="slots",
    )(route, seg)


def _dispatch_body(slot_ref, h_ref, xs_in, xs_out, sem):
    del xs_in
    tm = h_ref.shape[0] // ROW_SLABS

    def issue(i, carry):
        for u in range(ROW_DMA_UNROLL):
            t = i * ROW_DMA_UNROLL + u
            for k in range(2):
                pltpu.make_async_copy(_slab(h_ref, t), _slab(xs_out, slot_ref[0, k, t]), sem).start(priority=k)
        return carry

    lax.fori_loop(0, tm // ROW_DMA_UNROLL, issue, 0)

    for k in range(2):
        pltpu.make_async_copy(h_ref, xs_out.at[pl.ds(0, tm * ROW_SLABS), :], sem).wait()


def _dispatch_call(h2p, slots, n_rows, tm):
    xs0 = jnp.zeros((n_rows * ROW_SLABS, 128), h2p.dtype)
    return pl.pallas_call(
        _dispatch_body,
        grid=(slots.shape[0],),
        in_specs=[pl.BlockSpec((1, 8, tm), lambda i: (i, 0, 0), memory_space=pltpu.SMEM),
                  pl.BlockSpec((tm * ROW_SLABS, 128), lambda i: (i, 0)),
                  pl.BlockSpec(memory_space=pl.ANY)],
        out_specs=pl.BlockSpec(memory_space=pl.ANY),
        out_shape=jax.ShapeDtypeStruct(xs0.shape, xs0.dtype),
        input_output_aliases={2: 0},
        scratch_shapes=[pltpu.SemaphoreType.DMA(())],
        compiler_params=_cparams(("arbitrary",)),
        name="dispatch",
    )(slots, h2p, xs0)


def _experts_body(te_ref, nu_ref, xs_ref, wg_ref, wu_ref, wd_ref, y_ref, wg_scr, wu_scr, wd_scr):
    j = pl.program_id(0)

    @pl.when(j >= nu_ref[0])
    def _():
        y_ref[...] = jnp.zeros_like(y_ref)

    @pl.when(jnp.logical_and(j < nu_ref[0], jnp.logical_or(j == 0, te_ref[j] != te_ref[jnp.maximum(j - 1, 0)])))
    def _():
        wg_scr[...] = wg_ref[0].astype(BF16)
        wu_scr[...] = wu_ref[0].astype(BF16)
        wd_scr[...] = wd_ref[0].astype(BF16)

    @pl.when(j < nu_ref[0])
    def _():
        x = jnp.concatenate([c.astype(BF16) for c in _load_packed_rows(xs_ref)], axis=1)
        gt = _dot(x, wg_scr[...])
        a = gt * _sigmoid_t(gt) * _dot(x, wu_scr[...])
        _store_packed_rows(y_ref, _dot(a.astype(BF16), wd_scr[...]))


def _experts_call(tile_expert, n_used, xs, wg, wu, wd, layer, tr):
    rows, w = xs.shape
    _, ne, d, de = wg.shape
    tr = tr * ROW_SLABS
    used = lambda j, te, nu: jnp.minimum(j, nu[0] - 1)
    return pl.pallas_call(
        _experts_body,
        grid_spec=pltpu.PrefetchScalarGridSpec(
            num_scalar_prefetch=2,
            grid=(rows // tr,),
            in_specs=[pl.BlockSpec((tr, w), lambda j, te, nu: (used(j, te, nu), 0)),
                      pl.BlockSpec((None, 1, d, de), lambda j, te, nu: (layer, te[used(j, te, nu)], 0, 0)),
                      pl.BlockSpec((None, 1, d, de), lambda j, te, nu: (layer, te[used(j, te, nu)], 0, 0)),
                      pl.BlockSpec((None, 1, de, d), lambda j, te, nu: (layer, te[used(j, te, nu)], 0, 0))],
            out_specs=pl.BlockSpec((tr, w), lambda j, te, nu: (j, 0)),
            scratch_shapes=[pltpu.VMEM((d, de), BF16), pltpu.VMEM((d, de), BF16), pltpu.VMEM((de, d), BF16)]),
        out_shape=jax.ShapeDtypeStruct((rows, w), jnp.uint32),
        compiler_params=_cparams(("arbitrary",)),
        name="experts",
    )(tile_expert, n_used, xs, wg, wu, wd)


def _combine_body(slot_ref, y_hbm, x1_ref, route_ref, lng_ref, lnb_ref, mod_ref, o_ref,
                  b1_scr, b2_scr, sem, *, alpha):
    tm = x1_ref.shape[0]
    bufs = (b1_scr, b2_scr)

    def issue(i, carry):
        for u in range(ROW_DMA_UNROLL):
            t = i * ROW_DMA_UNROLL + u
            for k in range(2):
                pltpu.make_async_copy(_slab(y_hbm, slot_ref[0, k, t]), _slab(bufs[k], t), sem).start(priority=k)
        return carry

    lax.fori_loop(0, tm // ROW_DMA_UNROLL, issue, 0)

    for k in range(2):
        pltpu.make_async_copy(y_hbm.at[pl.ds(0, tm * ROW_SLABS), :], bufs[k], sem).wait()

    w1 = route_ref[:, 0:1]
    w2 = route_ref[:, 1:2]
    moe = jnp.concatenate([w1 * a1 + w2 * a2 for a1, a2 in zip(_load_packed_rows(b1_scr), _load_packed_rows(b2_scr))],
                          axis=1)
    r = alpha * x1_ref[...] + mod_ref[0, 5:6, :] * moe
    o_ref[...] = _ln(r) * lng_ref[...] + lnb_ref[...]


def _combine_call(slots, y, x1, route, lng, lnb, mod, tm, modrow, alpha):
    na, d = x1.shape
    full = lambda a: pl.BlockSpec(a.shape, lambda i: (0,) * a.ndim)
    return pl.pallas_call(
        functools.partial(_combine_body, alpha=alpha),
        grid=(na // tm,),
        in_specs=[pl.BlockSpec((1, 8, tm), lambda i: (i, 0, 0), memory_space=pltpu.SMEM),
                  pl.BlockSpec(memory_space=pl.ANY),
                  pl.BlockSpec((tm, d), lambda i: (i, 0)),
                  pl.BlockSpec((tm, 128), lambda i: (i, 0)),
                  full(lng), full(lnb),
                  pl.BlockSpec((1, 6, d), lambda i: (modrow(i), 0, 0))],
        out_specs=pl.BlockSpec((tm, d), lambda i: (i, 0)),
        out_shape=jax.ShapeDtypeStruct((na, d), F32),
        scratch_shapes=[pltpu.VMEM((tm * ROW_SLABS, 128), jnp.uint32),
                        pltpu.VMEM((tm * ROW_SLABS, 128), jnp.uint32),
                        pltpu.SemaphoreType.DMA(())],
        compiler_params=_cparams(("arbitrary",)),
        name="combine",
    )(slots, y, x1, route, lng, lnb, mod)


def _routing_tables(counts, na, tr):
    cnt = counts[0, :N_EXPERTS].astype(jnp.int32)
    ntile = (cnt + tr - 1) // tr
    tile_start = jnp.cumsum(ntile) - ntile
    n_used = jnp.sum(ntile)
    n_tiles = -(-2 * na // tr) + N_EXPERTS
    seg = jnp.zeros((1, 128), F32).at[0, :N_EXPERTS].set((tile_start * tr).astype(F32))
    tile_expert = jnp.sum(jnp.arange(n_tiles, dtype=jnp.int32)[:, None] >= tile_start[None, :], axis=1) - 1
    return seg, n_tiles * tr, tile_expert.astype(jnp.int32), n_used.reshape(1).astype(jnp.int32)


def _lower_bounds(logits):
    p = jax.nn.softmax(logits.astype(F32), axis=0)
    return jnp.cumsum(p, axis=0) - p[:1]


def _row_tile(limit, *sizes):
    tm = limit
    while any(s % tm for s in sizes):
        tm //= 2
    return tm


def kernel(x, c, ctx, c_ctx, w_ada, b_ada, w_in, w_pool, pool_scale, lb_logits_fwd, lb_logits_bwd, hg_gain, rpb, w_br_a, w_br_b, w_br_c, w_out, ln1_g, ln1_b, w_rg, b_rg, w_re, b_re, w_gate, w_up, w_down, ln2_g, ln2_b):
    nbatch, t_len, d = x.shape
    c_len = ctx.shape[1]
    depth = w_ada.shape[0]
    assert c_len == SEQ_TILE and t_len % SEQ_TILE == 0 and t_len % GRID_W == 0
    alpha = (2.0 * depth) ** 0.25
    n_lat = nbatch * t_len
    nt = t_len // SEQ_TILE

    xa = jnp.concatenate([x.reshape(n_lat, d), ctx.reshape(nbatch * c_len, d)], axis=0)

    mod_rows = -(-(nbatch + 1) // 8) * 8
    cc = jnp.zeros((mod_rows, d), F32).at[:nbatch].set(c).at[nbatch].set(c_ctx)
    ada = _ada_call(cc, w_ada, b_ada)

    lb_f = _lower_bounds(lb_logits_fwd).reshape(depth, HG_HEADS, 1, HG_DK)
    lb_b = _lower_bounds(lb_logits_bwd).reshape(depth, HG_HEADS, 1, HG_DK)
    pool_consts = _pool_consts()
    hg_f = _hgrn_consts(False)
    hg_b = _hgrn_consts(True)

    tm_big = _row_tile(1024, t_len, nbatch * c_len)
    tm_mid = _row_tile(512, t_len, nbatch * c_len)

    def modrow_for(tm):
        return lambda i: jnp.where(i * tm < n_lat, (i * tm) // t_len, nbatch)

    for l in range(depth):
        last = l == depth - 1
        mod = ada[l].reshape(mod_rows, 6, d)
        z = _inproj_call(xa, mod, w_in[l].astype(BF16), tm_big, modrow_for(tm_big))
        ya = _pool_call(z, pool_consts, w_pool[l].astype(BF16), pool_scale[l].reshape(1, -1),
                        nt, nbatch * nt)
        o_f = _hgrn_call(z, lb_f[l], hg_f, nbatch, nt, reverse=False)
        yb = _hgrn_call(z, lb_b[l], hg_b, nbatch, nt, reverse=True, o_fwd=o_f,
                        gain=hg_gain[l].reshape(HG_HEADS, 1, HG_DK))
        bias = _na_bias_table(rpb[l], t_len // GRID_W)
        yc = _na_call(z, bias, nbatch, t_len, c_len, with_ctx=not last)
        wr = jnp.zeros((d, 128), F32).at[:, :N_EXPERTS].set(w_re[l]).at[:, N_EXPERTS:N_EXPERTS + N_GROUPS].set(w_rg[l])
        br = jnp.zeros((1, 128), F32).at[0, :N_EXPERTS].set(b_re[l]).at[0, N_EXPERTS:N_EXPERTS + N_GROUPS].set(b_rg[l])
        x1, h2p, route, counts = _merge_call(
            xa, ya, yb, yc, z, mod, w_br_a[l].astype(BF16), w_br_b[l].astype(BF16), w_br_c[l].astype(BF16),
            w_out[l].astype(BF16), ln1_g[l].reshape(1, d), ln1_b[l].reshape(1, d), wr, br, tm_mid,
            modrow_for(tm_mid), alpha)
        seg, n_rows, tile_expert, n_used = _routing_tables(counts, route.shape[0], EXPERT_TILE)
        slots = _slots_call(route, seg, tm_mid)
        xs = _dispatch_call(h2p, slots, n_rows, tm_mid)
        ys = _experts_call(tile_expert, n_used, xs, w_gate, w_up, w_down, l, EXPERT_TILE)
        xa = _combine_call(slots, ys, x1, route, ln2_g[l].reshape(1, d), ln2_b[l].reshape(1, d), mod,
                           tm_mid, modrow_for(tm_mid), alpha)
    return xa.reshape(nbatch, t_len, d)
```

```python
import functools

import numpy as np
import jax
import jax.numpy as jnp
from jax import lax
from jax.experimental import pallas as pl
from jax.experimental.pallas import tpu as pltpu

F32 = jnp.float32
BF16 = jnp.bfloat16
HIGHEST = lax.Precision.HIGHEST

GRID_W = 64
POOL_WINDOWS = (2, 4, 8, 16)
POOL_GDIM = 128
HG_HEADS = 4
HG_DK = 128
HG_BLOCK = 16
NA_HEADS = 8
NA_HD = 64
NA_ROWS = 8
NA_COLS = 16
NA_QROWS = 4
NA_KROWS = 12
N_GROUPS = 4
EXP_PER_GROUP = 8
N_EXPERTS = N_GROUPS * EXP_PER_GROUP
LN_EPS = 1e-5
RMS_EPS = 1e-6
NEG_BIG = -1e30
SEQ_TILE = 256
EXPERT_TILE = 512
POOL_TILE = 512
VMEM_LIMIT = 56 * 1024 * 1024

CB_A, CB_Q, CB_FF, CB_FB, CB_I, CB_G, CB_NQ, CB_NK, CB_NV, CB_GATE = 0, 4, 8, 12, 16, 20, 24, 28, 32, 36


def _cparams(sem):
    return pltpu.CompilerParams(dimension_semantics=sem, vmem_limit_bytes=VMEM_LIMIT)


def _ln(x):
    mu = jnp.mean(x, axis=-1, keepdims=True)
    xc = x - mu
    var = jnp.mean(xc * xc, axis=-1, keepdims=True)
    return xc * lax.rsqrt(var + LN_EPS)


def _sigmoid(x):
    return 1.0 / (1.0 + jnp.exp(-x))


def _sigmoid_t(x):
    return 0.5 * jnp.tanh(0.5 * x) + 0.5


def _dot(a, b):
    return jnp.dot(a, b, preferred_element_type=F32)


def _dot_nt(a, b):
    return lax.dot_general(a, b, (((1,), (1,)), ((), ())), preferred_element_type=F32)


def _dot_tn(a, b):
    return lax.dot_general(a, b, (((0,), (0,)), ((), ())), preferred_element_type=F32)


def _dot01(m01, x, pieces=3):
    x1 = x.astype(BF16)
    r1 = x - x1.astype(F32)
    x2 = r1.astype(BF16)
    out = _dot(m01, x1) + _dot(m01, x2)
    if pieces == 3:
        out = out + _dot(m01, (r1 - x2.astype(F32)).astype(BF16))
    return out


def _ada_body(c_ref, w_ref, b_ref, o_ref):
    cs = c_ref[...]
    s = cs * _sigmoid(cs)
    o_ref[0] = jnp.dot(s, w_ref[0], preferred_element_type=F32, precision=HIGHEST) + b_ref[0]


def _ada_call(cc, w_ada, b_ada):
    depth, d, n6 = w_ada.shape
    rows = cc.shape[0]
    return pl.pallas_call(
        _ada_body,
        grid=(depth, n6 // d),
        in_specs=[pl.BlockSpec((rows, d), lambda l, j: (0, 0)),
                  pl.BlockSpec((1, d, d), lambda l, j: (l, 0, j)),
                  pl.BlockSpec((1, 1, d), lambda l, j: (l, 0, j))],
        out_specs=pl.BlockSpec((1, rows, d), lambda l, j: (l, 0, j)),
        out_shape=jax.ShapeDtypeStruct((depth, rows, n6), F32),
        compiler_params=_cparams(("parallel", "parallel")),
        name="ada",
    )(cc, w_ada, b_ada.reshape(depth, 1, n6))


def _inproj_body(x_ref, mod_ref, w_ref, z_ref, h_scr):
    @pl.when(pl.program_id(1) == 0)
    def _():
        h = _ln(x_ref[...]) * (1.0 + mod_ref[0, 1:2, :]) + mod_ref[0, 0:1, :]
        h_scr[...] = h.astype(BF16)

    z_ref[...] = _dot(h_scr[...], w_ref[...]).astype(z_ref.dtype)


def _inproj_call(xa, mod, w_in, tm, modrow):
    na, d = xa.shape
    d_in = w_in.shape[1]
    tn = 1536
    assert d_in % tn == 0
    return pl.pallas_call(
        _inproj_body,
        grid=(na // tm, d_in // tn),
        in_specs=[pl.BlockSpec((tm, d), lambda i, j: (i, 0)),
                  pl.BlockSpec((1, 6, d), lambda i, j: (modrow(i), 0, 0)),
                  pl.BlockSpec((d, tn), lambda i, j: (0, j))],
        out_specs=pl.BlockSpec((tm, tn), lambda i, j: (i, j)),
        out_shape=jax.ShapeDtypeStruct((na, d_in), BF16),
        scratch_shapes=[pltpu.VMEM((tm, d), BF16)],
        compiler_params=_cparams(("parallel", "arbitrary")),
        name="inproj",
    )(xa, mod, w_in)


def _pool_consts(n):
    t = np.arange(n)[:, None]
    bc = np.zeros((4, n, n), np.float32)
    bp = np.zeros((4, n, 16), np.float32)
    bn = np.zeros((4, n, 16), np.float32)
    for g, win in enumerate(POOL_WINDOWS):
        lo, hi = t - win // 2, t + win // 2 - 1
        s = np.arange(n)[None, :]
        bc[g] = (s >= lo) & (s <= hi)
        s = np.arange(16)[None, :] - 16
        bp[g] = (s >= lo) & (s <= hi)
        s = np.arange(16)[None, :] + n
        bn[g] = (s >= lo) & (s <= hi)
    cnt = np.stack([bc.sum(-1), bp.sum(-1), bn.sum(-1)], axis=1)
    cnt = np.broadcast_to(cnt[..., None], (4, 3, n, 128)).astype(np.float32)
    return (jnp.asarray(bc, BF16), jnp.asarray(bp, BF16), jnp.asarray(bn, BF16), jnp.asarray(cnt))


def _pool_body(prev_ref, cur_ref, next_ref, bc_ref, bp_ref, bn_ref, cnt_ref, wp_ref, ps_ref, o_ref,
               *, nt):
    k = pl.program_id(0) % nt
    has_prev = jnp.where(k != 0, 1.0, 0.0).astype(F32)
    has_next = jnp.where(k != nt - 1, 1.0, 0.0).astype(F32)
    for g in range(len(POOL_WINDOWS)):
        sl = slice(g * POOL_GDIM, (g + 1) * POOL_GDIM)
        u = cur_ref[:, sl]
        ssum = (_dot(bc_ref[g], u) + has_prev * _dot(bp_ref[g], prev_ref[:, sl])
                + has_next * _dot(bn_ref[g], next_ref[:, sl]))
        cnt = cnt_ref[g, 0] + has_prev * cnt_ref[g, 1] + has_next * cnt_ref[g, 2]
        dlt = ssum / cnt - u.astype(F32)
        y = _dot(dlt.astype(BF16), wp_ref[g]) * ps_ref[:, sl]
        o_ref[:, sl] = y.astype(o_ref.dtype)


def _pool_call(z, consts, w_pool, pool_scale, row0, n_rows, seq_len):
    bc, bp, bn, cnt = consts
    n = bc.shape[1]
    assert seq_len % n == 0 and row0 % n == 0 and n_rows % n == 0
    hb = n // 16
    off = row0 // n
    last16 = z.shape[0] // 16 - 1
    full = lambda a: pl.BlockSpec(a.shape, lambda i: (0,) * a.ndim)
    return pl.pallas_call(
        functools.partial(_pool_body, nt=seq_len // n),
        grid=(n_rows // n,),
        in_specs=[pl.BlockSpec((16, 512), lambda i: (jnp.maximum((i + off) * hb - 1, 0), CB_A // 4)),
                  pl.BlockSpec((n, 512), lambda i: (i + off, CB_A // 4)),
                  pl.BlockSpec((16, 512), lambda i: (jnp.minimum((i + off + 1) * hb, last16), CB_A // 4)),
                  full(bc), full(bp), full(bn), full(cnt), full(w_pool), full(pool_scale)],
        out_specs=pl.BlockSpec((n, 512), lambda i: (i, 0)),
        out_shape=jax.ShapeDtypeStruct((n_rows, 512), BF16),
        compiler_params=_cparams(("parallel",)),
        name="pool",
    )(z, z, z, bc, bp, bn, cnt, w_pool, pool_scale)


def _hgrn_consts(reverse):
    n, bs = SEQ_TILE, HG_BLOCK
    nb = n // bs
    t = np.arange(n)
    o = (n - 1 - t) if reverse else t
    blk = t // bs
    jb = np.arange(nb)
    ob = (nb - 1 - jb) if reverse else jb
    cum = ((blk[:, None] == blk[None, :]) & (o[None, :] <= o[:, None])).astype(np.float32)
    bsum = (jb[:, None] == blk[None, :]).astype(np.float32)
    widths = [2 ** l for l in range(1, int(np.log2(nb)) + 1)]
    lvl = np.full((n, n), -1, np.int32)
    obt = ob[blk]
    same = blk[:, None] == blk[None, :]
    lvl[same & (o[None, :] <= o[:, None])] = 0
    for li, w in reversed(list(enumerate(widths, start=1))):
        m = (obt[:, None] // w == obt[None, :] // w) & (obt[None, :] < obt[:, None]) & ~same
        lvl[m] = li
    mats = []
    for w in widths:
        mid = (ob // w) * w + w // 2
        mats.append((mid[:, None] <= ob[None, :]) & (ob[None, :] < ob[:, None]))
    for w in widths:
        mid = (ob // w) * w + w // 2
        mats.append((ob[:, None] < ob[None, :]) & (ob[None, :] < mid[:, None]))
    mats.append(ob[None, :] < ob[:, None])
    mats.append(ob[None, :] > ob[:, None])
    mats.append(np.ones((nb, nb), bool))
    tsm = np.concatenate(mats, axis=0).astype(np.float32)
    return (jnp.asarray(cum, BF16), jnp.asarray(bsum, BF16), jnp.asarray(lvl), jnp.asarray(tsm, BF16),
            len(widths))


def _expand_blocks(c, n):
    nb, lanes = c.shape
    return jnp.broadcast_to(c[:, None, :], (nb, n // nb, lanes)).reshape(n, lanes)


def _hgrn_body(*refs, n_levels, final):
    if final:
        (zq_ref, zf_ref, zi_ref, lb_ref, cum_ref, bsum_ref, lvl_ref, tsm_ref,
         of_ref, zg_ref, gain_ref, o_ref, st_scr) = refs
    else:
        zq_ref, zf_ref, zi_ref, lb_ref, cum_ref, bsum_ref, lvl_ref, tsm_ref, o_ref, st_scr = refs
    n = zq_ref.shape[0]
    nb = n // HG_BLOCK

    @pl.when(pl.program_id(1) == 0)
    def _():
        st_scr[...] = jnp.zeros_like(st_scr)

    for h in range(HG_HEADS):
        sl = slice(h * HG_DK, (h + 1) * HG_DK)
        zq = zq_ref[:, sl].astype(F32)
        zf = zf_ref[:, sl].astype(F32)
        lb = lb_ref[h]
        lf = jnp.log(lb + (1.0 - lb) * _sigmoid(zf))
        k = (1.0 - lb) * _sigmoid(-zf)
        q = zq * _sigmoid_t(zq)
        v = zi_ref[:, sl]

        lf_hi = lf.astype(BF16)
        lf2 = jnp.concatenate([lf_hi, (lf - lf_hi.astype(F32)).astype(BF16)], axis=1)
        b2 = _dot(cum_ref[...], lf2)
        b = b2[:, :HG_DK] + b2[:, HG_DK:]
        t2 = _dot(bsum_ref[...], lf2)
        tot = t2[:, :HG_DK] + t2[:, HG_DK:]
        coef = _dot01(tsm_ref[...], tot)
        crow = lambda idx: coef[idx * nb:(idx + 1) * nb]

        qd = q * jnp.exp(b)
        kd = (k * jnp.exp(-b)).astype(BF16)
        ks = k * jnp.exp(_expand_blocks(tot, n) - b)

        lvl = lvl_ref[...]
        a = jnp.where(lvl == 0, _dot_nt(qd.astype(BF16), kd), 0.0)
        for li in range(1, n_levels + 1):
            qw = (qd * _expand_blocks(jnp.exp(crow(li - 1)), n)).astype(BF16)
            kw = (ks * _expand_blocks(jnp.exp(crow(n_levels + li - 1)), n)).astype(BF16)
            a = jnp.where(lvl == li, _dot_nt(qw, kw), a)
        o = _dot(a.astype(BF16), v)

        st = st_scr[h]
        qs = (qd * _expand_blocks(jnp.exp(crow(2 * n_levels)), n)).astype(BF16)
        o = o + _dot_nt(qs, st.astype(BF16))
        kn = (ks * _expand_blocks(jnp.exp(crow(2 * n_levels + 1)), n)).astype(BF16)
        dec = jnp.exp(coef[(2 * n_levels + 2) * nb:(2 * n_levels + 2) * nb + 1])
        st_scr[h] = st * dec + _dot_tn(v, kn)

        if final:
            o = o + of_ref[:, sl]
            o = o * lax.rsqrt(jnp.mean(o * o, axis=-1, keepdims=True) + RMS_EPS) * gain_ref[h]
            zg = zg_ref[:, sl].astype(F32)
            o_ref[:, sl] = (o * (zg * _sigmoid_t(zg))).astype(o_ref.dtype)
        else:
            o_ref[:, sl] = o


def _hgrn_call(z, lb, consts, nbatch, nt, reverse, o_fwd=None, gain=None):
    na = z.shape[0]
    n = SEQ_TILE
    cum, bsum, lvl, tsm, n_levels = consts
    ctx_base = nbatch * nt
    final = o_fwd is not None

    def tile(b, s):
        lat = (b * nt + nt - s) if reverse else (b * nt + s - 1)
        return jnp.where(s == 0, ctx_base + b, lat)

    width = HG_HEADS * HG_DK

    def col(cb):
        return pl.BlockSpec((n, width), lambda b, s: (tile(b, s), cb // HG_HEADS))

    full = lambda a: pl.BlockSpec(a.shape, lambda b, s: (0,) * a.ndim)
    in_specs = [col(CB_Q), col(CB_FB if reverse else CB_FF), col(CB_I), full(lb),
                full(cum), full(bsum), full(lvl), full(tsm)]
    args = [z, z, z, lb, cum, bsum, lvl, tsm]
    if final:
        in_specs += [col(0), col(CB_G), full(gain)]
        args += [o_fwd, z, gain]
    return pl.pallas_call(
        functools.partial(_hgrn_body, n_levels=n_levels, final=final),
        grid=(nbatch, nt + 1),
        in_specs=in_specs,
        out_specs=col(0),
        out_shape=jax.ShapeDtypeStruct((na, width), BF16 if final else F32),
        scratch_shapes=[pltpu.VMEM((HG_HEADS, HG_DK, HG_DK), F32)],
        compiler_params=_cparams(("parallel", "arbitrary")),
        name="hgrn_bwd" if final else "hgrn_fwd",
    )(*args)


def _na_bias_table(rpb, rows):
    nrb = rows // NA_QROWS
    assert nrb >= 3 and rows >= NA_KROWS
    kr = min(NA_ROWS, rows)
    qc = np.arange(GRID_W)
    c0 = np.clip(qc - NA_COLS // 2, 0, GRID_W - NA_COLS)
    kc = np.arange(GRID_W)
    col_ok = (kc[None, :] >= c0[:, None]) & (kc[None, :] < c0[:, None] + NA_COLS)
    dc = np.clip(kc[None, :] - qc[:, None] + NA_COLS - 1, 0, 2 * NA_COLS - 2)
    sel_c = (dc[..., None] == np.arange(2 * NA_COLS - 1)).astype(np.float32)
    sel_r, oks = [], []
    for rb in (0, 1, nrb - 1):
        start = int(np.clip(NA_QROWS * rb - 4, 0, rows - NA_KROWS))
        r = NA_QROWS * rb + np.arange(NA_QROWS)
        r0 = np.clip(r - kr // 2, 0, rows - kr)
        keyrow = start + np.arange(NA_KROWS)
        row_ok = (keyrow[None, :] >= r0[:, None]) & (keyrow[None, :] < r0[:, None] + kr)
        dr = np.clip(keyrow[None, :] - r[:, None] + NA_ROWS - 1, 0, 2 * NA_ROWS - 2)
        sel_r.append((dr[..., None] == np.arange(2 * NA_ROWS - 1)).astype(np.float32))
        oks.append(row_ok[:, None, :, None] & col_ok[None, :, None, :])
    bias = jnp.einsum("hij,paki,cdj->phackd", rpb.astype(F32), jnp.asarray(np.stack(sel_r)),
                      jnp.asarray(sel_c), precision=HIGHEST)
    bias = jnp.where(jnp.asarray(np.stack(oks))[:, None], bias, NEG_BIG)
    bias = bias.reshape(3, NA_HEADS, NA_QROWS * GRID_W, NA_KROWS * GRID_W)
    return jnp.concatenate([bias, jnp.full_like(bias[:1], NEG_BIG)], axis=0)


def _na_body(q_ref, k_ref, v_ref, kc_ref, vc_ref, bias_ref, o_ref, *, rows):
    rb = pl.program_id(1)
    nk = NA_KROWS * GRID_W
    start_row = jnp.clip(NA_QROWS * rb - 4, 0, rows - NA_KROWS)
    start = pl.multiple_of(start_row * GRID_W, GRID_W)
    nq = q_ref.shape[0]
    lane = lax.broadcasted_iota(jnp.int32, (nq, 128), 1)
    scale = NA_HD ** -0.5
    for p in range(NA_HEADS // 2):
        sl = slice(128 * p, 128 * (p + 1))
        qp = q_ref[:, sl] * scale
        kp = k_ref[pl.ds(start, nk), sl]
        vp = v_ref[pl.ds(start, nk), sl]
        kcp = kc_ref[:, sl]
        vcp = vc_ref[:, sl]
        outs = []
        for hh in range(2):
            sel = (lane < NA_HD) if hh == 0 else (lane >= NA_HD)
            qh = jnp.where(sel, qp, jnp.zeros_like(qp))
            s_loc = _dot_nt(qh, kp) + bias_ref[0, 2 * p + hh]
            s_ctx = _dot_nt(qh, kcp)
            m = jnp.maximum(jnp.max(s_loc, axis=-1, keepdims=True), jnp.max(s_ctx, axis=-1, keepdims=True))
            p_loc = jnp.exp(s_loc - m)
            p_ctx = jnp.exp(s_ctx - m)
            den = jnp.sum(p_loc, axis=-1, keepdims=True) + jnp.sum(p_ctx, axis=-1, keepdims=True)
            o = _dot(p_loc.astype(BF16), vp) + _dot(p_ctx.astype(BF16), vcp)
            outs.append(o / den)
        o_ref[:, sl] = jnp.where(lane < NA_HD, outs[0], outs[1]).astype(o_ref.dtype)


def _na_call(z, bias, nbatch, t_len, c_len, with_ctx):
    na = z.shape[0]
    rows = t_len // GRID_W
    nrb = rows // NA_QROWS
    nq = NA_QROWS * GRID_W
    assert nq == c_len
    ctx_base = nbatch * nrb
    steps = nrb + 1 if with_ctx else nrb

    def qtile(b, r):
        return jnp.where(r < nrb, b * nrb + r, ctx_base + b)

    def pattern(b, r):
        return jnp.where(r == 0, 0, jnp.where(r == nrb - 1, 2, jnp.where(r == nrb, 3, 1)))

    return pl.pallas_call(
        functools.partial(_na_body, rows=rows),
        grid=(nbatch, steps),
        in_specs=[pl.BlockSpec((nq, 512), lambda b, r: (qtile(b, r), CB_NQ // 4)),
                  pl.BlockSpec((t_len, 512), lambda b, r: (b, CB_NK // 4)),
                  pl.BlockSpec((t_len, 512), lambda b, r: (b, CB_NV // 4)),
                  pl.BlockSpec((c_len, 512), lambda b, r: (ctx_base + b, CB_NK // 4)),
                  pl.BlockSpec((c_len, 512), lambda b, r: (ctx_base + b, CB_NV // 4)),
                  pl.BlockSpec((1,) + bias.shape[1:], lambda b, r: (pattern(b, r), 0, 0, 0))],
        out_specs=pl.BlockSpec((nq, 512), lambda b, r: (qtile(b, r), 0)),
        out_shape=jax.ShapeDtypeStruct((na if with_ctx else nbatch * t_len, 512), BF16),
        compiler_params=_cparams(("parallel", "arbitrary")),
        name="natten",
    )(z, z, z, z, z, bias)


ROW_SLABS = 4


def _store_packed_rows(ref, x):
    m = x.shape[0]

    def bits(v):
        return lax.bitcast_convert_type(v.astype(BF16).astype(F32), jnp.uint32)

    for s in range(ROW_SLABS):
        lo = x[:, 128 * s:128 * (s + 1)]
        hi = x[:, 512 + 128 * s:512 + 128 * (s + 1)]
        ref[pl.ds(s, m, stride=ROW_SLABS), :] = (bits(hi) & jnp.uint32(0xFFFF0000)) | (bits(lo) >> 16)


def _load_packed_rows(ref):
    m = ref.shape[0] // ROW_SLABS
    los, his = [], []
    for s in range(ROW_SLABS):
        p = ref[pl.ds(s, m, stride=ROW_SLABS), :]
        los.append(lax.bitcast_convert_type(p << 16, F32))
        his.append(lax.bitcast_convert_type(p & jnp.uint32(0xFFFF0000), F32))
    return los + his


def _merge_body(x_ref, ya_ref, yb_ref, yc_ref, g0, g1, g2, g3, g4, g5, wa_ref, wb_ref, wc_ref, wo_ref,
                lng_ref, lnb_ref, mod_ref, wr2_ref, br_ref, tril_ref, x1_ref, h2_ref, route_ref, cnt_ref,
                cnt_scr, *, alpha):
    @pl.when(pl.program_id(0) == 0)
    def _():
        cnt_scr[...] = jnp.zeros_like(cnt_scr)

    gates = ((g0, g1), (g2, g3), (g4, g5))
    ys = (ya_ref[...], yb_ref[...], yc_ref[...])
    ws = (wa_ref, wb_ref, wc_ref)
    half = wa_ref.shape[1] // 2
    mix = None
    for n in range(2):
        m = None
        for kbr in range(3):
            pr = _dot(ys[kbr], ws[kbr][:, n * half:(n + 1) * half])
            term = _sigmoid_t(gates[kbr][n][...].astype(F32)) * pr
            m = term if m is None else m + term
        part = _dot(m.astype(BF16), wo_ref[n * half:(n + 1) * half, :])
        mix = part if mix is None else mix + part
    r = alpha * x_ref[...] + mod_ref[0, 2:3, :] * mix
    x1 = _ln(r) * lng_ref[...] + lnb_ref[...]
    x1_ref[...] = x1
    h2 = _ln(x1) * (1.0 + mod_ref[0, 4:5, :]) + mod_ref[0, 3:4, :]
    _store_packed_rows(h2_ref, h2)

    h2_hi = h2.astype(BF16)
    h2_lo = (h2 - h2_hi.astype(F32)).astype(BF16)
    hh = _dot(h2_hi, wr2_ref[...])
    logits = (hh[:, :128] + _dot(h2_lo, wr2_ref[:, :128]) + hh[:, 128:]) + br_ref[...]
    lane = lax.broadcasted_iota(jnp.int32, logits.shape, 1).astype(F32)
    is_grp = jnp.where(lane >= N_EXPERTS, jnp.where(lane < N_EXPERTS + N_GROUPS, 1.0, 0.0), 0.0) > 0.5
    lgm = jnp.where(is_grp, logits, NEG_BIG)
    mg = jnp.max(lgm, axis=-1, keepdims=True)
    p_grp = 1.0 / jnp.sum(jnp.exp(lgm - mg), axis=-1, keepdims=True)
    grp = jnp.min(jnp.where(lgm == mg, lane, 1e9), axis=-1, keepdims=True) - N_EXPERTS
    lo = grp * EXP_PER_GROUP
    in_grp = jnp.where(lane >= lo, jnp.where(lane < lo + EXP_PER_GROUP, 1.0, 0.0), 0.0) > 0.5
    lem = jnp.where(in_grp, logits, NEG_BIG)
    m1 = jnp.max(lem, axis=-1, keepdims=True)
    id1 = jnp.min(jnp.where(lem == m1, lane, 1e9), axis=-1, keepdims=True)
    lem2 = jnp.where(lane == id1, NEG_BIG, lem)
    m2 = jnp.max(lem2, axis=-1, keepdims=True)
    id2 = jnp.min(jnp.where(lem2 == m2, lane, 1e9), axis=-1, keepdims=True)
    u2 = jnp.exp(m2 - m1)
    w1 = p_grp / (1.0 + u2)
    w2 = p_grp * u2 / (1.0 + u2)
    oh1 = jnp.where(lane == id1, 1.0, 0.0)
    oh2 = jnp.where(lane == id2, 1.0, 0.0)
    oh = oh1 + oh2
    before = _dot(tril_ref[...], oh.astype(BF16)) + cnt_scr[...]
    rank1 = jnp.sum(before * oh1, axis=-1, keepdims=True)
    rank2 = jnp.sum(before * oh2, axis=-1, keepdims=True)
    cnt_scr[...] += jnp.sum(oh, axis=0, keepdims=True)
    cnt_ref[...] = jnp.broadcast_to(cnt_scr[...], cnt_ref.shape)
    route = jnp.zeros_like(logits)
    for ln, val in enumerate((w1, w2, id1, id2, rank1, rank2)):
        route = jnp.where(lane == ln, val, route)
    route_ref[...] = route


def _merge_call(xa, ya, yb, yc, z, mod, wa, wb, wc, wo, lng, lnb, wr, br, tm, modrow, alpha):
    na, d = yc.shape[0], xa.shape[1]
    row = lambda w: pl.BlockSpec((tm, w), lambda i: (i, 0))
    gate = lambda cb: pl.BlockSpec((tm, 512), lambda i: (i, cb))
    full = lambda a: pl.BlockSpec(a.shape, lambda i: (0,) * a.ndim)
    g0 = CB_GATE // 4
    tril = jnp.asarray(np.tril(np.ones((tm, tm), np.float32), -1), BF16)
    wr_hi = lax.reduce_precision(wr, exponent_bits=8, mantissa_bits=7)
    wr2 = jnp.concatenate([wr_hi, wr - wr_hi], axis=1).astype(BF16)
    return pl.pallas_call(
        functools.partial(_merge_body, alpha=alpha),
        grid=(na // tm,),
        in_specs=[row(d), row(512), row(512), row(512)] + [gate(g0 + j) for j in range(6)]
                 + [full(wa), full(wb), full(wc), full(wo), full(lng), full(lnb),
                    pl.BlockSpec((1, 6, d), lambda i: (modrow(i), 0, 0)), full(wr2), full(br), full(tril)],
        out_specs=[row(d), pl.BlockSpec((tm * ROW_SLABS, 128), lambda i: (i, 0)), row(128),
                   pl.BlockSpec((8, 128), lambda i: (0, 0))],
        out_shape=[jax.ShapeDtypeStruct((na, d), F32), jax.ShapeDtypeStruct((na * ROW_SLABS, 128), jnp.uint32),
                   jax.ShapeDtypeStruct((na, 128), F32), jax.ShapeDtypeStruct((8, 128), F32)],
        scratch_shapes=[pltpu.VMEM((1, 128), F32)],
        compiler_params=_cparams(("arbitrary",)),
        name="merge",
    )(xa, ya, yb, yc, z, z, z, z, z, z, wa, wb, wc, wo, lng, lnb, mod, wr2, br, tril)


ROW_DMA_UNROLL = 8


def _slab(ref, row):
    start = row * ROW_SLABS
    if not isinstance(row, int):
        start = pl.multiple_of(start, ROW_SLABS)
    return ref.at[pl.ds(start, ROW_SLABS), :]


def _slots_body(route_ref, seg_ref, o_ref):
    route = route_ref[...]
    lane = lax.broadcasted_iota(jnp.int32, route.shape, 1).astype(F32)
    seg = seg_ref[...]
    slots = []
    for k in range(2):
        eid = route[:, 2 + k:3 + k]
        start = jnp.sum(jnp.where(lane == eid, seg, 0.0), axis=-1, keepdims=True)
        slots.append(start + route[:, 4 + k:5 + k])
    both = jnp.where(lane == 0.0, slots[0], jnp.where(lane == 1.0, slots[1], 0.0))
    o_ref[0] = both.T[:8].astype(jnp.int32)


def _slots_call(route, seg, tm):
    na = route.shape[0]
    return pl.pallas_call(
        _slots_body,
        grid=(na // tm,),
        in_specs=[pl.BlockSpec((tm, 128), lambda i: (i, 0)), pl.BlockSpec((1, 128), lambda i: (0, 0))],
        out_specs=pl.BlockSpec((1, 8, tm), lambda i: (i, 0, 0)),
        out_shape=jax.ShapeDtypeStruct((na // tm, 8, tm), jnp.int32),
        compiler_params=_cparams(("parallel",)),
        name="slots",
    )(route, seg)


def _dispatch_body(slot_ref, h_ref, xs_in, xs_out, sem):
    del xs_in
    tm = h_ref.shape[0] // ROW_SLABS

    def issue(i, carry):
        for u in range(ROW_DMA_UNROLL):
            t = i * ROW_DMA_UNROLL + u
            for k in range(2):
                pltpu.make_async_copy(_slab(h_ref, t), _slab(xs_out, slot_ref[0, k, t]), sem).start(priority=k)
        return carry

    lax.fori_loop(0, tm // ROW_DMA_UNROLL, issue, 0)

    for k in range(2):
        pltpu.make_async_copy(h_ref, xs_out.at[pl.ds(0, tm * ROW_SLABS), :], sem).wait()


def _dispatch_call(h2p, slots, n_rows, tm):
    xs0 = jnp.zeros((n_rows * ROW_SLABS, 128), h2p.dtype)
    return pl.pallas_call(
        _dispatch_body,
        grid=(slots.shape[0],),
        in_specs=[pl.BlockSpec((1, 8, tm), lambda i: (i, 0, 0), memory_space=pltpu.SMEM),
                  pl.BlockSpec((tm * ROW_SLABS, 128), lambda i: (i, 0)),
                  pl.BlockSpec(memory_space=pl.ANY)],
        out_specs=pl.BlockSpec(memory_space=pl.ANY),
        out_shape=jax.ShapeDtypeStruct(xs0.shape, xs0.dtype),
        input_output_aliases={2: 0},
        scratch_shapes=[pltpu.SemaphoreType.DMA(())],
        compiler_params=_cparams(("arbitrary",)),
        name="dispatch",
    )(slots, h2p, xs0)


def _experts_body(te_ref, nu_ref, xs_ref, wg_ref, wu_ref, wd_ref, y_ref, wg_scr, wu_scr, wd_scr):
    j = pl.program_id(0)

    @pl.when(j >= nu_ref[0])
    def _():
        y_ref[...] = jnp.zeros_like(y_ref)

    @pl.when(jnp.logical_and(j < nu_ref[0], jnp.logical_or(j == 0, te_ref[j] != te_ref[jnp.maximum(j - 1, 0)])))
    def _():
        wg_scr[...] = wg_ref[0].astype(BF16)
        wu_scr[...] = wu_ref[0].astype(BF16)
        wd_scr[...] = wd_ref[0].astype(BF16)

    @pl.when(j < nu_ref[0])
    def _():
        x = jnp.concatenate([c.astype(BF16) for c in _load_packed_rows(xs_ref)], axis=1)
        gt = _dot(x, wg_scr[...])
        a = gt * _sigmoid_t(gt) * _dot(x, wu_scr[...])
        _store_packed_rows(y_ref, _dot(a.astype(BF16), wd_scr[...]))


def _experts_call(tile_expert, n_used, xs, wg, wu, wd, layer, tr):
    rows, w = xs.shape
    _, ne, d, de = wg.shape
    tr = tr * ROW_SLABS
    used = lambda j, te, nu: jnp.minimum(j, nu[0] - 1)
    return pl.pallas_call(
        _experts_body,
        grid_spec=pltpu.PrefetchScalarGridSpec(
            num_scalar_prefetch=2,
            grid=(rows // tr,),
            in_specs=[pl.BlockSpec((tr, w), lambda j, te, nu: (used(j, te, nu), 0)),
                      pl.BlockSpec((None, 1, d, de), lambda j, te, nu: (layer, te[used(j, te, nu)], 0, 0)),
                      pl.BlockSpec((None, 1, d, de), lambda j, te, nu: (layer, te[used(j, te, nu)], 0, 0)),
                      pl.BlockSpec((None, 1, de, d), lambda j, te, nu: (layer, te[used(j, te, nu)], 0, 0))],
            out_specs=pl.BlockSpec((tr, w), lambda j, te, nu: (j, 0)),
            scratch_shapes=[pltpu.VMEM((d, de), BF16), pltpu.VMEM((d, de), BF16), pltpu.VMEM((de, d), BF16)]),
        out_shape=jax.ShapeDtypeStruct((rows, w), jnp.uint32),
        compiler_params=_cparams(("arbitrary",)),
        name="experts",
    )(tile_expert, n_used, xs, wg, wu, wd)


def _combine_body(slot_ref, next_ref, y_hbm, x1_ref, route_ref, lng_ref, lnb_ref, mod_ref, o_ref,
                  buf_scr, sems, *, alpha):
    tm = x1_ref.shape[0]
    i = pl.program_id(0)
    cur = i % 2

    def gather(idx_ref, b):
        def issue(j, carry):
            for u in range(ROW_DMA_UNROLL):
                t = j * ROW_DMA_UNROLL + u
                for k in range(2):
                    pltpu.make_async_copy(_slab(y_hbm, idx_ref[0, k, t]), _slab(buf_scr.at[b, k], t),
                                          sems.at[b]).start(priority=k)
            return carry

        lax.fori_loop(0, tm // ROW_DMA_UNROLL, issue, 0)

    @pl.when(i == 0)
    def _():
        gather(slot_ref, cur)

    @pl.when(i + 1 < pl.num_programs(0))
    def _():
        gather(next_ref, 1 - cur)

    for k in range(2):
        pltpu.make_async_copy(y_hbm.at[pl.ds(0, tm * ROW_SLABS), :], buf_scr.at[cur, k], sems.at[cur]).wait()

    w1 = route_ref[:, 0:1]
    w2 = route_ref[:, 1:2]
    moe = jnp.concatenate([w1 * a1 + w2 * a2 for a1, a2 in zip(_load_packed_rows(buf_scr.at[cur, 0]),
                                                               _load_packed_rows(buf_scr.at[cur, 1]))], axis=1)
    r = alpha * x1_ref[...] + mod_ref[0, 5:6, :] * moe
    o_ref[...] = _ln(r) * lng_ref[...] + lnb_ref[...]


def _combine_call(slots, y, x1, route, lng, lnb, mod, tm, modrow, alpha):
    na, d = x1.shape
    full = lambda a: pl.BlockSpec(a.shape, lambda i: (0,) * a.ndim)
    last = na // tm - 1
    return pl.pallas_call(
        functools.partial(_combine_body, alpha=alpha),
        grid=(na // tm,),
        in_specs=[pl.BlockSpec((1, 8, tm), lambda i: (i, 0, 0), memory_space=pltpu.SMEM),
                  pl.BlockSpec((1, 8, tm), lambda i: (jnp.minimum(i + 1, last), 0, 0), memory_space=pltpu.SMEM),
                  pl.BlockSpec(memory_space=pl.ANY),
                  pl.BlockSpec((tm, d), lambda i: (i, 0)),
                  pl.BlockSpec((tm, 128), lambda i: (i, 0)),
                  full(lng), full(lnb),
                  pl.BlockSpec((1, 6, d), lambda i: (modrow(i), 0, 0))],
        out_specs=pl.BlockSpec((tm, d), lambda i: (i, 0)),
        out_shape=jax.ShapeDtypeStruct((na, d), F32),
        scratch_shapes=[pltpu.VMEM((2, 2, tm * ROW_SLABS, 128), jnp.uint32), pltpu.SemaphoreType.DMA((2,))],
        compiler_params=_cparams(("arbitrary",)),
        name="combine",
    )(slots, slots, y, x1, route, lng, lnb, mod)


def _routing_tables(counts, na, tr):
    cnt = counts[0, :N_EXPERTS].astype(jnp.int32)
    ntile = (cnt + tr - 1) // tr
    tile_start = jnp.cumsum(ntile) - ntile
    n_used = jnp.sum(ntile)
    n_tiles = -(-2 * na // tr) + N_EXPERTS
    seg = jnp.zeros((1, 128), F32).at[0, :N_EXPERTS].set((tile_start * tr).astype(F32))
    tile_expert = jnp.sum(jnp.arange(n_tiles, dtype=jnp.int32)[:, None] >= tile_start[None, :], axis=1) - 1
    return seg, n_tiles * tr, tile_expert.astype(jnp.int32), n_used.reshape(1).astype(jnp.int32)


def _lower_bounds(logits):
    p = jax.nn.softmax(logits.astype(F32), axis=0)
    return jnp.cumsum(p, axis=0) - p[:1]


def _row_tile(limit, *sizes):
    tm = limit
    while any(s % tm for s in sizes):
        tm //= 2
    return tm


def kernel(x, c, ctx, c_ctx, w_ada, b_ada, w_in, w_pool, pool_scale, lb_logits_fwd, lb_logits_bwd, hg_gain, rpb, w_br_a, w_br_b, w_br_c, w_out, ln1_g, ln1_b, w_rg, b_rg, w_re, b_re, w_gate, w_up, w_down, ln2_g, ln2_b):
    nbatch, t_len, d = x.shape
    c_len = ctx.shape[1]
    depth = w_ada.shape[0]
    assert c_len == SEQ_TILE and t_len % SEQ_TILE == 0 and t_len % GRID_W == 0
    alpha = (2.0 * depth) ** 0.25
    n_lat = nbatch * t_len
    nt = t_len // SEQ_TILE

    xa = jnp.concatenate([x.reshape(n_lat, d), ctx.reshape(nbatch * c_len, d)], axis=0)

    mod_rows = -(-(nbatch + 1) // 8) * 8
    cc = jnp.zeros((mod_rows, d), F32).at[:nbatch].set(c).at[nbatch].set(c_ctx)
    ada = _ada_call(cc, w_ada, b_ada)

    lb_f = _lower_bounds(lb_logits_fwd).reshape(depth, HG_HEADS, 1, HG_DK)
    lb_b = _lower_bounds(lb_logits_bwd).reshape(depth, HG_HEADS, 1, HG_DK)
    pool_lat = _pool_consts(_row_tile(POOL_TILE, t_len))
    pool_ctx = _pool_consts(_row_tile(POOL_TILE, c_len))
    hg_f = _hgrn_consts(False)
    hg_b = _hgrn_consts(True)

    tm_big = _row_tile(1024, t_len, nbatch * c_len)
    tm_mid = _row_tile(512, t_len, nbatch * c_len)

    def modrow_for(tm):
        return lambda i: jnp.where(i * tm < n_lat, (i * tm) // t_len, nbatch)

    for l in range(depth):
        last = l == depth - 1
        mod = ada[l].reshape(mod_rows, 6, d)
        z = _inproj_call(xa, mod, w_in[l].astype(BF16), tm_big, modrow_for(tm_big))
        wp, ps = w_pool[l].astype(BF16), pool_scale[l].reshape(1, -1)
        ya = _pool_call(z, pool_lat, wp, ps, 0, n_lat, t_len)
        if not last:
            ya = jnp.concatenate([ya, _pool_call(z, pool_ctx, wp, ps, n_lat, nbatch * c_len, c_len)], axis=0)
        o_f = _hgrn_call(z, lb_f[l], hg_f, nbatch, nt, reverse=False)
        yb = _hgrn_call(z, lb_b[l], hg_b, nbatch, nt, reverse=True, o_fwd=o_f,
                        gain=hg_gain[l].reshape(HG_HEADS, 1, HG_DK))
        bias = _na_bias_table(rpb[l], t_len // GRID_W)
        yc = _na_call(z, bias, nbatch, t_len, c_len, with_ctx=not last)
        wr = jnp.zeros((d, 128), F32).at[:, :N_EXPERTS].set(w_re[l]).at[:, N_EXPERTS:N_EXPERTS + N_GROUPS].set(w_rg[l])
        br = jnp.zeros((1, 128), F32).at[0, :N_EXPERTS].set(b_re[l]).at[0, N_EXPERTS:N_EXPERTS + N_GROUPS].set(b_rg[l])
        x1, h2p, route, counts = _merge_call(
            xa, ya, yb, yc, z, mod, w_br_a[l].astype(BF16), w_br_b[l].astype(BF16), w_br_c[l].astype(BF16),
            w_out[l].astype(BF16), ln1_g[l].reshape(1, d), ln1_b[l].reshape(1, d), wr, br, tm_mid,
            modrow_for(tm_mid), alpha)
        seg, n_rows, tile_expert, n_used = _routing_tables(counts, route.shape[0], EXPERT_TILE)
        slots = _slots_call(route, seg, tm_mid)
        xs = _dispatch_call(h2p, slots, n_rows, tm_mid)
        ys = _experts_call(tile_expert, n_used, xs, w_gate, w_up, w_down, l, EXPERT_TILE)
        xa = _combine_call(slots, ys, x1, route, ln2_g[l].reshape(1, d), ln2_b[l].reshape(1, d), mod,
                           tm_mid, modrow_for(tm_mid), alpha)
    return xa.reshape(nbatch, t_len, d)
```

```python
import functools

import numpy as np
import jax
import jax.numpy as jnp
from jax import lax
from jax.experimental import pallas as pl
from jax.experimental.pallas import tpu as pltpu

F32 = jnp.float32
BF16 = jnp.bfloat16
HIGHEST = lax.Precision.HIGHEST

GRID_W = 64
POOL_WINDOWS = (2, 4, 8, 16)
POOL_GDIM = 128
HG_HEADS = 4
HG_DK = 128
HG_BLOCK = 16
NA_HEADS = 8
NA_HD = 64
NA_ROWS = 8
NA_COLS = 16
NA_QROWS = 4
NA_KROWS = 12
N_GROUPS = 4
EXP_PER_GROUP = 8
N_EXPERTS = N_GROUPS * EXP_PER_GROUP
LN_EPS = 1e-5
RMS_EPS = 1e-6
NEG_BIG = -1e30
SEQ_TILE = 256
EXPERT_TILE = 512
POOL_TILE = 512
VMEM_LIMIT = 56 * 1024 * 1024

CB_A, CB_Q, CB_FF, CB_FB, CB_I, CB_G, CB_NQ, CB_NK, CB_NV, CB_GATE = 0, 4, 8, 12, 16, 20, 24, 28, 32, 36


def _cparams(sem):
    return pltpu.CompilerParams(dimension_semantics=sem, vmem_limit_bytes=VMEM_LIMIT)


def _ln(x):
    mu = jnp.mean(x, axis=-1, keepdims=True)
    xc = x - mu
    var = jnp.mean(xc * xc, axis=-1, keepdims=True)
    return xc * lax.rsqrt(var + LN_EPS)


def _sigmoid(x):
    return 1.0 / (1.0 + jnp.exp(-x))


def _sigmoid_t(x):
    return 0.5 * jnp.tanh(0.5 * x) + 0.5


def _dot(a, b):
    return jnp.dot(a, b, preferred_element_type=F32)


def _dot_nt(a, b):
    return lax.dot_general(a, b, (((1,), (1,)), ((), ())), preferred_element_type=F32)


def _dot_tn(a, b):
    return lax.dot_general(a, b, (((0,), (0,)), ((), ())), preferred_element_type=F32)


def _dot01(m01, x, pieces=3):
    x1 = x.astype(BF16)
    r1 = x - x1.astype(F32)
    x2 = r1.astype(BF16)
    out = _dot(m01, x1) + _dot(m01, x2)
    if pieces == 3:
        out = out + _dot(m01, (r1 - x2.astype(F32)).astype(BF16))
    return out


def _ada_body(c_ref, w_ref, b_ref, o_ref):
    cs = c_ref[...]
    s = cs * _sigmoid(cs)
    o_ref[0] = jnp.dot(s, w_ref[0], preferred_element_type=F32, precision=HIGHEST) + b_ref[0]


def _ada_call(cc, w_ada, b_ada):
    depth, d, n6 = w_ada.shape
    rows = cc.shape[0]
    return pl.pallas_call(
        _ada_body,
        grid=(depth, n6 // d),
        in_specs=[pl.BlockSpec((rows, d), lambda l, j: (0, 0)),
                  pl.BlockSpec((1, d, d), lambda l, j: (l, 0, j)),
                  pl.BlockSpec((1, 1, d), lambda l, j: (l, 0, j))],
        out_specs=pl.BlockSpec((1, rows, d), lambda l, j: (l, 0, j)),
        out_shape=jax.ShapeDtypeStruct((depth, rows, n6), F32),
        compiler_params=_cparams(("parallel", "parallel")),
        name="ada",
    )(cc, w_ada, b_ada.reshape(depth, 1, n6))


def _inproj_body(x_ref, mod_ref, w_ref, z_ref, h_scr):
    @pl.when(pl.program_id(1) == 0)
    def _():
        h = _ln(x_ref[...]) * (1.0 + mod_ref[0, 1:2, :]) + mod_ref[0, 0:1, :]
        h_scr[...] = h.astype(BF16)

    z_ref[...] = _dot(h_scr[...], w_ref[...]).astype(z_ref.dtype)


def _inproj_call(xa, mod, w_in, tm, modrow):
    na, d = xa.shape
    d_in = w_in.shape[1]
    tn = 1536
    assert d_in % tn == 0
    return pl.pallas_call(
        _inproj_body,
        grid=(na // tm, d_in // tn),
        in_specs=[pl.BlockSpec((tm, d), lambda i, j: (i, 0)),
                  pl.BlockSpec((1, 6, d), lambda i, j: (modrow(i), 0, 0)),
                  pl.BlockSpec((d, tn), lambda i, j: (0, j))],
        out_specs=pl.BlockSpec((tm, tn), lambda i, j: (i, j)),
        out_shape=jax.ShapeDtypeStruct((na, d_in), BF16),
        scratch_shapes=[pltpu.VMEM((tm, d), BF16)],
        compiler_params=_cparams(("parallel", "arbitrary")),
        name="inproj",
    )(xa, mod, w_in)


def _pool_consts(n):
    t = np.arange(n)[:, None]
    bc = np.zeros((4, n, n), np.float32)
    bp = np.zeros((4, n, 16), np.float32)
    bn = np.zeros((4, n, 16), np.float32)
    for g, win in enumerate(POOL_WINDOWS):
        lo, hi = t - win // 2, t + win // 2 - 1
        s = np.arange(n)[None, :]
        bc[g] = (s >= lo) & (s <= hi)
        s = np.arange(16)[None, :] - 16
        bp[g] = (s >= lo) & (s <= hi)
        s = np.arange(16)[None, :] + n
        bn[g] = (s >= lo) & (s <= hi)
    cnt = np.stack([bc.sum(-1), bp.sum(-1), bn.sum(-1)], axis=1)
    cnt = np.broadcast_to(cnt[..., None], (4, 3, n, 128)).astype(np.float32)
    return (jnp.asarray(bc, BF16), jnp.asarray(bp, BF16), jnp.asarray(bn, BF16), jnp.asarray(cnt))


def _pool_body(prev_ref, cur_ref, next_ref, bc_ref, bp_ref, bn_ref, cnt_ref, wp_ref, ps_ref, o_ref,
               *, nt):
    k = pl.program_id(0) % nt
    has_prev = jnp.where(k != 0, 1.0, 0.0).astype(F32)
    has_next = jnp.where(k != nt - 1, 1.0, 0.0).astype(F32)
    for g in range(len(POOL_WINDOWS)):
        sl = slice(g * POOL_GDIM, (g + 1) * POOL_GDIM)
        u = cur_ref[:, sl]
        ssum = (_dot(bc_ref[g], u) + has_prev * _dot(bp_ref[g], prev_ref[:, sl])
                + has_next * _dot(bn_ref[g], next_ref[:, sl]))
        cnt = cnt_ref[g, 0] + has_prev * cnt_ref[g, 1] + has_next * cnt_ref[g, 2]
        dlt = ssum / cnt - u.astype(F32)
        y = _dot(dlt.astype(BF16), wp_ref[g]) * ps_ref[:, sl]
        o_ref[:, sl] = y.astype(o_ref.dtype)


def _pool_call(z, consts, w_pool, pool_scale, row0, n_rows, seq_len):
    bc, bp, bn, cnt = consts
    n = bc.shape[1]
    assert seq_len % n == 0 and row0 % n == 0 and n_rows % n == 0
    hb = n // 16
    off = row0 // n
    last16 = z.shape[0] // 16 - 1
    full = lambda a: pl.BlockSpec(a.shape, lambda i: (0,) * a.ndim)
    return pl.pallas_call(
        functools.partial(_pool_body, nt=seq_len // n),
        grid=(n_rows // n,),
        in_specs=[pl.BlockSpec((16, 512), lambda i: (jnp.maximum((i + off) * hb - 1, 0), CB_A // 4)),
                  pl.BlockSpec((n, 512), lambda i: (i + off, CB_A // 4)),
                  pl.BlockSpec((16, 512), lambda i: (jnp.minimum((i + off + 1) * hb, last16), CB_A // 4)),
                  full(bc), full(bp), full(bn), full(cnt), full(w_pool), full(pool_scale)],
        out_specs=pl.BlockSpec((n, 512), lambda i: (i, 0)),
        out_shape=jax.ShapeDtypeStruct((n_rows, 512), BF16),
        compiler_params=_cparams(("parallel",)),
        name="pool",
    )(z, z, z, bc, bp, bn, cnt, w_pool, pool_scale)


def _hgrn_consts(reverse):
    n, bs = SEQ_TILE, HG_BLOCK
    nb = n // bs
    t = np.arange(n)
    o = (n - 1 - t) if reverse else t
    blk = t // bs
    jb = np.arange(nb)
    ob = (nb - 1 - jb) if reverse else jb
    cum = ((blk[:, None] == blk[None, :]) & (o[None, :] <= o[:, None])).astype(np.float32)
    bsum = (jb[:, None] == blk[None, :]).astype(np.float32)
    widths = [2 ** l for l in range(1, int(np.log2(nb)) + 1)]
    lvl = np.full((n, n), -1, np.int32)
    obt = ob[blk]
    same = blk[:, None] == blk[None, :]
    lvl[same & (o[None, :] <= o[:, None])] = 0
    for li, w in reversed(list(enumerate(widths, start=1))):
        m = (obt[:, None] // w == obt[None, :] // w) & (obt[None, :] < obt[:, None]) & ~same
        lvl[m] = li
    mats = []
    for w in widths:
        mid = (ob // w) * w + w // 2
        mats.append((mid[:, None] <= ob[None, :]) & (ob[None, :] < ob[:, None]))
    for w in widths:
        mid = (ob // w) * w + w // 2
        mats.append((ob[:, None] < ob[None, :]) & (ob[None, :] < mid[:, None]))
    mats.append(ob[None, :] < ob[:, None])
    mats.append(ob[None, :] > ob[:, None])
    mats.append(np.ones((nb, nb), bool))
    tsm = np.concatenate(mats, axis=0).astype(np.float32)
    return (jnp.asarray(cum, BF16), jnp.asarray(bsum, BF16), jnp.asarray(lvl, BF16), jnp.asarray(tsm, BF16),
            len(widths))


def _expand_blocks(c, n):
    nb, lanes = c.shape
    return jnp.concatenate([jnp.broadcast_to(c[j:j + 1, :], (n // nb, lanes)) for j in range(nb)], axis=0)


def _hgrn_body(*refs, n_levels, final):
    if final:
        (zq_ref, zf_ref, zi_ref, lb_ref, cum_ref, bsum_ref, lvl_ref, tsm_ref,
         of_ref, zg_ref, gain_ref, o_ref, st_scr) = refs
    else:
        zq_ref, zf_ref, zi_ref, lb_ref, cum_ref, bsum_ref, lvl_ref, tsm_ref, o_ref, st_scr = refs
    n = zq_ref.shape[0]
    nb = n // HG_BLOCK

    @pl.when(pl.program_id(1) == 0)
    def _():
        st_scr[...] = jnp.zeros_like(st_scr)

    zq = zq_ref[...].astype(F32)
    zf = zf_ref[...].astype(F32)
    lb = jnp.concatenate([lb_ref[h] for h in range(HG_HEADS)], axis=1)
    sig = _sigmoid(zf)
    lf = jnp.log(lb + (1.0 - lb) * sig)
    k_all = (1.0 - lb) * (1.0 - sig)
    q_all = zq * _sigmoid_t(zq)
    lf_hi = lf.astype(BF16)
    lf2 = jnp.concatenate([lf_hi, (lf - lf_hi.astype(F32)).astype(BF16)], axis=1)
    w = HG_HEADS * HG_DK
    b2 = _dot(cum_ref[...], lf2)
    b_all = b2[:, :w] + b2[:, w:]
    t2 = _dot(bsum_ref[...], lf2)
    tot_all = t2[:, :w] + t2[:, w:]
    coef_all = _dot01(tsm_ref[...], tot_all)
    qd_all = q_all * jnp.exp(b_all)
    kd_all = (k_all * jnp.exp(-b_all)).astype(BF16)
    ks_all = k_all * jnp.exp(_expand_blocks(tot_all, n) - b_all)

    ecoef = jnp.exp(coef_all)
    scale = lambda idx: _expand_blocks(ecoef[idx * nb:(idx + 1) * nb], n)
    q_lv = [qd_all.astype(BF16)] + [(qd_all * scale(li)).astype(BF16) for li in range(n_levels)]
    k_lv = [kd_all] + [(ks_all * scale(n_levels + li)).astype(BF16) for li in range(n_levels)]
    qs_all = (qd_all * scale(2 * n_levels)).astype(BF16)
    kn_all = (ks_all * scale(2 * n_levels + 1)).astype(BF16)
    dec_all = ecoef[(2 * n_levels + 2) * nb:(2 * n_levels + 2) * nb + 1]

    for h in range(HG_HEADS):
        sl = slice(h * HG_DK, (h + 1) * HG_DK)
        v = zi_ref[:, sl]

        lvl = lvl_ref[...]
        a = jnp.zeros((n, n), BF16)
        for li in range(n_levels + 1):
            a = jnp.where(lvl == float(li), _dot_nt(q_lv[li][:, sl], k_lv[li][:, sl]).astype(BF16), a)
        o = _dot(a, v)

        st = st_scr[h]
        o = o + _dot_nt(qs_all[:, sl], st.astype(BF16))
        st_scr[h] = st * dec_all[:, sl] + _dot_tn(v, kn_all[:, sl])

        if final:
            o = o + of_ref[:, sl]
            o = o * lax.rsqrt(jnp.mean(o * o, axis=-1, keepdims=True) + RMS_EPS) * gain_ref[h]
            zg = zg_ref[:, sl].astype(F32)
            o_ref[:, sl] = (o * (zg * _sigmoid_t(zg))).astype(o_ref.dtype)
        else:
            o_ref[:, sl] = o


def _hgrn_call(z, lb, consts, nbatch, nt, reverse, o_fwd=None, gain=None):
    na = z.shape[0]
    n = SEQ_TILE
    cum, bsum, lvl, tsm, n_levels = consts
    ctx_base = nbatch * nt
    final = o_fwd is not None

    def tile(b, s):
        lat = (b * nt + nt - s) if reverse else (b * nt + s - 1)
        return jnp.where(s == 0, ctx_base + b, lat)

    width = HG_HEADS * HG_DK

    def col(cb):
        return pl.BlockSpec((n, width), lambda b, s: (tile(b, s), cb // HG_HEADS))

    full = lambda a: pl.BlockSpec(a.shape, lambda b, s: (0,) * a.ndim)
    in_specs = [col(CB_Q), col(CB_FB if reverse else CB_FF), col(CB_I), full(lb),
                full(cum), full(bsum), full(lvl), full(tsm)]
    args = [z, z, z, lb, cum, bsum, lvl, tsm]
    if final:
        in_specs += [col(0), col(CB_G), full(gain)]
        args += [o_fwd, z, gain]
    return pl.pallas_call(
        functools.partial(_hgrn_body, n_levels=n_levels, final=final),
        grid=(nbatch, nt + 1),
        in_specs=in_specs,
        out_specs=col(0),
        out_shape=jax.ShapeDtypeStruct((na, width), BF16 if final else F32),
        scratch_shapes=[pltpu.VMEM((HG_HEADS, HG_DK, HG_DK), F32)],
        compiler_params=_cparams(("parallel", "arbitrary")),
        name="hgrn_bwd" if final else "hgrn_fwd",
    )(*args)


def _na_bias_table(rpb, rows):
    nrb = rows // NA_QROWS
    assert nrb >= 3 and rows >= NA_KROWS
    kr = min(NA_ROWS, rows)
    qc = np.arange(GRID_W)
    c0 = np.clip(qc - NA_COLS // 2, 0, GRID_W - NA_COLS)
    kc = np.arange(GRID_W)
    col_ok = (kc[None, :] >= c0[:, None]) & (kc[None, :] < c0[:, None] + NA_COLS)
    dc = np.clip(kc[None, :] - qc[:, None] + NA_COLS - 1, 0, 2 * NA_COLS - 2)
    sel_c = (dc[..., None] == np.arange(2 * NA_COLS - 1)).astype(np.float32)
    sel_r, oks = [], []
    for rb in (0, 1, nrb - 1):
        start = int(np.clip(NA_QROWS * rb - 4, 0, rows - NA_KROWS))
        r = NA_QROWS * rb + np.arange(NA_QROWS)
        r0 = np.clip(r - kr // 2, 0, rows - kr)
        keyrow = start + np.arange(NA_KROWS)
        row_ok = (keyrow[None, :] >= r0[:, None]) & (keyrow[None, :] < r0[:, None] + kr)
        dr = np.clip(keyrow[None, :] - r[:, None] + NA_ROWS - 1, 0, 2 * NA_ROWS - 2)
        sel_r.append((dr[..., None] == np.arange(2 * NA_ROWS - 1)).astype(np.float32))
        oks.append(row_ok[:, None, :, None] & col_ok[None, :, None, :])
    bias = jnp.einsum("hij,paki,cdj->phackd", rpb.astype(F32), jnp.asarray(np.stack(sel_r)),
                      jnp.asarray(sel_c), precision=HIGHEST)
    bias = jnp.where(jnp.asarray(np.stack(oks))[:, None], bias, NEG_BIG)
    bias = bias.reshape(3, NA_HEADS, NA_QROWS * GRID_W, NA_KROWS * GRID_W)
    return jnp.concatenate([bias, jnp.full_like(bias[:1], NEG_BIG)], axis=0)


def _na_body(q_ref, k_ref, v_ref, kc_ref, vc_ref, bias_ref, o_ref, *, rows):
    rb = pl.program_id(1)
    nk = NA_KROWS * GRID_W
    start_row = jnp.clip(NA_QROWS * rb - 4, 0, rows - NA_KROWS)
    start = pl.multiple_of(start_row * GRID_W, GRID_W)
    nq = q_ref.shape[0]
    lane = lax.broadcasted_iota(jnp.int32, (nq, 128), 1)
    scale = NA_HD ** -0.5
    for p in range(NA_HEADS // 2):
        sl = slice(128 * p, 128 * (p + 1))
        qp = q_ref[:, sl] * scale
        kp = k_ref[pl.ds(start, nk), sl]
        vp = v_ref[pl.ds(start, nk), sl]
        kcp = kc_ref[:, sl]
        vcp = vc_ref[:, sl]
        zero = jnp.zeros_like(qp)
        q2 = jnp.concatenate([jnp.where(lane < NA_HD, qp, zero), jnp.where(lane >= NA_HD, qp, zero)], axis=0)
        s_loc = _dot_nt(q2, kp) + bias_ref[0, 2 * p:2 * p + 2].reshape(2 * nq, nk)
        s_ctx = _dot_nt(q2, kcp)
        m = jnp.maximum(jnp.max(s_loc, axis=-1, keepdims=True), jnp.max(s_ctx, axis=-1, keepdims=True))
        p_loc = jnp.exp(s_loc - m)
        p_ctx = jnp.exp(s_ctx - m)
        den = jnp.sum(p_loc, axis=-1, keepdims=True) + jnp.sum(p_ctx, axis=-1, keepdims=True)
        o2 = (_dot(p_loc.astype(BF16), vp) + _dot(p_ctx.astype(BF16), vcp)) / den
        o_ref[:, sl] = jnp.where(lane < NA_HD, o2[:nq], o2[nq:]).astype(o_ref.dtype)


def _na_call(z, bias, nbatch, t_len, c_len, with_ctx):
    na = z.shape[0]
    rows = t_len // GRID_W
    nrb = rows // NA_QROWS
    nq = NA_QROWS * GRID_W
    assert nq == c_len
    ctx_base = nbatch * nrb
    steps = nrb + 1 if with_ctx else nrb

    def qtile(b, r):
        return jnp.where(r < nrb, b * nrb + r, ctx_base + b)

    def pattern(b, r):
        return jnp.where(r == 0, 0, jnp.where(r == nrb - 1, 2, jnp.where(r == nrb, 3, 1)))

    return pl.pallas_call(
        functools.partial(_na_body, rows=rows),
        grid=(nbatch, steps),
        in_specs=[pl.BlockSpec((nq, 512), lambda b, r: (qtile(b, r), CB_NQ // 4)),
                  pl.BlockSpec((t_len, 512), lambda b, r: (b, CB_NK // 4)),
                  pl.BlockSpec((t_len, 512), lambda b, r: (b, CB_NV // 4)),
                  pl.BlockSpec((c_len, 512), lambda b, r: (ctx_base + b, CB_NK // 4)),
                  pl.BlockSpec((c_len, 512), lambda b, r: (ctx_base + b, CB_NV // 4)),
                  pl.BlockSpec((1,) + bias.shape[1:], lambda b, r: (pattern(b, r), 0, 0, 0))],
        out_specs=pl.BlockSpec((nq, 512), lambda b, r: (qtile(b, r), 0)),
        out_shape=jax.ShapeDtypeStruct((na if with_ctx else nbatch * t_len, 512), BF16),
        compiler_params=_cparams(("parallel", "arbitrary")),
        name="natten",
    )(z, z, z, z, z, bias)


ROW_SLABS = 4


def _store_packed_rows(ref, x):
    m = x.shape[0]

    def bits(v):
        return lax.bitcast_convert_type(v.astype(BF16).astype(F32), jnp.uint32)

    for s in range(ROW_SLABS):
        lo = x[:, 128 * s:128 * (s + 1)]
        hi = x[:, 512 + 128 * s:512 + 128 * (s + 1)]
        ref[pl.ds(s, m, stride=ROW_SLABS), :] = (bits(hi) & jnp.uint32(0xFFFF0000)) | (bits(lo) >> 16)


def _load_packed_rows(ref):
    m = ref.shape[0] // ROW_SLABS
    los, his = [], []
    for s in range(ROW_SLABS):
        p = ref[pl.ds(s, m, stride=ROW_SLABS), :]
        los.append(lax.bitcast_convert_type(p << 16, F32))
        his.append(lax.bitcast_convert_type(p & jnp.uint32(0xFFFF0000), F32))
    return los + his


def _merge_body(x_ref, ya_ref, yb_ref, yc_ref, g0, g1, g2, g3, g4, g5, wa_ref, wb_ref, wc_ref, wo_ref,
                lng_ref, lnb_ref, mod_ref, wr2_ref, br_ref, tril_ref, x1_ref, h2_ref, route_ref, cnt_ref,
                cnt_scr, *, alpha):
    @pl.when(pl.program_id(0) == 0)
    def _():
        cnt_scr[...] = jnp.zeros_like(cnt_scr)

    gates = ((g0, g1), (g2, g3), (g4, g5))
    ys = (ya_ref[...], yb_ref[...], yc_ref[...])
    ws = (wa_ref, wb_ref, wc_ref)
    half = wa_ref.shape[1] // 2
    mix = None
    for n in range(2):
        m = None
        for kbr in range(3):
            pr = _dot(ys[kbr], ws[kbr][:, n * half:(n + 1) * half])
            term = _sigmoid_t(gates[kbr][n][...].astype(F32)) * pr
            m = term if m is None else m + term
        part = _dot(m.astype(BF16), wo_ref[n * half:(n + 1) * half, :])
        mix = part if mix is None else mix + part
    r = alpha * x_ref[...] + mod_ref[0, 2:3, :] * mix
    x1 = _ln(r) * lng_ref[...] + lnb_ref[...]
    x1_ref[...] = x1
    h2 = _ln(x1) * (1.0 + mod_ref[0, 4:5, :]) + mod_ref[0, 3:4, :]
    _store_packed_rows(h2_ref, h2)

    h2_hi = h2.astype(BF16)
    h2_lo = (h2 - h2_hi.astype(F32)).astype(BF16)
    hh = _dot(h2_hi, wr2_ref[...])
    logits = (hh[:, :128] + _dot(h2_lo, wr2_ref[:, :128]) + hh[:, 128:]) + br_ref[...]
    lane = lax.broadcasted_iota(jnp.int32, logits.shape, 1).astype(F32)
    is_grp = jnp.where(lane >= N_EXPERTS, jnp.where(lane < N_EXPERTS + N_GROUPS, 1.0, 0.0), 0.0) > 0.5
    lgm = jnp.where(is_grp, logits, NEG_BIG)
    mg = jnp.max(lgm, axis=-1, keepdims=True)
    p_grp = 1.0 / jnp.sum(jnp.exp(lgm - mg), axis=-1, keepdims=True)
    grp = jnp.min(jnp.where(lgm == mg, lane, 1e9), axis=-1, keepdims=True) - N_EXPERTS
    lo = grp * EXP_PER_GROUP
    in_grp = jnp.where(lane >= lo, jnp.where(lane < lo + EXP_PER_GROUP, 1.0, 0.0), 0.0) > 0.5
    lem = jnp.where(in_grp, logits, NEG_BIG)
    m1 = jnp.max(lem, axis=-1, keepdims=True)
    id1 = jnp.min(jnp.where(lem == m1, lane, 1e9), axis=-1, keepdims=True)
    lem2 = jnp.where(lane == id1, NEG_BIG, lem)
    m2 = jnp.max(lem2, axis=-1, keepdims=True)
    id2 = jnp.min(jnp.where(lem2 == m2, lane, 1e9), axis=-1, keepdims=True)
    u2 = jnp.exp(m2 - m1)
    w1 = p_grp / (1.0 + u2)
    w2 = p_grp * u2 / (1.0 + u2)
    oh1 = jnp.where(lane == id1, 1.0, 0.0)
    oh2 = jnp.where(lane == id2, 1.0, 0.0)
    oh = oh1 + oh2
    before = _dot(tril_ref[...], oh.astype(BF16)) + cnt_scr[...]
    rank1 = jnp.sum(before * oh1, axis=-1, keepdims=True)
    rank2 = jnp.sum(before * oh2, axis=-1, keepdims=True)
    cnt_scr[...] += jnp.sum(oh, axis=0, keepdims=True)
    cnt_ref[...] = jnp.broadcast_to(cnt_scr[...], cnt_ref.shape)
    route = jnp.zeros_like(logits)
    for ln, val in enumerate((w1, w2, id1, id2, rank1, rank2)):
        route = jnp.where(lane == ln, val, route)
    route_ref[...] = route


def _merge_call(xa, ya, yb, yc, z, mod, wa, wb, wc, wo, lng, lnb, wr, br, tm, modrow, alpha):
    na, d = yc.shape[0], xa.shape[1]
    row = lambda w: pl.BlockSpec((tm, w), lambda i: (i, 0))
    gate = lambda cb: pl.BlockSpec((tm, 512), lambda i: (i, cb))
    full = lambda a: pl.BlockSpec(a.shape, lambda i: (0,) * a.ndim)
    g0 = CB_GATE // 4
    tril = jnp.asarray(np.tril(np.ones((tm, tm), np.float32), -1), BF16)
    wr_hi = lax.reduce_precision(wr, exponent_bits=8, mantissa_bits=7)
    wr2 = jnp.concatenate([wr_hi, wr - wr_hi], axis=1).astype(BF16)
    return pl.pallas_call(
        functools.partial(_merge_body, alpha=alpha),
        grid=(na // tm,),
        in_specs=[row(d), row(512), row(512), row(512)] + [gate(g0 + j) for j in range(6)]
                 + [full(wa), full(wb), full(wc), full(wo), full(lng), full(lnb),
                    pl.BlockSpec((1, 6, d), lambda i: (modrow(i), 0, 0)), full(wr2), full(br), full(tril)],
        out_specs=[row(d), pl.BlockSpec((tm * ROW_SLABS, 128), lambda i: (i, 0)), row(128),
                   pl.BlockSpec((8, 128), lambda i: (0, 0))],
        out_shape=[jax.ShapeDtypeStruct((na, d), F32), jax.ShapeDtypeStruct((na * ROW_SLABS, 128), jnp.uint32),
                   jax.ShapeDtypeStruct((na, 128), F32), jax.ShapeDtypeStruct((8, 128), F32)],
        scratch_shapes=[pltpu.VMEM((1, 128), F32)],
        compiler_params=_cparams(("arbitrary",)),
        name="merge",
    )(xa, ya, yb, yc, z, z, z, z, z, z, wa, wb, wc, wo, lng, lnb, mod, wr2, br, tril)


ROW_DMA_UNROLL = 8


def _slab(ref, row):
    start = row * ROW_SLABS
    if not isinstance(row, int):
        start = pl.multiple_of(start, ROW_SLABS)
    return ref.at[pl.ds(start, ROW_SLABS), :]


def _slots_body(route_ref, seg_ref, o_ref):
    route = route_ref[...]
    lane = lax.broadcasted_iota(jnp.int32, route.shape, 1).astype(F32)
    seg = seg_ref[...]
    slots = []
    for k in range(2):
        eid = route[:, 2 + k:3 + k]
        start = jnp.sum(jnp.where(lane == eid, seg, 0.0), axis=-1, keepdims=True)
        slots.append(start + route[:, 4 + k:5 + k])
    both = jnp.where(lane == 0.0, slots[0], jnp.where(lane == 1.0, slots[1], 0.0))
    o_ref[0] = both.T[:8].astype(jnp.int32)


def _slots_call(route, seg, tm):
    na = route.shape[0]
    return pl.pallas_call(
        _slots_body,
        grid=(na // tm,),
        in_specs=[pl.BlockSpec((tm, 128), lambda i: (i, 0)), pl.BlockSpec((1, 128), lambda i: (0, 0))],
        out_specs=pl.BlockSpec((1, 8, tm), lambda i: (i, 0, 0)),
        out_shape=jax.ShapeDtypeStruct((na // tm, 8, tm), jnp.int32),
        compiler_params=_cparams(("parallel",)),
        name="slots",
    )(route, seg)


def _dispatch_body(slot_ref, h_ref, xs_in, xs_out, sem):
    del xs_in
    tm = h_ref.shape[0] // ROW_SLABS

    def issue(i, carry):
        for u in range(ROW_DMA_UNROLL):
            t = i * ROW_DMA_UNROLL + u
            for k in range(2):
                pltpu.make_async_copy(_slab(h_ref, t), _slab(xs_out, slot_ref[0, k, t]), sem).start(priority=k)
        return carry

    lax.fori_loop(0, tm // ROW_DMA_UNROLL, issue, 0)

    for k in range(2):
        pltpu.make_async_copy(h_ref, xs_out.at[pl.ds(0, tm * ROW_SLABS), :], sem).wait()


def _dispatch_call(h2p, slots, n_rows, tm):
    xs0 = jnp.zeros((n_rows * ROW_SLABS, 128), h2p.dtype)
    return pl.pallas_call(
        _dispatch_body,
        grid=(slots.shape[0],),
        in_specs=[pl.BlockSpec((1, 8, tm), lambda i: (i, 0, 0), memory_space=pltpu.SMEM),
                  pl.BlockSpec((tm * ROW_SLABS, 128), lambda i: (i, 0)),
                  pl.BlockSpec(memory_space=pl.ANY)],
        out_specs=pl.BlockSpec(memory_space=pl.ANY),
        out_shape=jax.ShapeDtypeStruct(xs0.shape, xs0.dtype),
        input_output_aliases={2: 0},
        scratch_shapes=[pltpu.SemaphoreType.DMA(())],
        compiler_params=_cparams(("arbitrary",)),
        name="dispatch",
    )(slots, h2p, xs0)


def _experts_body(te_ref, nu_ref, xs_ref, wg_ref, wu_ref, wd_ref, y_ref, wg_scr, wu_scr, wd_scr):
    j = pl.program_id(0)

    @pl.when(j >= nu_ref[0])
    def _():
        y_ref[...] = jnp.zeros_like(y_ref)

    @pl.when(jnp.logical_and(j < nu_ref[0], jnp.logical_or(j == 0, te_ref[j] != te_ref[jnp.maximum(j - 1, 0)])))
    def _():
        wg_scr[...] = wg_ref[0].astype(BF16)
        wu_scr[...] = wu_ref[0].astype(BF16)
        wd_scr[...] = wd_ref[0].astype(BF16)

    @pl.when(j < nu_ref[0])
    def _():
        x = jnp.concatenate([c.astype(BF16) for c in _load_packed_rows(xs_ref)], axis=1)
        gt = _dot(x, wg_scr[...])
        a = gt * _sigmoid_t(gt) * _dot(x, wu_scr[...])
        _store_packed_rows(y_ref, _dot(a.astype(BF16), wd_scr[...]))


def _experts_call(tile_expert, n_used, xs, wg, wu, wd, layer, tr):
    rows, w = xs.shape
    _, ne, d, de = wg.shape
    tr = tr * ROW_SLABS
    used = lambda j, te, nu: jnp.minimum(j, nu[0] - 1)
    return pl.pallas_call(
        _experts_body,
        grid_spec=pltpu.PrefetchScalarGridSpec(
            num_scalar_prefetch=2,
            grid=(rows // tr,),
            in_specs=[pl.BlockSpec((tr, w), lambda j, te, nu: (used(j, te, nu), 0)),
                      pl.BlockSpec((None, 1, d, de), lambda j, te, nu: (layer, te[used(j, te, nu)], 0, 0)),
                      pl.BlockSpec((None, 1, d, de), lambda j, te, nu: (layer, te[used(j, te, nu)], 0, 0)),
                      pl.BlockSpec((None, 1, de, d), lambda j, te, nu: (layer, te[used(j, te, nu)], 0, 0))],
            out_specs=pl.BlockSpec((tr, w), lambda j, te, nu: (j, 0)),
            scratch_shapes=[pltpu.VMEM((d, de), BF16), pltpu.VMEM((d, de), BF16), pltpu.VMEM((de, d), BF16)]),
        out_shape=jax.ShapeDtypeStruct((rows, w), jnp.uint32),
        compiler_params=_cparams(("arbitrary",)),
        name="experts",
    )(tile_expert, n_used, xs, wg, wu, wd)


def _combine_body(slot_ref, next_ref, y_hbm, x1_ref, route_ref, lng_ref, lnb_ref, mod_ref, o_ref,
                  buf_scr, sems, *, alpha):
    tm = x1_ref.shape[0]
    i = pl.program_id(0)
    cur = i % 2

    def gather(idx_ref, b):
        def issue(j, carry):
            for u in range(ROW_DMA_UNROLL):
                t = j * ROW_DMA_UNROLL + u
                for k in range(2):
                    pltpu.make_async_copy(_slab(y_hbm, idx_ref[0, k, t]), _slab(buf_scr.at[b, k], t),
                                          sems.at[b]).start(priority=k)
            return carry

        lax.fori_loop(0, tm // ROW_DMA_UNROLL, issue, 0)

    @pl.when(i == 0)
    def _():
        gather(slot_ref, cur)

    @pl.when(i + 1 < pl.num_programs(0))
    def _():
        gather(next_ref, 1 - cur)

    for k in range(2):
        pltpu.make_async_copy(y_hbm.at[pl.ds(0, tm * ROW_SLABS), :], buf_scr.at[cur, k], sems.at[cur]).wait()

    w1 = route_ref[:, 0:1]
    w2 = route_ref[:, 1:2]
    moe = jnp.concatenate([w1 * a1 + w2 * a2 for a1, a2 in zip(_load_packed_rows(buf_scr.at[cur, 0]),
                                                               _load_packed_rows(buf_scr.at[cur, 1]))], axis=1)
    r = alpha * x1_ref[...] + mod_ref[0, 5:6, :] * moe
    o_ref[...] = _ln(r) * lng_ref[...] + lnb_ref[...]


def _combine_call(slots, y, x1, route, lng, lnb, mod, tm, modrow, alpha):
    na, d = x1.shape
    full = lambda a: pl.BlockSpec(a.shape, lambda i: (0,) * a.ndim)
    last = na // tm - 1
    return pl.pallas_call(
        functools.partial(_combine_body, alpha=alpha),
        grid=(na // tm,),
        in_specs=[pl.BlockSpec((1, 8, tm), lambda i: (i, 0, 0), memory_space=pltpu.SMEM),
                  pl.BlockSpec((1, 8, tm), lambda i: (jnp.minimum(i + 1, last), 0, 0), memory_space=pltpu.SMEM),
                  pl.BlockSpec(memory_space=pl.ANY),
                  pl.BlockSpec((tm, d), lambda i: (i, 0)),
                  pl.BlockSpec((tm, 128), lambda i: (i, 0)),
                  full(lng), full(lnb),
                  pl.BlockSpec((1, 6, d), lambda i: (modrow(i), 0, 0))],
        out_specs=pl.BlockSpec((tm, d), lambda i: (i, 0)),
        out_shape=jax.ShapeDtypeStruct((na, d), F32),
        scratch_shapes=[pltpu.VMEM((2, 2, tm * ROW_SLABS, 128), jnp.uint32), pltpu.SemaphoreType.DMA((2,))],
        compiler_params=_cparams(("arbitrary",)),
        name="combine",
    )(slots, slots, y, x1, route, lng, lnb, mod)


def _routing_tables(counts, na, tr):
    cnt = counts[0, :N_EXPERTS].astype(jnp.int32)
    ntile = (cnt + tr - 1) // tr
    tile_start = jnp.cumsum(ntile) - ntile
    n_used = jnp.sum(ntile)
    n_tiles = -(-2 * na // tr) + N_EXPERTS
    seg = jnp.zeros((1, 128), F32).at[0, :N_EXPERTS].set((tile_start * tr).astype(F32))
    tile_expert = jnp.sum(jnp.arange(n_tiles, dtype=jnp.int32)[:, None] >= tile_start[None, :], axis=1) - 1
    return seg, n_tiles * tr, tile_expert.astype(jnp.int32), n_used.reshape(1).astype(jnp.int32)


def _lower_bounds(logits):
    p = jax.nn.softmax(logits.astype(F32), axis=0)
    return jnp.cumsum(p, axis=0) - p[:1]


def _row_tile(limit, *sizes):
    tm = limit
    while any(s % tm for s in sizes):
        tm //= 2
    return tm


def kernel(x, c, ctx, c_ctx, w_ada, b_ada, w_in, w_pool, pool_scale, lb_logits_fwd, lb_logits_bwd, hg_gain, rpb, w_br_a, w_br_b, w_br_c, w_out, ln1_g, ln1_b, w_rg, b_rg, w_re, b_re, w_gate, w_up, w_down, ln2_g, ln2_b):
    nbatch, t_len, d = x.shape
    c_len = ctx.shape[1]
    depth = w_ada.shape[0]
    assert c_len == SEQ_TILE and t_len % SEQ_TILE == 0 and t_len % GRID_W == 0
    alpha = (2.0 * depth) ** 0.25
    n_lat = nbatch * t_len
    nt = t_len // SEQ_TILE

    xa = jnp.concatenate([x.reshape(n_lat, d), ctx.reshape(nbatch * c_len, d)], axis=0)

    mod_rows = -(-(nbatch + 1) // 8) * 8
    cc = jnp.zeros((mod_rows, d), F32).at[:nbatch].set(c).at[nbatch].set(c_ctx)
    ada = _ada_call(cc, w_ada, b_ada)

    lb_f = _lower_bounds(lb_logits_fwd).reshape(depth, HG_HEADS, 1, HG_DK)
    lb_b = _lower_bounds(lb_logits_bwd).reshape(depth, HG_HEADS, 1, HG_DK)
    pool_lat = _pool_consts(_row_tile(POOL_TILE, t_len))
    pool_ctx = _pool_consts(_row_tile(POOL_TILE, c_len))
    hg_f = _hgrn_consts(False)
    hg_b = _hgrn_consts(True)

    tm_big = _row_tile(1024, t_len, nbatch * c_len)
    tm_mid = _row_tile(512, t_len, nbatch * c_len)

    def modrow_for(tm):
        return lambda i: jnp.where(i * tm < n_lat, (i * tm) // t_len, nbatch)

    for l in range(depth):
        last = l == depth - 1
        mod = ada[l].reshape(mod_rows, 6, d)
        z = _inproj_call(xa, mod, w_in[l].astype(BF16), tm_big, modrow_for(tm_big))
        wp, ps = w_pool[l].astype(BF16), pool_scale[l].reshape(1, -1)
        ya = _pool_call(z, pool_lat, wp, ps, 0, n_lat, t_len)
        if not last:
            ya = jnp.concatenate([ya, _pool_call(z, pool_ctx, wp, ps, n_lat, nbatch * c_len, c_len)], axis=0)
        o_f = _hgrn_call(z, lb_f[l], hg_f, nbatch, nt, reverse=False)
        yb = _hgrn_call(z, lb_b[l], hg_b, nbatch, nt, reverse=True, o_fwd=o_f,
                        gain=hg_gain[l].reshape(HG_HEADS, 1, HG_DK))
        bias = _na_bias_table(rpb[l], t_len // GRID_W)
        yc = _na_call(z, bias, nbatch, t_len, c_len, with_ctx=not last)
        wr = jnp.zeros((d, 128), F32).at[:, :N_EXPERTS].set(w_re[l]).at[:, N_EXPERTS:N_EXPERTS + N_GROUPS].set(w_rg[l])
        br = jnp.zeros((1, 128), F32).at[0, :N_EXPERTS].set(b_re[l]).at[0, N_EXPERTS:N_EXPERTS + N_GROUPS].set(b_rg[l])
        x1, h2p, route, counts = _merge_call(
            xa, ya, yb, yc, z, mod, w_br_a[l].astype(BF16), w_br_b[l].astype(BF16), w_br_c[l].astype(BF16),
            w_out[l].astype(BF16), ln1_g[l].reshape(1, d), ln1_b[l].reshape(1, d), wr, br, tm_mid,
            modrow_for(tm_mid), alpha)
        seg, n_rows, tile_expert, n_used = _routing_tables(counts, route.shape[0], EXPERT_TILE)
        slots = _slots_call(route, seg, tm_mid)
        xs = _dispatch_call(h2p, slots, n_rows, tm_mid)
        ys = _experts_call(tile_expert, n_used, xs, w_gate, w_up, w_down, l, EXPERT_TILE)
        xa = _combine_call(slots, ys, x1, route, ln2_g[l].reshape(1, d), ln2_b[l].reshape(1, d), mod,
                           tm_mid, modrow_for(tm_mid), alpha)
    return xa.reshape(nbatch, t_len, d)
```

```python
import functools

import numpy as np
import jax
import jax.numpy as jnp
from jax import lax
from jax.experimental import pallas as pl
from jax.experimental.pallas import tpu as pltpu

F32 = jnp.float32
BF16 = jnp.bfloat16
HIGHEST = lax.Precision.HIGHEST

GRID_W = 64
POOL_WINDOWS = (2, 4, 8, 16)
POOL_GDIM = 128
HG_HEADS = 4
HG_DK = 128
HG_BLOCK = 16
NA_HEADS = 8
NA_HD = 64
NA_ROWS = 8
NA_COLS = 16
NA_QROWS = 4
NA_KROWS = 12
N_GROUPS = 4
EXP_PER_GROUP = 8
N_EXPERTS = N_GROUPS * EXP_PER_GROUP
LN_EPS = 1e-5
RMS_EPS = 1e-6
NEG_BIG = -1e30
SEQ_TILE = 256
EXPERT_TILE = 512
POOL_TILE = 512
VMEM_LIMIT = 56 * 1024 * 1024

CB_A, CB_Q, CB_FF, CB_FB, CB_I, CB_G, CB_NQ, CB_NK, CB_NV, CB_GATE = 0, 4, 8, 12, 16, 20, 24, 28, 32, 36


def _cparams(sem):
    return pltpu.CompilerParams(dimension_semantics=sem, vmem_limit_bytes=VMEM_LIMIT)


def _ln(x):
    mu = jnp.mean(x, axis=-1, keepdims=True)
    xc = x - mu
    var = jnp.mean(xc * xc, axis=-1, keepdims=True)
    return xc * lax.rsqrt(var + LN_EPS)


def _sigmoid(x):
    return 1.0 / (1.0 + jnp.exp(-x))


def _sigmoid_t(x):
    return 0.5 * jnp.tanh(0.5 * x) + 0.5


def _dot(a, b):
    return jnp.dot(a, b, preferred_element_type=F32)


def _dot_nt(a, b):
    return lax.dot_general(a, b, (((1,), (1,)), ((), ())), preferred_element_type=F32)


def _dot_tn(a, b):
    return lax.dot_general(a, b, (((0,), (0,)), ((), ())), preferred_element_type=F32)


def _dot01(m01, x, pieces=3):
    x1 = x.astype(BF16)
    r1 = x - x1.astype(F32)
    x2 = r1.astype(BF16)
    out = _dot(m01, x1) + _dot(m01, x2)
    if pieces == 3:
        out = out + _dot(m01, (r1 - x2.astype(F32)).astype(BF16))
    return out


def _ada_body(c_ref, w_ref, b_ref, o_ref):
    cs = c_ref[...]
    s = cs * _sigmoid(cs)
    o_ref[0] = jnp.dot(s, w_ref[0], preferred_element_type=F32, precision=HIGHEST) + b_ref[0]


def _ada_call(cc, w_ada, b_ada):
    depth, d, n6 = w_ada.shape
    rows = cc.shape[0]
    return pl.pallas_call(
        _ada_body,
        grid=(depth, n6 // d),
        in_specs=[pl.BlockSpec((rows, d), lambda l, j: (0, 0)),
                  pl.BlockSpec((1, d, d), lambda l, j: (l, 0, j)),
                  pl.BlockSpec((1, 1, d), lambda l, j: (l, 0, j))],
        out_specs=pl.BlockSpec((1, rows, d), lambda l, j: (l, 0, j)),
        out_shape=jax.ShapeDtypeStruct((depth, rows, n6), F32),
        compiler_params=_cparams(("parallel", "parallel")),
        name="ada",
    )(cc, w_ada, b_ada.reshape(depth, 1, n6))


def _token_specs(tok, tm):
    lat, ctx, ctx_row0 = tok
    split = (ctx_row0 if ctx is lat else lat.shape[0]) // tm
    off = ctx_row0 // tm
    d = lat.shape[1]
    lat_map = lambda i, *_: (jnp.minimum(i, split - 1), 0)
    ctx_map = lambda i, *_: (jnp.maximum(i - split, 0) + off, 0)
    return split, [pl.BlockSpec((tm, d), lat_map), pl.BlockSpec((tm, d), ctx_map)]


def _inproj_body(xl_ref, xc_ref, mod_ref, w_ref, z_ref, h_scr, *, split):
    def modulated(x_ref):
        h = _ln(x_ref[...]) * (1.0 + mod_ref[0, 1:2, :]) + mod_ref[0, 0:1, :]
        h_scr[...] = h.astype(BF16)

    first = pl.program_id(1) == 0
    pl.when(jnp.logical_and(first, pl.program_id(0) < split))(lambda: modulated(xl_ref))
    pl.when(jnp.logical_and(first, pl.program_id(0) >= split))(lambda: modulated(xc_ref))
    z_ref[...] = _dot(h_scr[...], w_ref[...]).astype(z_ref.dtype)


def _inproj_call(tok, na, mod, w_in, tm, modrow):
    d, d_in = w_in.shape
    tn = 1536
    assert d_in % tn == 0
    split, tok_specs = _token_specs(tok, tm)
    return pl.pallas_call(
        functools.partial(_inproj_body, split=split),
        grid=(na // tm, d_in // tn),
        in_specs=tok_specs + [pl.BlockSpec((1, 6, d), lambda i, j: (modrow(i), 0, 0)),
                              pl.BlockSpec((d, tn), lambda i, j: (0, j))],
        out_specs=pl.BlockSpec((tm, tn), lambda i, j: (i, j)),
        out_shape=jax.ShapeDtypeStruct((na, d_in), BF16),
        scratch_shapes=[pltpu.VMEM((tm, d), BF16)],
        compiler_params=_cparams(("parallel", "arbitrary")),
        name="inproj",
    )(tok[0], tok[1], mod, w_in)


def _pool_consts(n, c_len):
    t = np.arange(n)[:, None]
    bc = np.zeros((2, 4, n, n), np.float32)
    bp = np.zeros((4, n, 16), np.float32)
    bn = np.zeros((4, n, 16), np.float32)
    for g, win in enumerate(POOL_WINDOWS):
        lo, hi = t - win // 2, t + win // 2 - 1
        s = np.arange(n)[None, :]
        bc[0, g] = (s >= lo) & (s <= hi)
        bc[1, g] = bc[0, g] * (s // c_len == t // c_len)
        s = np.arange(16)[None, :] - 16
        bp[g] = (s >= lo) & (s <= hi)
        s = np.arange(16)[None, :] + n
        bn[g] = (s >= lo) & (s <= hi)
    cnt = np.stack([np.stack([bc[0].sum(-1), bp.sum(-1), bn.sum(-1)], axis=1),
                    np.stack([bc[1].sum(-1), 0 * bp.sum(-1), 0 * bn.sum(-1)], axis=1)])
    cnt = np.broadcast_to(cnt[..., None], (2, 4, 3, n, 128)).astype(np.float32)
    return (jnp.asarray(bc, BF16), jnp.asarray(bp, BF16), jnp.asarray(bn, BF16), jnp.asarray(cnt))


def _pool_body(prev_ref, cur_ref, next_ref, bc_ref, bp_ref, bn_ref, cnt_ref, wp_ref, ps_ref, o_ref,
               *, nt, n_lat_tiles):
    i = pl.program_id(0)
    k = i % nt
    lat = i < n_lat_tiles
    has_prev = jnp.where(jnp.logical_and(lat, k != 0), 1.0, 0.0).astype(F32)
    has_next = jnp.where(jnp.logical_and(lat, k != nt - 1), 1.0, 0.0).astype(F32)
    for g in range(len(POOL_WINDOWS)):
        sl = slice(g * POOL_GDIM, (g + 1) * POOL_GDIM)
        u = cur_ref[:, sl]
        ssum = (_dot(bc_ref[0, g], u) + has_prev * _dot(bp_ref[g], prev_ref[:, sl])
                + has_next * _dot(bn_ref[g], next_ref[:, sl]))
        cnt = cnt_ref[0, g, 0] + has_prev * cnt_ref[0, g, 1] + has_next * cnt_ref[0, g, 2]
        dlt = ssum / cnt - u.astype(F32)
        y = _dot(dlt.astype(BF16), wp_ref[g]) * ps_ref[:, sl]
        o_ref[:, sl] = y.astype(o_ref.dtype)


def _pool_call(z, consts, w_pool, pool_scale, n_lat, n_rows, t_len):
    bc, bp, bn, cnt = consts
    n = bc.shape[2]
    assert t_len % n == 0 and n_lat % n == 0 and n_rows % n == 0
    hb = n // 16
    last16 = z.shape[0] // 16 - 1
    n_lat_tiles = n_lat // n
    full = lambda a: pl.BlockSpec(a.shape, lambda i: (0,) * a.ndim)
    variant = lambda a: pl.BlockSpec((1,) + a.shape[1:],
                                     lambda i: (jnp.where(i < n_lat_tiles, 0, 1),) + (0,) * (a.ndim - 1))
    return pl.pallas_call(
        functools.partial(_pool_body, nt=t_len // n, n_lat_tiles=n_lat_tiles),
        grid=(n_rows // n,),
        in_specs=[pl.BlockSpec((16, 512), lambda i: (jnp.maximum(i * hb - 1, 0), CB_A // 4)),
                  pl.BlockSpec((n, 512), lambda i: (i, CB_A // 4)),
                  pl.BlockSpec((16, 512), lambda i: (jnp.minimum((i + 1) * hb, last16), CB_A // 4)),
                  variant(bc), full(bp), full(bn), variant(cnt), full(w_pool), full(pool_scale)],
        out_specs=pl.BlockSpec((n, 512), lambda i: (i, 0)),
        out_shape=jax.ShapeDtypeStruct((n_rows, 512), BF16),
        compiler_params=_cparams(("parallel",)),
        name="pool",
    )(z, z, z, bc, bp, bn, cnt, w_pool, pool_scale)


def _hgrn_consts(reverse):
    n, bs = SEQ_TILE, HG_BLOCK
    nb = n // bs
    t = np.arange(n)
    o = (n - 1 - t) if reverse else t
    blk = t // bs
    jb = np.arange(nb)
    ob = (nb - 1 - jb) if reverse else jb
    cum = ((blk[:, None] == blk[None, :]) & (o[None, :] <= o[:, None])).astype(np.float32)
    bsum = (jb[:, None] == blk[None, :]).astype(np.float32)
    widths = [2 ** l for l in range(1, int(np.log2(nb)) + 1)]
    lvl = np.full((n, n), -1, np.int32)
    obt = ob[blk]
    same = blk[:, None] == blk[None, :]
    lvl[same & (o[None, :] <= o[:, None])] = 0
    for li, w in reversed(list(enumerate(widths, start=1))):
        m = (obt[:, None] // w == obt[None, :] // w) & (obt[None, :] < obt[:, None]) & ~same
        lvl[m] = li
    mats = []
    for w in widths:
        mid = (ob // w) * w + w // 2
        mats.append((mid[:, None] <= ob[None, :]) & (ob[None, :] < ob[:, None]))
    for w in widths:
        mid = (ob // w) * w + w // 2
        mats.append((ob[:, None] < ob[None, :]) & (ob[None, :] < mid[:, None]))
    mats.append(ob[None, :] < ob[:, None])
    mats.append(ob[None, :] > ob[:, None])
    mats.append(np.ones((nb, nb), bool))
    tsm = np.concatenate(mats, axis=0).astype(np.float32)
    return (jnp.asarray(cum, BF16), jnp.asarray(bsum, BF16), jnp.asarray(lvl, BF16), jnp.asarray(tsm, BF16),
            len(widths))


def _expand_blocks(c, n):
    nb, lanes = c.shape
    return jnp.concatenate([jnp.broadcast_to(c[j:j + 1, :], (n // nb, lanes)) for j in range(nb)], axis=0)


def _hgrn_body(*refs, n_levels, final):
    if final:
        (zq_ref, zf_ref, zi_ref, lb_ref, cum_ref, bsum_ref, lvl_ref, tsm_ref,
         of_ref, zg_ref, gain_ref, o_ref, st_scr) = refs
    else:
        zq_ref, zf_ref, zi_ref, lb_ref, cum_ref, bsum_ref, lvl_ref, tsm_ref, o_ref, st_scr = refs
    n = zq_ref.shape[0]
    nb = n // HG_BLOCK

    @pl.when(pl.program_id(1) == 0)
    def _():
        st_scr[...] = jnp.zeros_like(st_scr)

    zq = zq_ref[...].astype(F32)
    zf = zf_ref[...].astype(F32)
    lb = jnp.concatenate([lb_ref[h] for h in range(HG_HEADS)], axis=1)
    sig = _sigmoid(zf)
    lf = jnp.log(lb + (1.0 - lb) * sig)
    k_all = (1.0 - lb) * (1.0 - sig)
    q_all = zq * _sigmoid_t(zq)
    lf_hi = lf.astype(BF16)
    lf2 = jnp.concatenate([lf_hi, (lf - lf_hi.astype(F32)).astype(BF16)], axis=1)
    w = HG_HEADS * HG_DK
    b2 = _dot(cum_ref[...], lf2)
    b_all = b2[:, :w] + b2[:, w:]
    t2 = _dot(bsum_ref[...], lf2)
    tot_all = t2[:, :w] + t2[:, w:]
    coef_all = _dot01(tsm_ref[...], tot_all)
    qd_all = q_all * jnp.exp(b_all)
    kd_all = (k_all * jnp.exp(-b_all)).astype(BF16)
    ks_all = k_all * jnp.exp(_expand_blocks(tot_all, n) - b_all)

    ecoef = jnp.exp(coef_all)
    scale = lambda idx: _expand_blocks(ecoef[idx * nb:(idx + 1) * nb], n)
    q_lv = [qd_all.astype(BF16)] + [(qd_all * scale(li)).astype(BF16) for li in range(n_levels)]
    k_lv = [kd_all] + [(ks_all * scale(n_levels + li)).astype(BF16) for li in range(n_levels)]
    qs_all = (qd_all * scale(2 * n_levels)).astype(BF16)
    kn_all = (ks_all * scale(2 * n_levels + 1)).astype(BF16)
    dec_all = ecoef[(2 * n_levels + 2) * nb:(2 * n_levels + 2) * nb + 1]

    for h in range(HG_HEADS):
        sl = slice(h * HG_DK, (h + 1) * HG_DK)
        v = zi_ref[:, sl]

        lvl = lvl_ref[...]
        a = jnp.zeros((n, n), BF16)
        for li in range(n_levels + 1):
            a = jnp.where(lvl == float(li), _dot_nt(q_lv[li][:, sl], k_lv[li][:, sl]).astype(BF16), a)
        o = _dot(a, v)

        st = st_scr[h]
        o = o + _dot_nt(qs_all[:, sl], st.astype(BF16))
        st_scr[h] = st * dec_all[:, sl] + _dot_tn(v, kn_all[:, sl])

        if final:
            o = o + of_ref[:, sl]
            o = o * lax.rsqrt(jnp.mean(o * o, axis=-1, keepdims=True) + RMS_EPS) * gain_ref[h]
            zg = zg_ref[:, sl].astype(F32)
            o_ref[:, sl] = (o * (zg * _sigmoid_t(zg))).astype(o_ref.dtype)
        else:
            o_ref[:, sl] = o


def _hgrn_call(z, lb, consts, nbatch, nt, reverse, o_fwd=None, gain=None):
    na = z.shape[0]
    n = SEQ_TILE
    cum, bsum, lvl, tsm, n_levels = consts
    ctx_base = nbatch * nt
    final = o_fwd is not None

    def tile(b, s):
        lat = (b * nt + nt - s) if reverse else (b * nt + s - 1)
        return jnp.where(s == 0, ctx_base + b, lat)

    width = HG_HEADS * HG_DK

    def col(cb):
        return pl.BlockSpec((n, width), lambda b, s: (tile(b, s), cb // HG_HEADS))

    full = lambda a: pl.BlockSpec(a.shape, lambda b, s: (0,) * a.ndim)
    in_specs = [col(CB_Q), col(CB_FB if reverse else CB_FF), col(CB_I), full(lb),
                full(cum), full(bsum), full(lvl), full(tsm)]
    args = [z, z, z, lb, cum, bsum, lvl, tsm]
    if final:
        in_specs += [col(0), col(CB_G), full(gain)]
        args += [o_fwd, z, gain]
    return pl.pallas_call(
        functools.partial(_hgrn_body, n_levels=n_levels, final=final),
        grid=(nbatch, nt + 1),
        in_specs=in_specs,
        out_specs=col(0),
        out_shape=jax.ShapeDtypeStruct((na, width), BF16 if final else F32),
        scratch_shapes=[pltpu.VMEM((HG_HEADS, HG_DK, HG_DK), F32)],
        compiler_params=_cparams(("parallel", "arbitrary")),
        name="hgrn_bwd" if final else "hgrn_fwd",
    )(*args)


def _na_bias_table(rpb, rows):
    nrb = rows // NA_QROWS
    assert nrb >= 3 and rows >= NA_KROWS
    kr = min(NA_ROWS, rows)
    qc = np.arange(GRID_W)
    c0 = np.clip(qc - NA_COLS // 2, 0, GRID_W - NA_COLS)
    kc = np.arange(GRID_W)
    col_ok = (kc[None, :] >= c0[:, None]) & (kc[None, :] < c0[:, None] + NA_COLS)
    dc = np.clip(kc[None, :] - qc[:, None] + NA_COLS - 1, 0, 2 * NA_COLS - 2)
    sel_c = (dc[..., None] == np.arange(2 * NA_COLS - 1)).astype(np.float32)
    sel_r, oks = [], []
    for rb in (0, 1, nrb - 1):
        start = int(np.clip(NA_QROWS * rb - 4, 0, rows - NA_KROWS))
        r = NA_QROWS * rb + np.arange(NA_QROWS)
        r0 = np.clip(r - kr // 2, 0, rows - kr)
        keyrow = start + np.arange(NA_KROWS)
        row_ok = (keyrow[None, :] >= r0[:, None]) & (keyrow[None, :] < r0[:, None] + kr)
        dr = np.clip(keyrow[None, :] - r[:, None] + NA_ROWS - 1, 0, 2 * NA_ROWS - 2)
        sel_r.append((dr[..., None] == np.arange(2 * NA_ROWS - 1)).astype(np.float32))
        oks.append(row_ok[:, None, :, None] & col_ok[None, :, None, :])
    bias = jnp.einsum("hij,paki,cdj->phackd", rpb.astype(F32), jnp.asarray(np.stack(sel_r)),
                      jnp.asarray(sel_c), precision=HIGHEST)
    bias = jnp.where(jnp.asarray(np.stack(oks))[:, None], bias, NEG_BIG)
    bias = bias.reshape(3, NA_HEADS, NA_QROWS * GRID_W, NA_KROWS * GRID_W)
    return jnp.concatenate([bias, jnp.full_like(bias[:1], NEG_BIG)], axis=0)


def _na_body(q_ref, k_ref, v_ref, kc_ref, vc_ref, bias_ref, o_ref, *, rows):
    rb = pl.program_id(1)
    nk = NA_KROWS * GRID_W
    start_row = jnp.clip(NA_QROWS * rb - 4, 0, rows - NA_KROWS)
    start = pl.multiple_of(start_row * GRID_W, GRID_W)
    nq = q_ref.shape[0]
    lane = lax.broadcasted_iota(jnp.int32, (nq, 128), 1)
    scale = NA_HD ** -0.5
    for p in range(NA_HEADS // 2):
        sl = slice(128 * p, 128 * (p + 1))
        qp = q_ref[:, sl] * scale
        kp = k_ref[pl.ds(start, nk), sl]
        vp = v_ref[pl.ds(start, nk), sl]
        kcp = kc_ref[:, sl]
        vcp = vc_ref[:, sl]
        zero = jnp.zeros_like(qp)
        q2 = jnp.concatenate([jnp.where(lane < NA_HD, qp, zero), jnp.where(lane >= NA_HD, qp, zero)], axis=0)
        s_loc = _dot_nt(q2, kp) + bias_ref[0, 2 * p:2 * p + 2].reshape(2 * nq, nk)
        s_ctx = _dot_nt(q2, kcp)
        m = jnp.maximum(jnp.max(s_loc, axis=-1, keepdims=True), jnp.max(s_ctx, axis=-1, keepdims=True))
        p_loc = jnp.exp(s_loc - m)
        p_ctx = jnp.exp(s_ctx - m)
        den = jnp.sum(p_loc, axis=-1, keepdims=True) + jnp.sum(p_ctx, axis=-1, keepdims=True)
        o2 = (_dot(p_loc.astype(BF16), vp) + _dot(p_ctx.astype(BF16), vcp)) / den
        o_ref[:, sl] = jnp.where(lane < NA_HD, o2[:nq], o2[nq:]).astype(o_ref.dtype)


def _na_call(z, bias, nbatch, t_len, c_len, with_ctx):
    na = z.shape[0]
    rows = t_len // GRID_W
    nrb = rows // NA_QROWS
    nq = NA_QROWS * GRID_W
    assert nq == c_len
    ctx_base = nbatch * nrb
    steps = nrb + 1 if with_ctx else nrb

    def qtile(b, r):
        return jnp.where(r < nrb, b * nrb + r, ctx_base + b)

    def pattern(b, r):
        return jnp.where(r == 0, 0, jnp.where(r == nrb - 1, 2, jnp.where(r == nrb, 3, 1)))

    return pl.pallas_call(
        functools.partial(_na_body, rows=rows),
        grid=(nbatch, steps),
        in_specs=[pl.BlockSpec((nq, 512), lambda b, r: (qtile(b, r), CB_NQ // 4)),
                  pl.BlockSpec((t_len, 512), lambda b, r: (b, CB_NK // 4)),
                  pl.BlockSpec((t_len, 512), lambda b, r: (b, CB_NV // 4)),
                  pl.BlockSpec((c_len, 512), lambda b, r: (ctx_base + b, CB_NK // 4)),
                  pl.BlockSpec((c_len, 512), lambda b, r: (ctx_base + b, CB_NV // 4)),
                  pl.BlockSpec((1,) + bias.shape[1:], lambda b, r: (pattern(b, r), 0, 0, 0))],
        out_specs=pl.BlockSpec((nq, 512), lambda b, r: (qtile(b, r), 0)),
        out_shape=jax.ShapeDtypeStruct((na if with_ctx else nbatch * t_len, 512), BF16),
        compiler_params=_cparams(("parallel", "arbitrary")),
        name="natten",
    )(z, z, z, z, z, bias)


ROW_SLABS = 4


def _store_packed_rows(ref, x, row0=0):
    m = x.shape[0]

    def bits(v):
        return lax.bitcast_convert_type(v.astype(BF16).astype(F32), jnp.uint32)

    for s in range(ROW_SLABS):
        lo = x[:, 128 * s:128 * (s + 1)]
        hi = x[:, 512 + 128 * s:512 + 128 * (s + 1)]
        ref[pl.ds(row0 * ROW_SLABS + s, m, stride=ROW_SLABS), :] = ((bits(hi) & jnp.uint32(0xFFFF0000))
                                                                    | (bits(lo) >> 16))


def _load_packed_rows(ref):
    m = ref.shape[0] // ROW_SLABS
    los, his = [], []
    for s in range(ROW_SLABS):
        p = ref[pl.ds(s, m, stride=ROW_SLABS), :]
        los.append(lax.bitcast_convert_type(p << 16, F32))
        his.append(lax.bitcast_convert_type(p & jnp.uint32(0xFFFF0000), F32))
    return los + his


def _merge_body(xl_ref, xc_ref, ya_ref, yb_ref, yc_ref, g0, g1, g2, g3, g4, g5, wa_ref, wb_ref, wc_ref, wo_ref,
                lng_ref, lnb_ref, mod_ref, wr2_ref, br_ref, tril_ref, x1_ref, h2_ref, route_ref, cnt_ref,
                cnt_scr, *, alpha, split):
    @pl.when(pl.program_id(0) == 0)
    def _():
        cnt_scr[...] = jnp.zeros_like(cnt_scr)

    gates = ((g0, g1), (g2, g3), (g4, g5))
    yrefs = (ya_ref, yb_ref, yc_ref)
    ws = (wa_ref, wb_ref, wc_ref)
    half = wa_ref.shape[1] // 2
    is_lat = pl.program_id(0) < split

    def token_rows(r0, r1):
        mix = None
        for n in range(2):
            m = None
            for kbr in range(3):
                pr = _dot(yrefs[kbr][r0:r1, :], ws[kbr][:, n * half:(n + 1) * half])
                term = _sigmoid_t(gates[kbr][n][r0:r1, :].astype(F32)) * pr
                m = term if m is None else m + term
            part = _dot(m.astype(BF16), wo_ref[n * half:(n + 1) * half, :])
            mix = part if mix is None else mix + part
        x = jnp.where(is_lat, xl_ref[r0:r1, :], xc_ref[r0:r1, :])
        r = alpha * x + mod_ref[0, 2:3, :] * mix
        x1 = _ln(r) * lng_ref[...] + lnb_ref[...]
        x1_ref[r0:r1, :] = x1
        h2 = _ln(x1) * (1.0 + mod_ref[0, 4:5, :]) + mod_ref[0, 3:4, :]
        _store_packed_rows(h2_ref, h2, r0)
        h2_hi = h2.astype(BF16)
        h2_lo = (h2 - h2_hi.astype(F32)).astype(BF16)
        hh = _dot(h2_hi, wr2_ref[...])
        return (hh[:, :128] + _dot(h2_lo, wr2_ref[:, :128]) + hh[:, 128:]) + br_ref[...]

    logits = token_rows(0, x1_ref.shape[0])
    lane = lax.broadcasted_iota(jnp.int32, logits.shape, 1).astype(F32)
    is_grp = jnp.where(lane >= N_EXPERTS, jnp.where(lane < N_EXPERTS + N_GROUPS, 1.0, 0.0), 0.0) > 0.5
    lgm = jnp.where(is_grp, logits, NEG_BIG)
    mg = jnp.max(lgm, axis=-1, keepdims=True)
    p_grp = 1.0 / jnp.sum(jnp.exp(lgm - mg), axis=-1, keepdims=True)
    grp = jnp.min(jnp.where(lgm == mg, lane, 1e9), axis=-1, keepdims=True) - N_EXPERTS
    lo = grp * EXP_PER_GROUP
    in_grp = jnp.where(lane >= lo, jnp.where(lane < lo + EXP_PER_GROUP, 1.0, 0.0), 0.0) > 0.5
    lem = jnp.where(in_grp, logits, NEG_BIG)
    m1 = jnp.max(lem, axis=-1, keepdims=True)
    id1 = jnp.min(jnp.where(lem == m1, lane, 1e9), axis=-1, keepdims=True)
    lem2 = jnp.where(lane == id1, NEG_BIG, lem)
    m2 = jnp.max(lem2, axis=-1, keepdims=True)
    id2 = jnp.min(jnp.where(lem2 == m2, lane, 1e9), axis=-1, keepdims=True)
    u2 = jnp.exp(m2 - m1)
    w1 = p_grp / (1.0 + u2)
    w2 = p_grp * u2 / (1.0 + u2)
    oh1 = jnp.where(lane == id1, 1.0, 0.0)
    oh2 = jnp.where(lane == id2, 1.0, 0.0)
    oh = oh1 + oh2
    before = _dot(tril_ref[...], oh.astype(BF16)) + cnt_scr[...]
    rank1 = jnp.sum(before * oh1, axis=-1, keepdims=True)
    rank2 = jnp.sum(before * oh2, axis=-1, keepdims=True)
    cnt_scr[...] += jnp.sum(oh, axis=0, keepdims=True)
    cnt_ref[...] = jnp.broadcast_to(cnt_scr[...], cnt_ref.shape)
    route = jnp.zeros_like(logits)
    for ln, val in enumerate((w1, w2, id1, id2, rank1, rank2)):
        route = jnp.where(lane == ln, val, route)
    route_ref[...] = route


def _merge_call(tok, ya, yb, yc, z, mod, wa, wb, wc, wo, lng, lnb, wr, br, tm, modrow, alpha):
    na, d = yc.shape[0], tok[0].shape[1]
    split, tok_specs = _token_specs(tok, tm)
    row = lambda w: pl.BlockSpec((tm, w), lambda i: (i, 0))
    gate = lambda cb: pl.BlockSpec((tm, 512), lambda i: (i, cb))
    full = lambda a: pl.BlockSpec(a.shape, lambda i: (0,) * a.ndim)
    g0 = CB_GATE // 4
    tril = jnp.asarray(np.tril(np.ones((tm, tm), np.float32), -1), BF16)
    wr_hi = lax.reduce_precision(wr, exponent_bits=8, mantissa_bits=7)
    wr2 = jnp.concatenate([wr_hi, wr - wr_hi], axis=1).astype(BF16)
    return pl.pallas_call(
        functools.partial(_merge_body, alpha=alpha, split=split),
        grid=(na // tm,),
        in_specs=tok_specs + [row(512), row(512), row(512)] + [gate(g0 + j) for j in range(6)]
                 + [full(wa), full(wb), full(wc), full(wo), full(lng), full(lnb),
                    pl.BlockSpec((1, 6, d), lambda i: (modrow(i), 0, 0)), full(wr2), full(br), full(tril)],
        out_specs=[row(d), pl.BlockSpec((tm * ROW_SLABS, 128), lambda i: (i, 0)), row(128),
                   pl.BlockSpec((8, 128), lambda i: (0, 0))],
        out_shape=[jax.ShapeDtypeStruct((na, d), F32), jax.ShapeDtypeStruct((na * ROW_SLABS, 128), jnp.uint32),
                   jax.ShapeDtypeStruct((na, 128), F32), jax.ShapeDtypeStruct((8, 128), F32)],
        scratch_shapes=[pltpu.VMEM((1, 128), F32)],
        compiler_params=_cparams(("arbitrary",)),
        name="merge",
    )(tok[0], tok[1], ya, yb, yc, z, z, z, z, z, z, wa, wb, wc, wo, lng, lnb, mod, wr2, br, tril)


ROW_DMA_UNROLL = 8


def _slab(ref, row):
    start = row * ROW_SLABS
    if not isinstance(row, int):
        start = pl.multiple_of(start, ROW_SLABS)
    return ref.at[pl.ds(start, ROW_SLABS), :]


def _slots_body(route_ref, seg_ref, o_ref):
    route = route_ref[...]
    lane = lax.broadcasted_iota(jnp.int32, route.shape, 1).astype(F32)
    seg = seg_ref[...]
    slots = []
    for k in range(2):
        eid = route[:, 2 + k:3 + k]
        start = jnp.sum(jnp.where(lane == eid, seg, 0.0), axis=-1, keepdims=True)
        slots.append(start + route[:, 4 + k:5 + k])
    both = jnp.where(lane == 0.0, slots[0], jnp.where(lane == 1.0, slots[1], 0.0))
    o_ref[0] = both.T[:8].astype(jnp.int32)


def _slots_call(route, seg, tm):
    na = route.shape[0]
    return pl.pallas_call(
        _slots_body,
        grid=(na // tm,),
        in_specs=[pl.BlockSpec((tm, 128), lambda i: (i, 0)), pl.BlockSpec((1, 128), lambda i: (0, 0))],
        out_specs=pl.BlockSpec((1, 8, tm), lambda i: (i, 0, 0)),
        out_shape=jax.ShapeDtypeStruct((na // tm, 8, tm), jnp.int32),
        compiler_params=_cparams(("parallel",)),
        name="slots",
    )(route, seg)


def _dispatch_body(slot_ref, h_ref, xs_in, xs_out, sem):
    del xs_in
    tm = h_ref.shape[0] // ROW_SLABS

    def issue(i, carry):
        for u in range(ROW_DMA_UNROLL):
            t = i * ROW_DMA_UNROLL + u
            for k in range(2):
                pltpu.make_async_copy(_slab(h_ref, t), _slab(xs_out, slot_ref[0, k, t]), sem).start(priority=k)
        return carry

    lax.fori_loop(0, tm // ROW_DMA_UNROLL, issue, 0)

    for k in range(2):
        pltpu.make_async_copy(h_ref, xs_out.at[pl.ds(0, tm * ROW_SLABS), :], sem).wait()


def _dispatch_call(h2p, slots, n_rows, tm):
    xs0 = jnp.zeros((n_rows * ROW_SLABS, 128), h2p.dtype)
    return pl.pallas_call(
        _dispatch_body,
        grid=(slots.shape[0],),
        in_specs=[pl.BlockSpec((1, 8, tm), lambda i: (i, 0, 0), memory_space=pltpu.SMEM),
                  pl.BlockSpec((tm * ROW_SLABS, 128), lambda i: (i, 0)),
                  pl.BlockSpec(memory_space=pl.ANY)],
        out_specs=pl.BlockSpec(memory_space=pl.ANY),
        out_shape=jax.ShapeDtypeStruct(xs0.shape, xs0.dtype),
        input_output_aliases={2: 0},
        scratch_shapes=[pltpu.SemaphoreType.DMA(())],
        compiler_params=_cparams(("arbitrary",)),
        name="dispatch",
    )(slots, h2p, xs0)


def _experts_body(te_ref, nu_ref, xs_ref, wg_ref, wu_ref, wd_ref, y_ref, wg_scr, wu_scr, wd_scr):
    j = pl.program_id(0)

    @pl.when(j >= nu_ref[0])
    def _():
        y_ref[...] = jnp.zeros_like(y_ref)

    @pl.when(jnp.logical_and(j < nu_ref[0], jnp.logical_or(j == 0, te_ref[j] != te_ref[jnp.maximum(j - 1, 0)])))
    def _():
        wg_scr[...] = wg_ref[0].astype(BF16)
        wu_scr[...] = wu_ref[0].astype(BF16)
        wd_scr[...] = wd_ref[0].astype(BF16)

    @pl.when(j < nu_ref[0])
    def _():
        x = jnp.concatenate([c.astype(BF16) for c in _load_packed_rows(xs_ref)], axis=1)
        gt = _dot(x, wg_scr[...])
        a = gt * _sigmoid_t(gt) * _dot(x, wu_scr[...])
        _store_packed_rows(y_ref, _dot(a.astype(BF16), wd_scr[...]))


def _experts_call(tile_expert, n_used, xs, wg, wu, wd, layer, tr):
    rows, w = xs.shape
    _, ne, d, de = wg.shape
    tr = tr * ROW_SLABS
    used = lambda j, te, nu: jnp.minimum(j, nu[0] - 1)
    return pl.pallas_call(
        _experts_body,
        grid_spec=pltpu.PrefetchScalarGridSpec(
            num_scalar_prefetch=2,
            grid=(rows // tr,),
            in_specs=[pl.BlockSpec((tr, w), lambda j, te, nu: (used(j, te, nu), 0)),
                      pl.BlockSpec((None, 1, d, de), lambda j, te, nu: (layer, te[used(j, te, nu)], 0, 0)),
                      pl.BlockSpec((None, 1, d, de), lambda j, te, nu: (layer, te[used(j, te, nu)], 0, 0)),
                      pl.BlockSpec((None, 1, de, d), lambda j, te, nu: (layer, te[used(j, te, nu)], 0, 0))],
            out_specs=pl.BlockSpec((tr, w), lambda j, te, nu: (j, 0)),
            scratch_shapes=[pltpu.VMEM((d, de), BF16), pltpu.VMEM((d, de), BF16), pltpu.VMEM((de, d), BF16)]),
        out_shape=jax.ShapeDtypeStruct((rows, w), jnp.uint32),
        compiler_params=_cparams(("arbitrary",)),
        name="experts",
    )(tile_expert, n_used, xs, wg, wu, wd)


def _combine_body(slot_ref, next_ref, y_hbm, x1_ref, route_ref, lng_ref, lnb_ref, mod_ref, o_ref,
                  buf_scr, sems, *, alpha):
    tm = x1_ref.shape[0]
    i = pl.program_id(0)
    cur = i % 2

    def gather(idx_ref, b):
        def issue(j, carry):
            for u in range(ROW_DMA_UNROLL):
                t = j * ROW_DMA_UNROLL + u
                for k in range(2):
                    pltpu.make_async_copy(_slab(y_hbm, idx_ref[0, k, t]), _slab(buf_scr.at[b, k], t),
                                          sems.at[b]).start(priority=k)
            return carry

        lax.fori_loop(0, tm // ROW_DMA_UNROLL, issue, 0)

    @pl.when(i == 0)
    def _():
        gather(slot_ref, cur)

    @pl.when(i + 1 < pl.num_programs(0))
    def _():
        gather(next_ref, 1 - cur)

    for k in range(2):
        pltpu.make_async_copy(y_hbm.at[pl.ds(0, tm * ROW_SLABS), :], buf_scr.at[cur, k], sems.at[cur]).wait()

    w1 = route_ref[:, 0:1]
    w2 = route_ref[:, 1:2]
    moe = jnp.concatenate([w1 * a1 + w2 * a2 for a1, a2 in zip(_load_packed_rows(buf_scr.at[cur, 0]),
                                                               _load_packed_rows(buf_scr.at[cur, 1]))], axis=1)
    r = alpha * x1_ref[...] + mod_ref[0, 5:6, :] * moe
    o_ref[...] = _ln(r) * lng_ref[...] + lnb_ref[...]


def _combine_call(slots, y, x1, route, lng, lnb, mod, tm, modrow, alpha):
    na, d = x1.shape
    full = lambda a: pl.BlockSpec(a.shape, lambda i: (0,) * a.ndim)
    last = na // tm - 1
    return pl.pallas_call(
        functools.partial(_combine_body, alpha=alpha),
        grid=(na // tm,),
        in_specs=[pl.BlockSpec((1, 8, tm), lambda i: (i, 0, 0), memory_space=pltpu.SMEM),
                  pl.BlockSpec((1, 8, tm), lambda i: (jnp.minimum(i + 1, last), 0, 0), memory_space=pltpu.SMEM),
                  pl.BlockSpec(memory_space=pl.ANY),
                  pl.BlockSpec((tm, d), lambda i: (i, 0)),
                  pl.BlockSpec((tm, 128), lambda i: (i, 0)),
                  full(lng), full(lnb),
                  pl.BlockSpec((1, 6, d), lambda i: (modrow(i), 0, 0))],
        out_specs=pl.BlockSpec((tm, d), lambda i: (i, 0)),
        out_shape=jax.ShapeDtypeStruct((na, d), F32),
        scratch_shapes=[pltpu.VMEM((2, 2, tm * ROW_SLABS, 128), jnp.uint32), pltpu.SemaphoreType.DMA((2,))],
        compiler_params=_cparams(("arbitrary",)),
        name="combine",
    )(slots, slots, y, x1, route, lng, lnb, mod)


def _routing_tables(counts, na, tr):
    cnt = counts[0, :N_EXPERTS].astype(jnp.int32)
    ntile = (cnt + tr - 1) // tr
    tile_start = jnp.cumsum(ntile) - ntile
    n_used = jnp.sum(ntile)
    n_tiles = -(-2 * na // tr) + N_EXPERTS
    seg = jnp.zeros((1, 128), F32).at[0, :N_EXPERTS].set((tile_start * tr).astype(F32))
    tile_expert = jnp.sum(jnp.arange(n_tiles, dtype=jnp.int32)[:, None] >= tile_start[None, :], axis=1) - 1
    return seg, n_tiles * tr, tile_expert.astype(jnp.int32), n_used.reshape(1).astype(jnp.int32)


def _lower_bounds(logits):
    p = jax.nn.softmax(logits.astype(F32), axis=0)
    return jnp.cumsum(p, axis=0) - p[:1]


def _row_tile(limit, *sizes):
    tm = limit
    while any(s % tm for s in sizes):
        tm //= 2
    return tm


def kernel(x, c, ctx, c_ctx, w_ada, b_ada, w_in, w_pool, pool_scale, lb_logits_fwd, lb_logits_bwd, hg_gain, rpb, w_br_a, w_br_b, w_br_c, w_out, ln1_g, ln1_b, w_rg, b_rg, w_re, b_re, w_gate, w_up, w_down, ln2_g, ln2_b):
    nbatch, t_len, d = x.shape
    c_len = ctx.shape[1]
    depth = w_ada.shape[0]
    assert c_len == SEQ_TILE and t_len % SEQ_TILE == 0 and t_len % GRID_W == 0
    alpha = (2.0 * depth) ** 0.25
    n_lat = nbatch * t_len
    nt = t_len // SEQ_TILE

    na = n_lat + nbatch * c_len
    tok = (x.reshape(n_lat, d), ctx.reshape(nbatch * c_len, d), 0)

    mod_rows = -(-(nbatch + 1) // 8) * 8
    cc = jnp.zeros((mod_rows, d), F32).at[:nbatch].set(c).at[nbatch].set(c_ctx)
    ada = _ada_call(cc, w_ada, b_ada)

    lb_f = _lower_bounds(lb_logits_fwd).reshape(depth, HG_HEADS, 1, HG_DK)
    lb_b = _lower_bounds(lb_logits_bwd).reshape(depth, HG_HEADS, 1, HG_DK)
    pool_tile = _row_tile(POOL_TILE, t_len, nbatch * c_len)
    assert pool_tile % c_len == 0
    pool_consts = _pool_consts(pool_tile, c_len)
    hg_f = _hgrn_consts(False)
    hg_b = _hgrn_consts(True)

    tm_big = _row_tile(1024, t_len, nbatch * c_len)
    tm_mid = _row_tile(512, t_len, nbatch * c_len)

    def modrow_for(tm):
        return lambda i: jnp.where(i * tm < n_lat, (i * tm) // t_len, nbatch)

    for l in range(depth):
        last = l == depth - 1
        mod = ada[l].reshape(mod_rows, 6, d)
        z = _inproj_call(tok, na, mod, w_in[l].astype(BF16), tm_big, modrow_for(tm_big))
        ya = _pool_call(z, pool_consts, w_pool[l].astype(BF16), pool_scale[l].reshape(1, -1), n_lat,
                        n_lat if last else n_lat + nbatch * c_len, t_len)
        o_f = _hgrn_call(z, lb_f[l], hg_f, nbatch, nt, reverse=False)
        yb = _hgrn_call(z, lb_b[l], hg_b, nbatch, nt, reverse=True, o_fwd=o_f,
                        gain=hg_gain[l].reshape(HG_HEADS, 1, HG_DK))
        bias = _na_bias_table(rpb[l], t_len // GRID_W)
        yc = _na_call(z, bias, nbatch, t_len, c_len, with_ctx=not last)
        wr = jnp.zeros((d, 128), F32).at[:, :N_EXPERTS].set(w_re[l]).at[:, N_EXPERTS:N_EXPERTS + N_GROUPS].set(w_rg[l])
        br = jnp.zeros((1, 128), F32).at[0, :N_EXPERTS].set(b_re[l]).at[0, N_EXPERTS:N_EXPERTS + N_GROUPS].set(b_rg[l])
        x1, h2p, route, counts = _merge_call(
            tok, ya, yb, yc, z, mod, w_br_a[l].astype(BF16), w_br_b[l].astype(BF16), w_br_c[l].astype(BF16),
            w_out[l].astype(BF16), ln1_g[l].reshape(1, d), ln1_b[l].reshape(1, d), wr, br, tm_mid,
            modrow_for(tm_mid), alpha)
        seg, n_rows, tile_expert, n_used = _routing_tables(counts, route.shape[0], EXPERT_TILE)
        slots = _slots_call(route, seg, tm_mid)
        xs = _dispatch_call(h2p, slots, n_rows, tm_mid)
        ys = _experts_call(tile_expert, n_used, xs, w_gate, w_up, w_down, l, EXPERT_TILE)
        xa = _combine_call(slots, ys, x1, route, ln2_g[l].reshape(1, d), ln2_b[l].reshape(1, d), mod,
                           tm_mid, modrow_for(tm_mid), alpha)
        tok = (xa, xa, n_lat)
    return xa.reshape(nbatch, t_len, d)
```

```python
import functools

import numpy as np
import jax
import jax.numpy as jnp
from jax import lax
from jax.experimental import pallas as pl
from jax.experimental.pallas import tpu as pltpu

F32 = jnp.float32
BF16 = jnp.bfloat16
HIGHEST = lax.Precision.HIGHEST

GRID_W = 64
POOL_WINDOWS = (2, 4, 8, 16)
POOL_GDIM = 128
HG_HEADS = 4
HG_DK = 128
HG_BLOCK = 16
NA_HEADS = 8
NA_HD = 64
NA_ROWS = 8
NA_COLS = 16
NA_QROWS = 4
NA_KROWS = 12
N_GROUPS = 4
EXP_PER_GROUP = 8
N_EXPERTS = N_GROUPS * EXP_PER_GROUP
LN_EPS = 1e-5
RMS_EPS = 1e-6
NEG_BIG = -1e30
SEQ_TILE = 256
EXPERT_TILE = 512
POOL_TILE = 512
VMEM_LIMIT = 56 * 1024 * 1024

CB_A, CB_Q, CB_FF, CB_FB, CB_I, CB_G, CB_NQ, CB_NK, CB_NV, CB_GATE = 0, 4, 8, 12, 16, 20, 24, 28, 32, 36


def _cparams(sem):
    return pltpu.CompilerParams(dimension_semantics=sem, vmem_limit_bytes=VMEM_LIMIT)


def _ln(x):
    mu = jnp.mean(x, axis=-1, keepdims=True)
    xc = x - mu
    var = jnp.mean(xc * xc, axis=-1, keepdims=True)
    return xc * lax.rsqrt(var + LN_EPS)


def _sigmoid(x):
    return 1.0 / (1.0 + jnp.exp(-x))


def _sigmoid_t(x):
    return 0.5 * jnp.tanh(0.5 * x) + 0.5


def _dot(a, b):
    return jnp.dot(a, b, preferred_element_type=F32)


def _dot_nt(a, b):
    return lax.dot_general(a, b, (((1,), (1,)), ((), ())), preferred_element_type=F32)


def _dot_tn(a, b):
    return lax.dot_general(a, b, (((0,), (0,)), ((), ())), preferred_element_type=F32)


def _dot01(m01, x, pieces=3):
    x1 = x.astype(BF16)
    r1 = x - x1.astype(F32)
    x2 = r1.astype(BF16)
    out = _dot(m01, x1) + _dot(m01, x2)
    if pieces == 3:
        out = out + _dot(m01, (r1 - x2.astype(F32)).astype(BF16))
    return out


def _ada_body(c_ref, w_ref, b_ref, o_ref):
    cs = c_ref[...]
    s = cs * _sigmoid(cs)
    o_ref[0] = jnp.dot(s, w_ref[0], preferred_element_type=F32, precision=HIGHEST) + b_ref[0]


def _ada_call(cc, w_ada, b_ada):
    depth, d, n6 = w_ada.shape
    rows = cc.shape[0]
    return pl.pallas_call(
        _ada_body,
        grid=(depth, n6 // d),
        in_specs=[pl.BlockSpec((rows, d), lambda l, j: (0, 0)),
                  pl.BlockSpec((1, d, d), lambda l, j: (l, 0, j)),
                  pl.BlockSpec((1, 1, d), lambda l, j: (l, 0, j))],
        out_specs=pl.BlockSpec((1, rows, d), lambda l, j: (l, 0, j)),
        out_shape=jax.ShapeDtypeStruct((depth, rows, n6), F32),
        compiler_params=_cparams(("parallel", "parallel")),
        name="ada",
    )(cc, w_ada, b_ada.reshape(depth, 1, n6))


def _token_specs(tok, tm):
    lat, ctx, ctx_row0 = tok
    split = (ctx_row0 if ctx is lat else lat.shape[0]) // tm
    off = ctx_row0 // tm
    d = lat.shape[1]
    lat_map = lambda i, *_: (jnp.minimum(i, split - 1), 0)
    ctx_map = lambda i, *_: (jnp.maximum(i - split, 0) + off, 0)
    return split, [pl.BlockSpec((tm, d), lat_map), pl.BlockSpec((tm, d), ctx_map)]


def _inproj_body(xl_ref, xc_ref, mod_ref, w_ref, z_ref, h_scr, *, split):
    def modulated(x_ref):
        h = _ln(x_ref[...]) * (1.0 + mod_ref[0, 1:2, :]) + mod_ref[0, 0:1, :]
        h_scr[...] = h.astype(BF16)

    first = pl.program_id(1) == 0
    pl.when(jnp.logical_and(first, pl.program_id(0) < split))(lambda: modulated(xl_ref))
    pl.when(jnp.logical_and(first, pl.program_id(0) >= split))(lambda: modulated(xc_ref))
    z_ref[...] = _dot(h_scr[...], w_ref[...]).astype(z_ref.dtype)


def _inproj_call(tok, na, mod, w_in, tm, modrow):
    d, d_in = w_in.shape
    tn = 1536
    assert d_in % tn == 0
    split, tok_specs = _token_specs(tok, tm)
    return pl.pallas_call(
        functools.partial(_inproj_body, split=split),
        grid=(na // tm, d_in // tn),
        in_specs=tok_specs + [pl.BlockSpec((1, 6, d), lambda i, j: (modrow(i), 0, 0)),
                              pl.BlockSpec((d, tn), lambda i, j: (0, j))],
        out_specs=pl.BlockSpec((tm, tn), lambda i, j: (i, j)),
        out_shape=jax.ShapeDtypeStruct((na, d_in), BF16),
        scratch_shapes=[pltpu.VMEM((tm, d), BF16)],
        compiler_params=_cparams(("parallel", "arbitrary")),
        name="inproj",
    )(tok[0], tok[1], mod, w_in)


def _pool_consts(n, c_len):
    t = np.arange(n)[:, None]
    bc = np.zeros((2, 4, n, n), np.float32)
    bp = np.zeros((4, n, 16), np.float32)
    bn = np.zeros((4, n, 16), np.float32)
    for g, win in enumerate(POOL_WINDOWS):
        lo, hi = t - win // 2, t + win // 2 - 1
        s = np.arange(n)[None, :]
        bc[0, g] = (s >= lo) & (s <= hi)
        bc[1, g] = bc[0, g] * (s // c_len == t // c_len)
        s = np.arange(16)[None, :] - 16
        bp[g] = (s >= lo) & (s <= hi)
        s = np.arange(16)[None, :] + n
        bn[g] = (s >= lo) & (s <= hi)
    cnt = np.stack([np.stack([bc[0].sum(-1), bp.sum(-1), bn.sum(-1)], axis=1),
                    np.stack([bc[1].sum(-1), 0 * bp.sum(-1), 0 * bn.sum(-1)], axis=1)])
    cnt = np.broadcast_to(cnt[..., None], (2, 4, 3, n, 128)).astype(np.float32)
    return (jnp.asarray(bc, BF16), jnp.asarray(bp, BF16), jnp.asarray(bn, BF16), jnp.asarray(cnt))


def _pool_body(prev_ref, cur_ref, next_ref, bc_ref, bp_ref, bn_ref, cnt_ref, wp_ref, ps_ref, o_ref,
               *, nt, n_lat_tiles):
    i = pl.program_id(0)
    k = i % nt
    lat = i < n_lat_tiles
    has_prev = jnp.where(jnp.logical_and(lat, k != 0), 1.0, 0.0).astype(F32)
    has_next = jnp.where(jnp.logical_and(lat, k != nt - 1), 1.0, 0.0).astype(F32)
    for g in range(len(POOL_WINDOWS)):
        sl = slice(g * POOL_GDIM, (g + 1) * POOL_GDIM)
        u = cur_ref[:, sl]
        ssum = (_dot(bc_ref[0, g], u) + has_prev * _dot(bp_ref[g], prev_ref[:, sl])
                + has_next * _dot(bn_ref[g], next_ref[:, sl]))
        cnt = cnt_ref[0, g, 0] + has_prev * cnt_ref[0, g, 1] + has_next * cnt_ref[0, g, 2]
        dlt = ssum / cnt - u.astype(F32)
        y = _dot(dlt.astype(BF16), wp_ref[g]) * ps_ref[:, sl]
        o_ref[:, sl] = y.astype(o_ref.dtype)


def _pool_call(z, consts, w_pool, pool_scale, n_lat, n_rows, t_len):
    bc, bp, bn, cnt = consts
    n = bc.shape[2]
    assert t_len % n == 0 and n_lat % n == 0 and n_rows % n == 0
    hb = n // 16
    last16 = z.shape[0] // 16 - 1
    n_lat_tiles = n_lat // n
    full = lambda a: pl.BlockSpec(a.shape, lambda i: (0,) * a.ndim)
    variant = lambda a: pl.BlockSpec((1,) + a.shape[1:],
                                     lambda i: (jnp.where(i < n_lat_tiles, 0, 1),) + (0,) * (a.ndim - 1))
    return pl.pallas_call(
        functools.partial(_pool_body, nt=t_len // n, n_lat_tiles=n_lat_tiles),
        grid=(n_rows // n,),
        in_specs=[pl.BlockSpec((16, 512), lambda i: (jnp.maximum(i * hb - 1, 0), CB_A // 4)),
                  pl.BlockSpec((n, 512), lambda i: (i, CB_A // 4)),
                  pl.BlockSpec((16, 512), lambda i: (jnp.minimum((i + 1) * hb, last16), CB_A // 4)),
                  variant(bc), full(bp), full(bn), variant(cnt), full(w_pool), full(pool_scale)],
        out_specs=pl.BlockSpec((n, 512), lambda i: (i, 0)),
        out_shape=jax.ShapeDtypeStruct((n_rows, 512), BF16),
        compiler_params=_cparams(("parallel",)),
        name="pool",
    )(z, z, z, bc, bp, bn, cnt, w_pool, pool_scale)


def _hgrn_consts(reverse):
    n, bs = SEQ_TILE, HG_BLOCK
    nb = n // bs
    t = np.arange(n)
    o = (n - 1 - t) if reverse else t
    blk = t // bs
    jb = np.arange(nb)
    ob = (nb - 1 - jb) if reverse else jb
    cum = ((blk[:, None] == blk[None, :]) & (o[None, :] <= o[:, None])).astype(np.float32)
    bsum = (jb[:, None] == blk[None, :]).astype(np.float32)
    widths = [2 ** l for l in range(1, int(np.log2(nb)) + 1)]
    lvl = np.full((n, n), -1, np.int32)
    obt = ob[blk]
    same = blk[:, None] == blk[None, :]
    lvl[same & (o[None, :] <= o[:, None])] = 0
    for li, w in reversed(list(enumerate(widths, start=1))):
        m = (obt[:, None] // w == obt[None, :] // w) & (obt[None, :] < obt[:, None]) & ~same
        lvl[m] = li
    mats = []
    for w in widths:
        mid = (ob // w) * w + w // 2
        mats.append((mid[:, None] <= ob[None, :]) & (ob[None, :] < ob[:, None]))
    for w in widths:
        mid = (ob // w) * w + w // 2
        mats.append((ob[:, None] < ob[None, :]) & (ob[None, :] < mid[:, None]))
    mats.append(ob[None, :] < ob[:, None])
    mats.append(ob[None, :] > ob[:, None])
    mats.append(np.ones((nb, nb), bool))
    tsm = np.concatenate(mats, axis=0).astype(np.float32)
    return (jnp.asarray(cum, BF16), jnp.asarray(bsum, BF16), jnp.asarray(lvl, BF16), jnp.asarray(tsm, BF16),
            len(widths))


def _expand_blocks(c, n):
    nb, lanes = c.shape
    return jnp.concatenate([jnp.broadcast_to(c[j:j + 1, :], (n // nb, lanes)) for j in range(nb)], axis=0)


def _hgrn_body(*refs, n_levels, final):
    if final:
        (zq_ref, zf_ref, zi_ref, lb_ref, cum_ref, bsum_ref, lvl_ref, tsm_ref,
         of_ref, zg_ref, gain_ref, o_ref, st_scr) = refs
    else:
        zq_ref, zf_ref, zi_ref, lb_ref, cum_ref, bsum_ref, lvl_ref, tsm_ref, o_ref, st_scr = refs
    n = zq_ref.shape[0]
    nb = n // HG_BLOCK

    @pl.when(pl.program_id(1) == 0)
    def _():
        st_scr[...] = jnp.zeros_like(st_scr)

    zq = zq_ref[...].astype(F32)
    zf = zf_ref[...].astype(F32)
    lb = jnp.concatenate([lb_ref[h] for h in range(HG_HEADS)], axis=1)
    sig = _sigmoid(zf)
    lf = jnp.log(lb + (1.0 - lb) * sig)
    k_all = (1.0 - lb) * (1.0 - sig)
    q_all = zq * _sigmoid_t(zq)
    lf_hi = lf.astype(BF16)
    lf2 = jnp.concatenate([lf_hi, (lf - lf_hi.astype(F32)).astype(BF16)], axis=1)
    w = HG_HEADS * HG_DK
    b2 = _dot(cum_ref[...], lf2)
    b_all = b2[:, :w] + b2[:, w:]
    t2 = _dot(bsum_ref[...], lf2)
    tot_all = t2[:, :w] + t2[:, w:]
    coef_all = _dot01(tsm_ref[...], tot_all)
    qd_all = q_all * jnp.exp(b_all)
    kd_all = (k_all * jnp.exp(-b_all)).astype(BF16)
    ks_all = k_all * jnp.exp(_expand_blocks(tot_all, n) - b_all)

    ecoef = jnp.exp(coef_all)
    scale = lambda idx: _expand_blocks(ecoef[idx * nb:(idx + 1) * nb], n)
    q_lv = [qd_all.astype(BF16)] + [(qd_all * scale(li)).astype(BF16) for li in range(n_levels)]
    k_lv = [kd_all] + [(ks_all * scale(n_levels + li)).astype(BF16) for li in range(n_levels)]
    qs_all = (qd_all * scale(2 * n_levels)).astype(BF16)
    kn_all = (ks_all * scale(2 * n_levels + 1)).astype(BF16)
    dec_all = ecoef[(2 * n_levels + 2) * nb:(2 * n_levels + 2) * nb + 1]

    for h in range(HG_HEADS):
        sl = slice(h * HG_DK, (h + 1) * HG_DK)
        v = zi_ref[:, sl]

        lvl = lvl_ref[...]
        a = jnp.zeros((n, n), BF16)
        for li in range(n_levels + 1):
            a = jnp.where(lvl == float(li), _dot_nt(q_lv[li][:, sl], k_lv[li][:, sl]).astype(BF16), a)
        o = _dot(a, v)

        st = st_scr[h]
        o = o + _dot_nt(qs_all[:, sl], st.astype(BF16))
        st_scr[h] = st * dec_all[:, sl] + _dot_tn(v, kn_all[:, sl])

        if final:
            o = o + of_ref[:, sl]
            o = o * lax.rsqrt(jnp.mean(o * o, axis=-1, keepdims=True) + RMS_EPS) * gain_ref[h]
            zg = zg_ref[:, sl].astype(F32)
            o_ref[:, sl] = (o * (zg * _sigmoid_t(zg))).astype(o_ref.dtype)
        else:
            o_ref[:, sl] = o


def _hgrn_call(z, lb, consts, nbatch, nt, reverse, o_fwd=None, gain=None):
    na = z.shape[0]
    n = SEQ_TILE
    cum, bsum, lvl, tsm, n_levels = consts
    ctx_base = nbatch * nt
    final = o_fwd is not None

    def tile(b, s):
        lat = (b * nt + nt - s) if reverse else (b * nt + s - 1)
        return jnp.where(s == 0, ctx_base + b, lat)

    width = HG_HEADS * HG_DK

    def col(cb):
        return pl.BlockSpec((n, width), lambda b, s: (tile(b, s), cb // HG_HEADS))

    full = lambda a: pl.BlockSpec(a.shape, lambda b, s: (0,) * a.ndim)
    in_specs = [col(CB_Q), col(CB_FB if reverse else CB_FF), col(CB_I), full(lb),
                full(cum), full(bsum), full(lvl), full(tsm)]
    args = [z, z, z, lb, cum, bsum, lvl, tsm]
    if final:
        in_specs += [col(0), col(CB_G), full(gain)]
        args += [o_fwd, z, gain]
    return pl.pallas_call(
        functools.partial(_hgrn_body, n_levels=n_levels, final=final),
        grid=(nbatch, nt + 1),
        in_specs=in_specs,
        out_specs=col(0),
        out_shape=jax.ShapeDtypeStruct((na, width), BF16 if final else F32),
        scratch_shapes=[pltpu.VMEM((HG_HEADS, HG_DK, HG_DK), F32)],
        compiler_params=_cparams(("parallel", "arbitrary")),
        name="hgrn_bwd" if final else "hgrn_fwd",
    )(*args)


def _na_bias_table(rpb, rows):
    nrb = rows // NA_QROWS
    assert nrb >= 3 and rows >= NA_KROWS
    kr = min(NA_ROWS, rows)
    qc = np.arange(GRID_W)
    c0 = np.clip(qc - NA_COLS // 2, 0, GRID_W - NA_COLS)
    kc = np.arange(GRID_W)
    col_ok = (kc[None, :] >= c0[:, None]) & (kc[None, :] < c0[:, None] + NA_COLS)
    dc = np.clip(kc[None, :] - qc[:, None] + NA_COLS - 1, 0, 2 * NA_COLS - 2)
    sel_c = (dc[..., None] == np.arange(2 * NA_COLS - 1)).astype(np.float32)
    sel_r, oks = [], []
    for rb in (0, 1, nrb - 1):
        start = int(np.clip(NA_QROWS * rb - 4, 0, rows - NA_KROWS))
        r = NA_QROWS * rb + np.arange(NA_QROWS)
        r0 = np.clip(r - kr // 2, 0, rows - kr)
        keyrow = start + np.arange(NA_KROWS)
        row_ok = (keyrow[None, :] >= r0[:, None]) & (keyrow[None, :] < r0[:, None] + kr)
        dr = np.clip(keyrow[None, :] - r[:, None] + NA_ROWS - 1, 0, 2 * NA_ROWS - 2)
        sel_r.append((dr[..., None] == np.arange(2 * NA_ROWS - 1)).astype(np.float32))
        oks.append(row_ok[:, None, :, None] & col_ok[None, :, None, :])
    bias = jnp.einsum("hij,paki,cdj->phackd", rpb.astype(F32), jnp.asarray(np.stack(sel_r)),
                      jnp.asarray(sel_c), precision=HIGHEST)
    bias = jnp.where(jnp.asarray(np.stack(oks))[:, None], bias, NEG_BIG)
    bias = bias.reshape(3, NA_HEADS, NA_QROWS * GRID_W, NA_KROWS * GRID_W)
    return jnp.concatenate([bias, jnp.full_like(bias[:1], NEG_BIG)], axis=0)


def _na_body(q_ref, k_ref, v_ref, kc_ref, vc_ref, bias_ref, o_ref, *, rows):
    rb = pl.program_id(1)
    nk = NA_KROWS * GRID_W
    start_row = jnp.clip(NA_QROWS * rb - 4, 0, rows - NA_KROWS)
    start = pl.multiple_of(start_row * GRID_W, GRID_W)
    nq = q_ref.shape[0]
    lane = lax.broadcasted_iota(jnp.int32, (nq, 128), 1)
    scale = NA_HD ** -0.5
    for p in range(NA_HEADS // 2):
        sl = slice(128 * p, 128 * (p + 1))
        qp = q_ref[:, sl] * scale
        kp = k_ref[pl.ds(start, nk), sl]
        vp = v_ref[pl.ds(start, nk), sl]
        kcp = kc_ref[:, sl]
        vcp = vc_ref[:, sl]
        zero = jnp.zeros_like(qp)
        q2 = jnp.concatenate([jnp.where(lane < NA_HD, qp, zero), jnp.where(lane >= NA_HD, qp, zero)], axis=0)
        s_loc = _dot_nt(q2, kp) + bias_ref[0, 2 * p:2 * p + 2].reshape(2 * nq, nk)
        s_ctx = _dot_nt(q2, kcp)
        m = jnp.maximum(jnp.max(s_loc, axis=-1, keepdims=True), jnp.max(s_ctx, axis=-1, keepdims=True))
        p_loc = jnp.exp(s_loc - m)
        p_ctx = jnp.exp(s_ctx - m)
        den = jnp.sum(p_loc, axis=-1, keepdims=True) + jnp.sum(p_ctx, axis=-1, keepdims=True)
        o2 = (_dot(p_loc.astype(BF16), vp) + _dot(p_ctx.astype(BF16), vcp)) / den
        o_ref[:, sl] = jnp.where(lane < NA_HD, o2[:nq], o2[nq:]).astype(o_ref.dtype)


def _na_call(z, bias, nbatch, t_len, c_len, with_ctx):
    na = z.shape[0]
    rows = t_len // GRID_W
    nrb = rows // NA_QROWS
    nq = NA_QROWS * GRID_W
    assert nq == c_len
    ctx_base = nbatch * nrb
    steps = nrb + 1 if with_ctx else nrb

    def qtile(b, r):
        return jnp.where(r < nrb, b * nrb + r, ctx_base + b)

    def pattern(b, r):
        return jnp.where(r == 0, 0, jnp.where(r == nrb - 1, 2, jnp.where(r == nrb, 3, 1)))

    return pl.pallas_call(
        functools.partial(_na_body, rows=rows),
        grid=(nbatch, steps),
        in_specs=[pl.BlockSpec((nq, 512), lambda b, r: (qtile(b, r), CB_NQ // 4)),
                  pl.BlockSpec((t_len, 512), lambda b, r: (b, CB_NK // 4)),
                  pl.BlockSpec((t_len, 512), lambda b, r: (b, CB_NV // 4)),
                  pl.BlockSpec((c_len, 512), lambda b, r: (ctx_base + b, CB_NK // 4)),
                  pl.BlockSpec((c_len, 512), lambda b, r: (ctx_base + b, CB_NV // 4)),
                  pl.BlockSpec((1,) + bias.shape[1:], lambda b, r: (pattern(b, r), 0, 0, 0))],
        out_specs=pl.BlockSpec((nq, 512), lambda b, r: (qtile(b, r), 0)),
        out_shape=jax.ShapeDtypeStruct((na if with_ctx else nbatch * t_len, 512), BF16),
        compiler_params=_cparams(("parallel", "arbitrary")),
        name="natten",
    )(z, z, z, z, z, bias)


ROW_SLABS = 4


def _store_packed_rows(ref, x, row0=0):
    m = x.shape[0]

    def bits(v):
        return lax.bitcast_convert_type(v.astype(BF16).astype(F32), jnp.uint32)

    for s in range(ROW_SLABS):
        lo = x[:, 128 * s:128 * (s + 1)]
        hi = x[:, 512 + 128 * s:512 + 128 * (s + 1)]
        ref[pl.ds(row0 * ROW_SLABS + s, m, stride=ROW_SLABS), :] = ((bits(hi) & jnp.uint32(0xFFFF0000))
                                                                    | (bits(lo) >> 16))


def _load_packed_rows(ref):
    m = ref.shape[0] // ROW_SLABS
    los, his = [], []
    for s in range(ROW_SLABS):
        p = ref[pl.ds(s, m, stride=ROW_SLABS), :]
        los.append(lax.bitcast_convert_type(p << 16, F32))
        his.append(lax.bitcast_convert_type(p & jnp.uint32(0xFFFF0000), F32))
    return los + his


def _merge_body(xl_ref, xc_ref, ya_ref, yb_ref, yc_ref, g0, g1, g2, g3, g4, g5, wa_ref, wb_ref, wc_ref, wo_ref,
                lng_ref, lnb_ref, mod_ref, wr2_ref, br_ref, tril_ref, x1_ref, h2_ref, route_ref, cnt_ref,
                cnt_scr, *, alpha, split):
    @pl.when(pl.program_id(0) == 0)
    def _():
        cnt_scr[...] = jnp.zeros_like(cnt_scr)

    gates = ((g0, g1), (g2, g3), (g4, g5))
    yrefs = (ya_ref, yb_ref, yc_ref)
    ws = (wa_ref, wb_ref, wc_ref)
    half = wa_ref.shape[1] // 2
    is_lat = pl.program_id(0) < split

    def token_rows(r0, r1):
        mix = None
        for n in range(2):
            m = None
            for kbr in range(3):
                pr = _dot(yrefs[kbr][r0:r1, :], ws[kbr][:, n * half:(n + 1) * half])
                term = _sigmoid_t(gates[kbr][n][r0:r1, :].astype(F32)) * pr
                m = term if m is None else m + term
            part = _dot(m.astype(BF16), wo_ref[n * half:(n + 1) * half, :])
            mix = part if mix is None else mix + part
        x = jnp.where(is_lat, xl_ref[r0:r1, :], xc_ref[r0:r1, :])
        r = alpha * x + mod_ref[0, 2:3, :] * mix
        x1 = _ln(r) * lng_ref[...] + lnb_ref[...]
        x1_ref[r0:r1, :] = x1
        h2 = _ln(x1) * (1.0 + mod_ref[0, 4:5, :]) + mod_ref[0, 3:4, :]
        _store_packed_rows(h2_ref, h2, r0)
        h2_hi = h2.astype(BF16)
        h2_lo = (h2 - h2_hi.astype(F32)).astype(BF16)
        hh = _dot(h2_hi, wr2_ref[...])
        return (hh[:, :128] + _dot(h2_lo, wr2_ref[:, :128]) + hh[:, 128:]) + br_ref[...]

    logits = token_rows(0, x1_ref.shape[0])
    lane = lax.broadcasted_iota(jnp.int32, logits.shape, 1).astype(F32)
    is_grp = jnp.where(lane >= N_EXPERTS, jnp.where(lane < N_EXPERTS + N_GROUPS, 1.0, 0.0), 0.0) > 0.5
    lgm = jnp.where(is_grp, logits, NEG_BIG)
    mg = jnp.max(lgm, axis=-1, keepdims=True)
    p_grp = 1.0 / jnp.sum(jnp.exp(lgm - mg), axis=-1, keepdims=True)
    grp = jnp.min(jnp.where(lgm == mg, lane, 1e9), axis=-1, keepdims=True) - N_EXPERTS
    lo = grp * EXP_PER_GROUP
    in_grp = jnp.where(lane >= lo, jnp.where(lane < lo + EXP_PER_GROUP, 1.0, 0.0), 0.0) > 0.5
    lem = jnp.where(in_grp, logits, NEG_BIG)
    m1 = jnp.max(lem, axis=-1, keepdims=True)
    id1 = jnp.min(jnp.where(lem == m1, lane, 1e9), axis=-1, keepdims=True)
    lem2 = jnp.where(lane == id1, NEG_BIG, lem)
    m2 = jnp.max(lem2, axis=-1, keepdims=True)
    id2 = jnp.min(jnp.where(lem2 == m2, lane, 1e9), axis=-1, keepdims=True)
    u2 = jnp.exp(m2 - m1)
    w1 = p_grp / (1.0 + u2)
    w2 = p_grp * u2 / (1.0 + u2)
    oh1 = jnp.where(lane == id1, 1.0, 0.0)
    oh2 = jnp.where(lane == id2, 1.0, 0.0)
    oh = oh1 + oh2
    before = _dot(tril_ref[...], oh.astype(BF16)) + cnt_scr[...]
    rank1 = jnp.sum(before * oh1, axis=-1, keepdims=True)
    rank2 = jnp.sum(before * oh2, axis=-1, keepdims=True)
    cnt_scr[...] += jnp.sum(oh, axis=0, keepdims=True)
    cnt_ref[...] = jnp.broadcast_to(cnt_scr[...], cnt_ref.shape)
    route = jnp.zeros_like(logits)
    for ln, val in enumerate((w1, w2, id1, id2, rank1, rank2)):
        route = jnp.where(lane == ln, val, route)
    route_ref[...] = route


def _merge_call(tok, ya, yb, yc, z, mod, wa, wb, wc, wo, lng, lnb, wr, br, tm, modrow, alpha):
    na, d = yc.shape[0], tok[0].shape[1]
    split, tok_specs = _token_specs(tok, tm)
    row = lambda w: pl.BlockSpec((tm, w), lambda i: (i, 0))
    gate = lambda cb: pl.BlockSpec((tm, 512), lambda i: (i, cb))
    full = lambda a: pl.BlockSpec(a.shape, lambda i: (0,) * a.ndim)
    g0 = CB_GATE // 4
    tril = jnp.asarray(np.tril(np.ones((tm, tm), np.float32), -1), BF16)
    wr_hi = lax.reduce_precision(wr, exponent_bits=8, mantissa_bits=7)
    wr2 = jnp.concatenate([wr_hi, wr - wr_hi], axis=1).astype(BF16)
    return pl.pallas_call(
        functools.partial(_merge_body, alpha=alpha, split=split),
        grid=(na // tm,),
        in_specs=tok_specs + [row(512), row(512), row(512)] + [gate(g0 + j) for j in range(6)]
                 + [full(wa), full(wb), full(wc), full(wo), full(lng), full(lnb),
                    pl.BlockSpec((1, 6, d), lambda i: (modrow(i), 0, 0)), full(wr2), full(br), full(tril)],
        out_specs=[row(d), pl.BlockSpec((tm * ROW_SLABS, 128), lambda i: (i, 0)), row(128),
                   pl.BlockSpec((8, 128), lambda i: (0, 0))],
        out_shape=[jax.ShapeDtypeStruct((na, d), F32), jax.ShapeDtypeStruct((na * ROW_SLABS, 128), jnp.uint32),
                   jax.ShapeDtypeStruct((na, 128), F32), jax.ShapeDtypeStruct((8, 128), F32)],
        scratch_shapes=[pltpu.VMEM((1, 128), F32)],
        compiler_params=_cparams(("arbitrary",)),
        name="merge",
    )(tok[0], tok[1], ya, yb, yc, z, z, z, z, z, z, wa, wb, wc, wo, lng, lnb, mod, wr2, br, tril)


ROW_DMA_UNROLL = 8


def _slab(ref, row):
    start = row * ROW_SLABS
    if not isinstance(row, int):
        start = pl.multiple_of(start, ROW_SLABS)
    return ref.at[pl.ds(start, ROW_SLABS), :]


def _slots_body(route_ref, seg_ref, o_ref):
    route = route_ref[...]
    lane = lax.broadcasted_iota(jnp.int32, route.shape, 1).astype(F32)
    seg = seg_ref[...]
    slots = []
    for k in range(2):
        eid = route[:, 2 + k:3 + k]
        start = jnp.sum(jnp.where(lane == eid, seg, 0.0), axis=-1, keepdims=True)
        slots.append(start + route[:, 4 + k:5 + k])
    both = jnp.where(lane == 0.0, slots[0], jnp.where(lane == 1.0, slots[1], 0.0))
    by_token = both.T[:8].astype(jnp.int32)
    tm = o_ref.shape[2]
    for c in range(o_ref.shape[0]):
        o_ref[c] = by_token[:, c * tm:(c + 1) * tm]


def _slots_call(route, seg, tm):
    na = route.shape[0]
    group = _row_tile(4, na // tm)
    return pl.pallas_call(
        _slots_body,
        grid=(na // (tm * group),),
        in_specs=[pl.BlockSpec((tm * group, 128), lambda i: (i, 0)), pl.BlockSpec((1, 128), lambda i: (0, 0))],
        out_specs=pl.BlockSpec((group, 8, tm), lambda i: (i, 0, 0)),
        out_shape=jax.ShapeDtypeStruct((na // tm, 8, tm), jnp.int32),
        compiler_params=_cparams(("parallel",)),
        name="slots",
    )(route, seg)


def _dispatch_body(slot_ref, h_ref, xs_in, xs_out, sem):
    del xs_in
    tm = h_ref.shape[0] // ROW_SLABS

    def issue(i, carry):
        for u in range(ROW_DMA_UNROLL):
            t = i * ROW_DMA_UNROLL + u
            for k in range(2):
                pltpu.make_async_copy(_slab(h_ref, t), _slab(xs_out, slot_ref[0, k, t]), sem).start(priority=k)
        return carry

    lax.fori_loop(0, tm // ROW_DMA_UNROLL, issue, 0)

    for k in range(2):
        pltpu.make_async_copy(h_ref, xs_out.at[pl.ds(0, tm * ROW_SLABS), :], sem).wait()


def _dispatch_call(h2p, slots, n_rows, tm, xs_prev=None):
    xs0 = jnp.zeros((n_rows * ROW_SLABS, 128), h2p.dtype) if xs_prev is None else xs_prev
    assert xs0.shape[0] >= n_rows * ROW_SLABS
    return pl.pallas_call(
        _dispatch_body,
        grid=(slots.shape[0],),
        in_specs=[pl.BlockSpec((1, 8, tm), lambda i: (i, 0, 0), memory_space=pltpu.SMEM),
                  pl.BlockSpec((tm * ROW_SLABS, 128), lambda i: (i, 0)),
                  pl.BlockSpec(memory_space=pl.ANY)],
        out_specs=pl.BlockSpec(memory_space=pl.ANY),
        out_shape=jax.ShapeDtypeStruct(xs0.shape, xs0.dtype),
        input_output_aliases={2: 0},
        scratch_shapes=[pltpu.SemaphoreType.DMA(())],
        compiler_params=_cparams(("arbitrary",)),
        name="dispatch",
    )(slots, h2p, xs0)


def _experts_body(te_ref, nu_ref, xs_ref, wg_ref, wu_ref, wd_ref, y_ref, wg_scr, wu_scr, wd_scr):
    j = pl.program_id(0)

    @pl.when(j >= nu_ref[0])
    def _():
        y_ref[...] = jnp.zeros_like(y_ref)

    @pl.when(jnp.logical_and(j < nu_ref[0], jnp.logical_or(j == 0, te_ref[j] != te_ref[jnp.maximum(j - 1, 0)])))
    def _():
        wg_scr[...] = wg_ref[0].astype(BF16)
        wu_scr[...] = wu_ref[0].astype(BF16)
        wd_scr[...] = wd_ref[0].astype(BF16)

    @pl.when(j < nu_ref[0])
    def _():
        x = jnp.concatenate([c.astype(BF16) for c in _load_packed_rows(xs_ref)], axis=1)
        gt = _dot(x, wg_scr[...])
        a = gt * _sigmoid_t(gt) * _dot(x, wu_scr[...])
        _store_packed_rows(y_ref, _dot(a.astype(BF16), wd_scr[...]))


def _experts_call(tile_expert, n_used, xs, wg, wu, wd, layer, tr):
    rows, w = xs.shape
    _, ne, d, de = wg.shape
    tr = tr * ROW_SLABS
    used = lambda j, te, nu: jnp.minimum(j, nu[0] - 1)
    return pl.pallas_call(
        _experts_body,
        grid_spec=pltpu.PrefetchScalarGridSpec(
            num_scalar_prefetch=2,
            grid=(rows // tr,),
            in_specs=[pl.BlockSpec((tr, w), lambda j, te, nu: (used(j, te, nu), 0)),
                      pl.BlockSpec((None, 1, d, de), lambda j, te, nu: (layer, te[used(j, te, nu)], 0, 0)),
                      pl.BlockSpec((None, 1, d, de), lambda j, te, nu: (layer, te[used(j, te, nu)], 0, 0)),
                      pl.BlockSpec((None, 1, de, d), lambda j, te, nu: (layer, te[used(j, te, nu)], 0, 0))],
            out_specs=pl.BlockSpec((tr, w), lambda j, te, nu: (j, 0)),
            scratch_shapes=[pltpu.VMEM((d, de), BF16), pltpu.VMEM((d, de), BF16), pltpu.VMEM((de, d), BF16)]),
        out_shape=jax.ShapeDtypeStruct((rows, w), jnp.uint32),
        compiler_params=_cparams(("arbitrary",)),
        name="experts",
    )(tile_expert, n_used, xs, wg, wu, wd)


def _combine_body(slot_ref, next_ref, y_hbm, x1_ref, route_ref, lng_ref, lnb_ref, mod_ref, o_ref,
                  buf_scr, sems, *, alpha):
    tm = x1_ref.shape[0]
    i = pl.program_id(0)
    cur = i % 2

    def gather(idx_ref, b):
        def issue(j, carry):
            for u in range(ROW_DMA_UNROLL):
                t = j * ROW_DMA_UNROLL + u
                for k in range(2):
                    pltpu.make_async_copy(_slab(y_hbm, idx_ref[0, k, t]), _slab(buf_scr.at[b, k], t),
                                          sems.at[b]).start(priority=k)
            return carry

        lax.fori_loop(0, tm // ROW_DMA_UNROLL, issue, 0)

    @pl.when(i == 0)
    def _():
        gather(slot_ref, cur)

    @pl.when(i + 1 < pl.num_programs(0))
    def _():
        gather(next_ref, 1 - cur)

    for k in range(2):
        pltpu.make_async_copy(y_hbm.at[pl.ds(0, tm * ROW_SLABS), :], buf_scr.at[cur, k], sems.at[cur]).wait()

    w1 = route_ref[:, 0:1]
    w2 = route_ref[:, 1:2]
    moe = jnp.concatenate([w1 * a1 + w2 * a2 for a1, a2 in zip(_load_packed_rows(buf_scr.at[cur, 0]),
                                                               _load_packed_rows(buf_scr.at[cur, 1]))], axis=1)
    r = alpha * x1_ref[...] + mod_ref[0, 5:6, :] * moe
    o_ref[...] = _ln(r) * lng_ref[...] + lnb_ref[...]


def _combine_call(slots, y, x1, route, lng, lnb, mod, tm, modrow, alpha):
    na, d = x1.shape
    full = lambda a: pl.BlockSpec(a.shape, lambda i: (0,) * a.ndim)
    last = na // tm - 1
    return pl.pallas_call(
        functools.partial(_combine_body, alpha=alpha),
        grid=(na // tm,),
        in_specs=[pl.BlockSpec((1, 8, tm), lambda i: (i, 0, 0), memory_space=pltpu.SMEM),
                  pl.BlockSpec((1, 8, tm), lambda i: (jnp.minimum(i + 1, last), 0, 0), memory_space=pltpu.SMEM),
                  pl.BlockSpec(memory_space=pl.ANY),
                  pl.BlockSpec((tm, d), lambda i: (i, 0)),
                  pl.BlockSpec((tm, 128), lambda i: (i, 0)),
                  full(lng), full(lnb),
                  pl.BlockSpec((1, 6, d), lambda i: (modrow(i), 0, 0))],
        out_specs=pl.BlockSpec((tm, d), lambda i: (i, 0)),
        out_shape=jax.ShapeDtypeStruct((na, d), F32),
        scratch_shapes=[pltpu.VMEM((2, 2, tm * ROW_SLABS, 128), jnp.uint32), pltpu.SemaphoreType.DMA((2,))],
        compiler_params=_cparams(("arbitrary",)),
        name="combine",
    )(slots, slots, y, x1, route, lng, lnb, mod)


def _expert_buffer_tiles(n_tokens, tr):
    return -(-2 * n_tokens // tr) + N_EXPERTS


def _routing_tables(counts, n_tiles, tr):
    cnt = counts[0, :N_EXPERTS].astype(jnp.int32)
    ntile = (cnt + tr - 1) // tr
    tile_start = jnp.cumsum(ntile) - ntile
    n_used = jnp.sum(ntile)
    seg = jnp.zeros((1, 128), F32).at[0, :N_EXPERTS].set((tile_start * tr).astype(F32))
    tile_expert = jnp.sum(jnp.arange(n_tiles, dtype=jnp.int32)[:, None] >= tile_start[None, :], axis=1) - 1
    return seg, tile_expert.astype(jnp.int32), n_used.reshape(1).astype(jnp.int32)


def _lower_bounds(logits):
    p = jax.nn.softmax(logits.astype(F32), axis=0)
    return jnp.cumsum(p, axis=0) - p[:1]


def _row_tile(limit, *sizes):
    tm = limit
    while any(s % tm for s in sizes):
        tm //= 2
    return tm


def kernel(x, c, ctx, c_ctx, w_ada, b_ada, w_in, w_pool, pool_scale, lb_logits_fwd, lb_logits_bwd, hg_gain, rpb, w_br_a, w_br_b, w_br_c, w_out, ln1_g, ln1_b, w_rg, b_rg, w_re, b_re, w_gate, w_up, w_down, ln2_g, ln2_b):
    nbatch, t_len, d = x.shape
    c_len = ctx.shape[1]
    depth = w_ada.shape[0]
    assert c_len == SEQ_TILE and t_len % SEQ_TILE == 0 and t_len % GRID_W == 0
    alpha = (2.0 * depth) ** 0.25
    n_lat = nbatch * t_len
    nt = t_len // SEQ_TILE

    na = n_lat + nbatch * c_len
    tok = (x.reshape(n_lat, d), ctx.reshape(nbatch * c_len, d), 0)

    mod_rows = -(-(nbatch + 1) // 8) * 8
    cc = jnp.zeros((mod_rows, d), F32).at[:nbatch].set(c).at[nbatch].set(c_ctx)
    ada = _ada_call(cc, w_ada, b_ada)

    lb_f = _lower_bounds(lb_logits_fwd).reshape(depth, HG_HEADS, 1, HG_DK)
    lb_b = _lower_bounds(lb_logits_bwd).reshape(depth, HG_HEADS, 1, HG_DK)
    pool_tile = _row_tile(POOL_TILE, t_len, nbatch * c_len)
    assert pool_tile % c_len == 0
    pool_consts = _pool_consts(pool_tile, c_len)
    hg_f = _hgrn_consts(False)
    hg_b = _hgrn_consts(True)

    tm_big = _row_tile(1024, t_len, nbatch * c_len)
    tm_mid = _row_tile(512, t_len, nbatch * c_len)

    def modrow_for(tm):
        return lambda i: jnp.where(i * tm < n_lat, (i * tm) // t_len, nbatch)

    n_tiles = _expert_buffer_tiles(na, EXPERT_TILE)
    xs = None
    for l in range(depth):
        last = l == depth - 1
        mod = ada[l].reshape(mod_rows, 6, d)
        z = _inproj_call(tok, na, mod, w_in[l].astype(BF16), tm_big, modrow_for(tm_big))
        ya = _pool_call(z, pool_consts, w_pool[l].astype(BF16), pool_scale[l].reshape(1, -1), n_lat,
                        n_lat if last else n_lat + nbatch * c_len, t_len)
        o_f = _hgrn_call(z, lb_f[l], hg_f, nbatch, nt, reverse=False)
        yb = _hgrn_call(z, lb_b[l], hg_b, nbatch, nt, reverse=True, o_fwd=o_f,
                        gain=hg_gain[l].reshape(HG_HEADS, 1, HG_DK))
        bias = _na_bias_table(rpb[l], t_len // GRID_W)
        yc = _na_call(z, bias, nbatch, t_len, c_len, with_ctx=not last)
        wr = jnp.zeros((d, 128), F32).at[:, :N_EXPERTS].set(w_re[l]).at[:, N_EXPERTS:N_EXPERTS + N_GROUPS].set(w_rg[l])
        br = jnp.zeros((1, 128), F32).at[0, :N_EXPERTS].set(b_re[l]).at[0, N_EXPERTS:N_EXPERTS + N_GROUPS].set(b_rg[l])
        x1, h2p, route, counts = _merge_call(
            tok, ya, yb, yc, z, mod, w_br_a[l].astype(BF16), w_br_b[l].astype(BF16), w_br_c[l].astype(BF16),
            w_out[l].astype(BF16), ln1_g[l].reshape(1, d), ln1_b[l].reshape(1, d), wr, br, tm_mid,
            modrow_for(tm_mid), alpha)
        seg, tile_expert, n_used = _routing_tables(counts, n_tiles, EXPERT_TILE)
        slots = _slots_call(route, seg, tm_mid)
        xs = _dispatch_call(h2p, slots, n_tiles * EXPERT_TILE, tm_mid, xs)
        ys = _experts_call(tile_expert, n_used, xs, w_gate, w_up, w_down, l, EXPERT_TILE)
        xa = _combine_call(slots, ys, x1, route, ln2_g[l].reshape(1, d), ln2_b[l].reshape(1, d), mod,
                           tm_mid, modrow_for(tm_mid), alpha)
        tok = (xa, xa, n_lat)
    return xa.reshape(nbatch, t_len, d)
```

```python
import functools

import numpy as np
import jax
import jax.numpy as jnp
from jax import lax
from jax.experimental import pallas as pl
from jax.experimental.pallas import tpu as pltpu

F32 = jnp.float32
BF16 = jnp.bfloat16
HIGHEST = lax.Precision.HIGHEST

GRID_W = 64
POOL_WINDOWS = (2, 4, 8, 16)
POOL_GDIM = 128
HG_HEADS = 4
HG_DK = 128
HG_BLOCK = 16
NA_HEADS = 8
NA_HD = 64
NA_ROWS = 8
NA_COLS = 16
NA_QROWS = 4
NA_KROWS = 12
N_GROUPS = 4
EXP_PER_GROUP = 8
N_EXPERTS = N_GROUPS * EXP_PER_GROUP
LN_EPS = 1e-5
RMS_EPS = 1e-6
NEG_BIG = -1e30
SEQ_TILE = 256
EXPERT_TILE = 512
POOL_TILE = 512
VMEM_LIMIT = 56 * 1024 * 1024

CB_A, CB_Q, CB_FF, CB_FB, CB_I, CB_G, CB_NQ, CB_NK, CB_NV, CB_GATE = 0, 4, 8, 12, 16, 20, 24, 28, 32, 36


def _cparams(sem):
    return pltpu.CompilerParams(dimension_semantics=sem, vmem_limit_bytes=VMEM_LIMIT)


def _ln(x):
    mu = jnp.mean(x, axis=-1, keepdims=True)
    xc = x - mu
    var = jnp.mean(xc * xc, axis=-1, keepdims=True)
    return xc * lax.rsqrt(var + LN_EPS)


def _sigmoid(x):
    return 1.0 / (1.0 + jnp.exp(-x))


def _sigmoid_t(x):
    return 0.5 * jnp.tanh(0.5 * x) + 0.5


def _dot(a, b):
    return jnp.dot(a, b, preferred_element_type=F32)


def _dot_nt(a, b):
    return lax.dot_general(a, b, (((1,), (1,)), ((), ())), preferred_element_type=F32)


def _dot_tn(a, b):
    return lax.dot_general(a, b, (((0,), (0,)), ((), ())), preferred_element_type=F32)


def _dot01(m01, x, pieces=3):
    x1 = x.astype(BF16)
    r1 = x - x1.astype(F32)
    x2 = r1.astype(BF16)
    out = _dot(m01, x1) + _dot(m01, x2)
    if pieces == 3:
        out = out + _dot(m01, (r1 - x2.astype(F32)).astype(BF16))
    return out


def _ada_body(c_ref, w_ref, b_ref, o_ref):
    cs = c_ref[...]
    s = cs * _sigmoid(cs)
    o_ref[0] = jnp.dot(s, w_ref[0], preferred_element_type=F32, precision=HIGHEST) + b_ref[0]


def _ada_call(cc, w_ada, b_ada):
    depth, d, n6 = w_ada.shape
    rows = cc.shape[0]
    return pl.pallas_call(
        _ada_body,
        grid=(depth, n6 // d),
        in_specs=[pl.BlockSpec((rows, d), lambda l, j: (0, 0)),
                  pl.BlockSpec((1, d, d), lambda l, j: (l, 0, j)),
                  pl.BlockSpec((1, 1, d), lambda l, j: (l, 0, j))],
        out_specs=pl.BlockSpec((1, rows, d), lambda l, j: (l, 0, j)),
        out_shape=jax.ShapeDtypeStruct((depth, rows, n6), F32),
        compiler_params=_cparams(("parallel", "parallel")),
        name="ada",
    )(cc, w_ada, b_ada.reshape(depth, 1, n6))


def _token_specs(tok, tm):
    lat, ctx, ctx_row0 = tok
    split = (ctx_row0 if ctx is lat else lat.shape[0]) // tm
    off = ctx_row0 // tm
    d = lat.shape[1]
    lat_map = lambda i, *_: (jnp.minimum(i, split - 1), 0)
    ctx_map = lambda i, *_: (jnp.maximum(i - split, 0) + off, 0)
    return split, [pl.BlockSpec((tm, d), lat_map), pl.BlockSpec((tm, d), ctx_map)]


def _inproj_body(xl_ref, xc_ref, mod_ref, w_ref, z_ref, h_scr, *, split):
    def modulated(x_ref):
        h = _ln(x_ref[...]) * (1.0 + mod_ref[0, 1:2, :]) + mod_ref[0, 0:1, :]
        h_scr[...] = h.astype(BF16)

    first = pl.program_id(1) == 0
    pl.when(jnp.logical_and(first, pl.program_id(0) < split))(lambda: modulated(xl_ref))
    pl.when(jnp.logical_and(first, pl.program_id(0) >= split))(lambda: modulated(xc_ref))
    z_ref[...] = _dot(h_scr[...], w_ref[...]).astype(z_ref.dtype)


def _inproj_call(tok, na, mod, w_in, tm, modrow):
    d, d_in = w_in.shape
    tn = 1536
    assert d_in % tn == 0
    split, tok_specs = _token_specs(tok, tm)
    return pl.pallas_call(
        functools.partial(_inproj_body, split=split),
        grid=(na // tm, d_in // tn),
        in_specs=tok_specs + [pl.BlockSpec((1, 6, d), lambda i, j: (modrow(i), 0, 0)),
                              pl.BlockSpec((d, tn), lambda i, j: (0, j))],
        out_specs=pl.BlockSpec((tm, tn), lambda i, j: (i, j)),
        out_shape=jax.ShapeDtypeStruct((na, d_in), BF16),
        scratch_shapes=[pltpu.VMEM((tm, d), BF16)],
        compiler_params=_cparams(("parallel", "arbitrary")),
        name="inproj",
    )(tok[0], tok[1], mod, w_in)


def _pool_consts(n, c_len):
    t = np.arange(n)[:, None]
    bc = np.zeros((2, 4, n, n), np.float32)
    bp = np.zeros((4, n, 16), np.float32)
    bn = np.zeros((4, n, 16), np.float32)
    for g, win in enumerate(POOL_WINDOWS):
        lo, hi = t - win // 2, t + win // 2 - 1
        s = np.arange(n)[None, :]
        bc[0, g] = (s >= lo) & (s <= hi)
        bc[1, g] = bc[0, g] * (s // c_len == t // c_len)
        s = np.arange(16)[None, :] - 16
        bp[g] = (s >= lo) & (s <= hi)
        s = np.arange(16)[None, :] + n
        bn[g] = (s >= lo) & (s <= hi)
    cnt = np.stack([np.stack([bc[0].sum(-1), bp.sum(-1), bn.sum(-1)], axis=1),
                    np.stack([bc[1].sum(-1), 0 * bp.sum(-1), 0 * bn.sum(-1)], axis=1)])
    cnt = np.broadcast_to(cnt[..., None], (2, 4, 3, n, 128)).astype(np.float32)
    return (jnp.asarray(bc, BF16), jnp.asarray(bp, BF16), jnp.asarray(bn, BF16), jnp.asarray(cnt))


def _pool_body(prev_ref, cur_ref, next_ref, bc_ref, bp_ref, bn_ref, cnt_ref, wp_ref, ps_ref, o_ref,
               *, nt, n_lat_tiles):
    i = pl.program_id(0)
    k = i % nt
    lat = i < n_lat_tiles
    has_prev = jnp.where(jnp.logical_and(lat, k != 0), 1.0, 0.0).astype(F32)
    has_next = jnp.where(jnp.logical_and(lat, k != nt - 1), 1.0, 0.0).astype(F32)
    for g in range(len(POOL_WINDOWS)):
        sl = slice(g * POOL_GDIM, (g + 1) * POOL_GDIM)
        u = cur_ref[:, sl]
        ssum = (_dot(bc_ref[0, g], u) + has_prev * _dot(bp_ref[g], prev_ref[:, sl])
                + has_next * _dot(bn_ref[g], next_ref[:, sl]))
        cnt = cnt_ref[0, g, 0] + has_prev * cnt_ref[0, g, 1] + has_next * cnt_ref[0, g, 2]
        dlt = ssum / cnt - u.astype(F32)
        y = _dot(dlt.astype(BF16), wp_ref[g]) * ps_ref[:, sl]
        o_ref[:, sl] = y.astype(o_ref.dtype)


def _pool_call(z, consts, w_pool, pool_scale, n_lat, n_rows, t_len):
    bc, bp, bn, cnt = consts
    n = bc.shape[2]
    assert t_len % n == 0 and n_lat % n == 0 and n_rows % n == 0
    hb = n // 16
    last16 = z.shape[0] // 16 - 1
    n_lat_tiles = n_lat // n
    full = lambda a: pl.BlockSpec(a.shape, lambda i: (0,) * a.ndim)
    variant = lambda a: pl.BlockSpec((1,) + a.shape[1:],
                                     lambda i: (jnp.where(i < n_lat_tiles, 0, 1),) + (0,) * (a.ndim - 1))
    return pl.pallas_call(
        functools.partial(_pool_body, nt=t_len // n, n_lat_tiles=n_lat_tiles),
        grid=(n_rows // n,),
        in_specs=[pl.BlockSpec((16, 512), lambda i: (jnp.maximum(i * hb - 1, 0), CB_A // 4)),
                  pl.BlockSpec((n, 512), lambda i: (i, CB_A // 4)),
                  pl.BlockSpec((16, 512), lambda i: (jnp.minimum((i + 1) * hb, last16), CB_A // 4)),
                  variant(bc), full(bp), full(bn), variant(cnt), full(w_pool), full(pool_scale)],
        out_specs=pl.BlockSpec((n, 512), lambda i: (i, 0)),
        out_shape=jax.ShapeDtypeStruct((n_rows, 512), BF16),
        compiler_params=_cparams(("parallel",)),
        name="pool",
    )(z, z, z, bc, bp, bn, cnt, w_pool, pool_scale)


def _hgrn_consts(reverse):
    n, bs = SEQ_TILE, HG_BLOCK
    nb = n // bs
    t = np.arange(n)
    o = (n - 1 - t) if reverse else t
    blk = t // bs
    jb = np.arange(nb)
    ob = (nb - 1 - jb) if reverse else jb
    cum = ((blk[:, None] == blk[None, :]) & (o[None, :] <= o[:, None])).astype(np.float32)
    bsum = (jb[:, None] == blk[None, :]).astype(np.float32)
    widths = [2 ** l for l in range(1, int(np.log2(nb)) + 1)]
    lvl = np.full((n, n), -1, np.int32)
    obt = ob[blk]
    same = blk[:, None] == blk[None, :]
    lvl[same & (o[None, :] <= o[:, None])] = 0
    for li, w in reversed(list(enumerate(widths, start=1))):
        m = (obt[:, None] // w == obt[None, :] // w) & (obt[None, :] < obt[:, None]) & ~same
        lvl[m] = li
    mats = []
    for w in widths:
        mid = (ob // w) * w + w // 2
        mats.append((mid[:, None] <= ob[None, :]) & (ob[None, :] < ob[:, None]))
    for w in widths:
        mid = (ob // w) * w + w // 2
        mats.append((ob[:, None] < ob[None, :]) & (ob[None, :] < mid[:, None]))
    mats.append(ob[None, :] < ob[:, None])
    mats.append(ob[None, :] > ob[:, None])
    mats.append(np.ones((nb, nb), bool))
    tsm = np.concatenate(mats, axis=0).astype(np.float32)
    return (jnp.asarray(cum, BF16), jnp.asarray(bsum, BF16), jnp.asarray(lvl, BF16), jnp.asarray(tsm, BF16),
            len(widths))


def _expand_blocks(c, n):
    nb, lanes = c.shape
    return jnp.concatenate([jnp.broadcast_to(c[j:j + 1, :], (n // nb, lanes)) for j in range(nb)], axis=0)


def _hgrn_body(*refs, n_levels, final):
    if final:
        (zq_ref, zf_ref, zi_ref, lb_ref, cum_ref, bsum_ref, lvl_ref, tsm_ref,
         of_ref, zg_ref, gain_ref, o_ref, st_scr) = refs
    else:
        zq_ref, zf_ref, zi_ref, lb_ref, cum_ref, bsum_ref, lvl_ref, tsm_ref, o_ref, st_scr = refs
    n = zq_ref.shape[0]
    nb = n // HG_BLOCK

    @pl.when(pl.program_id(1) == 0)
    def _():
        st_scr[...] = jnp.zeros_like(st_scr)

    zq = zq_ref[...].astype(F32)
    zf = zf_ref[...].astype(F32)
    lb = jnp.concatenate([lb_ref[h] for h in range(HG_HEADS)], axis=1)
    sig = _sigmoid(zf)
    lf = jnp.log(lb + (1.0 - lb) * sig)
    k_all = (1.0 - lb) * (1.0 - sig)
    q_all = zq * _sigmoid_t(zq)
    lf_hi = lf.astype(BF16)
    lf2 = jnp.concatenate([lf_hi, (lf - lf_hi.astype(F32)).astype(BF16)], axis=1)
    w = HG_HEADS * HG_DK
    b2 = _dot(cum_ref[...], lf2)
    b_all = b2[:, :w] + b2[:, w:]
    t2 = _dot(bsum_ref[...], lf2)
    tot_all = t2[:, :w] + t2[:, w:]
    coef_all = _dot01(tsm_ref[...], tot_all)
    qd_all = q_all * jnp.exp(b_all)
    kd_all = (k_all * jnp.exp(-b_all)).astype(BF16)
    ks_all = k_all * jnp.exp(_expand_blocks(tot_all, n) - b_all)

    ecoef = jnp.exp(coef_all)
    scale = lambda idx: _expand_blocks(ecoef[idx * nb:(idx + 1) * nb], n)
    q_lv = [qd_all.astype(BF16)] + [(qd_all * scale(li)).astype(BF16) for li in range(n_levels)]
    k_lv = [kd_all] + [(ks_all * scale(n_levels + li)).astype(BF16) for li in range(n_levels)]
    qs_all = (qd_all * scale(2 * n_levels)).astype(BF16)
    kn_all = (ks_all * scale(2 * n_levels + 1)).astype(BF16)
    dec_all = ecoef[(2 * n_levels + 2) * nb:(2 * n_levels + 2) * nb + 1]

    for h in range(HG_HEADS):
        sl = slice(h * HG_DK, (h + 1) * HG_DK)
        v = zi_ref[:, sl]

        lvl = lvl_ref[...]
        a = jnp.zeros((n, n), BF16)
        for li in range(n_levels + 1):
            a = jnp.where(lvl == float(li), _dot_nt(q_lv[li][:, sl], k_lv[li][:, sl]).astype(BF16), a)
        o = _dot(a, v)

        st = st_scr[h]
        o = o + _dot_nt(qs_all[:, sl], st.astype(BF16))
        st_scr[h] = st * dec_all[:, sl] + _dot_tn(v, kn_all[:, sl])

        if final:
            o = o + of_ref[:, sl]
            o = o * lax.rsqrt(jnp.mean(o * o, axis=-1, keepdims=True) + RMS_EPS) * gain_ref[h]
            zg = zg_ref[:, sl].astype(F32)
            o_ref[:, sl] = (o * (zg * _sigmoid_t(zg))).astype(o_ref.dtype)
        else:
            o_ref[:, sl] = o


def _hgrn_call(z, lb, consts, nbatch, nt, reverse, o_fwd=None, gain=None):
    na = z.shape[0]
    n = SEQ_TILE
    cum, bsum, lvl, tsm, n_levels = consts
    ctx_base = nbatch * nt
    final = o_fwd is not None

    def tile(b, s):
        lat = (b * nt + nt - s) if reverse else (b * nt + s - 1)
        return jnp.where(s == 0, ctx_base + b, lat)

    width = HG_HEADS * HG_DK

    def col(cb):
        return pl.BlockSpec((n, width), lambda b, s: (tile(b, s), cb // HG_HEADS))

    full = lambda a: pl.BlockSpec(a.shape, lambda b, s: (0,) * a.ndim)
    in_specs = [col(CB_Q), col(CB_FB if reverse else CB_FF), col(CB_I), full(lb),
                full(cum), full(bsum), full(lvl), full(tsm)]
    args = [z, z, z, lb, cum, bsum, lvl, tsm]
    if final:
        in_specs += [col(0), col(CB_G), full(gain)]
        args += [o_fwd, z, gain]
    return pl.pallas_call(
        functools.partial(_hgrn_body, n_levels=n_levels, final=final),
        grid=(nbatch, nt + 1),
        in_specs=in_specs,
        out_specs=col(0),
        out_shape=jax.ShapeDtypeStruct((na, width), BF16 if final else F32),
        scratch_shapes=[pltpu.VMEM((HG_HEADS, HG_DK, HG_DK), F32)],
        compiler_params=_cparams(("parallel", "arbitrary")),
        name="hgrn_bwd" if final else "hgrn_fwd",
    )(*args)


def _na_bias_table(rpb, rows):
    nrb = rows // NA_QROWS
    assert nrb >= 3 and rows >= NA_KROWS
    kr = min(NA_ROWS, rows)
    qc = np.arange(GRID_W)
    c0 = np.clip(qc - NA_COLS // 2, 0, GRID_W - NA_COLS)
    kc = np.arange(GRID_W)
    col_ok = (kc[None, :] >= c0[:, None]) & (kc[None, :] < c0[:, None] + NA_COLS)
    dc = np.clip(kc[None, :] - qc[:, None] + NA_COLS - 1, 0, 2 * NA_COLS - 2)
    sel_c = (dc[..., None] == np.arange(2 * NA_COLS - 1)).astype(np.float32)
    sel_r, oks = [], []
    for rb in (0, 1, nrb - 1):
        start = int(np.clip(NA_QROWS * rb - 4, 0, rows - NA_KROWS))
        r = NA_QROWS * rb + np.arange(NA_QROWS)
        r0 = np.clip(r - kr // 2, 0, rows - kr)
        keyrow = start + np.arange(NA_KROWS)
        row_ok = (keyrow[None, :] >= r0[:, None]) & (keyrow[None, :] < r0[:, None] + kr)
        dr = np.clip(keyrow[None, :] - r[:, None] + NA_ROWS - 1, 0, 2 * NA_ROWS - 2)
        sel_r.append((dr[..., None] == np.arange(2 * NA_ROWS - 1)).astype(np.float32))
        oks.append(row_ok[:, None, :, None] & col_ok[None, :, None, :])
    sel_r.append(np.zeros_like(sel_r[0]))
    oks.append(np.zeros_like(oks[0]))
    bias = jnp.einsum("hij,paki,cdj->phackd", rpb.astype(F32), jnp.asarray(np.stack(sel_r)),
                      jnp.asarray(sel_c), precision=HIGHEST)
    bias = jnp.where(jnp.asarray(np.stack(oks))[:, None], bias, NEG_BIG)
    return bias.reshape(4, NA_HEADS, NA_QROWS * GRID_W, NA_KROWS * GRID_W)


def _na_body(q_ref, k_ref, v_ref, kc_ref, vc_ref, bias_ref, o_ref, *, rows):
    rb = pl.program_id(1)
    nk = NA_KROWS * GRID_W
    start_row = jnp.clip(NA_QROWS * rb - 4, 0, rows - NA_KROWS)
    start = pl.multiple_of(start_row * GRID_W, GRID_W)
    nq = q_ref.shape[0]
    lane = lax.broadcasted_iota(jnp.int32, (nq, 128), 1)
    scale = NA_HD ** -0.5
    for p in range(NA_HEADS // 2):
        sl = slice(128 * p, 128 * (p + 1))
        qp = q_ref[:, sl] * scale
        kp = k_ref[pl.ds(start, nk), sl]
        vp = v_ref[pl.ds(start, nk), sl]
        kcp = kc_ref[:, sl]
        vcp = vc_ref[:, sl]
        zero = jnp.zeros_like(qp)
        q2 = jnp.concatenate([jnp.where(lane < NA_HD, qp, zero), jnp.where(lane >= NA_HD, qp, zero)], axis=0)
        s_loc = _dot_nt(q2, kp) + bias_ref[0, 2 * p:2 * p + 2].reshape(2 * nq, nk)
        s_ctx = _dot_nt(q2, kcp)
        m = jnp.maximum(jnp.max(s_loc, axis=-1, keepdims=True), jnp.max(s_ctx, axis=-1, keepdims=True))
        p_loc = jnp.exp(s_loc - m)
        p_ctx = jnp.exp(s_ctx - m)
        den = jnp.sum(p_loc, axis=-1, keepdims=True) + jnp.sum(p_ctx, axis=-1, keepdims=True)
        o2 = (_dot(p_loc.astype(BF16), vp) + _dot(p_ctx.astype(BF16), vcp)) / den
        o_ref[:, sl] = jnp.where(lane < NA_HD, o2[:nq], o2[nq:]).astype(o_ref.dtype)


def _na_call(z, bias, nbatch, t_len, c_len, with_ctx):
    na = z.shape[0]
    rows = t_len // GRID_W
    nrb = rows // NA_QROWS
    nq = NA_QROWS * GRID_W
    assert nq == c_len
    ctx_base = nbatch * nrb
    steps = nrb + 1 if with_ctx else nrb

    def qtile(b, r):
        return jnp.where(r < nrb, b * nrb + r, ctx_base + b)

    def pattern(b, r):
        return jnp.where(r == 0, 0, jnp.where(r == nrb - 1, 2, jnp.where(r == nrb, 3, 1)))

    return pl.pallas_call(
        functools.partial(_na_body, rows=rows),
        grid=(nbatch, steps),
        in_specs=[pl.BlockSpec((nq, 512), lambda b, r: (qtile(b, r), CB_NQ // 4)),
                  pl.BlockSpec((t_len, 512), lambda b, r: (b, CB_NK // 4)),
                  pl.BlockSpec((t_len, 512), lambda b, r: (b, CB_NV // 4)),
                  pl.BlockSpec((c_len, 512), lambda b, r: (ctx_base + b, CB_NK // 4)),
                  pl.BlockSpec((c_len, 512), lambda b, r: (ctx_base + b, CB_NV // 4)),
                  pl.BlockSpec((1,) + bias.shape[1:], lambda b, r: (pattern(b, r), 0, 0, 0))],
        out_specs=pl.BlockSpec((nq, 512), lambda b, r: (qtile(b, r), 0)),
        out_shape=jax.ShapeDtypeStruct((na if with_ctx else nbatch * t_len, 512), BF16),
        compiler_params=_cparams(("parallel", "arbitrary")),
        name="natten",
    )(z, z, z, z, z, bias)


ROW_SLABS = 4


def _store_packed_rows(ref, x, row0=0):
    m = x.shape[0]

    def bits(v):
        return lax.bitcast_convert_type(v.astype(BF16).astype(F32), jnp.uint32)

    for s in range(ROW_SLABS):
        lo = x[:, 128 * s:128 * (s + 1)]
        hi = x[:, 512 + 128 * s:512 + 128 * (s + 1)]
        ref[pl.ds(row0 * ROW_SLABS + s, m, stride=ROW_SLABS), :] = ((bits(hi) & jnp.uint32(0xFFFF0000))
                                                                    | (bits(lo) >> 16))


def _load_packed_rows(ref):
    m = ref.shape[0] // ROW_SLABS
    los, his = [], []
    for s in range(ROW_SLABS):
        p = ref[pl.ds(s, m, stride=ROW_SLABS), :]
        los.append(lax.bitcast_convert_type(p << 16, F32))
        his.append(lax.bitcast_convert_type(p & jnp.uint32(0xFFFF0000), F32))
    return los + his


def _merge_body(xl_ref, xc_ref, ya_ref, yb_ref, yc_ref, g0, g1, g2, g3, g4, g5, wa_ref, wb_ref, wc_ref, wo_ref,
                lng_ref, lnb_ref, mod_ref, wr2_ref, br_ref, tril_ref, x1_ref, h2_ref, route_ref, cnt_ref,
                cnt_scr, *, alpha, split):
    @pl.when(pl.program_id(0) == 0)
    def _():
        cnt_scr[...] = jnp.zeros_like(cnt_scr)

    gates = ((g0, g1), (g2, g3), (g4, g5))
    yrefs = (ya_ref, yb_ref, yc_ref)
    ws = (wa_ref, wb_ref, wc_ref)
    half = wa_ref.shape[1] // 2
    is_lat = pl.program_id(0) < split

    def token_rows(r0, r1):
        mix = None
        for n in range(2):
            m = None
            for kbr in range(3):
                pr = _dot(yrefs[kbr][r0:r1, :], ws[kbr][:, n * half:(n + 1) * half])
                term = _sigmoid_t(gates[kbr][n][r0:r1, :].astype(F32)) * pr
                m = term if m is None else m + term
            part = _dot(m.astype(BF16), wo_ref[n * half:(n + 1) * half, :])
            mix = part if mix is None else mix + part
        x = jnp.where(is_lat, xl_ref[r0:r1, :], xc_ref[r0:r1, :])
        r = alpha * x + mod_ref[0, 2:3, :] * mix
        x1 = _ln(r) * lng_ref[...] + lnb_ref[...]
        x1_ref[r0:r1, :] = x1
        h2 = _ln(x1) * (1.0 + mod_ref[0, 4:5, :]) + mod_ref[0, 3:4, :]
        _store_packed_rows(h2_ref, h2, r0)
        h2_hi = h2.astype(BF16)
        h2_lo = (h2 - h2_hi.astype(F32)).astype(BF16)
        hh = _dot(h2_hi, wr2_ref[...])
        return (hh[:, :128] + _dot(h2_lo, wr2_ref[:, :128]) + hh[:, 128:]) + br_ref[...]

    logits = token_rows(0, x1_ref.shape[0])
    lane = lax.broadcasted_iota(jnp.int32, logits.shape, 1).astype(F32)
    is_grp = jnp.where(lane >= N_EXPERTS, jnp.where(lane < N_EXPERTS + N_GROUPS, 1.0, 0.0), 0.0) > 0.5
    lgm = jnp.where(is_grp, logits, NEG_BIG)
    mg = jnp.max(lgm, axis=-1, keepdims=True)
    p_grp = 1.0 / jnp.sum(jnp.exp(lgm - mg), axis=-1, keepdims=True)
    grp = jnp.min(jnp.where(lgm == mg, lane, 1e9), axis=-1, keepdims=True) - N_EXPERTS
    lo = grp * EXP_PER_GROUP
    in_grp = jnp.where(lane >= lo, jnp.where(lane < lo + EXP_PER_GROUP, 1.0, 0.0), 0.0) > 0.5
    lem = jnp.where(in_grp, logits, NEG_BIG)
    m1 = jnp.max(lem, axis=-1, keepdims=True)
    id1 = jnp.min(jnp.where(lem == m1, lane, 1e9), axis=-1, keepdims=True)
    lem2 = jnp.where(lane == id1, NEG_BIG, lem)
    m2 = jnp.max(lem2, axis=-1, keepdims=True)
    id2 = jnp.min(jnp.where(lem2 == m2, lane, 1e9), axis=-1, keepdims=True)
    u2 = jnp.exp(m2 - m1)
    w1 = p_grp / (1.0 + u2)
    w2 = p_grp * u2 / (1.0 + u2)
    oh1 = jnp.where(lane == id1, 1.0, 0.0)
    oh2 = jnp.where(lane == id2, 1.0, 0.0)
    oh = oh1 + oh2
    before = _dot(tril_ref[...], oh.astype(BF16)) + cnt_scr[...]
    rank1 = jnp.sum(before * oh1, axis=-1, keepdims=True)
    rank2 = jnp.sum(before * oh2, axis=-1, keepdims=True)
    cnt_scr[...] += jnp.sum(oh, axis=0, keepdims=True)
    cnt_ref[...] = jnp.broadcast_to(cnt_scr[...], cnt_ref.shape)
    route = jnp.zeros_like(logits)
    for ln, val in enumerate((w1, w2, id1, id2, rank1, rank2)):
        route = jnp.where(lane == ln, val, route)
    route_ref[...] = route


def _merge_call(tok, ya, yb, yc, z, mod, wa, wb, wc, wo, lng, lnb, wr, br, tm, modrow, alpha):
    na, d = yc.shape[0], tok[0].shape[1]
    split, tok_specs = _token_specs(tok, tm)
    row = lambda w: pl.BlockSpec((tm, w), lambda i: (i, 0))
    gate = lambda cb: pl.BlockSpec((tm, 512), lambda i: (i, cb))
    full = lambda a: pl.BlockSpec(a.shape, lambda i: (0,) * a.ndim)
    g0 = CB_GATE // 4
    tril = jnp.asarray(np.tril(np.ones((tm, tm), np.float32), -1), BF16)
    wr_hi = lax.reduce_precision(wr, exponent_bits=8, mantissa_bits=7)
    wr2 = jnp.concatenate([wr_hi, wr - wr_hi], axis=1).astype(BF16)
    return pl.pallas_call(
        functools.partial(_merge_body, alpha=alpha, split=split),
        grid=(na // tm,),
        in_specs=tok_specs + [row(512), row(512), row(512)] + [gate(g0 + j) for j in range(6)]
                 + [full(wa), full(wb), full(wc), full(wo), full(lng), full(lnb),
                    pl.BlockSpec((1, 6, d), lambda i: (modrow(i), 0, 0)), full(wr2), full(br), full(tril)],
        out_specs=[row(d), pl.BlockSpec((tm * ROW_SLABS, 128), lambda i: (i, 0)), row(128),
                   pl.BlockSpec((8, 128), lambda i: (0, 0))],
        out_shape=[jax.ShapeDtypeStruct((na, d), F32), jax.ShapeDtypeStruct((na * ROW_SLABS, 128), jnp.uint32),
                   jax.ShapeDtypeStruct((na, 128), F32), jax.ShapeDtypeStruct((8, 128), F32)],
        scratch_shapes=[pltpu.VMEM((1, 128), F32)],
        compiler_params=_cparams(("arbitrary",)),
        name="merge",
    )(tok[0], tok[1], ya, yb, yc, z, z, z, z, z, z, wa, wb, wc, wo, lng, lnb, mod, wr2, br, tril)


ROW_DMA_UNROLL = 8


def _slab(ref, row):
    start = row * ROW_SLABS
    if not isinstance(row, int):
        start = pl.multiple_of(start, ROW_SLABS)
    return ref.at[pl.ds(start, ROW_SLABS), :]


def _slots_body(route_ref, seg_ref, o_ref):
    route = route_ref[...]
    lane = lax.broadcasted_iota(jnp.int32, route.shape, 1).astype(F32)
    seg = seg_ref[...]
    slots = []
    for k in range(2):
        eid = route[:, 2 + k:3 + k]
        start = jnp.sum(jnp.where(lane == eid, seg, 0.0), axis=-1, keepdims=True)
        slots.append(start + route[:, 4 + k:5 + k])
    both = jnp.where(lane == 0.0, slots[0], jnp.where(lane == 1.0, slots[1], 0.0))
    by_token = both.T[:8].astype(jnp.int32)
    tm = o_ref.shape[2]
    for c in range(o_ref.shape[0]):
        o_ref[c] = by_token[:, c * tm:(c + 1) * tm]


def _slots_call(route, seg, tm):
    na = route.shape[0]
    group = _row_tile(4, na // tm)
    return pl.pallas_call(
        _slots_body,
        grid=(na // (tm * group),),
        in_specs=[pl.BlockSpec((tm * group, 128), lambda i: (i, 0)), pl.BlockSpec((1, 128), lambda i: (0, 0))],
        out_specs=pl.BlockSpec((group, 8, tm), lambda i: (i, 0, 0)),
        out_shape=jax.ShapeDtypeStruct((na // tm, 8, tm), jnp.int32),
        compiler_params=_cparams(("parallel",)),
        name="slots",
    )(route, seg)


def _dispatch_body(slot_ref, h_ref, xs_in, xs_out, sem):
    del xs_in
    tm = h_ref.shape[0] // ROW_SLABS

    def issue(i, carry):
        for u in range(ROW_DMA_UNROLL):
            t = i * ROW_DMA_UNROLL + u
            for k in range(2):
                pltpu.make_async_copy(_slab(h_ref, t), _slab(xs_out, slot_ref[0, k, t]), sem).start(priority=k)
        return carry

    lax.fori_loop(0, tm // ROW_DMA_UNROLL, issue, 0)

    for k in range(2):
        pltpu.make_async_copy(h_ref, xs_out.at[pl.ds(0, tm * ROW_SLABS), :], sem).wait()


def _dispatch_call(h2p, slots, n_rows, tm, xs_prev=None):
    xs0 = jnp.zeros((n_rows * ROW_SLABS, 128), h2p.dtype) if xs_prev is None else xs_prev
    assert xs0.shape[0] >= n_rows * ROW_SLABS
    return pl.pallas_call(
        _dispatch_body,
        grid=(slots.shape[0],),
        in_specs=[pl.BlockSpec((1, 8, tm), lambda i: (i, 0, 0), memory_space=pltpu.SMEM),
                  pl.BlockSpec((tm * ROW_SLABS, 128), lambda i: (i, 0)),
                  pl.BlockSpec(memory_space=pl.ANY)],
        out_specs=pl.BlockSpec(memory_space=pl.ANY),
        out_shape=jax.ShapeDtypeStruct(xs0.shape, xs0.dtype),
        input_output_aliases={2: 0},
        scratch_shapes=[pltpu.SemaphoreType.DMA(())],
        compiler_params=_cparams(("arbitrary",)),
        name="dispatch",
    )(slots, h2p, xs0)


def _experts_body(te_ref, nu_ref, xs_ref, wg_ref, wu_ref, wd_ref, y_ref, wg_scr, wu_scr, wd_scr):
    j = pl.program_id(0)

    @pl.when(j >= nu_ref[0])
    def _():
        y_ref[...] = jnp.zeros_like(y_ref)

    @pl.when(jnp.logical_and(j < nu_ref[0], jnp.logical_or(j == 0, te_ref[j] != te_ref[jnp.maximum(j - 1, 0)])))
    def _():
        wg_scr[...] = wg_ref[0].astype(BF16)
        wu_scr[...] = wu_ref[0].astype(BF16)
        wd_scr[...] = wd_ref[0].astype(BF16)

    @pl.when(j < nu_ref[0])
    def _():
        x = jnp.concatenate([c.astype(BF16) for c in _load_packed_rows(xs_ref)], axis=1)
        gt = _dot(x, wg_scr[...])
        a = gt * _sigmoid_t(gt) * _dot(x, wu_scr[...])
        _store_packed_rows(y_ref, _dot(a.astype(BF16), wd_scr[...]))


def _experts_call(tile_expert, n_used, xs, wg, wu, wd, layer, tr):
    rows, w = xs.shape
    _, ne, d, de = wg.shape
    tr = tr * ROW_SLABS
    used = lambda j, te, nu: jnp.minimum(j, nu[0] - 1)
    return pl.pallas_call(
        _experts_body,
        grid_spec=pltpu.PrefetchScalarGridSpec(
            num_scalar_prefetch=2,
            grid=(rows // tr,),
            in_specs=[pl.BlockSpec((tr, w), lambda j, te, nu: (used(j, te, nu), 0)),
                      pl.BlockSpec((None, 1, d, de), lambda j, te, nu: (layer, te[used(j, te, nu)], 0, 0)),
                      pl.BlockSpec((None, 1, d, de), lambda j, te, nu: (layer, te[used(j, te, nu)], 0, 0)),
                      pl.BlockSpec((None, 1, de, d), lambda j, te, nu: (layer, te[used(j, te, nu)], 0, 0))],
            out_specs=pl.BlockSpec((tr, w), lambda j, te, nu: (j, 0)),
            scratch_shapes=[pltpu.VMEM((d, de), BF16), pltpu.VMEM((d, de), BF16), pltpu.VMEM((de, d), BF16)]),
        out_shape=jax.ShapeDtypeStruct((rows, w), jnp.uint32),
        compiler_params=_cparams(("arbitrary",)),
        name="experts",
    )(tile_expert, n_used, xs, wg, wu, wd)


def _combine_body(slot_ref, next_ref, y_hbm, x1_ref, route_ref, lng_ref, lnb_ref, mod_ref, o_ref,
                  buf_scr, sems, *, alpha):
    tm = x1_ref.shape[0]
    i = pl.program_id(0)
    cur = i % 2

    def gather(idx_ref, b):
        def issue(j, carry):
            for u in range(ROW_DMA_UNROLL):
                t = j * ROW_DMA_UNROLL + u
                for k in range(2):
                    pltpu.make_async_copy(_slab(y_hbm, idx_ref[0, k, t]), _slab(buf_scr.at[b, k], t),
                                          sems.at[b]).start(priority=k)
            return carry

        lax.fori_loop(0, tm // ROW_DMA_UNROLL, issue, 0)

    def wait_slot(b):
        for k in range(2):
            pltpu.make_async_copy(y_hbm.at[pl.ds(0, tm * ROW_SLABS), :], buf_scr.at[b, k], sems.at[b]).wait()

    @pl.when(i == 0)
    def _():
        gather(slot_ref, cur)

    wait_slot(cur)
    for t in range(tm):
        for k in range(2):
            pltpu.make_async_copy(_slab(y_hbm, next_ref[0, k, t]), _slab(buf_scr.at[1 - cur, k], t),
                                  sems.at[1 - cur]).start(priority=k)

    w1 = route_ref[:, 0:1]
    w2 = route_ref[:, 1:2]
    moe = jnp.concatenate([w1 * a1 + w2 * a2 for a1, a2 in zip(_load_packed_rows(buf_scr.at[cur, 0]),
                                                               _load_packed_rows(buf_scr.at[cur, 1]))], axis=1)
    r = alpha * x1_ref[...] + mod_ref[0, 5:6, :] * moe
    o_ref[...] = _ln(r) * lng_ref[...] + lnb_ref[...]

    @pl.when(i == pl.num_programs(0) - 1)
    def _():
        wait_slot(1 - cur)


def _combine_call(slots, y, x1, route, lng, lnb, mod, tm, modrow, alpha):
    na, d = x1.shape
    full = lambda a: pl.BlockSpec(a.shape, lambda i: (0,) * a.ndim)
    last = na // tm - 1
    return pl.pallas_call(
        functools.partial(_combine_body, alpha=alpha),
        grid=(na // tm,),
        in_specs=[pl.BlockSpec((1, 8, tm), lambda i: (i, 0, 0), memory_space=pltpu.SMEM),
                  pl.BlockSpec((1, 8, tm), lambda i: (jnp.minimum(i + 1, last), 0, 0), memory_space=pltpu.SMEM),
                  pl.BlockSpec(memory_space=pl.ANY),
                  pl.BlockSpec((tm, d), lambda i: (i, 0)),
                  pl.BlockSpec((tm, 128), lambda i: (i, 0)),
                  full(lng), full(lnb),
                  pl.BlockSpec((1, 6, d), lambda i: (modrow(i), 0, 0))],
        out_specs=pl.BlockSpec((tm, d), lambda i: (i, 0)),
        out_shape=jax.ShapeDtypeStruct((na, d), F32),
        scratch_shapes=[pltpu.VMEM((2, 2, tm * ROW_SLABS, 128), jnp.uint32), pltpu.SemaphoreType.DMA((2,))],
        compiler_params=_cparams(("arbitrary",)),
        name="combine",
    )(slots, slots, y, x1, route, lng, lnb, mod)


def _expert_buffer_tiles(n_tokens, tr):
    return -(-2 * n_tokens // tr) + N_EXPERTS


def _routing_tables(counts, n_tiles, tr):
    cnt = counts[0, :N_EXPERTS].astype(jnp.int32)
    ntile = (cnt + tr - 1) // tr
    tile_start = jnp.cumsum(ntile) - ntile
    n_used = jnp.sum(ntile)
    seg = jnp.zeros((1, 128), F32).at[0, :N_EXPERTS].set((tile_start * tr).astype(F32))
    tile_expert = jnp.sum(jnp.arange(n_tiles, dtype=jnp.int32)[:, None] >= tile_start[None, :], axis=1) - 1
    return seg, tile_expert.astype(jnp.int32), n_used.reshape(1).astype(jnp.int32)


def _lower_bounds(logits):
    p = jax.nn.softmax(logits.astype(F32), axis=0)
    return jnp.cumsum(p, axis=0) - p[:1]


def _row_tile(limit, *sizes):
    tm = limit
    while any(s % tm for s in sizes):
        tm //= 2
    return tm


def kernel(x, c, ctx, c_ctx, w_ada, b_ada, w_in, w_pool, pool_scale, lb_logits_fwd, lb_logits_bwd, hg_gain, rpb, w_br_a, w_br_b, w_br_c, w_out, ln1_g, ln1_b, w_rg, b_rg, w_re, b_re, w_gate, w_up, w_down, ln2_g, ln2_b):
    nbatch, t_len, d = x.shape
    c_len = ctx.shape[1]
    depth = w_ada.shape[0]
    assert c_len == SEQ_TILE and t_len % SEQ_TILE == 0 and t_len % GRID_W == 0
    alpha = (2.0 * depth) ** 0.25
    n_lat = nbatch * t_len
    nt = t_len // SEQ_TILE

    na = n_lat + nbatch * c_len
    tok = (x.reshape(n_lat, d), ctx.reshape(nbatch * c_len, d), 0)

    mod_rows = -(-(nbatch + 1) // 8) * 8
    cc = jnp.zeros((mod_rows, d), F32).at[:nbatch].set(c).at[nbatch].set(c_ctx)
    ada = _ada_call(cc, w_ada, b_ada)

    lb_f = _lower_bounds(lb_logits_fwd).reshape(depth, HG_HEADS, 1, HG_DK)
    lb_b = _lower_bounds(lb_logits_bwd).reshape(depth, HG_HEADS, 1, HG_DK)
    pool_tile = _row_tile(POOL_TILE, t_len, nbatch * c_len)
    assert pool_tile % c_len == 0
    pool_consts = _pool_consts(pool_tile, c_len)
    hg_f = _hgrn_consts(False)
    hg_b = _hgrn_consts(True)

    tm_big = _row_tile(1024, t_len, nbatch * c_len)
    tm_mid = _row_tile(512, t_len, nbatch * c_len)

    def modrow_for(tm):
        return lambda i: jnp.where(i * tm < n_lat, (i * tm) // t_len, nbatch)

    n_tiles = _expert_buffer_tiles(na, EXPERT_TILE)
    xs = None
    for l in range(depth):
        last = l == depth - 1
        mod = ada[l].reshape(mod_rows, 6, d)
        z = _inproj_call(tok, na, mod, w_in[l].astype(BF16), tm_big, modrow_for(tm_big))
        ya = _pool_call(z, pool_consts, w_pool[l].astype(BF16), pool_scale[l].reshape(1, -1), n_lat,
                        n_lat if last else n_lat + nbatch * c_len, t_len)
        o_f = _hgrn_call(z, lb_f[l], hg_f, nbatch, nt, reverse=False)
        yb = _hgrn_call(z, lb_b[l], hg_b, nbatch, nt, reverse=True, o_fwd=o_f,
                        gain=hg_gain[l].reshape(HG_HEADS, 1, HG_DK))
        bias = _na_bias_table(rpb[l], t_len // GRID_W)
        yc = _na_call(z, bias, nbatch, t_len, c_len, with_ctx=not last)
        wr = jnp.zeros((d, 128), F32).at[:, :N_EXPERTS].set(w_re[l]).at[:, N_EXPERTS:N_EXPERTS + N_GROUPS].set(w_rg[l])
        br = jnp.zeros((1, 128), F32).at[0, :N_EXPERTS].set(b_re[l]).at[0, N_EXPERTS:N_EXPERTS + N_GROUPS].set(b_rg[l])
        x1, h2p, route, counts = _merge_call(
            tok, ya, yb, yc, z, mod, w_br_a[l].astype(BF16), w_br_b[l].astype(BF16), w_br_c[l].astype(BF16),
            w_out[l].astype(BF16), ln1_g[l].reshape(1, d), ln1_b[l].reshape(1, d), wr, br, tm_mid,
            modrow_for(tm_mid), alpha)
        seg, tile_expert, n_used = _routing_tables(counts, n_tiles, EXPERT_TILE)
        slots = _slots_call(route, seg, tm_mid)
        xs = _dispatch_call(h2p, slots, n_tiles * EXPERT_TILE, tm_mid, xs)
        ys = _experts_call(tile_expert, n_used, xs, w_gate, w_up, w_down, l, EXPERT_TILE)
        xa = _combine_call(slots, ys, x1, route, ln2_g[l].reshape(1, d), ln2_b[l].reshape(1, d), mod,
                           tm_mid, modrow_for(tm_mid), alpha)
        tok = (xa, xa, n_lat)
    return xa.reshape(nbatch, t_len, d)
```

```python
import functools

import numpy as np
import jax
import jax.numpy as jnp
from jax import lax
from jax.experimental import pallas as pl
from jax.experimental.pallas import tpu as pltpu

F32 = jnp.float32
BF16 = jnp.bfloat16
HIGHEST = lax.Precision.HIGHEST

GRID_W = 64
POOL_WINDOWS = (2, 4, 8, 16)
POOL_GDIM = 128
HG_HEADS = 4
HG_DK = 128
HG_BLOCK = 16
NA_HEADS = 8
NA_HD = 64
NA_ROWS = 8
NA_COLS = 16
NA_QROWS = 4
NA_KROWS = 12
N_GROUPS = 4
EXP_PER_GROUP = 8
N_EXPERTS = N_GROUPS * EXP_PER_GROUP
LN_EPS = 1e-5
RMS_EPS = 1e-6
NEG_BIG = -1e30
SEQ_TILE = 256
EXPERT_TILE = 512
POOL_TILE = 512
VMEM_LIMIT = 56 * 1024 * 1024

CB_A, CB_Q, CB_FF, CB_FB, CB_I, CB_G, CB_NQ, CB_NK, CB_NV, CB_GATE = 0, 4, 8, 12, 16, 20, 24, 28, 32, 36


def _cparams(sem):
    return pltpu.CompilerParams(dimension_semantics=sem, vmem_limit_bytes=VMEM_LIMIT)


def _ln(x):
    mu = jnp.mean(x, axis=-1, keepdims=True)
    xc = x - mu
    var = jnp.mean(xc * xc, axis=-1, keepdims=True)
    return xc * lax.rsqrt(var + LN_EPS)


def _sigmoid(x):
    return 1.0 / (1.0 + jnp.exp(-x))


def _sigmoid_t(x):
    return 0.5 * jnp.tanh(0.5 * x) + 0.5


def _dot(a, b):
    return jnp.dot(a, b, preferred_element_type=F32)


def _dot_nt(a, b):
    return lax.dot_general(a, b, (((1,), (1,)), ((), ())), preferred_element_type=F32)


def _dot_tn(a, b):
    return lax.dot_general(a, b, (((0,), (0,)), ((), ())), preferred_element_type=F32)


def _dot01(m01, x):
    x1 = x.astype(BF16)
    r1 = x - x1.astype(F32)
    x2 = r1.astype(BF16)
    x3 = (r1 - x2.astype(F32)).astype(BF16)
    return _dot(m01, x1) + _dot(m01, x2) + _dot(m01, x3)


def _ada_body(c_ref, w_ref, b_ref, o_ref):
    cs = c_ref[...]
    s = cs * _sigmoid(cs)
    o_ref[0] = jnp.dot(s, w_ref[0], preferred_element_type=F32, precision=HIGHEST) + b_ref[0]


def _ada_call(cc, w_ada, b_ada):
    depth, d, n6 = w_ada.shape
    rows = cc.shape[0]
    return pl.pallas_call(
        _ada_body,
        grid=(depth, n6 // d),
        in_specs=[pl.BlockSpec((rows, d), lambda l, j: (0, 0)),
                  pl.BlockSpec((1, d, d), lambda l, j: (l, 0, j)),
                  pl.BlockSpec((1, 1, d), lambda l, j: (l, 0, j))],
        out_specs=pl.BlockSpec((1, rows, d), lambda l, j: (l, 0, j)),
        out_shape=jax.ShapeDtypeStruct((depth, rows, n6), F32),
        compiler_params=_cparams(("parallel", "parallel")),
        name="ada",
    )(cc, w_ada, b_ada.reshape(depth, 1, n6))


def _token_specs(tok, tm):
    lat, ctx, ctx_row0 = tok
    split = (ctx_row0 if ctx is lat else lat.shape[0]) // tm
    off = ctx_row0 // tm
    d = lat.shape[1]
    lat_map = lambda i, *_: (jnp.minimum(i, split - 1), 0)
    ctx_map = lambda i, *_: (jnp.maximum(i - split, 0) + off, 0)
    return split, [pl.BlockSpec((tm, d), lat_map), pl.BlockSpec((tm, d), ctx_map)]


def _inproj_body(xl_ref, xc_ref, mod_ref, w_ref, z_ref, h_scr, *, split):
    def modulated(x_ref):
        h = _ln(x_ref[...]) * (1.0 + mod_ref[0, 1:2, :]) + mod_ref[0, 0:1, :]
        h_scr[...] = h.astype(BF16)

    first = pl.program_id(1) == 0
    pl.when(jnp.logical_and(first, pl.program_id(0) < split))(lambda: modulated(xl_ref))
    pl.when(jnp.logical_and(first, pl.program_id(0) >= split))(lambda: modulated(xc_ref))
    z_ref[...] = _dot(h_scr[...], w_ref[...]).astype(z_ref.dtype)


def _inproj_call(tok, na, mod, w_in, tm, modrow):
    d, d_in = w_in.shape
    tn = 2560
    assert d_in % tn == 0
    split, tok_specs = _token_specs(tok, tm)
    return pl.pallas_call(
        functools.partial(_inproj_body, split=split),
        grid=(na // tm, d_in // tn),
        in_specs=tok_specs + [pl.BlockSpec((1, 6, d), lambda i, j: (modrow(i), 0, 0)),
                              pl.BlockSpec((d, tn), lambda i, j: (0, j))],
        out_specs=pl.BlockSpec((tm, tn), lambda i, j: (i, j)),
        out_shape=jax.ShapeDtypeStruct((na, d_in), BF16),
        scratch_shapes=[pltpu.VMEM((tm, d), BF16)],
        compiler_params=_cparams(("parallel", "arbitrary")),
        name="inproj",
    )(tok[0], tok[1], mod, w_in)


def _pool_consts(n, c_len):
    t = np.arange(n)[:, None]
    bc = np.zeros((2, 4, n, n), np.float32)
    bp = np.zeros((4, n, 16), np.float32)
    bn = np.zeros((4, n, 16), np.float32)
    for g, win in enumerate(POOL_WINDOWS):
        lo, hi = t - win // 2, t + win // 2 - 1
        s = np.arange(n)[None, :]
        bc[0, g] = (s >= lo) & (s <= hi)
        bc[1, g] = bc[0, g] * (s // c_len == t // c_len)
        s = np.arange(16)[None, :] - 16
        bp[g] = (s >= lo) & (s <= hi)
        s = np.arange(16)[None, :] + n
        bn[g] = (s >= lo) & (s <= hi)
    cnt = np.stack([np.stack([bc[0].sum(-1), bp.sum(-1), bn.sum(-1)], axis=1),
                    np.stack([bc[1].sum(-1), 0 * bp.sum(-1), 0 * bn.sum(-1)], axis=1)])
    cnt = np.broadcast_to(cnt[..., None], (2, 4, 3, n, 128)).astype(np.float32)
    return (jnp.asarray(bc, BF16), jnp.asarray(bp, BF16), jnp.asarray(bn, BF16), jnp.asarray(cnt))


def _pool_body(prev_ref, cur_ref, next_ref, bc_ref, bp_ref, bn_ref, cnt_ref, wp_ref, ps_ref, o_ref,
               *, nt, n_lat_tiles):
    i = pl.program_id(0)
    k = i % nt
    lat = i < n_lat_tiles
    has_prev = jnp.where(jnp.logical_and(lat, k != 0), 1.0, 0.0).astype(F32)
    has_next = jnp.where(jnp.logical_and(lat, k != nt - 1), 1.0, 0.0).astype(F32)
    for g in range(len(POOL_WINDOWS)):
        sl = slice(g * POOL_GDIM, (g + 1) * POOL_GDIM)
        u = cur_ref[:, sl]
        ssum = (_dot(bc_ref[0, g], u) + has_prev * _dot(bp_ref[g], prev_ref[:, sl])
                + has_next * _dot(bn_ref[g], next_ref[:, sl]))
        cnt = cnt_ref[0, g, 0] + has_prev * cnt_ref[0, g, 1] + has_next * cnt_ref[0, g, 2]
        dlt = ssum / cnt - u.astype(F32)
        y = _dot(dlt.astype(BF16), wp_ref[g]) * ps_ref[:, sl]
        o_ref[:, sl] = y.astype(o_ref.dtype)


def _pool_call(z, consts, w_pool, pool_scale, n_lat, n_rows, t_len):
    bc, bp, bn, cnt = consts
    n = bc.shape[2]
    assert t_len % n == 0 and n_lat % n == 0 and n_rows % n == 0
    hb = n // 16
    last16 = z.shape[0] // 16 - 1
    n_lat_tiles = n_lat // n
    full = lambda a: pl.BlockSpec(a.shape, lambda i: (0,) * a.ndim)
    variant = lambda a: pl.BlockSpec((1,) + a.shape[1:],
                                     lambda i: (jnp.where(i < n_lat_tiles, 0, 1),) + (0,) * (a.ndim - 1))
    return pl.pallas_call(
        functools.partial(_pool_body, nt=t_len // n, n_lat_tiles=n_lat_tiles),
        grid=(n_rows // n,),
        in_specs=[pl.BlockSpec((16, 512), lambda i: (jnp.maximum(i * hb - 1, 0), CB_A // 4)),
                  pl.BlockSpec((n, 512), lambda i: (i, CB_A // 4)),
                  pl.BlockSpec((16, 512), lambda i: (jnp.minimum((i + 1) * hb, last16), CB_A // 4)),
                  variant(bc), full(bp), full(bn), variant(cnt), full(w_pool), full(pool_scale)],
        out_specs=pl.BlockSpec((n, 512), lambda i: (i, 0)),
        out_shape=jax.ShapeDtypeStruct((n_rows, 512), BF16),
        compiler_params=_cparams(("parallel",)),
        name="pool",
    )(z, z, z, bc, bp, bn, cnt, w_pool, pool_scale)


def _hgrn_consts(reverse):
    n, bs = SEQ_TILE, HG_BLOCK
    nb = n // bs
    t = np.arange(n)
    o = (n - 1 - t) if reverse else t
    blk = t // bs
    jb = np.arange(nb)
    ob = (nb - 1 - jb) if reverse else jb
    cum = ((blk[:, None] == blk[None, :]) & (o[None, :] <= o[:, None])).astype(np.float32)
    bsum = (jb[:, None] == blk[None, :]).astype(np.float32)
    widths = [2 ** l for l in range(1, int(np.log2(nb)) + 1)]
    lvl = np.full((n, n), -1, np.int32)
    obt = ob[blk]
    same = blk[:, None] == blk[None, :]
    lvl[same & (o[None, :] <= o[:, None])] = 0
    for li, w in reversed(list(enumerate(widths, start=1))):
        m = (obt[:, None] // w == obt[None, :] // w) & (obt[None, :] < obt[:, None]) & ~same
        lvl[m] = li
    mats = []
    for w in widths:
        mid = (ob // w) * w + w // 2
        mats.append((mid[:, None] <= ob[None, :]) & (ob[None, :] < ob[:, None]))
    for w in widths:
        mid = (ob // w) * w + w // 2
        mats.append((ob[:, None] < ob[None, :]) & (ob[None, :] < mid[:, None]))
    mats.append(ob[None, :] < ob[:, None])
    mats.append(ob[None, :] > ob[:, None])
    mats.append(np.ones((nb, nb), bool))
    tsm = np.concatenate(mats, axis=0).astype(np.float32)
    return (jnp.asarray(cum, BF16), jnp.asarray(bsum, BF16), jnp.asarray(lvl, BF16), jnp.asarray(tsm, BF16),
            len(widths))


def _expand_blocks(c, n):
    nb, lanes = c.shape
    return jnp.concatenate([jnp.broadcast_to(c[j:j + 1, :], (n // nb, lanes)) for j in range(nb)], axis=0)


def _hgrn_body(*refs, n_levels, final):
    if final:
        (zq_ref, zf_ref, zi_ref, lb_ref, cum_ref, bsum_ref, lvl_ref, tsm_ref,
         of_ref, zg_ref, gain_ref, o_ref, st_scr) = refs
    else:
        zq_ref, zf_ref, zi_ref, lb_ref, cum_ref, bsum_ref, lvl_ref, tsm_ref, o_ref, st_scr = refs
    n = zq_ref.shape[0]
    nb = n // HG_BLOCK

    @pl.when(pl.program_id(1) == 0)
    def _():
        st_scr[...] = jnp.zeros_like(st_scr)

    zq = zq_ref[...].astype(F32)
    zf = zf_ref[...].astype(F32)
    lb = jnp.concatenate([lb_ref[h] for h in range(HG_HEADS)], axis=1)
    sig = _sigmoid(zf)
    lf = jnp.log(lb + (1.0 - lb) * sig)
    k_all = (1.0 - lb) * (1.0 - sig)
    q_all = zq * _sigmoid_t(zq)
    lf_hi = lf.astype(BF16)
    lf2 = jnp.concatenate([lf_hi, (lf - lf_hi.astype(F32)).astype(BF16)], axis=1)
    w = HG_HEADS * HG_DK
    b2 = _dot(cum_ref[...], lf2)
    b_all = b2[:, :w] + b2[:, w:]
    t2 = _dot(bsum_ref[...], lf2)
    tot_all = t2[:, :w] + t2[:, w:]
    coef_all = _dot01(tsm_ref[...], tot_all)
    qd_all = q_all * jnp.exp(b_all)
    kd_all = (k_all * jnp.exp(-b_all)).astype(BF16)
    ks_all = k_all * jnp.exp(_expand_blocks(tot_all, n) - b_all)

    ecoef = jnp.exp(coef_all)
    scale = lambda idx: _expand_blocks(ecoef[idx * nb:(idx + 1) * nb], n)
    q_lv = [qd_all.astype(BF16)] + [(qd_all * scale(li)).astype(BF16) for li in range(n_levels)]
    k_lv = [kd_all] + [(ks_all * scale(n_levels + li)).astype(BF16) for li in range(n_levels)]
    qs_all = (qd_all * scale(2 * n_levels)).astype(BF16)
    kn_all = (ks_all * scale(2 * n_levels + 1)).astype(BF16)
    dec_all = ecoef[(2 * n_levels + 2) * nb:(2 * n_levels + 2) * nb + 1]

    for h in range(HG_HEADS):
        sl = slice(h * HG_DK, (h + 1) * HG_DK)
        v = zi_ref[:, sl]

        lvl = lvl_ref[...]
        a = jnp.zeros((n, n), BF16)
        for li in range(n_levels + 1):
            a = jnp.where(lvl == float(li), _dot_nt(q_lv[li][:, sl], k_lv[li][:, sl]).astype(BF16), a)
        o = _dot(a, v)

        st = st_scr[h]
        o = o + _dot_nt(qs_all[:, sl], st.astype(BF16))
        st_scr[h] = st * dec_all[:, sl] + _dot_tn(v, kn_all[:, sl])

        if final:
            o = o + of_ref[:, sl]
            o = o * lax.rsqrt(jnp.mean(o * o, axis=-1, keepdims=True) + RMS_EPS) * gain_ref[h]
            zg = zg_ref[:, sl].astype(F32)
            o_ref[:, sl] = (o * (zg * _sigmoid_t(zg))).astype(o_ref.dtype)
        else:
            o_ref[:, sl] = o


def _hgrn_call(z, lb, consts, nbatch, nt, reverse, o_fwd=None, gain=None):
    na = z.shape[0]
    n = SEQ_TILE
    cum, bsum, lvl, tsm, n_levels = consts
    ctx_base = nbatch * nt
    final = o_fwd is not None

    def tile(b, s):
        lat = (b * nt + nt - s) if reverse else (b * nt + s - 1)
        return jnp.where(s == 0, ctx_base + b, lat)

    width = HG_HEADS * HG_DK

    def col(cb):
        return pl.BlockSpec((n, width), lambda b, s: (tile(b, s), cb // HG_HEADS))

    full = lambda a: pl.BlockSpec(a.shape, lambda b, s: (0,) * a.ndim)
    in_specs = [col(CB_Q), col(CB_FB if reverse else CB_FF), col(CB_I), full(lb),
                full(cum), full(bsum), full(lvl), full(tsm)]
    args = [z, z, z, lb, cum, bsum, lvl, tsm]
    if final:
        in_specs += [col(0), col(CB_G), full(gain)]
        args += [o_fwd, z, gain]
    return pl.pallas_call(
        functools.partial(_hgrn_body, n_levels=n_levels, final=final),
        grid=(nbatch, nt + 1),
        in_specs=in_specs,
        out_specs=col(0),
        out_shape=jax.ShapeDtypeStruct((na, width), BF16 if final else F32),
        scratch_shapes=[pltpu.VMEM((HG_HEADS, HG_DK, HG_DK), F32)],
        compiler_params=_cparams(("parallel", "arbitrary")),
        name="hgrn_bwd" if final else "hgrn_fwd",
    )(*args)


def _na_bias_table(rpb, rows):
    nrb = rows // NA_QROWS
    assert nrb >= 3 and rows >= NA_KROWS
    kr = min(NA_ROWS, rows)
    qc = np.arange(GRID_W)
    c0 = np.clip(qc - NA_COLS // 2, 0, GRID_W - NA_COLS)
    kc = np.arange(GRID_W)
    col_ok = (kc[None, :] >= c0[:, None]) & (kc[None, :] < c0[:, None] + NA_COLS)
    dc = np.clip(kc[None, :] - qc[:, None] + NA_COLS - 1, 0, 2 * NA_COLS - 2)
    sel_c = (dc[..., None] == np.arange(2 * NA_COLS - 1)).astype(np.float32)
    sel_r, oks = [], []
    for rb in (0, 1, nrb - 1):
        start = int(np.clip(NA_QROWS * rb - 4, 0, rows - NA_KROWS))
        r = NA_QROWS * rb + np.arange(NA_QROWS)
        r0 = np.clip(r - kr // 2, 0, rows - kr)
        keyrow = start + np.arange(NA_KROWS)
        row_ok = (keyrow[None, :] >= r0[:, None]) & (keyrow[None, :] < r0[:, None] + kr)
        dr = np.clip(keyrow[None, :] - r[:, None] + NA_ROWS - 1, 0, 2 * NA_ROWS - 2)
        sel_r.append((dr[..., None] == np.arange(2 * NA_ROWS - 1)).astype(np.float32))
        oks.append(row_ok[:, None, :, None] & col_ok[None, :, None, :])
    sel_r.append(np.zeros_like(sel_r[0]))
    oks.append(np.zeros_like(oks[0]))
    bias = jnp.einsum("hij,paki,cdj->phackd", rpb.astype(F32), jnp.asarray(np.stack(sel_r)),
                      jnp.asarray(sel_c), precision=HIGHEST)
    bias = jnp.where(jnp.asarray(np.stack(oks))[:, None], bias, NEG_BIG)
    return bias.reshape(4, NA_HEADS, NA_QROWS * GRID_W, NA_KROWS * GRID_W)


def _na_body(q_ref, k_ref, v_ref, kc_ref, vc_ref, bias_ref, o_ref, *, rows):
    rb = pl.program_id(1)
    nk = NA_KROWS * GRID_W
    start_row = jnp.clip(NA_QROWS * rb - 4, 0, rows - NA_KROWS)
    start = pl.multiple_of(start_row * GRID_W, GRID_W)
    nq = q_ref.shape[0]
    lane = lax.broadcasted_iota(jnp.int32, (nq, 128), 1)
    scale = NA_HD ** -0.5
    for p in range(NA_HEADS // 2):
        sl = slice(128 * p, 128 * (p + 1))
        qp = q_ref[:, sl] * scale
        kp = k_ref[pl.ds(start, nk), sl]
        vp = v_ref[pl.ds(start, nk), sl]
        kcp = kc_ref[:, sl]
        vcp = vc_ref[:, sl]
        zero = jnp.zeros_like(qp)
        q2 = jnp.concatenate([jnp.where(lane < NA_HD, qp, zero), jnp.where(lane >= NA_HD, qp, zero)], axis=0)
        s_loc = _dot_nt(q2, kp) + bias_ref[0, 2 * p:2 * p + 2].reshape(2 * nq, nk)
        s_ctx = _dot_nt(q2, kcp)
        m = jnp.maximum(jnp.max(s_loc, axis=-1, keepdims=True), jnp.max(s_ctx, axis=-1, keepdims=True))
        p_loc = jnp.exp(s_loc - m)
        p_ctx = jnp.exp(s_ctx - m)
        den = jnp.sum(p_loc, axis=-1, keepdims=True) + jnp.sum(p_ctx, axis=-1, keepdims=True)
        o2 = (_dot(p_loc.astype(BF16), vp) + _dot(p_ctx.astype(BF16), vcp)) / den
        o_ref[:, sl] = jnp.where(lane < NA_HD, o2[:nq], o2[nq:]).astype(o_ref.dtype)


def _na_call(z, bias, nbatch, t_len, c_len, with_ctx):
    na = z.shape[0]
    rows = t_len // GRID_W
    nrb = rows // NA_QROWS
    nq = NA_QROWS * GRID_W
    assert nq == c_len
    ctx_base = nbatch * nrb
    steps = nrb + 1 if with_ctx else nrb

    def qtile(b, r):
        return jnp.where(r < nrb, b * nrb + r, ctx_base + b)

    def pattern(b, r):
        return jnp.where(r == 0, 0, jnp.where(r == nrb - 1, 2, jnp.where(r == nrb, 3, 1)))

    return pl.pallas_call(
        functools.partial(_na_body, rows=rows),
        grid=(nbatch, steps),
        in_specs=[pl.BlockSpec((nq, 512), lambda b, r: (qtile(b, r), CB_NQ // 4)),
                  pl.BlockSpec((t_len, 512), lambda b, r: (b, CB_NK // 4)),
                  pl.BlockSpec((t_len, 512), lambda b, r: (b, CB_NV // 4)),
                  pl.BlockSpec((c_len, 512), lambda b, r: (ctx_base + b, CB_NK // 4)),
                  pl.BlockSpec((c_len, 512), lambda b, r: (ctx_base + b, CB_NV // 4)),
                  pl.BlockSpec((1,) + bias.shape[1:], lambda b, r: (pattern(b, r), 0, 0, 0))],
        out_specs=pl.BlockSpec((nq, 512), lambda b, r: (qtile(b, r), 0)),
        out_shape=jax.ShapeDtypeStruct((na if with_ctx else nbatch * t_len, 512), BF16),
        compiler_params=_cparams(("parallel", "arbitrary")),
        name="natten",
    )(z, z, z, z, z, bias)


ROW_SLABS = 4


def _store_packed_rows(ref, x):
    m = x.shape[0]

    def bits(v):
        return lax.bitcast_convert_type(v.astype(BF16).astype(F32), jnp.uint32)

    for s in range(ROW_SLABS):
        lo = x[:, 128 * s:128 * (s + 1)]
        hi = x[:, 512 + 128 * s:512 + 128 * (s + 1)]
        ref[pl.ds(s, m, stride=ROW_SLABS), :] = (bits(hi) & jnp.uint32(0xFFFF0000)) | (bits(lo) >> 16)


def _load_packed_rows(ref):
    m = ref.shape[0] // ROW_SLABS
    los, his = [], []
    for s in range(ROW_SLABS):
        p = ref[pl.ds(s, m, stride=ROW_SLABS), :]
        los.append(lax.bitcast_convert_type(p << 16, F32))
        his.append(lax.bitcast_convert_type(p & jnp.uint32(0xFFFF0000), F32))
    return los + his


def _merge_body(xl_ref, xc_ref, ya_ref, yb_ref, yc_ref, g0, g1, g2, g3, g4, g5, wa_ref, wb_ref, wc_ref, wo_ref,
                lng_ref, lnb_ref, mod_ref, wr2_ref, br_ref, tril_ref, x1_ref, h2_ref, route_ref, cnt_ref,
                cnt_scr, *, alpha, split):
    @pl.when(pl.program_id(0) == 0)
    def _():
        cnt_scr[...] = jnp.zeros_like(cnt_scr)

    gates = ((g0, g1), (g2, g3), (g4, g5))
    yrefs = (ya_ref, yb_ref, yc_ref)
    ws = (wa_ref, wb_ref, wc_ref)
    half = wa_ref.shape[1] // 2
    mix = None
    for n in range(2):
        m = None
        for kbr in range(3):
            pr = _dot(yrefs[kbr][...], ws[kbr][:, n * half:(n + 1) * half])
            term = _sigmoid_t(gates[kbr][n][...].astype(F32)) * pr
            m = term if m is None else m + term
        part = _dot(m.astype(BF16), wo_ref[n * half:(n + 1) * half, :])
        mix = part if mix is None else mix + part
    x = jnp.where(pl.program_id(0) < split, xl_ref[...], xc_ref[...])
    r = alpha * x + mod_ref[0, 2:3, :] * mix
    x1 = _ln(r) * lng_ref[...] + lnb_ref[...]
    x1_ref[...] = x1
    h2 = _ln(x1) * (1.0 + mod_ref[0, 4:5, :]) + mod_ref[0, 3:4, :]
    _store_packed_rows(h2_ref, h2)

    h2_hi = h2.astype(BF16)
    h2_lo = (h2 - h2_hi.astype(F32)).astype(BF16)
    hh = _dot(h2_hi, wr2_ref[...])
    logits = (hh[:, :128] + _dot(h2_lo, wr2_ref[:, :128]) + hh[:, 128:]) + br_ref[...]
    lane = lax.broadcasted_iota(jnp.int32, logits.shape, 1).astype(F32)
    is_grp = jnp.where(lane >= N_EXPERTS, jnp.where(lane < N_EXPERTS + N_GROUPS, 1.0, 0.0), 0.0) > 0.5
    lgm = jnp.where(is_grp, logits, NEG_BIG)
    mg = jnp.max(lgm, axis=-1, keepdims=True)
    p_grp = 1.0 / jnp.sum(jnp.exp(lgm - mg), axis=-1, keepdims=True)
    grp = jnp.min(jnp.where(lgm == mg, lane, 1e9), axis=-1, keepdims=True) - N_EXPERTS
    lo = grp * EXP_PER_GROUP
    in_grp = jnp.where(lane >= lo, jnp.where(lane < lo + EXP_PER_GROUP, 1.0, 0.0), 0.0) > 0.5
    lem = jnp.where(in_grp, logits, NEG_BIG)
    m1 = jnp.max(lem, axis=-1, keepdims=True)
    id1 = jnp.min(jnp.where(lem == m1, lane, 1e9), axis=-1, keepdims=True)
    lem2 = jnp.where(lane == id1, NEG_BIG, lem)
    m2 = jnp.max(lem2, axis=-1, keepdims=True)
    id2 = jnp.min(jnp.where(lem2 == m2, lane, 1e9), axis=-1, keepdims=True)
    u2 = jnp.exp(m2 - m1)
    w1 = p_grp / (1.0 + u2)
    w2 = p_grp * u2 / (1.0 + u2)
    oh1 = jnp.where(lane == id1, 1.0, 0.0)
    oh2 = jnp.where(lane == id2, 1.0, 0.0)
    oh = oh1 + oh2
    before = _dot(tril_ref[...], oh.astype(BF16)) + cnt_scr[...]
    rank1 = jnp.sum(before * oh1, axis=-1, keepdims=True)
    rank2 = jnp.sum(before * oh2, axis=-1, keepdims=True)
    cnt_scr[...] += jnp.sum(oh, axis=0, keepdims=True)
    cnt_ref[...] = jnp.broadcast_to(cnt_scr[...], cnt_ref.shape)
    route = jnp.zeros_like(logits)
    for ln, val in enumerate((w1, w2, id1, id2, rank1, rank2)):
        route = jnp.where(lane == ln, val, route)
    route_ref[...] = route


def _merge_call(tok, ya, yb, yc, z, mod, wa, wb, wc, wo, lng, lnb, wr, br, tm, modrow, alpha):
    na, d = yc.shape[0], tok[0].shape[1]
    split, tok_specs = _token_specs(tok, tm)
    row = lambda w: pl.BlockSpec((tm, w), lambda i: (i, 0))
    gate = lambda cb: pl.BlockSpec((tm, 512), lambda i: (i, cb))
    full = lambda a: pl.BlockSpec(a.shape, lambda i: (0,) * a.ndim)
    g0 = CB_GATE // 4
    tril = jnp.asarray(np.tril(np.ones((tm, tm), np.float32), -1), BF16)
    wr_hi = lax.reduce_precision(wr, exponent_bits=8, mantissa_bits=7)
    wr2 = jnp.concatenate([wr_hi, wr - wr_hi], axis=1).astype(BF16)
    return pl.pallas_call(
        functools.partial(_merge_body, alpha=alpha, split=split),
        grid=(na // tm,),
        in_specs=tok_specs + [row(512), row(512), row(512)] + [gate(g0 + j) for j in range(6)]
                 + [full(wa), full(wb), full(wc), full(wo), full(lng), full(lnb),
                    pl.BlockSpec((1, 6, d), lambda i: (modrow(i), 0, 0)), full(wr2), full(br), full(tril)],
        out_specs=[row(d), pl.BlockSpec((tm * ROW_SLABS, 128), lambda i: (i, 0)), row(128),
                   pl.BlockSpec((8, 128), lambda i: (0, 0))],
        out_shape=[jax.ShapeDtypeStruct((na, d), F32), jax.ShapeDtypeStruct((na * ROW_SLABS, 128), jnp.uint32),
                   jax.ShapeDtypeStruct((na, 128), F32), jax.ShapeDtypeStruct((8, 128), F32)],
        scratch_shapes=[pltpu.VMEM((1, 128), F32)],
        compiler_params=_cparams(("arbitrary",)),
        name="merge",
    )(tok[0], tok[1], ya, yb, yc, z, z, z, z, z, z, wa, wb, wc, wo, lng, lnb, mod, wr2, br, tril)


ROW_DMA_UNROLL = 8


def _slab(ref, row):
    start = row * ROW_SLABS
    if not isinstance(row, int):
        start = pl.multiple_of(start, ROW_SLABS)
    return ref.at[pl.ds(start, ROW_SLABS), :]


def _slots_body(route_ref, seg_ref, o_ref):
    route = route_ref[...]
    lane = lax.broadcasted_iota(jnp.int32, route.shape, 1).astype(F32)
    seg = seg_ref[...]
    slots = []
    for k in range(2):
        eid = route[:, 2 + k:3 + k]
        start = jnp.sum(jnp.where(lane == eid, seg, 0.0), axis=-1, keepdims=True)
        slots.append(start + route[:, 4 + k:5 + k])
    both = jnp.where(lane == 0.0, slots[0], jnp.where(lane == 1.0, slots[1], 0.0))
    by_token = both.T[:8].astype(jnp.int32)
    tm = o_ref.shape[2]
    for c in range(o_ref.shape[0]):
        o_ref[c] = by_token[:, c * tm:(c + 1) * tm]


def _slots_call(route, seg, tm):
    na = route.shape[0]
    group = _row_tile(4, na // tm)
    return pl.pallas_call(
        _slots_body,
        grid=(na // (tm * group),),
        in_specs=[pl.BlockSpec((tm * group, 128), lambda i: (i, 0)), pl.BlockSpec((1, 128), lambda i: (0, 0))],
        out_specs=pl.BlockSpec((group, 8, tm), lambda i: (i, 0, 0)),
        out_shape=jax.ShapeDtypeStruct((na // tm, 8, tm), jnp.int32),
        compiler_params=_cparams(("parallel",)),
        name="slots",
    )(route, seg)


def _dispatch_body(slot_ref, h_ref, xs_in, xs_out, sem):
    del xs_in
    tm = h_ref.shape[0] // ROW_SLABS

    def issue(i, carry):
        for u in range(ROW_DMA_UNROLL):
            t = i * ROW_DMA_UNROLL + u
            for k in range(2):
                pltpu.make_async_copy(_slab(h_ref, t), _slab(xs_out, slot_ref[0, k, t]), sem).start(priority=k)
        return carry

    lax.fori_loop(0, tm // ROW_DMA_UNROLL, issue, 0)

    for k in range(2):
        pltpu.make_async_copy(h_ref, xs_out.at[pl.ds(0, tm * ROW_SLABS), :], sem).wait()


def _dispatch_call(h2p, slots, n_rows, tm, xs_prev=None):
    xs0 = jnp.zeros((n_rows * ROW_SLABS, 128), h2p.dtype) if xs_prev is None else xs_prev
    assert xs0.shape[0] >= n_rows * ROW_SLABS
    return pl.pallas_call(
        _dispatch_body,
        grid=(slots.shape[0],),
        in_specs=[pl.BlockSpec((1, 8, tm), lambda i: (i, 0, 0), memory_space=pltpu.SMEM),
                  pl.BlockSpec((tm * ROW_SLABS, 128), lambda i: (i, 0)),
                  pl.BlockSpec(memory_space=pl.ANY)],
        out_specs=pl.BlockSpec(memory_space=pl.ANY),
        out_shape=jax.ShapeDtypeStruct(xs0.shape, xs0.dtype),
        input_output_aliases={2: 0},
        scratch_shapes=[pltpu.SemaphoreType.DMA(())],
        compiler_params=_cparams(("arbitrary",)),
        name="dispatch",
    )(slots, h2p, xs0)


def _experts_body(te_ref, nu_ref, xs_ref, wg_ref, wu_ref, wd_ref, y_ref, wg_scr, wu_scr, wd_scr):
    j = pl.program_id(0)

    @pl.when(j >= nu_ref[0])
    def _():
        y_ref[...] = jnp.zeros_like(y_ref)

    @pl.when(jnp.logical_and(j < nu_ref[0], jnp.logical_or(j == 0, te_ref[j] != te_ref[jnp.maximum(j - 1, 0)])))
    def _():
        wg_scr[...] = wg_ref[0].astype(BF16)
        wu_scr[...] = wu_ref[0].astype(BF16)
        wd_scr[...] = wd_ref[0].astype(BF16)

    @pl.when(j < nu_ref[0])
    def _():
        x = jnp.concatenate([c.astype(BF16) for c in _load_packed_rows(xs_ref)], axis=1)
        gt = _dot(x, wg_scr[...])
        a = gt * _sigmoid_t(gt) * _dot(x, wu_scr[...])
        _store_packed_rows(y_ref, _dot(a.astype(BF16), wd_scr[...]))


def _experts_call(tile_expert, n_used, xs, wg, wu, wd, layer, tr):
    rows, w = xs.shape
    _, ne, d, de = wg.shape
    tr = tr * ROW_SLABS
    used = lambda j, te, nu: jnp.minimum(j, nu[0] - 1)
    return pl.pallas_call(
        _experts_body,
        grid_spec=pltpu.PrefetchScalarGridSpec(
            num_scalar_prefetch=2,
            grid=(rows // tr,),
            in_specs=[pl.BlockSpec((tr, w), lambda j, te, nu: (used(j, te, nu), 0)),
                      pl.BlockSpec((None, 1, d, de), lambda j, te, nu: (layer, te[used(j, te, nu)], 0, 0)),
                      pl.BlockSpec((None, 1, d, de), lambda j, te, nu: (layer, te[used(j, te, nu)], 0, 0)),
                      pl.BlockSpec((None, 1, de, d), lambda j, te, nu: (layer, te[used(j, te, nu)], 0, 0))],
            out_specs=pl.BlockSpec((tr, w), lambda j, te, nu: (j, 0)),
            scratch_shapes=[pltpu.VMEM((d, de), BF16), pltpu.VMEM((d, de), BF16), pltpu.VMEM((de, d), BF16)]),
        out_shape=jax.ShapeDtypeStruct((rows, w), jnp.uint32),
        compiler_params=_cparams(("arbitrary",)),
        name="experts",
    )(tile_expert, n_used, xs, wg, wu, wd)


def _combine_body(slot_ref, next_ref, y_hbm, x1_ref, route_ref, lng_ref, lnb_ref, mod_ref, o_ref,
                  buf_scr, sems, *, alpha):
    tm = x1_ref.shape[0]
    i = pl.program_id(0)
    cur = i % 2

    def gather(idx_ref, b):
        def issue(j, carry):
            for u in range(ROW_DMA_UNROLL):
                t = j * ROW_DMA_UNROLL + u
                for k in range(2):
                    pltpu.make_async_copy(_slab(y_hbm, idx_ref[0, k, t]), _slab(buf_scr.at[b, k], t),
                                          sems.at[b]).start(priority=k)
            return carry

        lax.fori_loop(0, tm // ROW_DMA_UNROLL, issue, 0)

    def wait_slot(b):
        for k in range(2):
            pltpu.make_async_copy(y_hbm.at[pl.ds(0, tm * ROW_SLABS), :], buf_scr.at[b, k], sems.at[b]).wait()

    @pl.when(i == 0)
    def _():
        gather(slot_ref, cur)

    wait_slot(cur)
    for t in range(tm):
        for k in range(2):
            pltpu.make_async_copy(_slab(y_hbm, next_ref[0, k, t]), _slab(buf_scr.at[1 - cur, k], t),
                                  sems.at[1 - cur]).start(priority=k)

    w1 = route_ref[:, 0:1]
    w2 = route_ref[:, 1:2]
    moe = jnp.concatenate([w1 * a1 + w2 * a2 for a1, a2 in zip(_load_packed_rows(buf_scr.at[cur, 0]),
                                                               _load_packed_rows(buf_scr.at[cur, 1]))], axis=1)
    r = alpha * x1_ref[...] + mod_ref[0, 5:6, :] * moe
    o_ref[...] = _ln(r) * lng_ref[...] + lnb_ref[...]

    @pl.when(i == pl.num_programs(0) - 1)
    def _():
        wait_slot(1 - cur)


def _combine_call(slots, y, x1, route, lng, lnb, mod, tm, modrow, alpha):
    na, d = x1.shape
    full = lambda a: pl.BlockSpec(a.shape, lambda i: (0,) * a.ndim)
    last = na // tm - 1
    return pl.pallas_call(
        functools.partial(_combine_body, alpha=alpha),
        grid=(na // tm,),
        in_specs=[pl.BlockSpec((1, 8, tm), lambda i: (i, 0, 0), memory_space=pltpu.SMEM),
                  pl.BlockSpec((1, 8, tm), lambda i: (jnp.minimum(i + 1, last), 0, 0), memory_space=pltpu.SMEM),
                  pl.BlockSpec(memory_space=pl.ANY),
                  pl.BlockSpec((tm, d), lambda i: (i, 0)),
                  pl.BlockSpec((tm, 128), lambda i: (i, 0)),
                  full(lng), full(lnb),
                  pl.BlockSpec((1, 6, d), lambda i: (modrow(i), 0, 0))],
        out_specs=pl.BlockSpec((tm, d), lambda i: (i, 0)),
        out_shape=jax.ShapeDtypeStruct((na, d), F32),
        scratch_shapes=[pltpu.VMEM((2, 2, tm * ROW_SLABS, 128), jnp.uint32), pltpu.SemaphoreType.DMA((2,))],
        compiler_params=_cparams(("arbitrary",)),
        name="combine",
    )(slots, slots, y, x1, route, lng, lnb, mod)


def _expert_buffer_tiles(n_tokens, tr):
    return -(-2 * n_tokens // tr) + N_EXPERTS


def _routing_tables(counts, n_tiles, tr):
    cnt = counts[0, :N_EXPERTS].astype(jnp.int32)
    ntile = (cnt + tr - 1) // tr
    tile_start = jnp.cumsum(ntile) - ntile
    n_used = jnp.sum(ntile)
    seg = jnp.zeros((1, 128), F32).at[0, :N_EXPERTS].set((tile_start * tr).astype(F32))
    tile_expert = jnp.sum(jnp.arange(n_tiles, dtype=jnp.int32)[:, None] >= tile_start[None, :], axis=1) - 1
    return seg, tile_expert.astype(jnp.int32), n_used.reshape(1).astype(jnp.int32)


def _lower_bounds(logits):
    p = jax.nn.softmax(logits.astype(F32), axis=0)
    return jnp.cumsum(p, axis=0) - p[:1]


def _row_tile(limit, *sizes):
    tm = limit
    while any(s % tm for s in sizes):
        tm //= 2
    return tm


def kernel(x, c, ctx, c_ctx, w_ada, b_ada, w_in, w_pool, pool_scale, lb_logits_fwd, lb_logits_bwd, hg_gain, rpb, w_br_a, w_br_b, w_br_c, w_out, ln1_g, ln1_b, w_rg, b_rg, w_re, b_re, w_gate, w_up, w_down, ln2_g, ln2_b):
    nbatch, t_len, d = x.shape
    c_len = ctx.shape[1]
    depth = w_ada.shape[0]
    assert c_len == SEQ_TILE and t_len % SEQ_TILE == 0 and t_len % GRID_W == 0
    alpha = (2.0 * depth) ** 0.25
    n_lat = nbatch * t_len
    nt = t_len // SEQ_TILE

    na = n_lat + nbatch * c_len
    tok = (x.reshape(n_lat, d), ctx.reshape(nbatch * c_len, d), 0)

    mod_rows = -(-(nbatch + 1) // 8) * 8
    cc = jnp.zeros((mod_rows, d), F32).at[:nbatch].set(c).at[nbatch].set(c_ctx)
    ada = _ada_call(cc, w_ada, b_ada)

    lb_f = _lower_bounds(lb_logits_fwd).reshape(depth, HG_HEADS, 1, HG_DK)
    lb_b = _lower_bounds(lb_logits_bwd).reshape(depth, HG_HEADS, 1, HG_DK)
    pool_tile = _row_tile(POOL_TILE, t_len, nbatch * c_len)
    assert pool_tile % c_len == 0
    pool_consts = _pool_consts(pool_tile, c_len)
    hg_f = _hgrn_consts(False)
    hg_b = _hgrn_consts(True)

    tm_big = _row_tile(1024, t_len, nbatch * c_len)
    tm_mid = _row_tile(512, t_len, nbatch * c_len)

    def modrow_for(tm):
        return lambda i: jnp.where(i * tm < n_lat, (i * tm) // t_len, nbatch)

    n_tiles = _expert_buffer_tiles(na, EXPERT_TILE)
    xs = None
    for l in range(depth):
        last = l == depth - 1
        mod = ada[l].reshape(mod_rows, 6, d)
        z = _inproj_call(tok, na, mod, w_in[l].astype(BF16), tm_big, modrow_for(tm_big))
        ya = _pool_call(z, pool_consts, w_pool[l].astype(BF16), pool_scale[l].reshape(1, -1), n_lat,
                        n_lat if last else n_lat + nbatch * c_len, t_len)
        o_f = _hgrn_call(z, lb_f[l], hg_f, nbatch, nt, reverse=False)
        yb = _hgrn_call(z, lb_b[l], hg_b, nbatch, nt, reverse=True, o_fwd=o_f,
                        gain=hg_gain[l].reshape(HG_HEADS, 1, HG_DK))
        bias = _na_bias_table(rpb[l], t_len // GRID_W)
        yc = _na_call(z, bias, nbatch, t_len, c_len, with_ctx=not last)
        wr = jnp.zeros((d, 128), F32).at[:, :N_EXPERTS].set(w_re[l]).at[:, N_EXPERTS:N_EXPERTS + N_GROUPS].set(w_rg[l])
        br = jnp.zeros((1, 128), F32).at[0, :N_EXPERTS].set(b_re[l]).at[0, N_EXPERTS:N_EXPERTS + N_GROUPS].set(b_rg[l])
        x1, h2p, route, counts = _merge_call(
            tok, ya, yb, yc, z, mod, w_br_a[l].astype(BF16), w_br_b[l].astype(BF16), w_br_c[l].astype(BF16),
            w_out[l].astype(BF16), ln1_g[l].reshape(1, d), ln1_b[l].reshape(1, d), wr, br, tm_mid,
            modrow_for(tm_mid), alpha)
        seg, tile_expert, n_used = _routing_tables(counts, n_tiles, EXPERT_TILE)
        slots = _slots_call(route, seg, tm_mid)
        xs = _dispatch_call(h2p, slots, n_tiles * EXPERT_TILE, tm_mid, xs)
        ys = _experts_call(tile_expert, n_used, xs, w_gate, w_up, w_down, l, EXPERT_TILE)
        xa = _combine_call(slots, ys, x1, route, ln2_g[l].reshape(1, d), ln2_b[l].reshape(1, d), mod,
                           tm_mid, modrow_for(tm_mid), alpha)
        tok = (xa, xa, n_lat)
    return xa.reshape(nbatch, t_len, d)
```

```python
import functools

import numpy as np
import jax
import jax.numpy as jnp
from jax import lax
from jax.experimental import pallas as pl
from jax.experimental.pallas import tpu as pltpu

F32 = jnp.float32
BF16 = jnp.bfloat16
HIGHEST = lax.Precision.HIGHEST

GRID_W = 64
POOL_WINDOWS = (2, 4, 8, 16)
POOL_GDIM = 128
HG_HEADS = 4
HG_DK = 128
HG_BLOCK = 16
NA_HEADS = 8
NA_HD = 64
NA_ROWS = 8
NA_COLS = 16
NA_QROWS = 4
NA_KROWS = 12
N_GROUPS = 4
EXP_PER_GROUP = 8
N_EXPERTS = N_GROUPS * EXP_PER_GROUP
LN_EPS = 1e-5
RMS_EPS = 1e-6
NEG_BIG = -1e30
SEQ_TILE = 256
EXPERT_TILE = 512
POOL_TILE = 512
VMEM_LIMIT = 56 * 1024 * 1024

CB_A, CB_Q, CB_FF, CB_FB, CB_I, CB_G, CB_NQ, CB_NK, CB_NV, CB_GATE = 0, 4, 8, 12, 16, 20, 24, 28, 32, 36


def _cparams(sem):
    return pltpu.CompilerParams(dimension_semantics=sem, vmem_limit_bytes=VMEM_LIMIT)


def _ln(x):
    mu = jnp.mean(x, axis=-1, keepdims=True)
    xc = x - mu
    var = jnp.mean(xc * xc, axis=-1, keepdims=True)
    return xc * lax.rsqrt(var + LN_EPS)


def _sigmoid(x):
    return 1.0 / (1.0 + jnp.exp(-x))


def _sigmoid_t(x):
    return 0.5 * jnp.tanh(0.5 * x) + 0.5


def _dot(a, b):
    return jnp.dot(a, b, preferred_element_type=F32)


def _dot_nt(a, b):
    return lax.dot_general(a, b, (((1,), (1,)), ((), ())), preferred_element_type=F32)


def _dot_tn(a, b):
    return lax.dot_general(a, b, (((0,), (0,)), ((), ())), preferred_element_type=F32)


def _dot01(m01, x):
    x1 = x.astype(BF16)
    r1 = x - x1.astype(F32)
    x2 = r1.astype(BF16)
    x3 = (r1 - x2.astype(F32)).astype(BF16)
    return _dot(m01, x1) + _dot(m01, x2) + _dot(m01, x3)


def _ada_body(c_ref, w_ref, b_ref, o_ref):
    cs = c_ref[...]
    s = cs * _sigmoid(cs)
    o_ref[0] = jnp.dot(s, w_ref[0], preferred_element_type=F32, precision=HIGHEST) + b_ref[0]


def _ada_call(cc, w_ada, b_ada):
    depth, d, n6 = w_ada.shape
    rows = cc.shape[0]
    return pl.pallas_call(
        _ada_body,
        grid=(depth, n6 // d),
        in_specs=[pl.BlockSpec((rows, d), lambda l, j: (0, 0)),
                  pl.BlockSpec((1, d, d), lambda l, j: (l, 0, j)),
                  pl.BlockSpec((1, 1, d), lambda l, j: (l, 0, j))],
        out_specs=pl.BlockSpec((1, rows, d), lambda l, j: (l, 0, j)),
        out_shape=jax.ShapeDtypeStruct((depth, rows, n6), F32),
        compiler_params=_cparams(("parallel", "parallel")),
        name="ada",
    )(cc, w_ada, b_ada.reshape(depth, 1, n6))


def _token_specs(tok, tm):
    lat, ctx, ctx_row0 = tok
    split = (ctx_row0 if ctx is lat else lat.shape[0]) // tm
    off = ctx_row0 // tm
    d = lat.shape[1]
    lat_map = lambda i, *_: (jnp.minimum(i, split - 1), 0)
    ctx_map = lambda i, *_: (jnp.maximum(i - split, 0) + off, 0)
    return split, [pl.BlockSpec((tm, d), lat_map), pl.BlockSpec((tm, d), ctx_map)]


def _inproj_body(xl_ref, xc_ref, mod_ref, w_ref, z_ref, h_scr, *, split):
    def modulated(x_ref):
        h = _ln(x_ref[...]) * (1.0 + mod_ref[0, 1:2, :]) + mod_ref[0, 0:1, :]
        h_scr[...] = h.astype(BF16)

    first = pl.program_id(1) == 0
    pl.when(jnp.logical_and(first, pl.program_id(0) < split))(lambda: modulated(xl_ref))
    pl.when(jnp.logical_and(first, pl.program_id(0) >= split))(lambda: modulated(xc_ref))
    z_ref[...] = _dot(h_scr[...], w_ref[...]).astype(z_ref.dtype)


def _inproj_call(tok, na, mod, w_in, tm, modrow):
    d, d_in = w_in.shape
    tn = 3840
    assert d_in % tn == 0
    split, tok_specs = _token_specs(tok, tm)
    return pl.pallas_call(
        functools.partial(_inproj_body, split=split),
        grid=(na // tm, d_in // tn),
        in_specs=tok_specs + [pl.BlockSpec((1, 6, d), lambda i, j: (modrow(i), 0, 0)),
                              pl.BlockSpec((d, tn), lambda i, j: (0, j))],
        out_specs=pl.BlockSpec((tm, tn), lambda i, j: (i, j)),
        out_shape=jax.ShapeDtypeStruct((na, d_in), BF16),
        scratch_shapes=[pltpu.VMEM((tm, d), BF16)],
        compiler_params=_cparams(("parallel", "arbitrary")),
        name="inproj",
    )(tok[0], tok[1], mod, w_in)


def _pool_consts(n, c_len):
    t = np.arange(n)[:, None]
    bc = np.zeros((2, 4, n, n), np.float32)
    bp = np.zeros((4, n, 16), np.float32)
    bn = np.zeros((4, n, 16), np.float32)
    for g, win in enumerate(POOL_WINDOWS):
        lo, hi = t - win // 2, t + win // 2 - 1
        s = np.arange(n)[None, :]
        bc[0, g] = (s >= lo) & (s <= hi)
        bc[1, g] = bc[0, g] * (s // c_len == t // c_len)
        s = np.arange(16)[None, :] - 16
        bp[g] = (s >= lo) & (s <= hi)
        s = np.arange(16)[None, :] + n
        bn[g] = (s >= lo) & (s <= hi)
    cnt = np.stack([np.stack([bc[0].sum(-1), bp.sum(-1), bn.sum(-1)], axis=1),
                    np.stack([bc[1].sum(-1), 0 * bp.sum(-1), 0 * bn.sum(-1)], axis=1)])
    cnt = np.broadcast_to(cnt[..., None], (2, 4, 3, n, 128)).astype(np.float32)
    return (jnp.asarray(bc, BF16), jnp.asarray(bp, BF16), jnp.asarray(bn, BF16), jnp.asarray(cnt))


def _pool_body(prev_ref, cur_ref, next_ref, bc_ref, bp_ref, bn_ref, cnt_ref, wp_ref, ps_ref, o_ref,
               *, nt, n_lat_tiles):
    i = pl.program_id(0)
    k = i % nt
    lat = i < n_lat_tiles
    has_prev = jnp.where(jnp.logical_and(lat, k != 0), 1.0, 0.0).astype(F32)
    has_next = jnp.where(jnp.logical_and(lat, k != nt - 1), 1.0, 0.0).astype(F32)
    for g in range(len(POOL_WINDOWS)):
        sl = slice(g * POOL_GDIM, (g + 1) * POOL_GDIM)
        u = cur_ref[:, sl]
        ssum = (_dot(bc_ref[0, g], u) + has_prev * _dot(bp_ref[g], prev_ref[:, sl])
                + has_next * _dot(bn_ref[g], next_ref[:, sl]))
        cnt = cnt_ref[0, g, 0] + has_prev * cnt_ref[0, g, 1] + has_next * cnt_ref[0, g, 2]
        dlt = ssum / cnt - u.astype(F32)
        y = _dot(dlt.astype(BF16), wp_ref[g]) * ps_ref[:, sl]
        o_ref[:, sl] = y.astype(o_ref.dtype)


def _pool_call(z, consts, w_pool, pool_scale, n_lat, n_rows, t_len):
    bc, bp, bn, cnt = consts
    n = bc.shape[2]
    assert t_len % n == 0 and n_lat % n == 0 and n_rows % n == 0
    hb = n // 16
    last16 = z.shape[0] // 16 - 1
    n_lat_tiles = n_lat // n
    full = lambda a: pl.BlockSpec(a.shape, lambda i: (0,) * a.ndim)
    variant = lambda a: pl.BlockSpec((1,) + a.shape[1:],
                                     lambda i: (jnp.where(i < n_lat_tiles, 0, 1),) + (0,) * (a.ndim - 1))
    return pl.pallas_call(
        functools.partial(_pool_body, nt=t_len // n, n_lat_tiles=n_lat_tiles),
        grid=(n_rows // n,),
        in_specs=[pl.BlockSpec((16, 512), lambda i: (jnp.maximum(i * hb - 1, 0), CB_A // 4)),
                  pl.BlockSpec((n, 512), lambda i: (i, CB_A // 4)),
                  pl.BlockSpec((16, 512), lambda i: (jnp.minimum((i + 1) * hb, last16), CB_A // 4)),
                  variant(bc), full(bp), full(bn), variant(cnt), full(w_pool), full(pool_scale)],
        out_specs=pl.BlockSpec((n, 512), lambda i: (i, 0)),
        out_shape=jax.ShapeDtypeStruct((n_rows, 512), BF16),
        compiler_params=_cparams(("parallel",)),
        name="pool",
    )(z, z, z, bc, bp, bn, cnt, w_pool, pool_scale)


def _hgrn_consts(reverse):
    n, bs = SEQ_TILE, HG_BLOCK
    nb = n // bs
    t = np.arange(n)
    o = (n - 1 - t) if reverse else t
    blk = t // bs
    jb = np.arange(nb)
    ob = (nb - 1 - jb) if reverse else jb
    cum = ((blk[:, None] == blk[None, :]) & (o[None, :] <= o[:, None])).astype(np.float32)
    bsum = (jb[:, None] == blk[None, :]).astype(np.float32)
    widths = [2 ** l for l in range(1, int(np.log2(nb)) + 1)]
    lvl = np.full((n, n), -1, np.int32)
    obt = ob[blk]
    same = blk[:, None] == blk[None, :]
    lvl[same & (o[None, :] <= o[:, None])] = 0
    for li, w in reversed(list(enumerate(widths, start=1))):
        m = (obt[:, None] // w == obt[None, :] // w) & (obt[None, :] < obt[:, None]) & ~same
        lvl[m] = li
    mats = []
    for w in widths:
        mid = (ob // w) * w + w // 2
        mats.append((mid[:, None] <= ob[None, :]) & (ob[None, :] < ob[:, None]))
    for w in widths:
        mid = (ob // w) * w + w // 2
        mats.append((ob[:, None] < ob[None, :]) & (ob[None, :] < mid[:, None]))
    mats.append(ob[None, :] < ob[:, None])
    mats.append(ob[None, :] > ob[:, None])
    mats.append(np.ones((nb, nb), bool))
    tsm = np.concatenate(mats, axis=0).astype(np.float32)
    return (jnp.asarray(cum, BF16), jnp.asarray(bsum, BF16), jnp.asarray(lvl, BF16), jnp.asarray(tsm, BF16),
            len(widths))


def _expand_blocks(c, n):
    nb, lanes = c.shape
    return jnp.concatenate([jnp.broadcast_to(c[j:j + 1, :], (n // nb, lanes)) for j in range(nb)], axis=0)


def _hgrn_body(*refs, n_levels, final):
    if final:
        (zq_ref, zf_ref, zi_ref, lb_ref, cum_ref, bsum_ref, lvl_ref, tsm_ref,
         of_ref, zg_ref, gain_ref, o_ref, st_scr) = refs
    else:
        zq_ref, zf_ref, zi_ref, lb_ref, cum_ref, bsum_ref, lvl_ref, tsm_ref, o_ref, st_scr = refs
    n = zq_ref.shape[0]
    nb = n // HG_BLOCK

    @pl.when(pl.program_id(1) == 0)
    def _():
        st_scr[...] = jnp.zeros_like(st_scr)

    zq = zq_ref[...].astype(F32)
    zf = zf_ref[...].astype(F32)
    lb = jnp.concatenate([lb_ref[h] for h in range(HG_HEADS)], axis=1)
    sig = _sigmoid(zf)
    lf = jnp.log(lb + (1.0 - lb) * sig)
    k_all = (1.0 - lb) * (1.0 - sig)
    q_all = zq * _sigmoid_t(zq)
    lf_hi = lf.astype(BF16)
    lf2 = jnp.concatenate([lf_hi, (lf - lf_hi.astype(F32)).astype(BF16)], axis=1)
    w = HG_HEADS * HG_DK
    b2 = _dot(cum_ref[...], lf2)
    b_all = b2[:, :w] + b2[:, w:]
    t2 = _dot(bsum_ref[...], lf2)
    tot_all = t2[:, :w] + t2[:, w:]
    coef_all = _dot01(tsm_ref[...], tot_all)
    qd_all = q_all * jnp.exp(b_all)
    kd_all = (k_all * jnp.exp(-b_all)).astype(BF16)
    ks_all = k_all * jnp.exp(_expand_blocks(tot_all, n) - b_all)

    ecoef = jnp.exp(coef_all)
    scale = lambda idx: _expand_blocks(ecoef[idx * nb:(idx + 1) * nb], n)
    q_lv = [qd_all.astype(BF16)] + [(qd_all * scale(li)).astype(BF16) for li in range(n_levels)]
    k_lv = [kd_all] + [(ks_all * scale(n_levels + li)).astype(BF16) for li in range(n_levels)]
    qs_all = (qd_all * scale(2 * n_levels)).astype(BF16)
    kn_all = (ks_all * scale(2 * n_levels + 1)).astype(BF16)
    dec_all = ecoef[(2 * n_levels + 2) * nb:(2 * n_levels + 2) * nb + 1]

    for h in range(HG_HEADS):
        sl = slice(h * HG_DK, (h + 1) * HG_DK)
        v = zi_ref[:, sl]

        lvl = lvl_ref[...]
        a = jnp.zeros((n, n), BF16)
        for li in range(n_levels + 1):
            a = jnp.where(lvl == float(li), _dot_nt(q_lv[li][:, sl], k_lv[li][:, sl]).astype(BF16), a)
        o = _dot(a, v)

        st = st_scr[h]
        o = o + _dot_nt(qs_all[:, sl], st.astype(BF16))
        st_scr[h] = st * dec_all[:, sl] + _dot_tn(v, kn_all[:, sl])

        if final:
            o = o + of_ref[:, sl]
            o = o * lax.rsqrt(jnp.mean(o * o, axis=-1, keepdims=True) + RMS_EPS) * gain_ref[h]
            zg = zg_ref[:, sl].astype(F32)
            o_ref[:, sl] = (o * (zg * _sigmoid_t(zg))).astype(o_ref.dtype)
        else:
            o_ref[:, sl] = o


def _hgrn_call(z, lb, consts, nbatch, nt, reverse, o_fwd=None, gain=None):
    na = z.shape[0]
    n = SEQ_TILE
    cum, bsum, lvl, tsm, n_levels = consts
    ctx_base = nbatch * nt
    final = o_fwd is not None

    def tile(b, s):
        lat = (b * nt + nt - s) if reverse else (b * nt + s - 1)
        return jnp.where(s == 0, ctx_base + b, lat)

    width = HG_HEADS * HG_DK

    def col(cb):
        return pl.BlockSpec((n, width), lambda b, s: (tile(b, s), cb // HG_HEADS))

    full = lambda a: pl.BlockSpec(a.shape, lambda b, s: (0,) * a.ndim)
    in_specs = [col(CB_Q), col(CB_FB if reverse else CB_FF), col(CB_I), full(lb),
                full(cum), full(bsum), full(lvl), full(tsm)]
    args = [z, z, z, lb, cum, bsum, lvl, tsm]
    if final:
        in_specs += [col(0), col(CB_G), full(gain)]
        args += [o_fwd, z, gain]
    return pl.pallas_call(
        functools.partial(_hgrn_body, n_levels=n_levels, final=final),
        grid=(nbatch, nt + 1),
        in_specs=in_specs,
        out_specs=col(0),
        out_shape=jax.ShapeDtypeStruct((na, width), BF16 if final else F32),
        scratch_shapes=[pltpu.VMEM((HG_HEADS, HG_DK, HG_DK), F32)],
        compiler_params=_cparams(("parallel", "arbitrary")),
        name="hgrn_bwd" if final else "hgrn_fwd",
    )(*args)


def _na_bias_table(rpb, rows):
    nrb = rows // NA_QROWS
    assert nrb >= 3 and rows >= NA_KROWS
    kr = min(NA_ROWS, rows)
    qc = np.arange(GRID_W)
    c0 = np.clip(qc - NA_COLS // 2, 0, GRID_W - NA_COLS)
    kc = np.arange(GRID_W)
    col_ok = (kc[None, :] >= c0[:, None]) & (kc[None, :] < c0[:, None] + NA_COLS)
    dc = np.clip(kc[None, :] - qc[:, None] + NA_COLS - 1, 0, 2 * NA_COLS - 2)
    sel_c = (dc[..., None] == np.arange(2 * NA_COLS - 1)).astype(np.float32)
    sel_r, oks = [], []
    for rb in (0, 1, nrb - 1):
        start = int(np.clip(NA_QROWS * rb - 4, 0, rows - NA_KROWS))
        r = NA_QROWS * rb + np.arange(NA_QROWS)
        r0 = np.clip(r - kr // 2, 0, rows - kr)
        keyrow = start + np.arange(NA_KROWS)
        row_ok = (keyrow[None, :] >= r0[:, None]) & (keyrow[None, :] < r0[:, None] + kr)
        dr = np.clip(keyrow[None, :] - r[:, None] + NA_ROWS - 1, 0, 2 * NA_ROWS - 2)
        sel_r.append((dr[..., None] == np.arange(2 * NA_ROWS - 1)).astype(np.float32))
        oks.append(row_ok[:, None, :, None] & col_ok[None, :, None, :])
    sel_r.append(np.zeros_like(sel_r[0]))
    oks.append(np.zeros_like(oks[0]))
    bias = jnp.einsum("hij,paki,cdj->phackd", rpb.astype(F32), jnp.asarray(np.stack(sel_r)),
                      jnp.asarray(sel_c), precision=HIGHEST)
    bias = jnp.where(jnp.asarray(np.stack(oks))[:, None], bias, NEG_BIG)
    return bias.reshape(4, NA_HEADS, NA_QROWS * GRID_W, NA_KROWS * GRID_W)


def _na_body(q_ref, k_ref, v_ref, kc_ref, vc_ref, bias_ref, o_ref, *, rows):
    rb = pl.program_id(1)
    nk = NA_KROWS * GRID_W
    start_row = jnp.clip(NA_QROWS * rb - 4, 0, rows - NA_KROWS)
    start = pl.multiple_of(start_row * GRID_W, GRID_W)
    nq = q_ref.shape[0]
    lane = lax.broadcasted_iota(jnp.int32, (nq, 128), 1)
    scale = NA_HD ** -0.5
    for p in range(NA_HEADS // 2):
        sl = slice(128 * p, 128 * (p + 1))
        qp = q_ref[:, sl] * scale
        kp = k_ref[pl.ds(start, nk), sl]
        vp = v_ref[pl.ds(start, nk), sl]
        kcp = kc_ref[:, sl]
        vcp = vc_ref[:, sl]
        zero = jnp.zeros_like(qp)
        q2 = jnp.concatenate([jnp.where(lane < NA_HD, qp, zero), jnp.where(lane >= NA_HD, qp, zero)], axis=0)
        s_loc = _dot_nt(q2, kp) + bias_ref[0, 2 * p:2 * p + 2].reshape(2 * nq, nk)
        s_ctx = _dot_nt(q2, kcp)
        m = jnp.maximum(jnp.max(s_loc, axis=-1, keepdims=True), jnp.max(s_ctx, axis=-1, keepdims=True))
        p_loc = jnp.exp(s_loc - m)
        p_ctx = jnp.exp(s_ctx - m)
        den = jnp.sum(p_loc, axis=-1, keepdims=True) + jnp.sum(p_ctx, axis=-1, keepdims=True)
        o2 = (_dot(p_loc.astype(BF16), vp) + _dot(p_ctx.astype(BF16), vcp)) / den
        o_ref[:, sl] = jnp.where(lane < NA_HD, o2[:nq], o2[nq:]).astype(o_ref.dtype)


def _na_call(z, bias, nbatch, t_len, c_len, with_ctx):
    na = z.shape[0]
    rows = t_len // GRID_W
    nrb = rows // NA_QROWS
    nq = NA_QROWS * GRID_W
    assert nq == c_len
    ctx_base = nbatch * nrb
    steps = nrb + 1 if with_ctx else nrb

    def qtile(b, r):
        return jnp.where(r < nrb, b * nrb + r, ctx_base + b)

    def pattern(b, r):
        return jnp.where(r == 0, 0, jnp.where(r == nrb - 1, 2, jnp.where(r == nrb, 3, 1)))

    return pl.pallas_call(
        functools.partial(_na_body, rows=rows),
        grid=(nbatch, steps),
        in_specs=[pl.BlockSpec((nq, 512), lambda b, r: (qtile(b, r), CB_NQ // 4)),
                  pl.BlockSpec((t_len, 512), lambda b, r: (b, CB_NK // 4)),
                  pl.BlockSpec((t_len, 512), lambda b, r: (b, CB_NV // 4)),
                  pl.BlockSpec((c_len, 512), lambda b, r: (ctx_base + b, CB_NK // 4)),
                  pl.BlockSpec((c_len, 512), lambda b, r: (ctx_base + b, CB_NV // 4)),
                  pl.BlockSpec((1,) + bias.shape[1:], lambda b, r: (pattern(b, r), 0, 0, 0))],
        out_specs=pl.BlockSpec((nq, 512), lambda b, r: (qtile(b, r), 0)),
        out_shape=jax.ShapeDtypeStruct((na if with_ctx else nbatch * t_len, 512), BF16),
        compiler_params=_cparams(("parallel", "arbitrary")),
        name="natten",
    )(z, z, z, z, z, bias)


ROW_SLABS = 4


def _store_packed_rows(ref, x):
    m = x.shape[0]

    def bits(v):
        return lax.bitcast_convert_type(v.astype(BF16).astype(F32), jnp.uint32)

    for s in range(ROW_SLABS):
        lo = x[:, 128 * s:128 * (s + 1)]
        hi = x[:, 512 + 128 * s:512 + 128 * (s + 1)]
        ref[pl.ds(s, m, stride=ROW_SLABS), :] = (bits(hi) & jnp.uint32(0xFFFF0000)) | (bits(lo) >> 16)


def _load_packed_rows(ref):
    m = ref.shape[0] // ROW_SLABS
    los, his = [], []
    for s in range(ROW_SLABS):
        p = ref[pl.ds(s, m, stride=ROW_SLABS), :]
        los.append(lax.bitcast_convert_type(p << 16, F32))
        his.append(lax.bitcast_convert_type(p & jnp.uint32(0xFFFF0000), F32))
    return los + his


def _merge_body(xl_ref, xc_ref, ya_ref, yb_ref, yc_ref, g0, g1, g2, g3, g4, g5, wa_ref, wb_ref, wc_ref, wo_ref,
                lng_ref, lnb_ref, mod_ref, wr2_ref, br_ref, tril_ref, x1_ref, h2_ref, route_ref, cnt_ref,
                cnt_scr, *, alpha, split):
    @pl.when(pl.program_id(0) == 0)
    def _():
        cnt_scr[...] = jnp.zeros_like(cnt_scr)

    gates = ((g0, g1), (g2, g3), (g4, g5))
    yrefs = (ya_ref, yb_ref, yc_ref)
    ws = (wa_ref, wb_ref, wc_ref)
    half = wa_ref.shape[1] // 2
    mix = None
    for n in range(2):
        m = None
        for kbr in range(3):
            pr = _dot(yrefs[kbr][...], ws[kbr][:, n * half:(n + 1) * half])
            term = _sigmoid_t(gates[kbr][n][...].astype(F32)) * pr
            m = term if m is None else m + term
        part = _dot(m.astype(BF16), wo_ref[n * half:(n + 1) * half, :])
        mix = part if mix is None else mix + part
    x = jnp.where(pl.program_id(0) < split, xl_ref[...], xc_ref[...])
    r = alpha * x + mod_ref[0, 2:3, :] * mix
    x1 = _ln(r) * lng_ref[...] + lnb_ref[...]
    x1_ref[...] = x1
    h2 = _ln(x1) * (1.0 + mod_ref[0, 4:5, :]) + mod_ref[0, 3:4, :]
    _store_packed_rows(h2_ref, h2)

    h2_hi = h2.astype(BF16)
    h2_lo = (h2 - h2_hi.astype(F32)).astype(BF16)
    hh = _dot(h2_hi, wr2_ref[...])
    logits = (hh[:, :128] + _dot(h2_lo, wr2_ref[:, :128]) + hh[:, 128:]) + br_ref[...]
    lane = lax.broadcasted_iota(jnp.int32, logits.shape, 1).astype(F32)
    is_grp = jnp.where(lane >= N_EXPERTS, jnp.where(lane < N_EXPERTS + N_GROUPS, 1.0, 0.0), 0.0) > 0.5
    lgm = jnp.where(is_grp, logits, NEG_BIG)
    mg = jnp.max(lgm, axis=-1, keepdims=True)
    p_grp = 1.0 / jnp.sum(jnp.exp(lgm - mg), axis=-1, keepdims=True)
    grp = jnp.min(jnp.where(lgm == mg, lane, 1e9), axis=-1, keepdims=True) - N_EXPERTS
    lo = grp * EXP_PER_GROUP
    in_grp = jnp.where(lane >= lo, jnp.where(lane < lo + EXP_PER_GROUP, 1.0, 0.0), 0.0) > 0.5
    lem = jnp.where(in_grp, logits, NEG_BIG)
    m1 = jnp.max(lem, axis=-1, keepdims=True)
    id1 = jnp.min(jnp.where(lem == m1, lane, 1e9), axis=-1, keepdims=True)
    lem2 = jnp.where(lane == id1, NEG_BIG, lem)
    m2 = jnp.max(lem2, axis=-1, keepdims=True)
    id2 = jnp.min(jnp.where(lem2 == m2, lane, 1e9), axis=-1, keepdims=True)
    u2 = jnp.exp(m2 - m1)
    w1 = p_grp / (1.0 + u2)
    w2 = p_grp * u2 / (1.0 + u2)
    oh1 = jnp.where(lane == id1, 1.0, 0.0)
    oh2 = jnp.where(lane == id2, 1.0, 0.0)
    oh = oh1 + oh2
    before = _dot(tril_ref[...], oh.astype(BF16)) + cnt_scr[...]
    rank1 = jnp.sum(before * oh1, axis=-1, keepdims=True)
    rank2 = jnp.sum(before * oh2, axis=-1, keepdims=True)
    cnt_scr[...] += jnp.sum(oh, axis=0, keepdims=True)
    cnt_ref[...] = jnp.broadcast_to(cnt_scr[...], cnt_ref.shape)
    route = jnp.zeros_like(logits)
    for ln, val in enumerate((w1, w2, id1, id2, rank1, rank2)):
        route = jnp.where(lane == ln, val, route)
    route_ref[...] = route


def _merge_call(tok, ya, yb, yc, z, mod, wa, wb, wc, wo, lng, lnb, wr, br, tm, modrow, alpha):
    na, d = yc.shape[0], tok[0].shape[1]
    split, tok_specs = _token_specs(tok, tm)
    row = lambda w: pl.BlockSpec((tm, w), lambda i: (i, 0))
    gate = lambda cb: pl.BlockSpec((tm, 512), lambda i: (i, cb))
    full = lambda a: pl.BlockSpec(a.shape, lambda i: (0,) * a.ndim)
    g0 = CB_GATE // 4
    tril = jnp.asarray(np.tril(np.ones((tm, tm), np.float32), -1), BF16)
    wr_hi = lax.reduce_precision(wr, exponent_bits=8, mantissa_bits=7)
    wr2 = jnp.concatenate([wr_hi, wr - wr_hi], axis=1).astype(BF16)
    return pl.pallas_call(
        functools.partial(_merge_body, alpha=alpha, split=split),
        grid=(na // tm,),
        in_specs=tok_specs + [row(512), row(512), row(512)] + [gate(g0 + j) for j in range(6)]
                 + [full(wa), full(wb), full(wc), full(wo), full(lng), full(lnb),
                    pl.BlockSpec((1, 6, d), lambda i: (modrow(i), 0, 0)), full(wr2), full(br), full(tril)],
        out_specs=[row(d), pl.BlockSpec((tm * ROW_SLABS, 128), lambda i: (i, 0)), row(128),
                   pl.BlockSpec((8, 128), lambda i: (0, 0))],
        out_shape=[jax.ShapeDtypeStruct((na, d), F32), jax.ShapeDtypeStruct((na * ROW_SLABS, 128), jnp.uint32),
                   jax.ShapeDtypeStruct((na, 128), F32), jax.ShapeDtypeStruct((8, 128), F32)],
        scratch_shapes=[pltpu.VMEM((1, 128), F32)],
        compiler_params=_cparams(("arbitrary",)),
        name="merge",
    )(tok[0], tok[1], ya, yb, yc, z, z, z, z, z, z, wa, wb, wc, wo, lng, lnb, mod, wr2, br, tril)


ROW_DMA_UNROLL = 8


def _slab(ref, row):
    start = row * ROW_SLABS
    if not isinstance(row, int):
        start = pl.multiple_of(start, ROW_SLABS)
    return ref.at[pl.ds(start, ROW_SLABS), :]


def _slots_body(route_ref, seg_ref, o_ref):
    route = route_ref[...]
    lane = lax.broadcasted_iota(jnp.int32, route.shape, 1).astype(F32)
    seg = seg_ref[...]
    slots = []
    for k in range(2):
        eid = route[:, 2 + k:3 + k]
        start = jnp.sum(jnp.where(lane == eid, seg, 0.0), axis=-1, keepdims=True)
        slots.append(start + route[:, 4 + k:5 + k])
    both = jnp.where(lane == 0.0, slots[0], jnp.where(lane == 1.0, slots[1], 0.0))
    by_token = both.T[:8].astype(jnp.int32)
    tm = o_ref.shape[2]
    for c in range(o_ref.shape[0]):
        o_ref[c] = by_token[:, c * tm:(c + 1) * tm]


def _slots_call(route, seg, tm):
    na = route.shape[0]
    group = _row_tile(4, na // tm)
    return pl.pallas_call(
        _slots_body,
        grid=(na // (tm * group),),
        in_specs=[pl.BlockSpec((tm * group, 128), lambda i: (i, 0)), pl.BlockSpec((1, 128), lambda i: (0, 0))],
        out_specs=pl.BlockSpec((group, 8, tm), lambda i: (i, 0, 0)),
        out_shape=jax.ShapeDtypeStruct((na // tm, 8, tm), jnp.int32),
        compiler_params=_cparams(("parallel",)),
        name="slots",
    )(route, seg)


def _dispatch_body(slot_ref, h_ref, xs_in, xs_out, sem):
    del xs_in
    tm = h_ref.shape[0] // ROW_SLABS

    def issue(i, carry):
        for u in range(ROW_DMA_UNROLL):
            t = i * ROW_DMA_UNROLL + u
            for k in range(2):
                pltpu.make_async_copy(_slab(h_ref, t), _slab(xs_out, slot_ref[0, k, t]), sem).start(priority=k)
        return carry

    lax.fori_loop(0, tm // ROW_DMA_UNROLL, issue, 0)

    for k in range(2):
        pltpu.make_async_copy(h_ref, xs_out.at[pl.ds(0, tm * ROW_SLABS), :], sem).wait()


def _dispatch_call(h2p, slots, n_rows, tm, xs_prev=None):
    xs0 = jnp.zeros((n_rows * ROW_SLABS, 128), h2p.dtype) if xs_prev is None else xs_prev
    assert xs0.shape[0] >= n_rows * ROW_SLABS
    return pl.pallas_call(
        _dispatch_body,
        grid=(slots.shape[0],),
        in_specs=[pl.BlockSpec((1, 8, tm), lambda i: (i, 0, 0), memory_space=pltpu.SMEM),
                  pl.BlockSpec((tm * ROW_SLABS, 128), lambda i: (i, 0)),
                  pl.BlockSpec(memory_space=pl.ANY)],
        out_specs=pl.BlockSpec(memory_space=pl.ANY),
        out_shape=jax.ShapeDtypeStruct(xs0.shape, xs0.dtype),
        input_output_aliases={2: 0},
        scratch_shapes=[pltpu.SemaphoreType.DMA(())],
        compiler_params=_cparams(("arbitrary",)),
        name="dispatch",
    )(slots, h2p, xs0)


def _experts_body(te_ref, nu_ref, xs_ref, wg_ref, wu_ref, wd_ref, y_ref, wg_scr, wu_scr, wd_scr):
    j = pl.program_id(0)

    @pl.when(j >= nu_ref[0])
    def _():
        y_ref[...] = jnp.zeros_like(y_ref)

    @pl.when(jnp.logical_and(j < nu_ref[0], jnp.logical_or(j == 0, te_ref[j] != te_ref[jnp.maximum(j - 1, 0)])))
    def _():
        wg_scr[...] = wg_ref[0].astype(BF16)
        wu_scr[...] = wu_ref[0].astype(BF16)
        wd_scr[...] = wd_ref[0].astype(BF16)

    @pl.when(j < nu_ref[0])
    def _():
        x = jnp.concatenate([c.astype(BF16) for c in _load_packed_rows(xs_ref)], axis=1)
        gt = _dot(x, wg_scr[...])
        a = gt * _sigmoid_t(gt) * _dot(x, wu_scr[...])
        _store_packed_rows(y_ref, _dot(a.astype(BF16), wd_scr[...]))


def _experts_call(tile_expert, n_used, xs, wg, wu, wd, layer, tr):
    rows, w = xs.shape
    _, ne, d, de = wg.shape
    tr = tr * ROW_SLABS
    used = lambda j, te, nu: jnp.minimum(j, nu[0] - 1)
    return pl.pallas_call(
        _experts_body,
        grid_spec=pltpu.PrefetchScalarGridSpec(
            num_scalar_prefetch=2,
            grid=(rows // tr,),
            in_specs=[pl.BlockSpec((tr, w), lambda j, te, nu: (used(j, te, nu), 0)),
                      pl.BlockSpec((None, 1, d, de), lambda j, te, nu: (layer, te[used(j, te, nu)], 0, 0)),
                      pl.BlockSpec((None, 1, d, de), lambda j, te, nu: (layer, te[used(j, te, nu)], 0, 0)),
                      pl.BlockSpec((None, 1, de, d), lambda j, te, nu: (layer, te[used(j, te, nu)], 0, 0))],
            out_specs=pl.BlockSpec((tr, w), lambda j, te, nu: (j, 0)),
            scratch_shapes=[pltpu.VMEM((d, de), BF16), pltpu.VMEM((d, de), BF16), pltpu.VMEM((de, d), BF16)]),
        out_shape=jax.ShapeDtypeStruct((rows, w), jnp.uint32),
        compiler_params=_cparams(("arbitrary",)),
        name="experts",
    )(tile_expert, n_used, xs, wg, wu, wd)


def _combine_body(slot_ref, next_ref, y_hbm, x1_ref, route_ref, lng_ref, lnb_ref, mod_ref, o_ref,
                  buf_scr, sems, *, alpha):
    tm = x1_ref.shape[0]
    i = pl.program_id(0)
    cur = i % 2

    def gather(idx_ref, b):
        def issue(j, carry):
            for u in range(ROW_DMA_UNROLL):
                t = j * ROW_DMA_UNROLL + u
                for k in range(2):
                    pltpu.make_async_copy(_slab(y_hbm, idx_ref[0, k, t]), _slab(buf_scr.at[b, k], t),
                                          sems.at[b]).start(priority=k)
            return carry

        lax.fori_loop(0, tm // ROW_DMA_UNROLL, issue, 0)

    def wait_slot(b):
        for k in range(2):
            pltpu.make_async_copy(y_hbm.at[pl.ds(0, tm * ROW_SLABS), :], buf_scr.at[b, k], sems.at[b]).wait()

    @pl.when(i == 0)
    def _():
        gather(slot_ref, cur)

    wait_slot(cur)
    for t in range(tm):
        for k in range(2):
            pltpu.make_async_copy(_slab(y_hbm, next_ref[0, k, t]), _slab(buf_scr.at[1 - cur, k], t),
                                  sems.at[1 - cur]).start(priority=k)

    w1 = route_ref[:, 0:1]
    w2 = route_ref[:, 1:2]
    moe = jnp.concatenate([w1 * a1 + w2 * a2 for a1, a2 in zip(_load_packed_rows(buf_scr.at[cur, 0]),
                                                               _load_packed_rows(buf_scr.at[cur, 1]))], axis=1)
    r = alpha * x1_ref[...] + mod_ref[0, 5:6, :] * moe
    o_ref[...] = _ln(r) * lng_ref[...] + lnb_ref[...]

    @pl.when(i == pl.num_programs(0) - 1)
    def _():
        wait_slot(1 - cur)


def _combine_call(slots, y, x1, route, lng, lnb, mod, tm, modrow, alpha):
    na, d = x1.shape
    full = lambda a: pl.BlockSpec(a.shape, lambda i: (0,) * a.ndim)
    last = na // tm - 1
    return pl.pallas_call(
        functools.partial(_combine_body, alpha=alpha),
        grid=(na // tm,),
        in_specs=[pl.BlockSpec((1, 8, tm), lambda i: (i, 0, 0), memory_space=pltpu.SMEM),
                  pl.BlockSpec((1, 8, tm), lambda i: (jnp.minimum(i + 1, last), 0, 0), memory_space=pltpu.SMEM),
                  pl.BlockSpec(memory_space=pl.ANY),
                  pl.BlockSpec((tm, d), lambda i: (i, 0)),
                  pl.BlockSpec((tm, 128), lambda i: (i, 0)),
                  full(lng), full(lnb),
                  pl.BlockSpec((1, 6, d), lambda i: (modrow(i), 0, 0))],
        out_specs=pl.BlockSpec((tm, d), lambda i: (i, 0)),
        out_shape=jax.ShapeDtypeStruct((na, d), F32),
        scratch_shapes=[pltpu.VMEM((2, 2, tm * ROW_SLABS, 128), jnp.uint32), pltpu.SemaphoreType.DMA((2,))],
        compiler_params=_cparams(("arbitrary",)),
        name="combine",
    )(slots, slots, y, x1, route, lng, lnb, mod)


def _expert_buffer_tiles(n_tokens, tr):
    return -(-2 * n_tokens // tr) + N_EXPERTS


def _routing_tables(counts, n_tiles, tr):
    cnt = counts[0, :N_EXPERTS].astype(jnp.int32)
    ntile = (cnt + tr - 1) // tr
    tile_start = jnp.cumsum(ntile) - ntile
    n_used = jnp.sum(ntile)
    seg = jnp.zeros((1, 128), F32).at[0, :N_EXPERTS].set((tile_start * tr).astype(F32))
    tile_expert = jnp.sum(jnp.arange(n_tiles, dtype=jnp.int32)[:, None] >= tile_start[None, :], axis=1) - 1
    return seg, tile_expert.astype(jnp.int32), n_used.reshape(1).astype(jnp.int32)


def _lower_bounds(logits):
    p = jax.nn.softmax(logits.astype(F32), axis=0)
    return jnp.cumsum(p, axis=0) - p[:1]


def _row_tile(limit, *sizes):
    tm = limit
    while any(s % tm for s in sizes):
        tm //= 2
    return tm


def kernel(x, c, ctx, c_ctx, w_ada, b_ada, w_in, w_pool, pool_scale, lb_logits_fwd, lb_logits_bwd, hg_gain, rpb, w_br_a, w_br_b, w_br_c, w_out, ln1_g, ln1_b, w_rg, b_rg, w_re, b_re, w_gate, w_up, w_down, ln2_g, ln2_b):
    nbatch, t_len, d = x.shape
    c_len = ctx.shape[1]
    depth = w_ada.shape[0]
    assert c_len == SEQ_TILE and t_len % SEQ_TILE == 0 and t_len % GRID_W == 0
    alpha = (2.0 * depth) ** 0.25
    n_lat = nbatch * t_len
    nt = t_len // SEQ_TILE

    na = n_lat + nbatch * c_len
    tok = (x.reshape(n_lat, d), ctx.reshape(nbatch * c_len, d), 0)

    mod_rows = -(-(nbatch + 1) // 8) * 8
    cc = jnp.zeros((mod_rows, d), F32).at[:nbatch].set(c).at[nbatch].set(c_ctx)
    ada = _ada_call(cc, w_ada, b_ada)

    lb_f = _lower_bounds(lb_logits_fwd).reshape(depth, HG_HEADS, 1, HG_DK)
    lb_b = _lower_bounds(lb_logits_bwd).reshape(depth, HG_HEADS, 1, HG_DK)
    pool_tile = _row_tile(POOL_TILE, t_len, nbatch * c_len)
    assert pool_tile % c_len == 0
    pool_consts = _pool_consts(pool_tile, c_len)
    hg_f = _hgrn_consts(False)
    hg_b = _hgrn_consts(True)

    tm_big = _row_tile(1024, t_len, nbatch * c_len)
    tm_mid = _row_tile(512, t_len, nbatch * c_len)

    def modrow_for(tm):
        return lambda i: jnp.where(i * tm < n_lat, (i * tm) // t_len, nbatch)

    n_tiles = _expert_buffer_tiles(na, EXPERT_TILE)
    xs = None
    for l in range(depth):
        last = l == depth - 1
        mod = ada[l].reshape(mod_rows, 6, d)
        z = _inproj_call(tok, na, mod, w_in[l].astype(BF16), tm_big, modrow_for(tm_big))
        ya = _pool_call(z, pool_consts, w_pool[l].astype(BF16), pool_scale[l].reshape(1, -1), n_lat,
                        n_lat if last else n_lat + nbatch * c_len, t_len)
        o_f = _hgrn_call(z, lb_f[l], hg_f, nbatch, nt, reverse=False)
        yb = _hgrn_call(z, lb_b[l], hg_b, nbatch, nt, reverse=True, o_fwd=o_f,
                        gain=hg_gain[l].reshape(HG_HEADS, 1, HG_DK))
        bias = _na_bias_table(rpb[l], t_len // GRID_W)
        yc = _na_call(z, bias, nbatch, t_len, c_len, with_ctx=not last)
        wr = jnp.zeros((d, 128), F32).at[:, :N_EXPERTS].set(w_re[l]).at[:, N_EXPERTS:N_EXPERTS + N_GROUPS].set(w_rg[l])
        br = jnp.zeros((1, 128), F32).at[0, :N_EXPERTS].set(b_re[l]).at[0, N_EXPERTS:N_EXPERTS + N_GROUPS].set(b_rg[l])
        x1, h2p, route, counts = _merge_call(
            tok, ya, yb, yc, z, mod, w_br_a[l].astype(BF16), w_br_b[l].astype(BF16), w_br_c[l].astype(BF16),
            w_out[l].astype(BF16), ln1_g[l].reshape(1, d), ln1_b[l].reshape(1, d), wr, br, tm_mid,
            modrow_for(tm_mid), alpha)
        seg, tile_expert, n_used = _routing_tables(counts, n_tiles, EXPERT_TILE)
        slots = _slots_call(route, seg, tm_mid)
        xs = _dispatch_call(h2p, slots, n_tiles * EXPERT_TILE, tm_mid, xs)
        ys = _experts_call(tile_expert, n_used, xs, w_gate, w_up, w_down, l, EXPERT_TILE)
        xa = _combine_call(slots, ys, x1, route, ln2_g[l].reshape(1, d), ln2_b[l].reshape(1, d), mod,
                           tm_mid, modrow_for(tm_mid), alpha)
        tok = (xa, xa, n_lat)
    return xa.reshape(nbatch, t_len, d)
```

```python
import functools

import numpy as np
import jax
import jax.numpy as jnp
from jax import lax
from jax.experimental import pallas as pl
from jax.experimental.pallas import tpu as pltpu

F32 = jnp.float32
BF16 = jnp.bfloat16
HIGHEST = lax.Precision.HIGHEST

GRID_W = 64
POOL_WINDOWS = (2, 4, 8, 16)
POOL_GDIM = 128
HG_HEADS = 4
HG_DK = 128
HG_BLOCK = 16
NA_HEADS = 8
NA_HD = 64
NA_ROWS = 8
NA_COLS = 16
NA_QROWS = 4
NA_KROWS = 12
N_GROUPS = 4
EXP_PER_GROUP = 8
N_EXPERTS = N_GROUPS * EXP_PER_GROUP
LN_EPS = 1e-5
RMS_EPS = 1e-6
NEG_BIG = -1e30
SEQ_TILE = 256
EXPERT_TILE = 512
VMEM_LIMIT = 56 * 1024 * 1024

CB_A, CB_Q, CB_FF, CB_FB, CB_I, CB_G, CB_NQ, CB_NK, CB_NV, CB_GATE = 0, 4, 8, 12, 16, 20, 24, 28, 32, 36


def _cparams(sem):
    return pltpu.CompilerParams(dimension_semantics=sem, vmem_limit_bytes=VMEM_LIMIT)


def _ln(x):
    mu = jnp.mean(x, axis=-1, keepdims=True)
    xc = x - mu
    var = jnp.mean(xc * xc, axis=-1, keepdims=True)
    return xc * lax.rsqrt(var + LN_EPS)


def _sigmoid(x):
    return 1.0 / (1.0 + jnp.exp(-x))


def _sigmoid_t(x):
    return 0.5 * jnp.tanh(0.5 * x) + 0.5


def _dot(a, b):
    return jnp.dot(a, b, preferred_element_type=F32)


def _dot_nt(a, b):
    return lax.dot_general(a, b, (((1,), (1,)), ((), ())), preferred_element_type=F32)


def _dot_tn(a, b):
    return lax.dot_general(a, b, (((0,), (0,)), ((), ())), preferred_element_type=F32)


def _dot01(m01, x):
    x1 = x.astype(BF16)
    r1 = x - x1.astype(F32)
    x2 = r1.astype(BF16)
    x3 = (r1 - x2.astype(F32)).astype(BF16)
    return _dot(m01, x1) + _dot(m01, x2) + _dot(m01, x3)


def _ada_body(c_ref, w_ref, b_ref, o_ref):
    cs = c_ref[...]
    s = cs * _sigmoid(cs)
    o_ref[0] = jnp.dot(s, w_ref[0], preferred_element_type=F32, precision=HIGHEST) + b_ref[0]


def _ada_call(cc, w_ada, b_ada):
    depth, d, n6 = w_ada.shape
    rows = cc.shape[0]
    return pl.pallas_call(
        _ada_body,
        grid=(depth, n6 // d),
        in_specs=[pl.BlockSpec((rows, d), lambda l, j: (0, 0)),
                  pl.BlockSpec((1, d, d), lambda l, j: (l, 0, j)),
                  pl.BlockSpec((1, 1, d), lambda l, j: (l, 0, j))],
        out_specs=pl.BlockSpec((1, rows, d), lambda l, j: (l, 0, j)),
        out_shape=jax.ShapeDtypeStruct((depth, rows, n6), F32),
        compiler_params=_cparams(("parallel", "parallel")),
        name="ada",
    )(cc, w_ada, b_ada.reshape(depth, 1, n6))


def _token_specs(tok, tm):
    lat, ctx, ctx_row0 = tok
    split = (ctx_row0 if ctx is lat else lat.shape[0]) // tm
    off = ctx_row0 // tm
    d = lat.shape[1]
    lat_map = lambda i, *_: (jnp.minimum(i, split - 1), 0)
    ctx_map = lambda i, *_: (jnp.maximum(i - split, 0) + off, 0)
    return split, [pl.BlockSpec((tm, d), lat_map), pl.BlockSpec((tm, d), ctx_map)]


def _inproj_body(xl_ref, xc_ref, mod_ref, w_ref, z_ref, h_scr, *, split):
    def modulated(x_ref):
        h = _ln(x_ref[...]) * (1.0 + mod_ref[0, 1:2, :]) + mod_ref[0, 0:1, :]
        h_scr[...] = h.astype(BF16)

    first = pl.program_id(1) == 0
    pl.when(jnp.logical_and(first, pl.program_id(0) < split))(lambda: modulated(xl_ref))
    pl.when(jnp.logical_and(first, pl.program_id(0) >= split))(lambda: modulated(xc_ref))
    z_ref[...] = _dot(h_scr[...], w_ref[...]).astype(z_ref.dtype)


def _inproj_call(tok, na, mod, w_in, tm, modrow):
    d, d_in = w_in.shape
    tn = 2560
    assert d_in % tn == 0
    split, tok_specs = _token_specs(tok, tm)
    return pl.pallas_call(
        functools.partial(_inproj_body, split=split),
        grid=(na // tm, d_in // tn),
        in_specs=tok_specs + [pl.BlockSpec((1, 6, d), lambda i, j: (modrow(i), 0, 0)),
                              pl.BlockSpec((d, tn), lambda i, j: (0, j))],
        out_specs=pl.BlockSpec((tm, tn), lambda i, j: (i, j)),
        out_shape=jax.ShapeDtypeStruct((na, d_in), BF16),
        scratch_shapes=[pltpu.VMEM((tm, d), BF16)],
        compiler_params=_cparams(("parallel", "arbitrary")),
        name="inproj",
    )(tok[0], tok[1], mod, w_in)


def _pool_consts(n, c_len):
    t = np.arange(n)[:, None]
    bc = np.zeros((2, 4, n, n), np.float32)
    bp = np.zeros((4, n, 16), np.float32)
    bn = np.zeros((4, n, 16), np.float32)
    for g, win in enumerate(POOL_WINDOWS):
        lo, hi = t - win // 2, t + win // 2 - 1
        s = np.arange(n)[None, :]
        bc[0, g] = (s >= lo) & (s <= hi)
        bc[1, g] = bc[0, g] * (s // c_len == t // c_len)
        s = np.arange(16)[None, :] - 16
        bp[g] = (s >= lo) & (s <= hi)
        s = np.arange(16)[None, :] + n
        bn[g] = (s >= lo) & (s <= hi)
    cnt = np.stack([np.stack([bc[0].sum(-1), bp.sum(-1), bn.sum(-1)], axis=1),
                    np.stack([bc[1].sum(-1), 0 * bp.sum(-1), 0 * bn.sum(-1)], axis=1)])
    cnt = np.broadcast_to(cnt[..., None], (2, 4, 3, n, 128)).astype(np.float32)
    return (jnp.asarray(bc, BF16), jnp.asarray(bp, BF16), jnp.asarray(bn, BF16), jnp.asarray(cnt))


def _pooled_tile(prev_ref, cur_ref, next_ref, bc_ref, bp_ref, bn_ref, cnt_ref, wp_ref, ps_ref, *, nt, n_lat_tiles):
    i = pl.program_id(0)
    k = i % nt
    lat = i < n_lat_tiles
    has_prev = jnp.where(jnp.logical_and(lat, k != 0), 1.0, 0.0).astype(F32)
    has_next = jnp.where(jnp.logical_and(lat, k != nt - 1), 1.0, 0.0).astype(F32)
    groups = []
    for g in range(len(POOL_WINDOWS)):
        sl = slice(g * POOL_GDIM, (g + 1) * POOL_GDIM)
        u = cur_ref[:, sl]
        ssum = (_dot(bc_ref[0, g], u) + has_prev * _dot(bp_ref[g], prev_ref[:, sl])
                + has_next * _dot(bn_ref[g], next_ref[:, sl]))
        cnt = cnt_ref[0, g, 0] + has_prev * cnt_ref[0, g, 1] + has_next * cnt_ref[0, g, 2]
        dlt = ssum / cnt - u.astype(F32)
        groups.append((_dot(dlt.astype(BF16), wp_ref[g]) * ps_ref[:, sl]).astype(BF16))
    return jnp.concatenate(groups, axis=1)


def _pool_specs(z, consts, w_pool, pool_scale, n_lat):
    bc, bp, bn, cnt = consts
    n = bc.shape[2]
    hb = n // 16
    last16 = z.shape[0] // 16 - 1
    n_lat_tiles = n_lat // n
    full = lambda a: pl.BlockSpec(a.shape, lambda i: (0,) * a.ndim)
    variant = lambda a: pl.BlockSpec((1,) + a.shape[1:],
                                     lambda i: (jnp.where(i < n_lat_tiles, 0, 1),) + (0,) * (a.ndim - 1))
    specs = [pl.BlockSpec((16, 512), lambda i: (jnp.maximum(i * hb - 1, 0), CB_A // 4)),
             pl.BlockSpec((n, 512), lambda i: (i, CB_A // 4)),
             pl.BlockSpec((16, 512), lambda i: (jnp.minimum((i + 1) * hb, last16), CB_A // 4)),
             variant(bc), full(bp), full(bn), variant(cnt), full(w_pool), full(pool_scale)]
    return specs, [z, z, z, bc, bp, bn, cnt, w_pool, pool_scale], n_lat_tiles


def _hgrn_consts(reverse):
    n, bs = SEQ_TILE, HG_BLOCK
    nb = n // bs
    t = np.arange(n)
    o = (n - 1 - t) if reverse else t
    blk = t // bs
    jb = np.arange(nb)
    ob = (nb - 1 - jb) if reverse else jb
    cum = ((blk[:, None] == blk[None, :]) & (o[None, :] <= o[:, None])).astype(np.float32)
    bsum = (jb[:, None] == blk[None, :]).astype(np.float32)
    widths = [2 ** l for l in range(1, int(np.log2(nb)) + 1)]
    lvl = np.full((n, n), -1, np.int32)
    obt = ob[blk]
    same = blk[:, None] == blk[None, :]
    lvl[same & (o[None, :] <= o[:, None])] = 0
    for li, w in reversed(list(enumerate(widths, start=1))):
        m = (obt[:, None] // w == obt[None, :] // w) & (obt[None, :] < obt[:, None]) & ~same
        lvl[m] = li
    mats = []
    for w in widths:
        mid = (ob // w) * w + w // 2
        mats.append((mid[:, None] <= ob[None, :]) & (ob[None, :] < ob[:, None]))
    for w in widths:
        mid = (ob // w) * w + w // 2
        mats.append((ob[:, None] < ob[None, :]) & (ob[None, :] < mid[:, None]))
    mats.append(ob[None, :] < ob[:, None])
    mats.append(ob[None, :] > ob[:, None])
    mats.append(np.ones((nb, nb), bool))
    tsm = np.concatenate(mats, axis=0).astype(np.float32)
    return (jnp.asarray(cum, BF16), jnp.asarray(bsum, BF16), jnp.asarray(lvl, BF16), jnp.asarray(tsm, BF16),
            len(widths))


def _expand_blocks(c, n):
    nb, lanes = c.shape
    return jnp.concatenate([jnp.broadcast_to(c[j:j + 1, :], (n // nb, lanes)) for j in range(nb)], axis=0)


def _hgrn_body(*refs, n_levels, final):
    if final:
        (zq_ref, zf_ref, zi_ref, lb_ref, cum_ref, bsum_ref, lvl_ref, tsm_ref,
         of_ref, zg_ref, gain_ref, o_ref, st_scr) = refs
    else:
        zq_ref, zf_ref, zi_ref, lb_ref, cum_ref, bsum_ref, lvl_ref, tsm_ref, o_ref, st_scr = refs
    n = zq_ref.shape[0]
    nb = n // HG_BLOCK

    @pl.when(pl.program_id(1) == 0)
    def _():
        st_scr[...] = jnp.zeros_like(st_scr)

    zq = zq_ref[...].astype(F32)
    zf = zf_ref[...].astype(F32)
    lb = jnp.concatenate([lb_ref[h] for h in range(HG_HEADS)], axis=1)
    sig = _sigmoid(zf)
    lf = jnp.log(lb + (1.0 - lb) * sig)
    k_all = (1.0 - lb) * (1.0 - sig)
    q_all = zq * _sigmoid_t(zq)
    lf_hi = lf.astype(BF16)
    lf2 = jnp.concatenate([lf_hi, (lf - lf_hi.astype(F32)).astype(BF16)], axis=1)
    w = HG_HEADS * HG_DK
    b2 = _dot(cum_ref[...], lf2)
    b_all = b2[:, :w] + b2[:, w:]
    t2 = _dot(bsum_ref[...], lf2)
    tot_all = t2[:, :w] + t2[:, w:]
    coef_all = _dot01(tsm_ref[...], tot_all)
    qd_all = q_all * jnp.exp(b_all)
    kd_all = (k_all * jnp.exp(-b_all)).astype(BF16)
    ks_all = k_all * jnp.exp(_expand_blocks(tot_all, n) - b_all)

    ecoef = jnp.exp(coef_all)
    scale = lambda idx: _expand_blocks(ecoef[idx * nb:(idx + 1) * nb], n)
    q_lv = [qd_all.astype(BF16)] + [(qd_all * scale(li)).astype(BF16) for li in range(n_levels)]
    k_lv = [kd_all] + [(ks_all * scale(n_levels + li)).astype(BF16) for li in range(n_levels)]
    qs_all = (qd_all * scale(2 * n_levels)).astype(BF16)
    kn_all = (ks_all * scale(2 * n_levels + 1)).astype(BF16)
    dec_all = ecoef[(2 * n_levels + 2) * nb:(2 * n_levels + 2) * nb + 1]

    for h in range(HG_HEADS):
        sl = slice(h * HG_DK, (h + 1) * HG_DK)
        v = zi_ref[:, sl]

        lvl = lvl_ref[...]
        a = jnp.zeros((n, n), BF16)
        for li in range(n_levels + 1):
            a = jnp.where(lvl == float(li), _dot_nt(q_lv[li][:, sl], k_lv[li][:, sl]).astype(BF16), a)
        o = _dot(a, v)

        st = st_scr[h]
        o = o + _dot_nt(qs_all[:, sl], st.astype(BF16))
        st_scr[h] = st * dec_all[:, sl] + _dot_tn(v, kn_all[:, sl])

        if final:
            o = o + of_ref[:, sl]
            o = o * lax.rsqrt(jnp.mean(o * o, axis=-1, keepdims=True) + RMS_EPS) * gain_ref[h]
            zg = zg_ref[:, sl].astype(F32)
            o_ref[:, sl] = (o * (zg * _sigmoid_t(zg))).astype(o_ref.dtype)
        else:
            o_ref[:, sl] = o


def _hgrn_call(z, lb, consts, nbatch, nt, reverse, o_fwd=None, gain=None):
    na = z.shape[0]
    n = SEQ_TILE
    cum, bsum, lvl, tsm, n_levels = consts
    ctx_base = nbatch * nt
    final = o_fwd is not None

    def tile(b, s):
        lat = (b * nt + nt - s) if reverse else (b * nt + s - 1)
        return jnp.where(s == 0, ctx_base + b, lat)

    width = HG_HEADS * HG_DK

    def col(cb):
        return pl.BlockSpec((n, width), lambda b, s: (tile(b, s), cb // HG_HEADS))

    full = lambda a: pl.BlockSpec(a.shape, lambda b, s: (0,) * a.ndim)
    in_specs = [col(CB_Q), col(CB_FB if reverse else CB_FF), col(CB_I), full(lb),
                full(cum), full(bsum), full(lvl), full(tsm)]
    args = [z, z, z, lb, cum, bsum, lvl, tsm]
    if final:
        in_specs += [col(0), col(CB_G), full(gain)]
        args += [o_fwd, z, gain]
    return pl.pallas_call(
        functools.partial(_hgrn_body, n_levels=n_levels, final=final),
        grid=(nbatch, nt + 1),
        in_specs=in_specs,
        out_specs=col(0),
        out_shape=jax.ShapeDtypeStruct((na, width), BF16 if final else F32),
        scratch_shapes=[pltpu.VMEM((HG_HEADS, HG_DK, HG_DK), F32)],
        compiler_params=_cparams(("parallel", "arbitrary")),
        name="hgrn_bwd" if final else "hgrn_fwd",
    )(*args)


def _na_bias_table(rpb, rows):
    nrb = rows // NA_QROWS
    assert nrb >= 3 and rows >= NA_KROWS
    kr = min(NA_ROWS, rows)
    qc = np.arange(GRID_W)
    c0 = np.clip(qc - NA_COLS // 2, 0, GRID_W - NA_COLS)
    kc = np.arange(GRID_W)
    col_ok = (kc[None, :] >= c0[:, None]) & (kc[None, :] < c0[:, None] + NA_COLS)
    dc = np.clip(kc[None, :] - qc[:, None] + NA_COLS - 1, 0, 2 * NA_COLS - 2)
    sel_c = (dc[..., None] == np.arange(2 * NA_COLS - 1)).astype(np.float32)
    sel_r, oks = [], []
    for rb in (0, 1, nrb - 1):
        start = int(np.clip(NA_QROWS * rb - 4, 0, rows - NA_KROWS))
        r = NA_QROWS * rb + np.arange(NA_QROWS)
        r0 = np.clip(r - kr // 2, 0, rows - kr)
        keyrow = start + np.arange(NA_KROWS)
        row_ok = (keyrow[None, :] >= r0[:, None]) & (keyrow[None, :] < r0[:, None] + kr)
        dr = np.clip(keyrow[None, :] - r[:, None] + NA_ROWS - 1, 0, 2 * NA_ROWS - 2)
        sel_r.append((dr[..., None] == np.arange(2 * NA_ROWS - 1)).astype(np.float32))
        oks.append(row_ok[:, None, :, None] & col_ok[None, :, None, :])
    sel_r.append(np.zeros_like(sel_r[0]))
    oks.append(np.zeros_like(oks[0]))
    bias = jnp.einsum("hij,paki,cdj->phackd", rpb.astype(F32), jnp.asarray(np.stack(sel_r)),
                      jnp.asarray(sel_c), precision=HIGHEST)
    bias = jnp.where(jnp.asarray(np.stack(oks))[:, None], bias, NEG_BIG)
    return bias.reshape(4, NA_HEADS, NA_QROWS * GRID_W, NA_KROWS * GRID_W)


def _na_body(q_ref, k_ref, v_ref, kc_ref, vc_ref, bias_ref, o_ref, *, rows):
    rb = pl.program_id(1)
    nk = NA_KROWS * GRID_W
    start_row = jnp.clip(NA_QROWS * rb - 4, 0, rows - NA_KROWS)
    start = pl.multiple_of(start_row * GRID_W, GRID_W)
    nq = q_ref.shape[0]
    lane = lax.broadcasted_iota(jnp.int32, (nq, 128), 1)
    scale = NA_HD ** -0.5
    for p in range(NA_HEADS // 2):
        sl = slice(128 * p, 128 * (p + 1))
        qp = q_ref[:, sl] * scale
        kp = k_ref[pl.ds(start, nk), sl]
        vp = v_ref[pl.ds(start, nk), sl]
        kcp = kc_ref[:, sl]
        vcp = vc_ref[:, sl]
        zero = jnp.zeros_like(qp)
        q2 = jnp.concatenate([jnp.where(lane < NA_HD, qp, zero), jnp.where(lane >= NA_HD, qp, zero)], axis=0)
        s_loc = _dot_nt(q2, kp) + bias_ref[0, 2 * p:2 * p + 2].reshape(2 * nq, nk)
        s_ctx = _dot_nt(q2, kcp)
        m = jnp.maximum(jnp.max(s_loc, axis=-1, keepdims=True), jnp.max(s_ctx, axis=-1, keepdims=True))
        p_loc = jnp.exp(s_loc - m)
        p_ctx = jnp.exp(s_ctx - m)
        den = jnp.sum(p_loc, axis=-1, keepdims=True) + jnp.sum(p_ctx, axis=-1, keepdims=True)
        o2 = (_dot(p_loc.astype(BF16), vp) + _dot(p_ctx.astype(BF16), vcp)) / den
        o_ref[:, sl] = jnp.where(lane < NA_HD, o2[:nq], o2[nq:]).astype(o_ref.dtype)


def _na_call(z, bias, nbatch, t_len, c_len, with_ctx):
    na = z.shape[0]
    rows = t_len // GRID_W
    nrb = rows // NA_QROWS
    nq = NA_QROWS * GRID_W
    assert nq == c_len
    ctx_base = nbatch * nrb
    steps = nrb + 1 if with_ctx else nrb

    def qtile(b, r):
        return jnp.where(r < nrb, b * nrb + r, ctx_base + b)

    def pattern(b, r):
        return jnp.where(r == 0, 0, jnp.where(r == nrb - 1, 2, jnp.where(r == nrb, 3, 1)))

    return pl.pallas_call(
        functools.partial(_na_body, rows=rows),
        grid=(nbatch, steps),
        in_specs=[pl.BlockSpec((nq, 512), lambda b, r: (qtile(b, r), CB_NQ // 4)),
                  pl.BlockSpec((t_len, 512), lambda b, r: (b, CB_NK // 4)),
                  pl.BlockSpec((t_len, 512), lambda b, r: (b, CB_NV // 4)),
                  pl.BlockSpec((c_len, 512), lambda b, r: (ctx_base + b, CB_NK // 4)),
                  pl.BlockSpec((c_len, 512), lambda b, r: (ctx_base + b, CB_NV // 4)),
                  pl.BlockSpec((1,) + bias.shape[1:], lambda b, r: (pattern(b, r), 0, 0, 0))],
        out_specs=pl.BlockSpec((nq, 512), lambda b, r: (qtile(b, r), 0)),
        out_shape=jax.ShapeDtypeStruct((na if with_ctx else nbatch * t_len, 512), BF16),
        compiler_params=_cparams(("parallel", "arbitrary")),
        name="natten",
    )(z, z, z, z, z, bias)


ROW_SLABS = 4


def _store_packed_rows(ref, x):
    m = x.shape[0]

    def bits(v):
        return lax.bitcast_convert_type(v.astype(BF16).astype(F32), jnp.uint32)

    for s in range(ROW_SLABS):
        lo = x[:, 128 * s:128 * (s + 1)]
        hi = x[:, 512 + 128 * s:512 + 128 * (s + 1)]
        ref[pl.ds(s, m, stride=ROW_SLABS), :] = (bits(hi) & jnp.uint32(0xFFFF0000)) | (bits(lo) >> 16)


def _load_packed_rows(ref):
    m = ref.shape[0] // ROW_SLABS
    los, his = [], []
    for s in range(ROW_SLABS):
        p = ref[pl.ds(s, m, stride=ROW_SLABS), :]
        los.append(lax.bitcast_convert_type(p << 16, F32))
        his.append(lax.bitcast_convert_type(p & jnp.uint32(0xFFFF0000), F32))
    return los + his


def _merge_body(xl_ref, xc_ref, zp_ref, za_ref, zn_ref, bc_ref, bp_ref, bn_ref, pcnt_ref, wp_ref, ps_ref,
                yb_ref, yc_ref, g0, g1, g2, g3, g4, g5, wa_ref, wb_ref, wc_ref, wo_ref,
                lng_ref, lnb_ref, mod_ref, wr2_ref, br_ref, tril_ref, x1_ref, h2_ref, route_ref, cnt_ref,
                cnt_scr, *, alpha, split, nt, n_lat_tiles):
    @pl.when(pl.program_id(0) == 0)
    def _():
        cnt_scr[...] = jnp.zeros_like(cnt_scr)

    gates = ((g0, g1), (g2, g3), (g4, g5))
    ya = _pooled_tile(zp_ref, za_ref, zn_ref, bc_ref, bp_ref, bn_ref, pcnt_ref, wp_ref, ps_ref,
                      nt=nt, n_lat_tiles=n_lat_tiles)
    ys = (ya, yb_ref[...], yc_ref[...])
    ws = (wa_ref, wb_ref, wc_ref)
    half = wa_ref.shape[1] // 2
    mix = None
    for n in range(2):
        m = None
        for kbr in range(3):
            pr = _dot(ys[kbr], ws[kbr][:, n * half:(n + 1) * half])
            term = _sigmoid_t(gates[kbr][n][...].astype(F32)) * pr
            m = term if m is None else m + term
        part = _dot(m.astype(BF16), wo_ref[n * half:(n + 1) * half, :])
        mix = part if mix is None else mix + part
    x = jnp.where(pl.program_id(0) < split, xl_ref[...], xc_ref[...])
    r = alpha * x + mod_ref[0, 2:3, :] * mix
    x1 = _ln(r) * lng_ref[...] + lnb_ref[...]
    x1_ref[...] = x1
    h2 = _ln(x1) * (1.0 + mod_ref[0, 4:5, :]) + mod_ref[0, 3:4, :]
    _store_packed_rows(h2_ref, h2)

    h2_hi = h2.astype(BF16)
    h2_lo = (h2 - h2_hi.astype(F32)).astype(BF16)
    hh = _dot(h2_hi, wr2_ref[...])
    logits = (hh[:, :128] + _dot(h2_lo, wr2_ref[:, :128]) + hh[:, 128:]) + br_ref[...]
    lane = lax.broadcasted_iota(jnp.int32, logits.shape, 1).astype(F32)
    is_grp = jnp.where(lane >= N_EXPERTS, jnp.where(lane < N_EXPERTS + N_GROUPS, 1.0, 0.0), 0.0) > 0.5
    lgm = jnp.where(is_grp, logits, NEG_BIG)
    mg = jnp.max(lgm, axis=-1, keepdims=True)
    p_grp = 1.0 / jnp.sum(jnp.exp(lgm - mg), axis=-1, keepdims=True)
    grp = jnp.min(jnp.where(lgm == mg, lane, 1e9), axis=-1, keepdims=True) - N_EXPERTS
    lo = grp * EXP_PER_GROUP
    in_grp = jnp.where(lane >= lo, jnp.where(lane < lo + EXP_PER_GROUP, 1.0, 0.0), 0.0) > 0.5
    lem = jnp.where(in_grp, logits, NEG_BIG)
    m1 = jnp.max(lem, axis=-1, keepdims=True)
    id1 = jnp.min(jnp.where(lem == m1, lane, 1e9), axis=-1, keepdims=True)
    lem2 = jnp.where(lane == id1, NEG_BIG, lem)
    m2 = jnp.max(lem2, axis=-1, keepdims=True)
    id2 = jnp.min(jnp.where(lem2 == m2, lane, 1e9), axis=-1, keepdims=True)
    u2 = jnp.exp(m2 - m1)
    w1 = p_grp / (1.0 + u2)
    w2 = p_grp * u2 / (1.0 + u2)
    oh1 = jnp.where(lane == id1, 1.0, 0.0)
    oh2 = jnp.where(lane == id2, 1.0, 0.0)
    oh = oh1 + oh2
    before = _dot(tril_ref[...], oh.astype(BF16)) + cnt_scr[...]
    rank1 = jnp.sum(before * oh1, axis=-1, keepdims=True)
    rank2 = jnp.sum(before * oh2, axis=-1, keepdims=True)
    cnt_scr[...] += jnp.sum(oh, axis=0, keepdims=True)
    cnt_ref[...] = jnp.broadcast_to(cnt_scr[...], cnt_ref.shape)
    route = jnp.zeros_like(logits)
    for ln, val in enumerate((w1, w2, id1, id2, rank1, rank2)):
        route = jnp.where(lane == ln, val, route)
    route_ref[...] = route


def _merge_call(tok, pool, yb, yc, z, mod, wa, wb, wc, wo, lng, lnb, wr, br, tm, modrow, alpha, t_len):
    na, d = yc.shape[0], tok[0].shape[1]
    split, tok_specs = _token_specs(tok, tm)
    assert pool[0][0].shape[2] == tm
    pool_specs, pool_args, n_lat_tiles = _pool_specs(z, *pool)
    row = lambda w: pl.BlockSpec((tm, w), lambda i: (i, 0))
    gate = lambda cb: pl.BlockSpec((tm, 512), lambda i: (i, cb))
    full = lambda a: pl.BlockSpec(a.shape, lambda i: (0,) * a.ndim)
    g0 = CB_GATE // 4
    tril = jnp.asarray(np.tril(np.ones((tm, tm), np.float32), -1), BF16)
    wr_hi = lax.reduce_precision(wr, exponent_bits=8, mantissa_bits=7)
    wr2 = jnp.concatenate([wr_hi, wr - wr_hi], axis=1).astype(BF16)
    return pl.pallas_call(
        functools.partial(_merge_body, alpha=alpha, split=split, nt=t_len // tm, n_lat_tiles=n_lat_tiles),
        grid=(na // tm,),
        in_specs=tok_specs + pool_specs + [row(512), row(512)] + [gate(g0 + j) for j in range(6)]
                 + [full(wa), full(wb), full(wc), full(wo), full(lng), full(lnb),
                    pl.BlockSpec((1, 6, d), lambda i: (modrow(i), 0, 0)), full(wr2), full(br), full(tril)],
        out_specs=[row(d), pl.BlockSpec((tm * ROW_SLABS, 128), lambda i: (i, 0)), row(128),
                   pl.BlockSpec((8, 128), lambda i: (0, 0))],
        out_shape=[jax.ShapeDtypeStruct((na, d), F32), jax.ShapeDtypeStruct((na * ROW_SLABS, 128), jnp.uint32),
                   jax.ShapeDtypeStruct((na, 128), F32), jax.ShapeDtypeStruct((8, 128), F32)],
        scratch_shapes=[pltpu.VMEM((1, 128), F32)],
        compiler_params=_cparams(("arbitrary",)),
        name="merge",
    )(tok[0], tok[1], *pool_args, yb, yc, z, z, z, z, z, z, wa, wb, wc, wo, lng, lnb, mod, wr2, br, tril)


ROW_DMA_UNROLL = 8


def _slab(ref, row):
    start = row * ROW_SLABS
    if not isinstance(row, int):
        start = pl.multiple_of(start, ROW_SLABS)
    return ref.at[pl.ds(start, ROW_SLABS), :]


def _slots_body(route_ref, seg_ref, o_ref):
    route = route_ref[...]
    lane = lax.broadcasted_iota(jnp.int32, route.shape, 1).astype(F32)
    seg = seg_ref[...]
    slots = []
    for k in range(2):
        eid = route[:, 2 + k:3 + k]
        start = jnp.sum(jnp.where(lane == eid, seg, 0.0), axis=-1, keepdims=True)
        slots.append(start + route[:, 4 + k:5 + k])
    both = jnp.where(lane == 0.0, slots[0], jnp.where(lane == 1.0, slots[1], 0.0))
    by_token = both.T[:8].astype(jnp.int32)
    tm = o_ref.shape[2]
    for c in range(o_ref.shape[0]):
        o_ref[c] = by_token[:, c * tm:(c + 1) * tm]


def _slots_call(route, seg, tm):
    na = route.shape[0]
    group = _row_tile(4, na // tm)
    return pl.pallas_call(
        _slots_body,
        grid=(na // (tm * group),),
        in_specs=[pl.BlockSpec((tm * group, 128), lambda i: (i, 0)), pl.BlockSpec((1, 128), lambda i: (0, 0))],
        out_specs=pl.BlockSpec((group, 8, tm), lambda i: (i, 0, 0)),
        out_shape=jax.ShapeDtypeStruct((na // tm, 8, tm), jnp.int32),
        compiler_params=_cparams(("parallel",)),
        name="slots",
    )(route, seg)


def _dispatch_body(slot_ref, h_ref, xs_in, xs_out, sem):
    del xs_in
    tm = h_ref.shape[0] // ROW_SLABS

    def issue(i, carry):
        for u in range(ROW_DMA_UNROLL):
            t = i * ROW_DMA_UNROLL + u
            for k in range(2):
                pltpu.make_async_copy(_slab(h_ref, t), _slab(xs_out, slot_ref[0, k, t]), sem).start(priority=k)
        return carry

    lax.fori_loop(0, tm // ROW_DMA_UNROLL, issue, 0)

    for k in range(2):
        pltpu.make_async_copy(h_ref, xs_out.at[pl.ds(0, tm * ROW_SLABS), :], sem).wait()


def _dispatch_call(h2p, slots, n_rows, tm, xs_prev=None):
    xs0 = jnp.zeros((n_rows * ROW_SLABS, 128), h2p.dtype) if xs_prev is None else xs_prev
    assert xs0.shape[0] >= n_rows * ROW_SLABS
    return pl.pallas_call(
        _dispatch_body,
        grid=(slots.shape[0],),
        in_specs=[pl.BlockSpec((1, 8, tm), lambda i: (i, 0, 0), memory_space=pltpu.SMEM),
                  pl.BlockSpec((tm * ROW_SLABS, 128), lambda i: (i, 0)),
                  pl.BlockSpec(memory_space=pl.ANY)],
        out_specs=pl.BlockSpec(memory_space=pl.ANY),
        out_shape=jax.ShapeDtypeStruct(xs0.shape, xs0.dtype),
        input_output_aliases={2: 0},
        scratch_shapes=[pltpu.SemaphoreType.DMA(())],
        compiler_params=_cparams(("arbitrary",)),
        name="dispatch",
    )(slots, h2p, xs0)


def _experts_body(te_ref, nu_ref, xs_ref, wg_ref, wu_ref, wd_ref, y_ref, wg_scr, wu_scr, wd_scr):
    j = pl.program_id(0)

    @pl.when(j >= nu_ref[0])
    def _():
        y_ref[...] = jnp.zeros_like(y_ref)

    @pl.when(jnp.logical_and(j < nu_ref[0], jnp.logical_or(j == 0, te_ref[j] != te_ref[jnp.maximum(j - 1, 0)])))
    def _():
        wg_scr[...] = wg_ref[0].astype(BF16)
        wu_scr[...] = wu_ref[0].astype(BF16)
        wd_scr[...] = wd_ref[0].astype(BF16)

    @pl.when(j < nu_ref[0])
    def _():
        x = jnp.concatenate([c.astype(BF16) for c in _load_packed_rows(xs_ref)], axis=1)
        gt = _dot(x, wg_scr[...])
        a = gt * _sigmoid_t(gt) * _dot(x, wu_scr[...])
        _store_packed_rows(y_ref, _dot(a.astype(BF16), wd_scr[...]))


def _experts_call(tile_expert, n_used, xs, wg, wu, wd, layer, tr):
    rows, w = xs.shape
    _, ne, d, de = wg.shape
    tr = tr * ROW_SLABS
    used = lambda j, te, nu: jnp.minimum(j, nu[0] - 1)
    return pl.pallas_call(
        _experts_body,
        grid_spec=pltpu.PrefetchScalarGridSpec(
            num_scalar_prefetch=2,
            grid=(rows // tr,),
            in_specs=[pl.BlockSpec((tr, w), lambda j, te, nu: (used(j, te, nu), 0)),
                      pl.BlockSpec((None, 1, d, de), lambda j, te, nu: (layer, te[used(j, te, nu)], 0, 0)),
                      pl.BlockSpec((None, 1, d, de), lambda j, te, nu: (layer, te[used(j, te, nu)], 0, 0)),
                      pl.BlockSpec((None, 1, de, d), lambda j, te, nu: (layer, te[used(j, te, nu)], 0, 0))],
            out_specs=pl.BlockSpec((tr, w), lambda j, te, nu: (j, 0)),
            scratch_shapes=[pltpu.VMEM((d, de), BF16), pltpu.VMEM((d, de), BF16), pltpu.VMEM((de, d), BF16)]),
        out_shape=jax.ShapeDtypeStruct((rows, w), jnp.uint32),
        compiler_params=_cparams(("arbitrary",)),
        name="experts",
    )(tile_expert, n_used, xs, wg, wu, wd)


def _combine_body(slot_ref, next_ref, y_hbm, x1_ref, route_ref, lng_ref, lnb_ref, mod_ref, o_ref,
                  buf_scr, sems, *, alpha):
    tm = x1_ref.shape[0]
    i = pl.program_id(0)
    cur = i % 2

    def gather(idx_ref, b):
        def issue(j, carry):
            for u in range(ROW_DMA_UNROLL):
                t = j * ROW_DMA_UNROLL + u
                for k in range(2):
                    pltpu.make_async_copy(_slab(y_hbm, idx_ref[0, k, t]), _slab(buf_scr.at[b, k], t),
                                          sems.at[b]).start(priority=k)
            return carry

        lax.fori_loop(0, tm // ROW_DMA_UNROLL, issue, 0)

    def wait_slot(b):
        for k in range(2):
            pltpu.make_async_copy(y_hbm.at[pl.ds(0, tm * ROW_SLABS), :], buf_scr.at[b, k], sems.at[b]).wait()

    @pl.when(i == 0)
    def _():
        gather(slot_ref, cur)

    wait_slot(cur)
    for t in range(tm):
        for k in range(2):
            pltpu.make_async_copy(_slab(y_hbm, next_ref[0, k, t]), _slab(buf_scr.at[1 - cur, k], t),
                                  sems.at[1 - cur]).start(priority=k)

    w1 = route_ref[:, 0:1]
    w2 = route_ref[:, 1:2]
    moe = jnp.concatenate([w1 * a1 + w2 * a2 for a1, a2 in zip(_load_packed_rows(buf_scr.at[cur, 0]),
                                                               _load_packed_rows(buf_scr.at[cur, 1]))], axis=1)
    r = alpha * x1_ref[...] + mod_ref[0, 5:6, :] * moe
    o_ref[...] = _ln(r) * lng_ref[...] + lnb_ref[...]

    @pl.when(i == pl.num_programs(0) - 1)
    def _():
        wait_slot(1 - cur)


def _combine_call(slots, y, x1, route, lng, lnb, mod, tm, modrow, alpha):
    na, d = x1.shape
    full = lambda a: pl.BlockSpec(a.shape, lambda i: (0,) * a.ndim)
    last = na // tm - 1
    return pl.pallas_call(
        functools.partial(_combine_body, alpha=alpha),
        grid=(na // tm,),
        in_specs=[pl.BlockSpec((1, 8, tm), lambda i: (i, 0, 0), memory_space=pltpu.SMEM),
                  pl.BlockSpec((1, 8, tm), lambda i: (jnp.minimum(i + 1, last), 0, 0), memory_space=pltpu.SMEM),
                  pl.BlockSpec(memory_space=pl.ANY),
                  pl.BlockSpec((tm, d), lambda i: (i, 0)),
                  pl.BlockSpec((tm, 128), lambda i: (i, 0)),
                  full(lng), full(lnb),
                  pl.BlockSpec((1, 6, d), lambda i: (modrow(i), 0, 0))],
        out_specs=pl.BlockSpec((tm, d), lambda i: (i, 0)),
        out_shape=jax.ShapeDtypeStruct((na, d), F32),
        scratch_shapes=[pltpu.VMEM((2, 2, tm * ROW_SLABS, 128), jnp.uint32), pltpu.SemaphoreType.DMA((2,))],
        compiler_params=_cparams(("arbitrary",)),
        name="combine",
    )(slots, slots, y, x1, route, lng, lnb, mod)


def _expert_buffer_tiles(n_tokens, tr):
    return -(-2 * n_tokens // tr) + N_EXPERTS


def _routing_tables(counts, n_tiles, tr):
    cnt = counts[0, :N_EXPERTS].astype(jnp.int32)
    ntile = (cnt + tr - 1) // tr
    tile_start = jnp.cumsum(ntile) - ntile
    n_used = jnp.sum(ntile)
    seg = jnp.zeros((1, 128), F32).at[0, :N_EXPERTS].set((tile_start * tr).astype(F32))
    tile_expert = jnp.sum(jnp.arange(n_tiles, dtype=jnp.int32)[:, None] >= tile_start[None, :], axis=1) - 1
    return seg, tile_expert.astype(jnp.int32), n_used.reshape(1).astype(jnp.int32)


def _lower_bounds(logits):
    p = jax.nn.softmax(logits.astype(F32), axis=0)
    return jnp.cumsum(p, axis=0) - p[:1]


def _row_tile(limit, *sizes):
    tm = limit
    while any(s % tm for s in sizes):
        tm //= 2
    return tm


def kernel(x, c, ctx, c_ctx, w_ada, b_ada, w_in, w_pool, pool_scale, lb_logits_fwd, lb_logits_bwd, hg_gain, rpb, w_br_a, w_br_b, w_br_c, w_out, ln1_g, ln1_b, w_rg, b_rg, w_re, b_re, w_gate, w_up, w_down, ln2_g, ln2_b):
    nbatch, t_len, d = x.shape
    c_len = ctx.shape[1]
    depth = w_ada.shape[0]
    assert c_len == SEQ_TILE and t_len % SEQ_TILE == 0 and t_len % GRID_W == 0
    alpha = (2.0 * depth) ** 0.25
    n_lat = nbatch * t_len
    nt = t_len // SEQ_TILE

    na = n_lat + nbatch * c_len
    tok = (x.reshape(n_lat, d), ctx.reshape(nbatch * c_len, d), 0)

    mod_rows = -(-(nbatch + 1) // 8) * 8
    cc = jnp.zeros((mod_rows, d), F32).at[:nbatch].set(c).at[nbatch].set(c_ctx)
    ada = _ada_call(cc, w_ada, b_ada)

    lb_f = _lower_bounds(lb_logits_fwd).reshape(depth, HG_HEADS, 1, HG_DK)
    lb_b = _lower_bounds(lb_logits_bwd).reshape(depth, HG_HEADS, 1, HG_DK)
    tm_big = _row_tile(1024, t_len, nbatch * c_len)
    tm_mid = _row_tile(512, t_len, nbatch * c_len)
    assert tm_mid % c_len == 0
    pool_consts = _pool_consts(tm_mid, c_len)
    hg_f = _hgrn_consts(False)
    hg_b = _hgrn_consts(True)

    def modrow_for(tm):
        return lambda i: jnp.where(i * tm < n_lat, (i * tm) // t_len, nbatch)

    n_tiles = _expert_buffer_tiles(na, EXPERT_TILE)
    xs = None
    for l in range(depth):
        last = l == depth - 1
        mod = ada[l].reshape(mod_rows, 6, d)
        z = _inproj_call(tok, na, mod, w_in[l].astype(BF16), tm_big, modrow_for(tm_big))
        pool = (pool_consts, w_pool[l].astype(BF16), pool_scale[l].reshape(1, -1), n_lat)
        o_f =_hgrn_call(z, lb_f[l], hg_f, nbatch, nt, reverse=False)
        yb = _hgrn_call(z, lb_b[l], hg_b, nbatch, nt, reverse=True, o_fwd=o_f,
                        gain=hg_gain[l].reshape(HG_HEADS, 1, HG_DK))
        bias = _na_bias_table(rpb[l], t_len // GRID_W)
        yc = _na_call(z, bias, nbatch, t_len, c_len, with_ctx=not last)
        wr = jnp.zeros((d, 128), F32).at[:, :N_EXPERTS].set(w_re[l]).at[:, N_EXPERTS:N_EXPERTS + N_GROUPS].set(w_rg[l])
        br = jnp.zeros((1, 128), F32).at[0, :N_EXPERTS].set(b_re[l]).at[0, N_EXPERTS:N_EXPERTS + N_GROUPS].set(b_rg[l])
        x1, h2p, route, counts = _merge_call(
            tok, pool, yb, yc, z, mod, w_br_a[l].astype(BF16), w_br_b[l].astype(BF16), w_br_c[l].astype(BF16),
            w_out[l].astype(BF16), ln1_g[l].reshape(1, d), ln1_b[l].reshape(1, d), wr, br, tm_mid,
            modrow_for(tm_mid), alpha, t_len)
        seg, tile_expert, n_used = _routing_tables(counts, n_tiles, EXPERT_TILE)
        slots = _slots_call(route, seg, tm_mid)
        xs = _dispatch_call(h2p, slots, n_tiles * EXPERT_TILE, tm_mid, xs)
        ys = _experts_call(tile_expert, n_used, xs, w_gate, w_up, w_down, l, EXPERT_TILE)
        xa = _combine_call(slots, ys, x1, route, ln2_g[l].reshape(1, d), ln2_b[l].reshape(1, d), mod,
                           tm_mid, modrow_for(tm_mid), alpha)
        tok = (xa, xa, n_lat)
    return xa.reshape(nbatch, t_len, d)
```

```python
import functools

import numpy as np
import jax
import jax.numpy as jnp
from jax import lax
from jax.experimental import pallas as pl
from jax.experimental.pallas import tpu as pltpu

F32 = jnp.float32
BF16 = jnp.bfloat16
HIGHEST = lax.Precision.HIGHEST

GRID_W = 64
POOL_WINDOWS = (2, 4, 8, 16)
POOL_GDIM = 128
HG_HEADS = 4
HG_DK = 128
HG_BLOCK = 16
NA_HEADS = 8
NA_HD = 64
NA_ROWS = 8
NA_COLS = 16
NA_QROWS = 4
NA_KROWS = 12
N_GROUPS = 4
EXP_PER_GROUP = 8
N_EXPERTS = N_GROUPS * EXP_PER_GROUP
LN_EPS = 1e-5
RMS_EPS = 1e-6
NEG_BIG = -1e30
SEQ_TILE = 256
EXPERT_TILE = 512
VMEM_LIMIT = 56 * 1024 * 1024

CB_A, CB_Q, CB_FF, CB_FB, CB_I, CB_G, CB_NQ, CB_NK, CB_NV, CB_GATE = 0, 4, 8, 12, 16, 20, 24, 28, 32, 36


def _cparams(sem):
    return pltpu.CompilerParams(dimension_semantics=sem, vmem_limit_bytes=VMEM_LIMIT)


def _ln(x):
    mu = jnp.mean(x, axis=-1, keepdims=True)
    xc = x - mu
    var = jnp.mean(xc * xc, axis=-1, keepdims=True)
    return xc * lax.rsqrt(var + LN_EPS)


def _sigmoid(x):
    return 1.0 / (1.0 + jnp.exp(-x))


def _sigmoid_t(x):
    return 0.5 * jnp.tanh(0.5 * x) + 0.5


def _dot(a, b):
    return jnp.dot(a, b, preferred_element_type=F32)


def _dot_nt(a, b):
    return lax.dot_general(a, b, (((1,), (1,)), ((), ())), preferred_element_type=F32)


def _dot_tn(a, b):
    return lax.dot_general(a, b, (((0,), (0,)), ((), ())), preferred_element_type=F32)


def _dot01(m01, x):
    x1 = x.astype(BF16)
    r1 = x - x1.astype(F32)
    x2 = r1.astype(BF16)
    x3 = (r1 - x2.astype(F32)).astype(BF16)
    return _dot(m01, x1) + _dot(m01, x2) + _dot(m01, x3)


def _ada_body(c_ref, w_ref, b_ref, o_ref):
    cs = c_ref[...]
    s = cs * _sigmoid(cs)
    o_ref[0] = jnp.dot(s, w_ref[0], preferred_element_type=F32, precision=HIGHEST) + b_ref[0]


def _ada_call(cc, w_ada, b_ada):
    depth, d, n6 = w_ada.shape
    rows = cc.shape[0]
    return pl.pallas_call(
        _ada_body,
        grid=(depth, n6 // d),
        in_specs=[pl.BlockSpec((rows, d), lambda l, j: (0, 0)),
                  pl.BlockSpec((1, d, d), lambda l, j: (l, 0, j)),
                  pl.BlockSpec((1, 1, d), lambda l, j: (l, 0, j))],
        out_specs=pl.BlockSpec((1, rows, d), lambda l, j: (l, 0, j)),
        out_shape=jax.ShapeDtypeStruct((depth, rows, n6), F32),
        compiler_params=_cparams(("parallel", "parallel")),
        name="ada",
    )(cc, w_ada, b_ada.reshape(depth, 1, n6))


def _token_specs(tok, tm):
    lat, ctx, ctx_row0 = tok
    split = (ctx_row0 if ctx is lat else lat.shape[0]) // tm
    off = ctx_row0 // tm
    d = lat.shape[1]
    lat_map = lambda i, *_: (jnp.minimum(i, split - 1), 0)
    ctx_map = lambda i, *_: (jnp.maximum(i - split, 0) + off, 0)
    return split, [pl.BlockSpec((tm, d), lat_map), pl.BlockSpec((tm, d), ctx_map)]


def _inproj_body(xl_ref, xc_ref, mod_ref, w_ref, z_ref, h_scr, *, split):
    def modulated(x_ref):
        h = _ln(x_ref[...]) * (1.0 + mod_ref[0, 1:2, :]) + mod_ref[0, 0:1, :]
        h_scr[...] = h.astype(BF16)

    first = pl.program_id(1) == 0
    pl.when(jnp.logical_and(first, pl.program_id(0) < split))(lambda: modulated(xl_ref))
    pl.when(jnp.logical_and(first, pl.program_id(0) >= split))(lambda: modulated(xc_ref))
    z_ref[...] = _dot(h_scr[...], w_ref[...]).astype(z_ref.dtype)


def _inproj_call(tok, na, mod, w_in, tm, modrow):
    d, d_in = w_in.shape
    tn = 2560
    assert d_in % tn == 0
    split, tok_specs = _token_specs(tok, tm)
    return pl.pallas_call(
        functools.partial(_inproj_body, split=split),
        grid=(na // tm, d_in // tn),
        in_specs=tok_specs + [pl.BlockSpec((1, 6, d), lambda i, j: (modrow(i), 0, 0)),
                              pl.BlockSpec((d, tn), lambda i, j: (0, j))],
        out_specs=pl.BlockSpec((tm, tn), lambda i, j: (i, j)),
        out_shape=jax.ShapeDtypeStruct((na, d_in), BF16),
        scratch_shapes=[pltpu.VMEM((tm, d), BF16)],
        compiler_params=_cparams(("parallel", "arbitrary")),
        name="inproj",
    )(tok[0], tok[1], mod, w_in)


def _pool_consts(n, c_len):
    t = np.arange(n)[:, None]
    bc = np.zeros((2, 4, n, n), np.float32)
    bp = np.zeros((4, n, 16), np.float32)
    bn = np.zeros((4, n, 16), np.float32)
    for g, win in enumerate(POOL_WINDOWS):
        lo, hi = t - win // 2, t + win // 2 - 1
        s = np.arange(n)[None, :]
        bc[0, g] = (s >= lo) & (s <= hi)
        bc[1, g] = bc[0, g] * (s // c_len == t // c_len)
        s = np.arange(16)[None, :] - 16
        bp[g] = (s >= lo) & (s <= hi)
        s = np.arange(16)[None, :] + n
        bn[g] = (s >= lo) & (s <= hi)
    cnt = np.stack([np.stack([bc[0].sum(-1), bp.sum(-1), bn.sum(-1)], axis=1),
                    np.stack([bc[1].sum(-1), 0 * bp.sum(-1), 0 * bn.sum(-1)], axis=1)])
    cnt = np.broadcast_to(cnt[..., None], (2, 4, 3, n, 128)).astype(np.float32)
    return (jnp.asarray(bc, BF16), jnp.asarray(bp, BF16), jnp.asarray(bn, BF16), jnp.asarray(cnt))


def _pooled_tile(prev_ref, cur_ref, next_ref, bc_ref, bp_ref, bn_ref, cnt_ref, wp_ref, ps_ref, *, nt, n_lat_tiles):
    i = pl.program_id(0)
    k = i % nt
    lat = i < n_lat_tiles
    has_prev = jnp.where(jnp.logical_and(lat, k != 0), 1.0, 0.0).astype(F32)
    has_next = jnp.where(jnp.logical_and(lat, k != nt - 1), 1.0, 0.0).astype(F32)
    groups = []
    for g in range(len(POOL_WINDOWS)):
        sl = slice(g * POOL_GDIM, (g + 1) * POOL_GDIM)
        u = cur_ref[:, sl]
        ssum = (_dot(bc_ref[0, g], u) + has_prev * _dot(bp_ref[g], prev_ref[:, sl])
                + has_next * _dot(bn_ref[g], next_ref[:, sl]))
        cnt = cnt_ref[0, g, 0] + has_prev * cnt_ref[0, g, 1] + has_next * cnt_ref[0, g, 2]
        dlt = ssum / cnt - u.astype(F32)
        groups.append((_dot(dlt.astype(BF16), wp_ref[g]) * ps_ref[:, sl]).astype(BF16))
    return jnp.concatenate(groups, axis=1)


def _pool_specs(z, consts, w_pool, pool_scale, n_lat):
    bc, bp, bn, cnt = consts
    n = bc.shape[2]
    hb = n // 16
    last16 = z.shape[0] // 16 - 1
    n_lat_tiles = n_lat // n
    full = lambda a: pl.BlockSpec(a.shape, lambda i: (0,) * a.ndim)
    variant = lambda a: pl.BlockSpec((1,) + a.shape[1:],
                                     lambda i: (jnp.where(i < n_lat_tiles, 0, 1),) + (0,) * (a.ndim - 1))
    specs = [pl.BlockSpec((16, 512), lambda i: (jnp.maximum(i * hb - 1, 0), CB_A // 4)),
             pl.BlockSpec((n, 512), lambda i: (i, CB_A // 4)),
             pl.BlockSpec((16, 512), lambda i: (jnp.minimum((i + 1) * hb, last16), CB_A // 4)),
             variant(bc), full(bp), full(bn), variant(cnt), full(w_pool), full(pool_scale)]
    return specs, [z, z, z, bc, bp, bn, cnt, w_pool, pool_scale], n_lat_tiles


def _hgrn_consts(reverse):
    n, bs = SEQ_TILE, HG_BLOCK
    nb = n // bs
    t = np.arange(n)
    o = (n - 1 - t) if reverse else t
    blk = t // bs
    jb = np.arange(nb)
    ob = (nb - 1 - jb) if reverse else jb
    cum = ((blk[:, None] == blk[None, :]) & (o[None, :] <= o[:, None])).astype(np.float32)
    bsum = (jb[:, None] == blk[None, :]).astype(np.float32)
    widths = [2 ** l for l in range(1, int(np.log2(nb)) + 1)]
    lvl = np.full((n, n), -1, np.int32)
    obt = ob[blk]
    same = blk[:, None] == blk[None, :]
    lvl[same & (o[None, :] <= o[:, None])] = 0
    for li, w in reversed(list(enumerate(widths, start=1))):
        m = (obt[:, None] // w == obt[None, :] // w) & (obt[None, :] < obt[:, None]) & ~same
        lvl[m] = li
    mats = []
    for w in widths:
        mid = (ob // w) * w + w // 2
        mats.append((mid[:, None] <= ob[None, :]) & (ob[None, :] < ob[:, None]))
    for w in widths:
        mid = (ob // w) * w + w // 2
        mats.append((ob[:, None] < ob[None, :]) & (ob[None, :] < mid[:, None]))
    mats.append(ob[None, :] < ob[:, None])
    mats.append(ob[None, :] > ob[:, None])
    mats.append(np.ones((nb, nb), bool))
    tsm = np.concatenate(mats, axis=0).astype(np.float32)
    return (jnp.asarray(cum, BF16), jnp.asarray(bsum, BF16), jnp.asarray(lvl, BF16), jnp.asarray(tsm, BF16),
            len(widths))


def _expand_blocks(c, n):
    nb, lanes = c.shape
    return jnp.concatenate([jnp.broadcast_to(c[j:j + 1, :], (n // nb, lanes)) for j in range(nb)], axis=0)


def _hgrn_body(*refs, n_levels, final):
    if final:
        (zq_ref, zf_ref, zi_ref, lb_ref, cum_ref, bsum_ref, lvl_ref, tsm_ref,
         of_ref, zg_ref, gain_ref, o_ref, st_scr) = refs
    else:
        zq_ref, zf_ref, zi_ref, lb_ref, cum_ref, bsum_ref, lvl_ref, tsm_ref, o_ref, st_scr = refs
    n = zq_ref.shape[0]
    nb = n // HG_BLOCK

    @pl.when(pl.program_id(1) == 0)
    def _():
        st_scr[...] = jnp.zeros_like(st_scr)

    zq = zq_ref[...].astype(F32)
    zf = zf_ref[...].astype(F32)
    lb = jnp.concatenate([lb_ref[h] for h in range(HG_HEADS)], axis=1)
    sig = _sigmoid(zf)
    lf = jnp.log(lb + (1.0 - lb) * sig)
    k_all = (1.0 - lb) * (1.0 - sig)
    q_all = zq * _sigmoid_t(zq)
    lf_hi = lf.astype(BF16)
    lf2 = jnp.concatenate([lf_hi, (lf - lf_hi.astype(F32)).astype(BF16)], axis=1)
    w = HG_HEADS * HG_DK
    b2 = _dot(cum_ref[...], lf2)
    b_all = b2[:, :w] + b2[:, w:]
    t2 = _dot(bsum_ref[...], lf2)
    tot_all = t2[:, :w] + t2[:, w:]
    coef_all = _dot01(tsm_ref[...], tot_all)
    qd_all = q_all * jnp.exp(b_all)
    kd_all = (k_all * jnp.exp(-b_all)).astype(BF16)
    ks_all = k_all * jnp.exp(_expand_blocks(tot_all, n) - b_all)

    ecoef = jnp.exp(coef_all)
    scale = lambda idx: _expand_blocks(ecoef[idx * nb:(idx + 1) * nb], n)
    q_lv = [qd_all.astype(BF16)] + [(qd_all * scale(li)).astype(BF16) for li in range(n_levels)]
    k_lv = [kd_all] + [(ks_all * scale(n_levels + li)).astype(BF16) for li in range(n_levels)]
    qs_all = (qd_all * scale(2 * n_levels)).astype(BF16)
    kn_all = (ks_all * scale(2 * n_levels + 1)).astype(BF16)
    dec_all = ecoef[(2 * n_levels + 2) * nb:(2 * n_levels + 2) * nb + 1]

    for h in range(HG_HEADS):
        sl = slice(h * HG_DK, (h + 1) * HG_DK)
        v = zi_ref[:, sl]

        lvl = lvl_ref[...]
        a = jnp.zeros((n, n), BF16)
        for li in range(n_levels + 1):
            a = jnp.where(lvl == float(li), _dot_nt(q_lv[li][:, sl], k_lv[li][:, sl]).astype(BF16), a)
        o = _dot(a, v)

        st = st_scr[h]
        o = o + _dot_nt(qs_all[:, sl], st.astype(BF16))
        st_scr[h] = st * dec_all[:, sl] + _dot_tn(v, kn_all[:, sl])

        if final:
            o = o + of_ref[:, sl]
            o = o * lax.rsqrt(jnp.mean(o * o, axis=-1, keepdims=True) + RMS_EPS) * gain_ref[h]
            zg = zg_ref[:, sl].astype(F32)
            o_ref[:, sl] = (o * (zg * _sigmoid_t(zg))).astype(o_ref.dtype)
        else:
            o_ref[:, sl] = o


def _hgrn_call(z, lb, consts, nbatch, nt, reverse, o_fwd=None, gain=None):
    na = z.shape[0]
    n = SEQ_TILE
    cum, bsum, lvl, tsm, n_levels = consts
    ctx_base = nbatch * nt
    final = o_fwd is not None

    def tile(b, s):
        lat = (b * nt + nt - s) if reverse else (b * nt + s - 1)
        return jnp.where(s == 0, ctx_base + b, lat)

    width = HG_HEADS * HG_DK

    def col(cb):
        return pl.BlockSpec((n, width), lambda b, s: (tile(b, s), cb // HG_HEADS))

    full = lambda a: pl.BlockSpec(a.shape, lambda b, s: (0,) * a.ndim)
    in_specs = [col(CB_Q), col(CB_FB if reverse else CB_FF), col(CB_I), full(lb),
                full(cum), full(bsum), full(lvl), full(tsm)]
    args = [z, z, z, lb, cum, bsum, lvl, tsm]
    if final:
        in_specs += [col(0), col(CB_G), full(gain)]
        args += [o_fwd, z, gain]
    return pl.pallas_call(
        functools.partial(_hgrn_body, n_levels=n_levels, final=final),
        grid=(nbatch, nt + 1),
        in_specs=in_specs,
        out_specs=col(0),
        out_shape=jax.ShapeDtypeStruct((na, width), BF16 if final else F32),
        scratch_shapes=[pltpu.VMEM((HG_HEADS, HG_DK, HG_DK), F32)],
        compiler_params=_cparams(("parallel", "arbitrary")),
        name="hgrn_bwd" if final else "hgrn_fwd",
    )(*args)


def _na_bias_table(rpb, rows):
    nrb = rows // NA_QROWS
    assert nrb >= 3 and rows >= NA_KROWS
    kr = min(NA_ROWS, rows)
    qc = np.arange(GRID_W)
    c0 = np.clip(qc - NA_COLS // 2, 0, GRID_W - NA_COLS)
    kc = np.arange(GRID_W)
    col_ok = (kc[None, :] >= c0[:, None]) & (kc[None, :] < c0[:, None] + NA_COLS)
    dc = np.clip(kc[None, :] - qc[:, None] + NA_COLS - 1, 0, 2 * NA_COLS - 2)
    sel_c = (dc[..., None] == np.arange(2 * NA_COLS - 1)).astype(np.float32)
    sel_r, oks = [], []
    for rb in (0, 1, nrb - 1):
        start = int(np.clip(NA_QROWS * rb - 4, 0, rows - NA_KROWS))
        r = NA_QROWS * rb + np.arange(NA_QROWS)
        r0 = np.clip(r - kr // 2, 0, rows - kr)
        keyrow = start + np.arange(NA_KROWS)
        row_ok = (keyrow[None, :] >= r0[:, None]) & (keyrow[None, :] < r0[:, None] + kr)
        dr = np.clip(keyrow[None, :] - r[:, None] + NA_ROWS - 1, 0, 2 * NA_ROWS - 2)
        sel_r.append((dr[..., None] == np.arange(2 * NA_ROWS - 1)).astype(np.float32))
        oks.append(row_ok[:, None, :, None] & col_ok[None, :, None, :])
    sel_r.append(np.zeros_like(sel_r[0]))
    oks.append(np.zeros_like(oks[0]))
    bias = jnp.einsum("hij,paki,cdj->phackd", rpb.astype(F32), jnp.asarray(np.stack(sel_r)),
                      jnp.asarray(sel_c), precision=HIGHEST)
    bias = jnp.where(jnp.asarray(np.stack(oks))[:, None], bias, NEG_BIG)
    return bias.reshape(4, NA_HEADS, NA_QROWS * GRID_W, NA_KROWS * GRID_W)


def _na_body(q_ref, k_ref, v_ref, kc_ref, vc_ref, bias_ref, o_ref, *, rows):
    rb = pl.program_id(1)
    nk = NA_KROWS * GRID_W
    start_row = jnp.clip(NA_QROWS * rb - 4, 0, rows - NA_KROWS)
    start = pl.multiple_of(start_row * GRID_W, GRID_W)
    nq = q_ref.shape[0]
    lane = lax.broadcasted_iota(jnp.int32, (nq, 128), 1)
    scale = NA_HD ** -0.5
    for p in range(NA_HEADS // 2):
        sl = slice(128 * p, 128 * (p + 1))
        qp = q_ref[:, sl] * scale
        kp = k_ref[pl.ds(start, nk), sl]
        vp = v_ref[pl.ds(start, nk), sl]
        kcp = kc_ref[:, sl]
        vcp = vc_ref[:, sl]
        zero = jnp.zeros_like(qp)
        q2 = jnp.concatenate([jnp.where(lane < NA_HD, qp, zero), jnp.where(lane >= NA_HD, qp, zero)], axis=0)
        s_loc = _dot_nt(q2, kp) + bias_ref[0, 2 * p:2 * p + 2].reshape(2 * nq, nk)
        s_ctx = _dot_nt(q2, kcp)
        m = jnp.maximum(jnp.max(s_loc, axis=-1, keepdims=True), jnp.max(s_ctx, axis=-1, keepdims=True))
        p_loc = jnp.exp(s_loc - m)
        p_ctx = jnp.exp(s_ctx - m)
        den = jnp.sum(p_loc, axis=-1, keepdims=True) + jnp.sum(p_ctx, axis=-1, keepdims=True)
        o2 = (_dot(p_loc.astype(BF16), vp) + _dot(p_ctx.astype(BF16), vcp)) / den
        o_ref[:, sl] = jnp.where(lane < NA_HD, o2[:nq], o2[nq:]).astype(o_ref.dtype)


def _na_call(z, bias, nbatch, t_len, c_len, with_ctx):
    na = z.shape[0]
    rows = t_len // GRID_W
    nrb = rows // NA_QROWS
    nq = NA_QROWS * GRID_W
    assert nq == c_len
    ctx_base = nbatch * nrb
    steps = nrb + 1 if with_ctx else nrb

    def qtile(b, r):
        return jnp.where(r < nrb, b * nrb + r, ctx_base + b)

    def pattern(b, r):
        return jnp.where(r == 0, 0, jnp.where(r == nrb - 1, 2, jnp.where(r == nrb, 3, 1)))

    return pl.pallas_call(
        functools.partial(_na_body, rows=rows),
        grid=(nbatch, steps),
        in_specs=[pl.BlockSpec((nq, 512), lambda b, r: (qtile(b, r), CB_NQ // 4)),
                  pl.BlockSpec((t_len, 512), lambda b, r: (b, CB_NK // 4)),
                  pl.BlockSpec((t_len, 512), lambda b, r: (b, CB_NV // 4)),
                  pl.BlockSpec((c_len, 512), lambda b, r: (ctx_base + b, CB_NK // 4)),
                  pl.BlockSpec((c_len, 512), lambda b, r: (ctx_base + b, CB_NV // 4)),
                  pl.BlockSpec((1,) + bias.shape[1:], lambda b, r: (pattern(b, r), 0, 0, 0))],
        out_specs=pl.BlockSpec((nq, 512), lambda b, r: (qtile(b, r), 0)),
        out_shape=jax.ShapeDtypeStruct((na if with_ctx else nbatch * t_len, 512), BF16),
        compiler_params=_cparams(("parallel", "arbitrary")),
        name="natten",
    )(z, z, z, z, z, bias)


ROW_SLABS = 4


def _store_packed_rows(ref, x):
    m = x.shape[0]

    def bits(v):
        return lax.bitcast_convert_type(v.astype(BF16).astype(F32), jnp.uint32)

    for s in range(ROW_SLABS):
        lo = x[:, 128 * s:128 * (s + 1)]
        hi = x[:, 512 + 128 * s:512 + 128 * (s + 1)]
        ref[pl.ds(s, m, stride=ROW_SLABS), :] = (bits(hi) & jnp.uint32(0xFFFF0000)) | (bits(lo) >> 16)


def _load_packed_rows(ref):
    m = ref.shape[0] // ROW_SLABS
    los, his = [], []
    for s in range(ROW_SLABS):
        p = ref[pl.ds(s, m, stride=ROW_SLABS), :]
        los.append(lax.bitcast_convert_type(p << 16, F32))
        his.append(lax.bitcast_convert_type(p & jnp.uint32(0xFFFF0000), F32))
    return los + his


def _merge_body(xl_ref, xc_ref, zp_ref, za_ref, zn_ref, bc_ref, bp_ref, bn_ref, pcnt_ref, wp_ref, ps_ref,
                yb_ref, yc_ref, g0, g1, g2, g3, g4, g5, wa_ref, wb_ref, wc_ref, wo_ref,
                lng_ref, lnb_ref, mod_ref, wr2_ref, br_ref, tril_ref, x1_ref, h2_ref, route_ref, cnt_ref,
                cnt_scr, *, alpha, split, nt, n_lat_tiles):
    @pl.when(pl.program_id(0) == 0)
    def _():
        cnt_scr[...] = jnp.zeros_like(cnt_scr)

    gates = ((g0, g1), (g2, g3), (g4, g5))
    ya = _pooled_tile(zp_ref, za_ref, zn_ref, bc_ref, bp_ref, bn_ref, pcnt_ref, wp_ref, ps_ref,
                      nt=nt, n_lat_tiles=n_lat_tiles)
    ys = (ya, yb_ref[...], yc_ref[...])
    ws = (wa_ref, wb_ref, wc_ref)
    half = wa_ref.shape[1] // 2
    mix = None
    for n in range(2):
        m = None
        for kbr in range(3):
            pr = _dot(ys[kbr], ws[kbr][:, n * half:(n + 1) * half])
            term = _sigmoid_t(gates[kbr][n][...].astype(F32)) * pr
            m = term if m is None else m + term
        part = _dot(m.astype(BF16), wo_ref[n * half:(n + 1) * half, :])
        mix = part if mix is None else mix + part
    x = jnp.where(pl.program_id(0) < split, xl_ref[...], xc_ref[...])
    r = alpha * x + mod_ref[0, 2:3, :] * mix
    x1 = _ln(r) * lng_ref[...] + lnb_ref[...]
    x1_ref[...] = x1
    h2 = _ln(x1) * (1.0 + mod_ref[0, 4:5, :]) + mod_ref[0, 3:4, :]
    _store_packed_rows(h2_ref, h2)

    h2_hi = h2.astype(BF16)
    h2_lo = (h2 - h2_hi.astype(F32)).astype(BF16)
    hh = _dot(h2_hi, wr2_ref[...])
    logits = (hh[:, :128] + _dot(h2_lo, wr2_ref[:, :128]) + hh[:, 128:]) + br_ref[...]
    lane = lax.broadcasted_iota(jnp.int32, logits.shape, 1).astype(F32)
    is_grp = jnp.where(lane >= N_EXPERTS, jnp.where(lane < N_EXPERTS + N_GROUPS, 1.0, 0.0), 0.0) > 0.5
    lgm = jnp.where(is_grp, logits, NEG_BIG)
    mg = jnp.max(lgm, axis=-1, keepdims=True)
    p_grp = 1.0 / jnp.sum(jnp.exp(lgm - mg), axis=-1, keepdims=True)
    grp = jnp.min(jnp.where(lgm == mg, lane, 1e9), axis=-1, keepdims=True) - N_EXPERTS
    lo = grp * EXP_PER_GROUP
    in_grp = jnp.where(lane >= lo, jnp.where(lane < lo + EXP_PER_GROUP, 1.0, 0.0), 0.0) > 0.5
    lem = jnp.where(in_grp, logits, NEG_BIG)
    m1 = jnp.max(lem, axis=-1, keepdims=True)
    id1 = jnp.min(jnp.where(lem == m1, lane, 1e9), axis=-1, keepdims=True)
    lem2 = jnp.where(lane == id1, NEG_BIG, lem)
    m2 = jnp.max(lem2, axis=-1, keepdims=True)
    id2 = jnp.min(jnp.where(lem2 == m2, lane, 1e9), axis=-1, keepdims=True)
    u2 = jnp.exp(m2 - m1)
    w1 = p_grp / (1.0 + u2)
    w2 = p_grp * u2 / (1.0 + u2)
    oh1 = jnp.where(lane == id1, 1.0, 0.0)
    oh2 = jnp.where(lane == id2, 1.0, 0.0)
    oh = oh1 + oh2
    before = _dot(tril_ref[...], oh.astype(BF16)) + cnt_scr[...]
    rank1 = jnp.sum(before * oh1, axis=-1, keepdims=True)
    rank2 = jnp.sum(before * oh2, axis=-1, keepdims=True)
    cnt_scr[...] += jnp.sum(oh, axis=0, keepdims=True)
    cnt_ref[...] = jnp.broadcast_to(cnt_scr[...], cnt_ref.shape)
    route = jnp.zeros_like(logits)
    for ln, val in enumerate((w1, w2, id1, id2, rank1, rank2)):
        route = jnp.where(lane == ln, val, route)
    route_ref[...] = route


def _merge_call(tok, pool, yb, yc, z, mod, wa, wb, wc, wo, lng, lnb, wr, br, tm, modrow, alpha, t_len):
    na, d = yc.shape[0], tok[0].shape[1]
    split, tok_specs = _token_specs(tok, tm)
    assert pool[0][0].shape[2] == tm
    pool_specs, pool_args, n_lat_tiles = _pool_specs(z, *pool)
    row = lambda w: pl.BlockSpec((tm, w), lambda i: (i, 0))
    gate = lambda cb: pl.BlockSpec((tm, 512), lambda i: (i, cb))
    full = lambda a: pl.BlockSpec(a.shape, lambda i: (0,) * a.ndim)
    g0 = CB_GATE // 4
    tril = jnp.asarray(np.tril(np.ones((tm, tm), np.float32), -1), BF16)
    wr_hi = lax.reduce_precision(wr, exponent_bits=8, mantissa_bits=7)
    wr2 = jnp.concatenate([wr_hi, wr - wr_hi], axis=1).astype(BF16)
    return pl.pallas_call(
        functools.partial(_merge_body, alpha=alpha, split=split, nt=t_len // tm, n_lat_tiles=n_lat_tiles),
        grid=(na // tm,),
        in_specs=tok_specs + pool_specs + [row(512), row(512)] + [gate(g0 + j) for j in range(6)]
                 + [full(wa), full(wb), full(wc), full(wo), full(lng), full(lnb),
                    pl.BlockSpec((1, 6, d), lambda i: (modrow(i), 0, 0)), full(wr2), full(br), full(tril)],
        out_specs=[row(d), pl.BlockSpec((tm * ROW_SLABS, 128), lambda i: (i, 0)), row(128),
                   pl.BlockSpec((8, 128), lambda i: (0, 0))],
        out_shape=[jax.ShapeDtypeStruct((na, d), F32), jax.ShapeDtypeStruct((na * ROW_SLABS, 128), jnp.uint32),
                   jax.ShapeDtypeStruct((na, 128), F32), jax.ShapeDtypeStruct((8, 128), F32)],
        scratch_shapes=[pltpu.VMEM((1, 128), F32)],
        compiler_params=_cparams(("arbitrary",)),
        name="merge",
    )(tok[0], tok[1], *pool_args, yb, yc, z, z, z, z, z, z, wa, wb, wc, wo, lng, lnb, mod, wr2, br, tril)


ROW_DMA_UNROLL = 8


def _slab(ref, row):
    start = row * ROW_SLABS
    if not isinstance(row, int):
        start = pl.multiple_of(start, ROW_SLABS)
    return ref.at[pl.ds(start, ROW_SLABS), :]


def _slots_body(route_ref, seg_ref, o_ref):
    route = route_ref[...]
    lane = lax.broadcasted_iota(jnp.int32, route.shape, 1).astype(F32)
    seg = seg_ref[...]
    slots = []
    for k in range(2):
        eid = route[:, 2 + k:3 + k]
        start = jnp.sum(jnp.where(lane == eid, seg, 0.0), axis=-1, keepdims=True)
        slots.append(start + route[:, 4 + k:5 + k])
    both = jnp.where(lane == 0.0, slots[0], jnp.where(lane == 1.0, slots[1], 0.0))
    by_token = both.T[:8].astype(jnp.int32)
    tm = o_ref.shape[2]
    for c in range(o_ref.shape[0]):
        o_ref[c] = by_token[:, c * tm:(c + 1) * tm]


def _slots_call(route, seg, tm):
    na = route.shape[0]
    group = _row_tile(4, na // tm)
    return pl.pallas_call(
        _slots_body,
        grid=(na // (tm * group),),
        in_specs=[pl.BlockSpec((tm * group, 128), lambda i: (i, 0)), pl.BlockSpec((1, 128), lambda i: (0, 0))],
        out_specs=pl.BlockSpec((group, 8, tm), lambda i: (i, 0, 0)),
        out_shape=jax.ShapeDtypeStruct((na // tm, 8, tm), jnp.int32),
        compiler_params=_cparams(("parallel",)),
        name="slots",
    )(route, seg)


def _dispatch_body(slot_ref, h_ref, xs_in, xs_out, sem):
    del xs_in
    tm = h_ref.shape[0] // ROW_SLABS

    for t in range(tm):
        for k in range(2):
            pltpu.make_async_copy(_slab(h_ref, t), _slab(xs_out, slot_ref[0, k, t]), sem).start(priority=k)

    for k in range(2):
        pltpu.make_async_copy(h_ref, xs_out.at[pl.ds(0, tm * ROW_SLABS), :], sem).wait()


def _dispatch_call(h2p, slots, n_rows, tm, xs_prev=None):
    xs0 = jnp.zeros((n_rows * ROW_SLABS, 128), h2p.dtype) if xs_prev is None else xs_prev
    assert xs0.shape[0] >= n_rows * ROW_SLABS
    return pl.pallas_call(
        _dispatch_body,
        grid=(slots.shape[0],),
        in_specs=[pl.BlockSpec((1, 8, tm), lambda i: (i, 0, 0), memory_space=pltpu.SMEM),
                  pl.BlockSpec((tm * ROW_SLABS, 128), lambda i: (i, 0)),
                  pl.BlockSpec(memory_space=pl.ANY)],
        out_specs=pl.BlockSpec(memory_space=pl.ANY),
        out_shape=jax.ShapeDtypeStruct(xs0.shape, xs0.dtype),
        input_output_aliases={2: 0},
        scratch_shapes=[pltpu.SemaphoreType.DMA(())],
        compiler_params=_cparams(("arbitrary",)),
        name="dispatch",
    )(slots, h2p, xs0)


def _experts_body(te_ref, nu_ref, xs_ref, wg_ref, wu_ref, wd_ref, y_ref, wg_scr, wu_scr, wd_scr):
    j = pl.program_id(0)

    @pl.when(j >= nu_ref[0])
    def _():
        y_ref[...] = jnp.zeros_like(y_ref)

    @pl.when(jnp.logical_and(j < nu_ref[0], jnp.logical_or(j == 0, te_ref[j] != te_ref[jnp.maximum(j - 1, 0)])))
    def _():
        wg_scr[...] = wg_ref[0].astype(BF16)
        wu_scr[...] = wu_ref[0].astype(BF16)
        wd_scr[...] = wd_ref[0].astype(BF16)

    @pl.when(j < nu_ref[0])
    def _():
        x = jnp.concatenate([c.astype(BF16) for c in _load_packed_rows(xs_ref)], axis=1)
        gt = _dot(x, wg_scr[...])
        a = gt * _sigmoid_t(gt) * _dot(x, wu_scr[...])
        _store_packed_rows(y_ref, _dot(a.astype(BF16), wd_scr[...]))


def _experts_call(tile_expert, n_used, xs, wg, wu, wd, layer, tr):
    rows, w = xs.shape
    _, ne, d, de = wg.shape
    tr = tr * ROW_SLABS
    used = lambda j, te, nu: jnp.minimum(j, nu[0] - 1)
    return pl.pallas_call(
        _experts_body,
        grid_spec=pltpu.PrefetchScalarGridSpec(
            num_scalar_prefetch=2,
            grid=(rows // tr,),
            in_specs=[pl.BlockSpec((tr, w), lambda j, te, nu: (used(j, te, nu), 0)),
                      pl.BlockSpec((None, 1, d, de), lambda j, te, nu: (layer, te[used(j, te, nu)], 0, 0)),
                      pl.BlockSpec((None, 1, d, de), lambda j, te, nu: (layer, te[used(j, te, nu)], 0, 0)),
                      pl.BlockSpec((None, 1, de, d), lambda j, te, nu: (layer, te[used(j, te, nu)], 0, 0))],
            out_specs=pl.BlockSpec((tr, w), lambda j, te, nu: (j, 0)),
            scratch_shapes=[pltpu.VMEM((d, de), BF16), pltpu.VMEM((d, de), BF16), pltpu.VMEM((de, d), BF16)]),
        out_shape=jax.ShapeDtypeStruct((rows, w), jnp.uint32),
        compiler_params=_cparams(("arbitrary",)),
        name="experts",
    )(tile_expert, n_used, xs, wg, wu, wd)


def _combine_body(slot_ref, next_ref, y_hbm, x1_ref, route_ref, lng_ref, lnb_ref, mod_ref, o_ref,
                  buf_scr, sems, *, alpha):
    tm = x1_ref.shape[0]
    i = pl.program_id(0)
    cur = i % 2

    def gather(idx_ref, b):
        def issue(j, carry):
            for u in range(ROW_DMA_UNROLL):
                t = j * ROW_DMA_UNROLL + u
                for k in range(2):
                    pltpu.make_async_copy(_slab(y_hbm, idx_ref[0, k, t]), _slab(buf_scr.at[b, k], t),
                                          sems.at[b]).start(priority=k)
            return carry

        lax.fori_loop(0, tm // ROW_DMA_UNROLL, issue, 0)

    def wait_slot(b):
        for k in range(2):
            pltpu.make_async_copy(y_hbm.at[pl.ds(0, tm * ROW_SLABS), :], buf_scr.at[b, k], sems.at[b]).wait()

    @pl.when(i == 0)
    def _():
        gather(slot_ref, cur)

    wait_slot(cur)
    for t in range(tm):
        for k in range(2):
            pltpu.make_async_copy(_slab(y_hbm, next_ref[0, k, t]), _slab(buf_scr.at[1 - cur, k], t),
                                  sems.at[1 - cur]).start(priority=k)

    w1 = route_ref[:, 0:1]
    w2 = route_ref[:, 1:2]
    moe = jnp.concatenate([w1 * a1 + w2 * a2 for a1, a2 in zip(_load_packed_rows(buf_scr.at[cur, 0]),
                                                               _load_packed_rows(buf_scr.at[cur, 1]))], axis=1)
    r = alpha * x1_ref[...] + mod_ref[0, 5:6, :] * moe
    o_ref[...] = _ln(r) * lng_ref[...] + lnb_ref[...]

    @pl.when(i == pl.num_programs(0) - 1)
    def _():
        wait_slot(1 - cur)


def _combine_call(slots, y, x1, route, lng, lnb, mod, tm, modrow, alpha):
    na, d = x1.shape
    full = lambda a: pl.BlockSpec(a.shape, lambda i: (0,) * a.ndim)
    last = na // tm - 1
    return pl.pallas_call(
        functools.partial(_combine_body, alpha=alpha),
        grid=(na // tm,),
        in_specs=[pl.BlockSpec((1, 8, tm), lambda i: (i, 0, 0), memory_space=pltpu.SMEM),
                  pl.BlockSpec((1, 8, tm), lambda i: (jnp.minimum(i + 1, last), 0, 0), memory_space=pltpu.SMEM),
                  pl.BlockSpec(memory_space=pl.ANY),
                  pl.BlockSpec((tm, d), lambda i: (i, 0)),
                  pl.BlockSpec((tm, 128), lambda i: (i, 0)),
                  full(lng), full(lnb),
                  pl.BlockSpec((1, 6, d), lambda i: (modrow(i), 0, 0))],
        out_specs=pl.BlockSpec((tm, d), lambda i: (i, 0)),
        out_shape=jax.ShapeDtypeStruct((na, d), F32),
        scratch_shapes=[pltpu.VMEM((2, 2, tm * ROW_SLABS, 128), jnp.uint32), pltpu.SemaphoreType.DMA((2,))],
        compiler_params=_cparams(("arbitrary",)),
        name="combine",
    )(slots, slots, y, x1, route, lng, lnb, mod)


def _expert_buffer_tiles(n_tokens, tr):
    return -(-2 * n_tokens // tr) + N_EXPERTS


def _routing_tables(counts, n_tiles, tr):
    cnt = counts[0, :N_EXPERTS].astype(jnp.int32)
    ntile = (cnt + tr - 1) // tr
    tile_start = jnp.cumsum(ntile) - ntile
    n_used = jnp.sum(ntile)
    seg = jnp.zeros((1, 128), F32).at[0, :N_EXPERTS].set((tile_start * tr).astype(F32))
    tile_expert = jnp.sum(jnp.arange(n_tiles, dtype=jnp.int32)[:, None] >= tile_start[None, :], axis=1) - 1
    return seg, tile_expert.astype(jnp.int32), n_used.reshape(1).astype(jnp.int32)


def _lower_bounds(logits):
    p = jax.nn.softmax(logits.astype(F32), axis=0)
    return jnp.cumsum(p, axis=0) - p[:1]


def _row_tile(limit, *sizes):
    tm = limit
    while any(s % tm for s in sizes):
        tm //= 2
    return tm


def kernel(x, c, ctx, c_ctx, w_ada, b_ada, w_in, w_pool, pool_scale, lb_logits_fwd, lb_logits_bwd, hg_gain, rpb, w_br_a, w_br_b, w_br_c, w_out, ln1_g, ln1_b, w_rg, b_rg, w_re, b_re, w_gate, w_up, w_down, ln2_g, ln2_b):
    nbatch, t_len, d = x.shape
    c_len = ctx.shape[1]
    depth = w_ada.shape[0]
    assert c_len == SEQ_TILE and t_len % SEQ_TILE == 0 and t_len % GRID_W == 0
    alpha = (2.0 * depth) ** 0.25
    n_lat = nbatch * t_len
    nt = t_len // SEQ_TILE

    na = n_lat + nbatch * c_len
    tok = (x.reshape(n_lat, d), ctx.reshape(nbatch * c_len, d), 0)

    mod_rows = -(-(nbatch + 1) // 8) * 8
    cc = jnp.zeros((mod_rows, d), F32).at[:nbatch].set(c).at[nbatch].set(c_ctx)
    ada = _ada_call(cc, w_ada, b_ada)

    lb_f = _lower_bounds(lb_logits_fwd).reshape(depth, HG_HEADS, 1, HG_DK)
    lb_b = _lower_bounds(lb_logits_bwd).reshape(depth, HG_HEADS, 1, HG_DK)
    tm_big = _row_tile(1024, t_len, nbatch * c_len)
    tm_mid = _row_tile(512, t_len, nbatch * c_len)
    assert tm_mid % c_len == 0
    pool_consts = _pool_consts(tm_mid, c_len)
    hg_f = _hgrn_consts(False)
    hg_b = _hgrn_consts(True)

    def modrow_for(tm):
        return lambda i: jnp.where(i * tm < n_lat, (i * tm) // t_len, nbatch)

    n_tiles = _expert_buffer_tiles(na, EXPERT_TILE)
    xs = None
    for l in range(depth):
        last = l == depth - 1
        mod = ada[l].reshape(mod_rows, 6, d)
        z = _inproj_call(tok, na, mod, w_in[l].astype(BF16), tm_big, modrow_for(tm_big))
        pool = (pool_consts, w_pool[l].astype(BF16), pool_scale[l].reshape(1, -1), n_lat)
        o_f =_hgrn_call(z, lb_f[l], hg_f, nbatch, nt, reverse=False)
        yb = _hgrn_call(z, lb_b[l], hg_b, nbatch, nt, reverse=True, o_fwd=o_f,
                        gain=hg_gain[l].reshape(HG_HEADS, 1, HG_DK))
        bias = _na_bias_table(rpb[l], t_len // GRID_W)
        yc = _na_call(z, bias, nbatch, t_len, c_len, with_ctx=not last)
        wr = jnp.zeros((d, 128), F32).at[:, :N_EXPERTS].set(w_re[l]).at[:, N_EXPERTS:N_EXPERTS + N_GROUPS].set(w_rg[l])
        br = jnp.zeros((1, 128), F32).at[0, :N_EXPERTS].set(b_re[l]).at[0, N_EXPERTS:N_EXPERTS + N_GROUPS].set(b_rg[l])
        x1, h2p, route, counts = _merge_call(
            tok, pool, yb, yc, z, mod, w_br_a[l].astype(BF16), w_br_b[l].astype(BF16), w_br_c[l].astype(BF16),
            w_out[l].astype(BF16), ln1_g[l].reshape(1, d), ln1_b[l].reshape(1, d), wr, br, tm_mid,
            modrow_for(tm_mid), alpha, t_len)
        seg, tile_expert, n_used = _routing_tables(counts, n_tiles, EXPERT_TILE)
        slots = _slots_call(route, seg, tm_mid)
        xs = _dispatch_call(h2p, slots, n_tiles * EXPERT_TILE, tm_mid, xs)
        ys = _experts_call(tile_expert, n_used, xs, w_gate, w_up, w_down, l, EXPERT_TILE)
        xa = _combine_call(slots, ys, x1, route, ln2_g[l].reshape(1, d), ln2_b[l].reshape(1, d), mod,
                           tm_mid, modrow_for(tm_mid), alpha)
        tok = (xa, xa, n_lat)
    return xa.reshape(nbatch, t_len, d)
```

```python
import functools

import numpy as np
import jax
import jax.numpy as jnp
from jax import lax
from jax.experimental import pallas as pl
from jax.experimental.pallas import tpu as pltpu

F32 = jnp.float32
BF16 = jnp.bfloat16
HIGHEST = lax.Precision.HIGHEST

GRID_W = 64
POOL_WINDOWS = (2, 4, 8, 16)
POOL_GDIM = 128
HG_HEADS = 4
HG_DK = 128
HG_BLOCK = 16
NA_HEADS = 8
NA_HD = 64
NA_ROWS = 8
NA_COLS = 16
NA_QROWS = 4
NA_KROWS = 12
N_GROUPS = 4
EXP_PER_GROUP = 8
N_EXPERTS = N_GROUPS * EXP_PER_GROUP
LN_EPS = 1e-5
RMS_EPS = 1e-6
NEG_BIG = -1e30
SEQ_TILE = 256
EXPERT_TILE = 512
VMEM_LIMIT = 56 * 1024 * 1024

CB_A, CB_Q, CB_FF, CB_FB, CB_I, CB_G, CB_NQ, CB_NK, CB_NV, CB_GATE = 0, 4, 8, 12, 16, 20, 24, 28, 32, 36


def _cparams(sem):
    return pltpu.CompilerParams(dimension_semantics=sem, vmem_limit_bytes=VMEM_LIMIT)


def _ln(x):
    mu = jnp.mean(x, axis=-1, keepdims=True)
    xc = x - mu
    var = jnp.mean(xc * xc, axis=-1, keepdims=True)
    return xc * lax.rsqrt(var + LN_EPS)


def _sigmoid(x):
    return 1.0 / (1.0 + jnp.exp(-x))


def _sigmoid_t(x):
    return 0.5 * jnp.tanh(0.5 * x) + 0.5


def _dot(a, b):
    return jnp.dot(a, b, preferred_element_type=F32)


def _dot_nt(a, b):
    return lax.dot_general(a, b, (((1,), (1,)), ((), ())), preferred_element_type=F32)


def _dot_tn(a, b):
    return lax.dot_general(a, b, (((0,), (0,)), ((), ())), preferred_element_type=F32)


def _dot01(m01, x):
    x1 = x.astype(BF16)
    r1 = x - x1.astype(F32)
    x2 = r1.astype(BF16)
    x3 = (r1 - x2.astype(F32)).astype(BF16)
    return _dot(m01, x1) + _dot(m01, x2) + _dot(m01, x3)


def _ada_body(c_ref, w_ref, b_ref, o_ref):
    cs = c_ref[...]
    s = cs * _sigmoid(cs)
    o_ref[0] = jnp.dot(s, w_ref[0], preferred_element_type=F32, precision=HIGHEST) + b_ref[0]


def _ada_call(cc, w_ada, b_ada):
    depth, d, n6 = w_ada.shape
    rows = cc.shape[0]
    return pl.pallas_call(
        _ada_body,
        grid=(depth, n6 // d),
        in_specs=[pl.BlockSpec((rows, d), lambda l, j: (0, 0)),
                  pl.BlockSpec((1, d, d), lambda l, j: (l, 0, j)),
                  pl.BlockSpec((1, 1, d), lambda l, j: (l, 0, j))],
        out_specs=pl.BlockSpec((1, rows, d), lambda l, j: (l, 0, j)),
        out_shape=jax.ShapeDtypeStruct((depth, rows, n6), F32),
        compiler_params=_cparams(("parallel", "parallel")),
        name="ada",
    )(cc, w_ada, b_ada.reshape(depth, 1, n6))


def _token_specs(tok, tm):
    lat, ctx, ctx_row0 = tok
    split = (ctx_row0 if ctx is lat else lat.shape[0]) // tm
    off = ctx_row0 // tm
    d = lat.shape[1]
    lat_map = lambda i, *_: (jnp.minimum(i, split - 1), 0)
    ctx_map = lambda i, *_: (jnp.maximum(i - split, 0) + off, 0)
    return split, [pl.BlockSpec((tm, d), lat_map), pl.BlockSpec((tm, d), ctx_map)]


def _inproj_body(xl_ref, xc_ref, mod_ref, w_ref, z_ref, h_scr, *, split):
    def modulated(x_ref):
        h = _ln(x_ref[...]) * (1.0 + mod_ref[0, 1:2, :]) + mod_ref[0, 0:1, :]
        h_scr[...] = h.astype(BF16)

    first = pl.program_id(1) == 0
    pl.when(jnp.logical_and(first, pl.program_id(0) < split))(lambda: modulated(xl_ref))
    pl.when(jnp.logical_and(first, pl.program_id(0) >= split))(lambda: modulated(xc_ref))
    z_ref[...] = _dot(h_scr[...], w_ref[...]).astype(z_ref.dtype)


def _inproj_call(tok, na, mod, w_in, tm, modrow):
    d, d_in = w_in.shape
    tn = 2560
    assert d_in % tn == 0
    split, tok_specs = _token_specs(tok, tm)
    return pl.pallas_call(
        functools.partial(_inproj_body, split=split),
        grid=(na // tm, d_in // tn),
        in_specs=tok_specs + [pl.BlockSpec((1, 6, d), lambda i, j: (modrow(i), 0, 0)),
                              pl.BlockSpec((d, tn), lambda i, j: (0, j))],
        out_specs=pl.BlockSpec((tm, tn), lambda i, j: (i, j)),
        out_shape=jax.ShapeDtypeStruct((na, d_in), BF16),
        scratch_shapes=[pltpu.VMEM((tm, d), BF16)],
        compiler_params=_cparams(("parallel", "arbitrary")),
        name="inproj",
    )(tok[0], tok[1], mod, w_in)


def _pool_consts(n, c_len):
    t = np.arange(n)[:, None]
    bc = np.zeros((2, 4, n, n), np.float32)
    bp = np.zeros((4, n, 16), np.float32)
    bn = np.zeros((4, n, 16), np.float32)
    for g, win in enumerate(POOL_WINDOWS):
        lo, hi = t - win // 2, t + win // 2 - 1
        s = np.arange(n)[None, :]
        bc[0, g] = (s >= lo) & (s <= hi)
        bc[1, g] = bc[0, g] * (s // c_len == t // c_len)
        s = np.arange(16)[None, :] - 16
        bp[g] = (s >= lo) & (s <= hi)
        s = np.arange(16)[None, :] + n
        bn[g] = (s >= lo) & (s <= hi)
    cnt = np.stack([np.stack([bc[0].sum(-1), bp.sum(-1), bn.sum(-1)], axis=1),
                    np.stack([bc[1].sum(-1), 0 * bp.sum(-1), 0 * bn.sum(-1)], axis=1)])
    cnt = np.broadcast_to(cnt[..., None], (2, 4, 3, n, 128)).astype(np.float32)
    return (jnp.asarray(bc, BF16), jnp.asarray(bp, BF16), jnp.asarray(bn, BF16), jnp.asarray(cnt))


def _pooled_tile(prev_ref, cur_ref, next_ref, bc_ref, bp_ref, bn_ref, cnt_ref, wp_ref, ps_ref, *, nt, n_lat_tiles):
    i = pl.program_id(0)
    k = i % nt
    lat = i < n_lat_tiles
    has_prev = jnp.where(jnp.logical_and(lat, k != 0), 1.0, 0.0).astype(F32)
    has_next = jnp.where(jnp.logical_and(lat, k != nt - 1), 1.0, 0.0).astype(F32)
    groups = []
    for g in range(len(POOL_WINDOWS)):
        sl = slice(g * POOL_GDIM, (g + 1) * POOL_GDIM)
        u = cur_ref[:, sl]
        ssum = (_dot(bc_ref[0, g], u) + has_prev * _dot(bp_ref[g], prev_ref[:, sl])
                + has_next * _dot(bn_ref[g], next_ref[:, sl]))
        cnt = cnt_ref[0, g, 0] + has_prev * cnt_ref[0, g, 1] + has_next * cnt_ref[0, g, 2]
        dlt = ssum / cnt - u.astype(F32)
        groups.append((_dot(dlt.astype(BF16), wp_ref[g]) * ps_ref[:, sl]).astype(BF16))
    return jnp.concatenate(groups, axis=1)


def _pool_specs(z, consts, w_pool, pool_scale, n_lat):
    bc, bp, bn, cnt = consts
    n = bc.shape[2]
    hb = n // 16
    last16 = z.shape[0] // 16 - 1
    n_lat_tiles = n_lat // n
    full = lambda a: pl.BlockSpec(a.shape, lambda i: (0,) * a.ndim)
    variant = lambda a: pl.BlockSpec((1,) + a.shape[1:],
                                     lambda i: (jnp.where(i < n_lat_tiles, 0, 1),) + (0,) * (a.ndim - 1))
    specs = [pl.BlockSpec((16, 512), lambda i: (jnp.maximum(i * hb - 1, 0), CB_A // 4)),
             pl.BlockSpec((n, 512), lambda i: (i, CB_A // 4)),
             pl.BlockSpec((16, 512), lambda i: (jnp.minimum((i + 1) * hb, last16), CB_A // 4)),
             variant(bc), full(bp), full(bn), variant(cnt), full(w_pool), full(pool_scale)]
    return specs, [z, z, z, bc, bp, bn, cnt, w_pool, pool_scale], n_lat_tiles


def _hgrn_consts(reverse):
    n, bs = SEQ_TILE, HG_BLOCK
    nb = n // bs
    t = np.arange(n)
    o = (n - 1 - t) if reverse else t
    blk = t // bs
    jb = np.arange(nb)
    ob = (nb - 1 - jb) if reverse else jb
    cum = ((blk[:, None] == blk[None, :]) & (o[None, :] <= o[:, None])).astype(np.float32)
    bsum = (jb[:, None] == blk[None, :]).astype(np.float32)
    widths = [2 ** l for l in range(1, int(np.log2(nb)) + 1)]
    lvl = np.full((n, n), -1, np.int32)
    obt = ob[blk]
    same = blk[:, None] == blk[None, :]
    lvl[same & (o[None, :] <= o[:, None])] = 0
    for li, w in reversed(list(enumerate(widths, start=1))):
        m = (obt[:, None] // w == obt[None, :] // w) & (obt[None, :] < obt[:, None]) & ~same
        lvl[m] = li
    mats = []
    for w in widths:
        mid = (ob // w) * w + w // 2
        mats.append((mid[:, None] <= ob[None, :]) & (ob[None, :] < ob[:, None]))
    for w in widths:
        mid = (ob // w) * w + w // 2
        mats.append((ob[:, None] < ob[None, :]) & (ob[None, :] < mid[:, None]))
    mats.append(ob[None, :] < ob[:, None])
    mats.append(ob[None, :] > ob[:, None])
    mats.append(np.ones((nb, nb), bool))
    tsm = np.concatenate(mats, axis=0).astype(np.float32)
    return (jnp.asarray(cum, BF16), jnp.asarray(bsum, BF16), jnp.asarray(lvl, BF16), jnp.asarray(tsm, BF16),
            len(widths))


def _expand_blocks(c, n):
    nb, lanes = c.shape
    return jnp.concatenate([jnp.broadcast_to(c[j:j + 1, :], (n // nb, lanes)) for j in range(nb)], axis=0)


def _hgrn_body(*refs, n_levels, final):
    if final:
        (zq_ref, zf_ref, zi_ref, lb_ref, cum_ref, bsum_ref, lvl_ref, tsm_ref,
         of_ref, zg_ref, gain_ref, o_ref, st_scr) = refs
    else:
        zq_ref, zf_ref, zi_ref, lb_ref, cum_ref, bsum_ref, lvl_ref, tsm_ref, o_ref, st_scr = refs
    n = zq_ref.shape[0]
    nb = n // HG_BLOCK

    @pl.when(pl.program_id(1) == 0)
    def _():
        st_scr[...] = jnp.zeros_like(st_scr)

    zq = zq_ref[...].astype(F32)
    zf = zf_ref[...].astype(F32)
    lb = jnp.concatenate([lb_ref[h] for h in range(HG_HEADS)], axis=1)
    sig = _sigmoid(zf)
    lf = jnp.log(lb + (1.0 - lb) * sig)
    k_all = (1.0 - lb) * (1.0 - sig)
    q_all = zq * _sigmoid_t(zq)
    lf_hi = lf.astype(BF16)
    lf2 = jnp.concatenate([lf_hi, (lf - lf_hi.astype(F32)).astype(BF16)], axis=1)
    w = HG_HEADS * HG_DK
    b2 = _dot(cum_ref[...], lf2)
    b_all = b2[:, :w] + b2[:, w:]
    t2 = _dot(bsum_ref[...], lf2)
    tot_all = t2[:, :w] + t2[:, w:]
    coef_all = _dot01(tsm_ref[...], tot_all)
    qd_all = q_all * jnp.exp(b_all)
    kd_all = (k_all * jnp.exp(-b_all)).astype(BF16)
    ks_all = k_all * jnp.exp(_expand_blocks(tot_all, n) - b_all)

    ecoef = jnp.exp(coef_all)
    scale = lambda idx: _expand_blocks(ecoef[idx * nb:(idx + 1) * nb], n)
    q_lv = [qd_all.astype(BF16)] + [(qd_all * scale(li)).astype(BF16) for li in range(n_levels)]
    k_lv = [kd_all] + [(ks_all * scale(n_levels + li)).astype(BF16) for li in range(n_levels)]
    qs_all = (qd_all * scale(2 * n_levels)).astype(BF16)
    kn_all = (ks_all * scale(2 * n_levels + 1)).astype(BF16)
    dec_all = ecoef[(2 * n_levels + 2) * nb:(2 * n_levels + 2) * nb + 1]

    for h in range(HG_HEADS):
        sl = slice(h * HG_DK, (h + 1) * HG_DK)
        v = zi_ref[:, sl]

        lvl = lvl_ref[...]
        a = jnp.zeros((n, n), BF16)
        for li in range(n_levels + 1):
            a = jnp.where(lvl == float(li), _dot_nt(q_lv[li][:, sl], k_lv[li][:, sl]).astype(BF16), a)
        o = _dot(a, v)

        st = st_scr[h]
        o = o + _dot_nt(qs_all[:, sl], st.astype(BF16))
        st_scr[h] = st * dec_all[:, sl] + _dot_tn(v, kn_all[:, sl])

        if final:
            o = o + of_ref[:, sl]
            o = o * lax.rsqrt(jnp.mean(o * o, axis=-1, keepdims=True) + RMS_EPS) * gain_ref[h]
            zg = zg_ref[:, sl].astype(F32)
            o_ref[:, sl] = (o * (zg * _sigmoid_t(zg))).astype(o_ref.dtype)
        else:
            o_ref[:, sl] = o


def _hgrn_call(z, lb, consts, nbatch, nt, reverse, o_fwd=None, gain=None):
    na = z.shape[0]
    n = SEQ_TILE
    cum, bsum, lvl, tsm, n_levels = consts
    ctx_base = nbatch * nt
    final = o_fwd is not None

    def tile(b, s):
        lat = (b * nt + nt - s) if reverse else (b * nt + s - 1)
        return jnp.where(s == 0, ctx_base + b, lat)

    width = HG_HEADS * HG_DK

    def col(cb):
        return pl.BlockSpec((n, width), lambda b, s: (tile(b, s), cb // HG_HEADS))

    full = lambda a: pl.BlockSpec(a.shape, lambda b, s: (0,) * a.ndim)
    in_specs = [col(CB_Q), col(CB_FB if reverse else CB_FF), col(CB_I), full(lb),
                full(cum), full(bsum), full(lvl), full(tsm)]
    args = [z, z, z, lb, cum, bsum, lvl, tsm]
    if final:
        in_specs += [col(0), col(CB_G), full(gain)]
        args += [o_fwd, z, gain]
    return pl.pallas_call(
        functools.partial(_hgrn_body, n_levels=n_levels, final=final),
        grid=(nbatch, nt + 1),
        in_specs=in_specs,
        out_specs=col(0),
        out_shape=jax.ShapeDtypeStruct((na, width), BF16 if final else F32),
        scratch_shapes=[pltpu.VMEM((HG_HEADS, HG_DK, HG_DK), F32)],
        compiler_params=_cparams(("parallel", "arbitrary")),
        name="hgrn_bwd" if final else "hgrn_fwd",
    )(*args)


def _na_bias_table(rpb, rows):
    nrb = rows // NA_QROWS
    assert nrb >= 3 and rows >= NA_KROWS
    kr = min(NA_ROWS, rows)
    qc = np.arange(GRID_W)
    c0 = np.clip(qc - NA_COLS // 2, 0, GRID_W - NA_COLS)
    kc = np.arange(GRID_W)
    col_ok = (kc[None, :] >= c0[:, None]) & (kc[None, :] < c0[:, None] + NA_COLS)
    n_dr, n_dc = 2 * NA_ROWS - 1, 2 * NA_COLS - 1
    dc = np.where(col_ok, kc[None, :] - qc[:, None] + NA_COLS - 1, n_dc)
    sel_c = (dc[..., None] == np.arange(n_dc + 1)).astype(np.float32)
    sel_r = []
    for rb in (0, 1, nrb - 1):
        start = int(np.clip(NA_QROWS * rb - 4, 0, rows - NA_KROWS))
        r = NA_QROWS * rb + np.arange(NA_QROWS)
        r0 = np.clip(r - kr // 2, 0, rows - kr)
        keyrow = start + np.arange(NA_KROWS)
        row_ok = (keyrow[None, :] >= r0[:, None]) & (keyrow[None, :] < r0[:, None] + kr)
        dr = np.where(row_ok, keyrow[None, :] - r[:, None] + NA_ROWS - 1, n_dr)
        sel_r.append((dr[..., None] == np.arange(n_dr + 1)).astype(np.float32))
    masked = np.zeros_like(sel_r[0])
    masked[..., n_dr] = 1.0
    sel_r.append(masked)
    rpb_ext = jnp.pad(rpb.astype(F32), ((0, 0), (0, 1), (0, 1)), constant_values=NEG_BIG)
    bias = jnp.einsum("hij,paki,cdj->phackd", rpb_ext, jnp.asarray(np.stack(sel_r)), jnp.asarray(sel_c),
                      precision=HIGHEST)
    return bias.reshape(4, NA_HEADS, NA_QROWS * GRID_W, NA_KROWS * GRID_W)


def _na_body(q_ref, k_ref, v_ref, kc_ref, vc_ref, bias_ref, o_ref, *, rows):
    rb = pl.program_id(1)
    nk = NA_KROWS * GRID_W
    start_row = jnp.clip(NA_QROWS * rb - 4, 0, rows - NA_KROWS)
    start = pl.multiple_of(start_row * GRID_W, GRID_W)
    nq = q_ref.shape[0]
    lane = lax.broadcasted_iota(jnp.int32, (nq, 128), 1)
    scale = NA_HD ** -0.5
    for p in range(NA_HEADS // 2):
        sl = slice(128 * p, 128 * (p + 1))
        qp = q_ref[:, sl] * scale
        kp = k_ref[pl.ds(start, nk), sl]
        vp = v_ref[pl.ds(start, nk), sl]
        kcp = kc_ref[:, sl]
        vcp = vc_ref[:, sl]
        zero = jnp.zeros_like(qp)
        q2 = jnp.concatenate([jnp.where(lane < NA_HD, qp, zero), jnp.where(lane >= NA_HD, qp, zero)], axis=0)
        s_loc = _dot_nt(q2, kp) + bias_ref[0, 2 * p:2 * p + 2].reshape(2 * nq, nk)
        s_ctx = _dot_nt(q2, kcp)
        m = jnp.maximum(jnp.max(s_loc, axis=-1, keepdims=True), jnp.max(s_ctx, axis=-1, keepdims=True))
        p_loc = jnp.exp(s_loc - m)
        p_ctx = jnp.exp(s_ctx - m)
        den = jnp.sum(p_loc, axis=-1, keepdims=True) + jnp.sum(p_ctx, axis=-1, keepdims=True)
        o2 = (_dot(p_loc.astype(BF16), vp) + _dot(p_ctx.astype(BF16), vcp)) / den
        o_ref[:, sl] = jnp.where(lane < NA_HD, o2[:nq], o2[nq:]).astype(o_ref.dtype)


def _na_call(z, bias, nbatch, t_len, c_len, with_ctx):
    na = z.shape[0]
    rows = t_len // GRID_W
    nrb = rows // NA_QROWS
    nq = NA_QROWS * GRID_W
    assert nq == c_len
    ctx_base = nbatch * nrb
    steps = nrb + 1 if with_ctx else nrb

    def qtile(b, r):
        return jnp.where(r < nrb, b * nrb + r, ctx_base + b)

    def pattern(b, r):
        return jnp.where(r == 0, 0, jnp.where(r == nrb - 1, 2, jnp.where(r == nrb, 3, 1)))

    return pl.pallas_call(
        functools.partial(_na_body, rows=rows),
        grid=(nbatch, steps),
        in_specs=[pl.BlockSpec((nq, 512), lambda b, r: (qtile(b, r), CB_NQ // 4)),
                  pl.BlockSpec((t_len, 512), lambda b, r: (b, CB_NK // 4)),
                  pl.BlockSpec((t_len, 512), lambda b, r: (b, CB_NV // 4)),
                  pl.BlockSpec((c_len, 512), lambda b, r: (ctx_base + b, CB_NK // 4)),
                  pl.BlockSpec((c_len, 512), lambda b, r: (ctx_base + b, CB_NV // 4)),
                  pl.BlockSpec((1,) + bias.shape[1:], lambda b, r: (pattern(b, r), 0, 0, 0))],
        out_specs=pl.BlockSpec((nq, 512), lambda b, r: (qtile(b, r), 0)),
        out_shape=jax.ShapeDtypeStruct((na if with_ctx else nbatch * t_len, 512), BF16),
        compiler_params=_cparams(("parallel", "arbitrary")),
        name="natten",
    )(z, z, z, z, z, bias)


ROW_SLABS = 4


def _store_packed_rows(ref, x):
    m = x.shape[0]

    def bits(v):
        return lax.bitcast_convert_type(v.astype(BF16).astype(F32), jnp.uint32)

    for s in range(ROW_SLABS):
        lo = x[:, 128 * s:128 * (s + 1)]
        hi = x[:, 512 + 128 * s:512 + 128 * (s + 1)]
        ref[pl.ds(s, m, stride=ROW_SLABS), :] = (bits(hi) & jnp.uint32(0xFFFF0000)) | (bits(lo) >> 16)


def _load_packed_rows(ref):
    m = ref.shape[0] // ROW_SLABS
    los, his = [], []
    for s in range(ROW_SLABS):
        p = ref[pl.ds(s, m, stride=ROW_SLABS), :]
        los.append(lax.bitcast_convert_type(p << 16, F32))
        his.append(lax.bitcast_convert_type(p & jnp.uint32(0xFFFF0000), F32))
    return los + his


def _merge_body(xl_ref, xc_ref, zp_ref, za_ref, zn_ref, bc_ref, bp_ref, bn_ref, pcnt_ref, wp_ref, ps_ref,
                yb_ref, yc_ref, g0, g1, g2, g3, g4, g5, wa_ref, wb_ref, wc_ref, wo_ref,
                lng_ref, lnb_ref, mod_ref, wr2_ref, br_ref, tril_ref, x1_ref, h2_ref, route_ref, cnt_ref,
                cnt_scr, *, alpha, split, nt, n_lat_tiles):
    @pl.when(pl.program_id(0) == 0)
    def _():
        cnt_scr[...] = jnp.zeros_like(cnt_scr)

    gates = ((g0, g1), (g2, g3), (g4, g5))
    ya = _pooled_tile(zp_ref, za_ref, zn_ref, bc_ref, bp_ref, bn_ref, pcnt_ref, wp_ref, ps_ref,
                      nt=nt, n_lat_tiles=n_lat_tiles)
    ys = (ya, yb_ref[...], yc_ref[...])
    ws = (wa_ref, wb_ref, wc_ref)
    half = wa_ref.shape[1] // 2
    mix = None
    for n in range(2):
        m = None
        for kbr in range(3):
            pr = _dot(ys[kbr], ws[kbr][:, n * half:(n + 1) * half])
            term = _sigmoid_t(gates[kbr][n][...].astype(F32)) * pr
            m = term if m is None else m + term
        part = _dot(m.astype(BF16), wo_ref[n * half:(n + 1) * half, :])
        mix = part if mix is None else mix + part
    x = jnp.where(pl.program_id(0) < split, xl_ref[...], xc_ref[...])
    r = alpha * x + mod_ref[0, 2:3, :] * mix
    x1 = _ln(r) * lng_ref[...] + lnb_ref[...]
    x1_ref[...] = x1
    h2 = _ln(x1) * (1.0 + mod_ref[0, 4:5, :]) + mod_ref[0, 3:4, :]
    _store_packed_rows(h2_ref, h2)

    h2_hi = h2.astype(BF16)
    h2_lo = (h2 - h2_hi.astype(F32)).astype(BF16)
    hh = _dot(h2_hi, wr2_ref[...])
    logits = (hh[:, :128] + _dot(h2_lo, wr2_ref[:, :128]) + hh[:, 128:]) + br_ref[...]
    lane = lax.broadcasted_iota(jnp.int32, logits.shape, 1).astype(F32)
    is_grp = jnp.where(lane >= N_EXPERTS, jnp.where(lane < N_EXPERTS + N_GROUPS, 1.0, 0.0), 0.0) > 0.5
    lgm = jnp.where(is_grp, logits, NEG_BIG)
    mg = jnp.max(lgm, axis=-1, keepdims=True)
    p_grp = 1.0 / jnp.sum(jnp.exp(lgm - mg), axis=-1, keepdims=True)
    grp = jnp.min(jnp.where(lgm == mg, lane, 1e9), axis=-1, keepdims=True) - N_EXPERTS
    lo = grp * EXP_PER_GROUP
    in_grp = jnp.where(lane >= lo, jnp.where(lane < lo + EXP_PER_GROUP, 1.0, 0.0), 0.0) > 0.5
    lem = jnp.where(in_grp, logits, NEG_BIG)
    m1 = jnp.max(lem, axis=-1, keepdims=True)
    id1 = jnp.min(jnp.where(lem == m1, lane, 1e9), axis=-1, keepdims=True)
    lem2 = jnp.where(lane == id1, NEG_BIG, lem)
    m2 = jnp.max(lem2, axis=-1, keepdims=True)
    id2 = jnp.min(jnp.where(lem2 == m2, lane, 1e9), axis=-1, keepdims=True)
    u2 = jnp.exp(m2 - m1)
    w1 = p_grp / (1.0 + u2)
    w2 = p_grp * u2 / (1.0 + u2)
    oh1 = jnp.where(lane == id1, 1.0, 0.0)
    oh2 = jnp.where(lane == id2, 1.0, 0.0)
    oh = oh1 + oh2
    before = _dot(tril_ref[...], oh.astype(BF16)) + cnt_scr[...]
    rank1 = jnp.sum(before * oh1, axis=-1, keepdims=True)
    rank2 = jnp.sum(before * oh2, axis=-1, keepdims=True)
    cnt_scr[...] += jnp.sum(oh, axis=0, keepdims=True)
    cnt_ref[...] = jnp.broadcast_to(cnt_scr[...], cnt_ref.shape)
    route = jnp.zeros_like(logits)
    for ln, val in enumerate((w1, w2, id1, id2, rank1, rank2)):
        route = jnp.where(lane == ln, val, route)
    route_ref[...] = route


def _merge_call(tok, pool, yb, yc, z, mod, wa, wb, wc, wo, lng, lnb, wr, br, tm, modrow, alpha, t_len):
    na, d = yc.shape[0], tok[0].shape[1]
    split, tok_specs = _token_specs(tok, tm)
    assert pool[0][0].shape[2] == tm
    pool_specs, pool_args, n_lat_tiles = _pool_specs(z, *pool)
    row = lambda w: pl.BlockSpec((tm, w), lambda i: (i, 0))
    gate = lambda cb: pl.BlockSpec((tm, 512), lambda i: (i, cb))
    full = lambda a: pl.BlockSpec(a.shape, lambda i: (0,) * a.ndim)
    g0 = CB_GATE // 4
    tril = jnp.asarray(np.tril(np.ones((tm, tm), np.float32), -1), BF16)
    wr_hi = lax.reduce_precision(wr, exponent_bits=8, mantissa_bits=7)
    wr2 = jnp.concatenate([wr_hi, wr - wr_hi], axis=1).astype(BF16)
    return pl.pallas_call(
        functools.partial(_merge_body, alpha=alpha, split=split, nt=t_len // tm, n_lat_tiles=n_lat_tiles),
        grid=(na // tm,),
        in_specs=tok_specs + pool_specs + [row(512), row(512)] + [gate(g0 + j) for j in range(6)]
                 + [full(wa), full(wb), full(wc), full(wo), full(lng), full(lnb),
                    pl.BlockSpec((1, 6, d), lambda i: (modrow(i), 0, 0)), full(wr2), full(br), full(tril)],
        out_specs=[row(d), pl.BlockSpec((tm * ROW_SLABS, 128), lambda i: (i, 0)), row(128),
                   pl.BlockSpec((8, 128), lambda i: (0, 0))],
        out_shape=[jax.ShapeDtypeStruct((na, d), F32), jax.ShapeDtypeStruct((na * ROW_SLABS, 128), jnp.uint32),
                   jax.ShapeDtypeStruct((na, 128), F32), jax.ShapeDtypeStruct((8, 128), F32)],
        scratch_shapes=[pltpu.VMEM((1, 128), F32)],
        compiler_params=_cparams(("arbitrary",)),
        name="merge",
    )(tok[0], tok[1], *pool_args, yb, yc, z, z, z, z, z, z, wa, wb, wc, wo, lng, lnb, mod, wr2, br, tril)


ROW_DMA_UNROLL = 8


def _slab(ref, row):
    start = row * ROW_SLABS
    if not isinstance(row, int):
        start = pl.multiple_of(start, ROW_SLABS)
    return ref.at[pl.ds(start, ROW_SLABS), :]


def _slots_body(route_ref, seg_ref, o_ref):
    route = route_ref[...]
    lane = lax.broadcasted_iota(jnp.int32, route.shape, 1).astype(F32)
    seg = seg_ref[...]
    slots = []
    for k in range(2):
        eid = route[:, 2 + k:3 + k]
        start = jnp.sum(jnp.where(lane == eid, seg, 0.0), axis=-1, keepdims=True)
        slots.append(start + route[:, 4 + k:5 + k])
    both = jnp.where(lane == 0.0, slots[0], jnp.where(lane == 1.0, slots[1], 0.0))
    by_token = both.T[:8].astype(jnp.int32)
    tm = o_ref.shape[2]
    for c in range(o_ref.shape[0]):
        o_ref[c] = by_token[:, c * tm:(c + 1) * tm]


def _slots_call(route, seg, tm):
    na = route.shape[0]
    group = _row_tile(4, na // tm)
    return pl.pallas_call(
        _slots_body,
        grid=(na // (tm * group),),
        in_specs=[pl.BlockSpec((tm * group, 128), lambda i: (i, 0)), pl.BlockSpec((1, 128), lambda i: (0, 0))],
        out_specs=pl.BlockSpec((group, 8, tm), lambda i: (i, 0, 0)),
        out_shape=jax.ShapeDtypeStruct((na // tm, 8, tm), jnp.int32),
        compiler_params=_cparams(("parallel",)),
        name="slots",
    )(route, seg)


def _dispatch_body(slot_ref, h_ref, xs_in, xs_out, sem):
    del xs_in
    tm = h_ref.shape[0] // ROW_SLABS

    for t in range(tm):
        for k in range(2):
            pltpu.make_async_copy(_slab(h_ref, t), _slab(xs_out, slot_ref[0, k, t]), sem).start(priority=k)

    for k in range(2):
        pltpu.make_async_copy(h_ref, xs_out.at[pl.ds(0, tm * ROW_SLABS), :], sem).wait()


def _dispatch_call(h2p, slots, n_rows, tm, xs_prev=None):
    xs0 = jnp.zeros((n_rows * ROW_SLABS, 128), h2p.dtype) if xs_prev is None else xs_prev
    assert xs0.shape[0] >= n_rows * ROW_SLABS
    return pl.pallas_call(
        _dispatch_body,
        grid=(slots.shape[0],),
        in_specs=[pl.BlockSpec((1, 8, tm), lambda i: (i, 0, 0), memory_space=pltpu.SMEM),
                  pl.BlockSpec((tm * ROW_SLABS, 128), lambda i: (i, 0)),
                  pl.BlockSpec(memory_space=pl.ANY)],
        out_specs=pl.BlockSpec(memory_space=pl.ANY),
        out_shape=jax.ShapeDtypeStruct(xs0.shape, xs0.dtype),
        input_output_aliases={2: 0},
        scratch_shapes=[pltpu.SemaphoreType.DMA(())],
        compiler_params=_cparams(("arbitrary",)),
        name="dispatch",
    )(slots, h2p, xs0)


def _experts_body(te_ref, nu_ref, xs_ref, wg_ref, wu_ref, wd_ref, y_ref, wg_scr, wu_scr, wd_scr):
    j = pl.program_id(0)

    @pl.when(j >= nu_ref[0])
    def _():
        y_ref[...] = jnp.zeros_like(y_ref)

    @pl.when(jnp.logical_and(j < nu_ref[0], jnp.logical_or(j == 0, te_ref[j] != te_ref[jnp.maximum(j - 1, 0)])))
    def _():
        wg_scr[...] = wg_ref[0].astype(BF16)
        wu_scr[...] = wu_ref[0].astype(BF16)
        wd_scr[...] = wd_ref[0].astype(BF16)

    @pl.when(j < nu_ref[0])
    def _():
        x = jnp.concatenate([c.astype(BF16) for c in _load_packed_rows(xs_ref)], axis=1)
        gt = _dot(x, wg_scr[...])
        a = gt * _sigmoid_t(gt) * _dot(x, wu_scr[...])
        _store_packed_rows(y_ref, _dot(a.astype(BF16), wd_scr[...]))


def _experts_call(tile_expert, n_used, xs, wg, wu, wd, layer, tr):
    rows, w = xs.shape
    _, ne, d, de = wg.shape
    tr = tr * ROW_SLABS
    used = lambda j, te, nu: jnp.minimum(j, nu[0] - 1)
    return pl.pallas_call(
        _experts_body,
        grid_spec=pltpu.PrefetchScalarGridSpec(
            num_scalar_prefetch=2,
            grid=(rows // tr,),
            in_specs=[pl.BlockSpec((tr, w), lambda j, te, nu: (used(j, te, nu), 0)),
                      pl.BlockSpec((None, 1, d, de), lambda j, te, nu: (layer, te[used(j, te, nu)], 0, 0)),
                      pl.BlockSpec((None, 1, d, de), lambda j, te, nu: (layer, te[used(j, te, nu)], 0, 0)),
                      pl.BlockSpec((None, 1, de, d), lambda j, te, nu: (layer, te[used(j, te, nu)], 0, 0))],
            out_specs=pl.BlockSpec((tr, w), lambda j, te, nu: (j, 0)),
            scratch_shapes=[pltpu.VMEM((d, de), BF16), pltpu.VMEM((d, de), BF16), pltpu.VMEM((de, d), BF16)]),
        out_shape=jax.ShapeDtypeStruct((rows, w), jnp.uint32),
        compiler_params=_cparams(("arbitrary",)),
        name="experts",
    )(tile_expert, n_used, xs, wg, wu, wd)


def _combine_body(slot_ref, next_ref, y_hbm, x1_ref, route_ref, lng_ref, lnb_ref, mod_ref, o_ref,
                  buf_scr, sems, *, alpha):
    tm = x1_ref.shape[0]
    i = pl.program_id(0)
    cur = i % 2

    def gather(idx_ref, b):
        def issue(j, carry):
            for u in range(ROW_DMA_UNROLL):
                t = j * ROW_DMA_UNROLL + u
                for k in range(2):
                    pltpu.make_async_copy(_slab(y_hbm, idx_ref[0, k, t]), _slab(buf_scr.at[b, k], t),
                                          sems.at[b]).start(priority=k)
            return carry

        lax.fori_loop(0, tm // ROW_DMA_UNROLL, issue, 0)

    def wait_slot(b):
        for k in range(2):
            pltpu.make_async_copy(y_hbm.at[pl.ds(0, tm * ROW_SLABS), :], buf_scr.at[b, k], sems.at[b]).wait()

    @pl.when(i == 0)
    def _():
        gather(slot_ref, cur)

    wait_slot(cur)
    for t in range(tm):
        for k in range(2):
            pltpu.make_async_copy(_slab(y_hbm, next_ref[0, k, t]), _slab(buf_scr.at[1 - cur, k], t),
                                  sems.at[1 - cur]).start(priority=k)

    w1 = route_ref[:, 0:1]
    w2 = route_ref[:, 1:2]
    moe = jnp.concatenate([w1 * a1 + w2 * a2 for a1, a2 in zip(_load_packed_rows(buf_scr.at[cur, 0]),
                                                               _load_packed_rows(buf_scr.at[cur, 1]))], axis=1)
    r = alpha * x1_ref[...] + mod_ref[0, 5:6, :] * moe
    o_ref[...] = _ln(r) * lng_ref[...] + lnb_ref[...]

    @pl.when(i == pl.num_programs(0) - 1)
    def _():
        wait_slot(1 - cur)


def _combine_call(slots, y, x1, route, lng, lnb, mod, tm, modrow, alpha):
    na, d = x1.shape
    full = lambda a: pl.BlockSpec(a.shape, lambda i: (0,) * a.ndim)
    last = na // tm - 1
    return pl.pallas_call(
        functools.partial(_combine_body, alpha=alpha),
        grid=(na // tm,),
        in_specs=[pl.BlockSpec((1, 8, tm), lambda i: (i, 0, 0), memory_space=pltpu.SMEM),
                  pl.BlockSpec((1, 8, tm), lambda i: (jnp.minimum(i + 1, last), 0, 0), memory_space=pltpu.SMEM),
                  pl.BlockSpec(memory_space=pl.ANY),
                  pl.BlockSpec((tm, d), lambda i: (i, 0)),
                  pl.BlockSpec((tm, 128), lambda i: (i, 0)),
                  full(lng), full(lnb),
                  pl.BlockSpec((1, 6, d), lambda i: (modrow(i), 0, 0))],
        out_specs=pl.BlockSpec((tm, d), lambda i: (i, 0)),
        out_shape=jax.ShapeDtypeStruct((na, d), F32),
        scratch_shapes=[pltpu.VMEM((2, 2, tm * ROW_SLABS, 128), jnp.uint32), pltpu.SemaphoreType.DMA((2,))],
        compiler_params=_cparams(("arbitrary",)),
        name="combine",
    )(slots, slots, y, x1, route, lng, lnb, mod)


def _expert_buffer_tiles(n_tokens, tr):
    return -(-2 * n_tokens // tr) + N_EXPERTS


def _routing_tables(counts, n_tiles, tr):
    cnt = counts[0, :N_EXPERTS].astype(jnp.int32)
    ntile = (cnt + tr - 1) // tr
    tile_start = jnp.cumsum(ntile) - ntile
    n_used = jnp.sum(ntile)
    seg = jnp.zeros((1, 128), F32).at[0, :N_EXPERTS].set((tile_start * tr).astype(F32))
    tile_expert = jnp.sum(jnp.arange(n_tiles, dtype=jnp.int32)[:, None] >= tile_start[None, :], axis=1) - 1
    return seg, tile_expert.astype(jnp.int32), n_used.reshape(1).astype(jnp.int32)


def _lower_bounds(logits):
    p = jax.nn.softmax(logits.astype(F32), axis=0)
    return jnp.cumsum(p, axis=0) - p[:1]


def _row_tile(limit, *sizes):
    tm = limit
    while any(s % tm for s in sizes):
        tm //= 2
    return tm


def kernel(x, c, ctx, c_ctx, w_ada, b_ada, w_in, w_pool, pool_scale, lb_logits_fwd, lb_logits_bwd, hg_gain, rpb, w_br_a, w_br_b, w_br_c, w_out, ln1_g, ln1_b, w_rg, b_rg, w_re, b_re, w_gate, w_up, w_down, ln2_g, ln2_b):
    nbatch, t_len, d = x.shape
    c_len = ctx.shape[1]
    depth = w_ada.shape[0]
    assert c_len == SEQ_TILE and t_len % SEQ_TILE == 0 and t_len % GRID_W == 0
    alpha = (2.0 * depth) ** 0.25
    n_lat = nbatch * t_len
    nt = t_len // SEQ_TILE

    na = n_lat + nbatch * c_len
    tok = (x.reshape(n_lat, d), ctx.reshape(nbatch * c_len, d), 0)

    mod_rows = -(-(nbatch + 1) // 8) * 8
    cc = jnp.zeros((mod_rows, d), F32).at[:nbatch].set(c).at[nbatch].set(c_ctx)
    ada = _ada_call(cc, w_ada, b_ada)

    lb_f = _lower_bounds(lb_logits_fwd).reshape(depth, HG_HEADS, 1, HG_DK)
    lb_b = _lower_bounds(lb_logits_bwd).reshape(depth, HG_HEADS, 1, HG_DK)
    tm_big = _row_tile(1024, t_len, nbatch * c_len)
    tm_mid = _row_tile(512, t_len, nbatch * c_len)
    assert tm_mid % c_len == 0
    pool_consts = _pool_consts(tm_mid, c_len)
    hg_f = _hgrn_consts(False)
    hg_b = _hgrn_consts(True)

    def modrow_for(tm):
        return lambda i: jnp.where(i * tm < n_lat, (i * tm) // t_len, nbatch)

    n_tiles = _expert_buffer_tiles(na, EXPERT_TILE)
    xs = None
    for l in range(depth):
        last = l == depth - 1
        mod = ada[l].reshape(mod_rows, 6, d)
        z = _inproj_call(tok, na, mod, w_in[l].astype(BF16), tm_big, modrow_for(tm_big))
        pool = (pool_consts, w_pool[l].astype(BF16), pool_scale[l].reshape(1, -1), n_lat)
        o_f =_hgrn_call(z, lb_f[l], hg_f, nbatch, nt, reverse=False)
        yb = _hgrn_call(z, lb_b[l], hg_b, nbatch, nt, reverse=True, o_fwd=o_f,
                        gain=hg_gain[l].reshape(HG_HEADS, 1, HG_DK))
        bias = _na_bias_table(rpb[l], t_len // GRID_W)
        yc = _na_call(z, bias, nbatch, t_len, c_len, with_ctx=not last)
        wr = jnp.zeros((d, 128), F32).at[:, :N_EXPERTS].set(w_re[l]).at[:, N_EXPERTS:N_EXPERTS + N_GROUPS].set(w_rg[l])
        br = jnp.zeros((1, 128), F32).at[0, :N_EXPERTS].set(b_re[l]).at[0, N_EXPERTS:N_EXPERTS + N_GROUPS].set(b_rg[l])
        x1, h2p, route, counts = _merge_call(
            tok, pool, yb, yc, z, mod, w_br_a[l].astype(BF16), w_br_b[l].astype(BF16), w_br_c[l].astype(BF16),
            w_out[l].astype(BF16), ln1_g[l].reshape(1, d), ln1_b[l].reshape(1, d), wr, br, tm_mid,
            modrow_for(tm_mid), alpha, t_len)
        seg, tile_expert, n_used = _routing_tables(counts, n_tiles, EXPERT_TILE)
        slots = _slots_call(route, seg, tm_mid)
        xs = _dispatch_call(h2p, slots, n_tiles * EXPERT_TILE, tm_mid, xs)
        ys = _experts_call(tile_expert, n_used, xs, w_gate, w_up, w_down, l, EXPERT_TILE)
        xa = _combine_call(slots, ys, x1, route, ln2_g[l].reshape(1, d), ln2_b[l].reshape(1, d), mod,
                           tm_mid, modrow_for(tm_mid), alpha)
        tok = (xa, xa, n_lat)
    return xa.reshape(nbatch, t_len, d)
```

```python
import functools

import numpy as np
import jax
import jax.numpy as jnp
from jax import lax
from jax.experimental import pallas as pl
from jax.experimental.pallas import tpu as pltpu

F32 = jnp.float32
BF16 = jnp.bfloat16
HIGHEST = lax.Precision.HIGHEST

GRID_W = 64
POOL_WINDOWS = (2, 4, 8, 16)
POOL_GDIM = 128
HG_HEADS = 4
HG_DK = 128
HG_BLOCK = 16
NA_HEADS = 8
NA_HD = 64
NA_ROWS = 8
NA_COLS = 16
NA_QROWS = 4
NA_KROWS = 12
N_GROUPS = 4
EXP_PER_GROUP = 8
N_EXPERTS = N_GROUPS * EXP_PER_GROUP
LN_EPS = 1e-5
RMS_EPS = 1e-6
NEG_BIG = -1e30
SEQ_TILE = 256
EXPERT_TILE = 512
VMEM_LIMIT = 56 * 1024 * 1024

CB_A, CB_Q, CB_FF, CB_FB, CB_I, CB_G, CB_NQ, CB_NK, CB_NV, CB_GATE = 0, 4, 8, 12, 16, 20, 24, 28, 32, 36


def _cparams(sem):
    return pltpu.CompilerParams(dimension_semantics=sem, vmem_limit_bytes=VMEM_LIMIT)


def _ln(x):
    mu = jnp.mean(x, axis=-1, keepdims=True)
    xc = x - mu
    var = jnp.mean(xc * xc, axis=-1, keepdims=True)
    return xc * lax.rsqrt(var + LN_EPS)


def _sigmoid(x):
    return 1.0 / (1.0 + jnp.exp(-x))


def _sigmoid_t(x):
    return 0.5 * jnp.tanh(0.5 * x) + 0.5


def _dot(a, b):
    return jnp.dot(a, b, preferred_element_type=F32)


def _dot_nt(a, b):
    return lax.dot_general(a, b, (((1,), (1,)), ((), ())), preferred_element_type=F32)


def _dot_tn(a, b):
    return lax.dot_general(a, b, (((0,), (0,)), ((), ())), preferred_element_type=F32)


def _dot01(m01, x):
    x1 = x.astype(BF16)
    r1 = x - x1.astype(F32)
    x2 = r1.astype(BF16)
    x3 = (r1 - x2.astype(F32)).astype(BF16)
    return _dot(m01, x1) + _dot(m01, x2) + _dot(m01, x3)


def _ada_body(c_ref, w_ref, b_ref, o_ref):
    cs = c_ref[...]
    s = cs * _sigmoid(cs)
    o_ref[0] = jnp.dot(s, w_ref[0], preferred_element_type=F32, precision=HIGHEST) + b_ref[0]


def _ada_call(cc, w_ada, b_ada):
    depth, d, n6 = w_ada.shape
    rows = cc.shape[0]
    return pl.pallas_call(
        _ada_body,
        grid=(depth, n6 // d),
        in_specs=[pl.BlockSpec((rows, d), lambda l, j: (0, 0)),
                  pl.BlockSpec((1, d, d), lambda l, j: (l, 0, j)),
                  pl.BlockSpec((1, 1, d), lambda l, j: (l, 0, j))],
        out_specs=pl.BlockSpec((1, rows, d), lambda l, j: (l, 0, j)),
        out_shape=jax.ShapeDtypeStruct((depth, rows, n6), F32),
        compiler_params=_cparams(("parallel", "parallel")),
        name="ada",
    )(cc, w_ada, b_ada.reshape(depth, 1, n6))


def _token_specs(tok, tm):
    lat, ctx, ctx_row0 = tok
    split = (ctx_row0 if ctx is lat else lat.shape[0]) // tm
    off = ctx_row0 // tm
    d = lat.shape[1]
    lat_map = lambda i, *_: (jnp.minimum(i, split - 1), 0)
    ctx_map = lambda i, *_: (jnp.maximum(i - split, 0) + off, 0)
    return split, [pl.BlockSpec((tm, d), lat_map), pl.BlockSpec((tm, d), ctx_map)]


def _inproj_body(xl_ref, xc_ref, mod_ref, w_ref, z_ref, h_scr, *, split):
    def modulated(x_ref):
        h = _ln(x_ref[...]) * (1.0 + mod_ref[0, 1:2, :]) + mod_ref[0, 0:1, :]
        h_scr[...] = h.astype(BF16)

    first = pl.program_id(1) == 0
    pl.when(jnp.logical_and(first, pl.program_id(0) < split))(lambda: modulated(xl_ref))
    pl.when(jnp.logical_and(first, pl.program_id(0) >= split))(lambda: modulated(xc_ref))
    z_ref[...] = _dot(h_scr[...], w_ref[...]).astype(z_ref.dtype)


def _inproj_call(tok, na, mod, w_in, tm, modrow):
    d, d_in = w_in.shape
    tn = 2560
    assert d_in % tn == 0
    split, tok_specs = _token_specs(tok, tm)
    return pl.pallas_call(
        functools.partial(_inproj_body, split=split),
        grid=(na // tm, d_in // tn),
        in_specs=tok_specs + [pl.BlockSpec((1, 6, d), lambda i, j: (modrow(i), 0, 0)),
                              pl.BlockSpec((d, tn), lambda i, j: (0, j))],
        out_specs=pl.BlockSpec((tm, tn), lambda i, j: (i, j)),
        out_shape=jax.ShapeDtypeStruct((na, d_in), BF16),
        scratch_shapes=[pltpu.VMEM((tm, d), BF16)],
        compiler_params=_cparams(("parallel", "arbitrary")),
        name="inproj",
    )(tok[0], tok[1], mod, w_in)


def _pool_consts(n, c_len):
    t = np.arange(n)[:, None]
    bc = np.zeros((2, 4, n, n), np.float32)
    bp = np.zeros((4, n, 16), np.float32)
    bn = np.zeros((4, n, 16), np.float32)
    for g, win in enumerate(POOL_WINDOWS):
        lo, hi = t - win // 2, t + win // 2 - 1
        s = np.arange(n)[None, :]
        bc[0, g] = (s >= lo) & (s <= hi)
        bc[1, g] = bc[0, g] * (s // c_len == t // c_len)
        s = np.arange(16)[None, :] - 16
        bp[g] = (s >= lo) & (s <= hi)
        s = np.arange(16)[None, :] + n
        bn[g] = (s >= lo) & (s <= hi)
    cnt = np.stack([np.stack([bc[0].sum(-1), bp.sum(-1), bn.sum(-1)], axis=1),
                    np.stack([bc[1].sum(-1), 0 * bp.sum(-1), 0 * bn.sum(-1)], axis=1)])
    cnt = np.broadcast_to(cnt[..., None], (2, 4, 3, n, 128)).astype(np.float32)
    return (jnp.asarray(bc, BF16), jnp.asarray(bp, BF16), jnp.asarray(bn, BF16), jnp.asarray(cnt))


def _pooled_tile(prev_ref, cur_ref, next_ref, bc_ref, bp_ref, bn_ref, cnt_ref, wp_ref, ps_ref, *, nt, n_lat_tiles):
    i = pl.program_id(0)
    k = i % nt
    lat = i < n_lat_tiles
    has_prev = jnp.where(jnp.logical_and(lat, k != 0), 1.0, 0.0).astype(F32)
    has_next = jnp.where(jnp.logical_and(lat, k != nt - 1), 1.0, 0.0).astype(F32)
    groups = []
    for g in range(len(POOL_WINDOWS)):
        sl = slice(g * POOL_GDIM, (g + 1) * POOL_GDIM)
        u = cur_ref[:, sl]
        ssum = (_dot(bc_ref[0, g], u) + has_prev * _dot(bp_ref[g], prev_ref[:, sl])
                + has_next * _dot(bn_ref[g], next_ref[:, sl]))
        cnt = cnt_ref[0, g, 0] + has_prev * cnt_ref[0, g, 1] + has_next * cnt_ref[0, g, 2]
        dlt = ssum / cnt - u.astype(F32)
        groups.append((_dot(dlt.astype(BF16), wp_ref[g]) * ps_ref[:, sl]).astype(BF16))
    return jnp.concatenate(groups, axis=1)


def _pool_specs(z, consts, w_pool, pool_scale, n_lat):
    bc, bp, bn, cnt = consts
    n = bc.shape[2]
    hb = n // 16
    last16 = z.shape[0] // 16 - 1
    n_lat_tiles = n_lat // n
    full = lambda a: pl.BlockSpec(a.shape, lambda i: (0,) * a.ndim)
    variant = lambda a: pl.BlockSpec((1,) + a.shape[1:],
                                     lambda i: (jnp.where(i < n_lat_tiles, 0, 1),) + (0,) * (a.ndim - 1))
    specs = [pl.BlockSpec((16, 512), lambda i: (jnp.maximum(i * hb - 1, 0), CB_A // 4)),
             pl.BlockSpec((n, 512), lambda i: (i, CB_A // 4)),
             pl.BlockSpec((16, 512), lambda i: (jnp.minimum((i + 1) * hb, last16), CB_A // 4)),
             variant(bc), full(bp), full(bn), variant(cnt), full(w_pool), full(pool_scale)]
    return specs, [z, z, z, bc, bp, bn, cnt, w_pool, pool_scale], n_lat_tiles


def _hgrn_consts(reverse):
    n, bs = SEQ_TILE, HG_BLOCK
    nb = n // bs
    t = np.arange(n)
    o = (n - 1 - t) if reverse else t
    blk = t // bs
    jb = np.arange(nb)
    ob = (nb - 1 - jb) if reverse else jb
    cum = ((blk[:, None] == blk[None, :]) & (o[None, :] <= o[:, None])).astype(np.float32)
    bsum = (jb[:, None] == blk[None, :]).astype(np.float32)
    widths = [2 ** l for l in range(1, int(np.log2(nb)) + 1)]
    lvl = np.full((n, n), -1, np.int32)
    obt = ob[blk]
    same = blk[:, None] == blk[None, :]
    lvl[same & (o[None, :] <= o[:, None])] = 0
    for li, w in reversed(list(enumerate(widths, start=1))):
        m = (obt[:, None] // w == obt[None, :] // w) & (obt[None, :] < obt[:, None]) & ~same
        lvl[m] = li
    mats = []
    for w in widths:
        mid = (ob // w) * w + w // 2
        mats.append((mid[:, None] <= ob[None, :]) & (ob[None, :] < ob[:, None]))
    for w in widths:
        mid = (ob // w) * w + w // 2
        mats.append((ob[:, None] < ob[None, :]) & (ob[None, :] < mid[:, None]))
    mats.append(ob[None, :] < ob[:, None])
    mats.append(ob[None, :] > ob[:, None])
    mats.append(np.ones((nb, nb), bool))
    tsm = np.concatenate(mats, axis=0).astype(np.float32)
    return (jnp.asarray(cum, BF16), jnp.asarray(bsum, BF16), jnp.asarray(lvl, BF16), jnp.asarray(tsm, BF16),
            len(widths))


def _expand_blocks(c, n):
    nb, lanes = c.shape
    return jnp.concatenate([jnp.broadcast_to(c[j:j + 1, :], (n // nb, lanes)) for j in range(nb)], axis=0)


def _hgrn_body(*refs, n_levels, final):
    if final:
        (zq_ref, zf_ref, zi_ref, lb_ref, cum_ref, bsum_ref, lvl_ref, tsm_ref,
         of_ref, zg_ref, gain_ref, o_ref, st_scr) = refs
    else:
        zq_ref, zf_ref, zi_ref, lb_ref, cum_ref, bsum_ref, lvl_ref, tsm_ref, o_ref, st_scr = refs
    n = zq_ref.shape[0]
    nb = n // HG_BLOCK

    @pl.when(pl.program_id(1) == 0)
    def _():
        st_scr[...] = jnp.zeros_like(st_scr)

    zq = zq_ref[...].astype(F32)
    zf = zf_ref[...].astype(F32)
    lb = jnp.concatenate([lb_ref[h] for h in range(HG_HEADS)], axis=1)
    sig = _sigmoid(zf)
    lf = jnp.log(lb + (1.0 - lb) * sig)
    k_all = (1.0 - lb) * (1.0 - sig)
    q_all = zq * _sigmoid_t(zq)
    lf_hi = lf.astype(BF16)
    lf2 = jnp.concatenate([lf_hi, (lf - lf_hi.astype(F32)).astype(BF16)], axis=1)
    w = HG_HEADS * HG_DK
    b2 = _dot(cum_ref[...], lf2)
    b_all = b2[:, :w] + b2[:, w:]
    t2 = _dot(bsum_ref[...], lf2)
    tot_all = t2[:, :w] + t2[:, w:]
    coef_all = _dot01(tsm_ref[...], tot_all)
    qd_all = q_all * jnp.exp(b_all)
    kd_all = (k_all * jnp.exp(-b_all)).astype(BF16)
    ks_all = k_all * jnp.exp(_expand_blocks(tot_all, n) - b_all)

    ecoef = jnp.exp(coef_all)
    scale = lambda idx: _expand_blocks(ecoef[idx * nb:(idx + 1) * nb], n)
    q_lv = [qd_all.astype(BF16)] + [(qd_all * scale(li)).astype(BF16) for li in range(n_levels)]
    k_lv = [kd_all] + [(ks_all * scale(n_levels + li)).astype(BF16) for li in range(n_levels)]
    qs_all = (qd_all * scale(2 * n_levels)).astype(BF16)
    kn_all = (ks_all * scale(2 * n_levels + 1)).astype(BF16)
    dec_all = ecoef[(2 * n_levels + 2) * nb:(2 * n_levels + 2) * nb + 1]

    lvl = lvl_ref[...]
    heads = [slice(h * HG_DK, (h + 1) * HG_DK) for h in range(HG_HEADS)]
    a_h = [jnp.zeros((n, n), BF16) for _ in heads]
    for li in range(n_levels + 1):
        for h, sl in enumerate(heads):
            a_h[h] = jnp.where(lvl == float(li), _dot_nt(q_lv[li][:, sl], k_lv[li][:, sl]).astype(BF16), a_h[h])

    for h, sl in enumerate(heads):
        v = zi_ref[:, sl]
        o = _dot(a_h[h], v)

        st = st_scr[h]
        o = o + _dot_nt(qs_all[:, sl], st.astype(BF16))
        st_scr[h] = st * dec_all[:, sl] + _dot_tn(v, kn_all[:, sl])

        if final:
            o = o + of_ref[:, sl]
            o = o * lax.rsqrt(jnp.mean(o * o, axis=-1, keepdims=True) + RMS_EPS) * gain_ref[h]
            zg = zg_ref[:, sl].astype(F32)
            o_ref[:, sl] = (o * (zg * _sigmoid_t(zg))).astype(o_ref.dtype)
        else:
            o_ref[:, sl] = o


def _hgrn_call(z, lb, consts, nbatch, nt, reverse, o_fwd=None, gain=None):
    na = z.shape[0]
    n = SEQ_TILE
    cum, bsum, lvl, tsm, n_levels = consts
    ctx_base = nbatch * nt
    final = o_fwd is not None

    def tile(b, s):
        lat = (b * nt + nt - s) if reverse else (b * nt + s - 1)
        return jnp.where(s == 0, ctx_base + b, lat)

    width = HG_HEADS * HG_DK

    def col(cb):
        return pl.BlockSpec((n, width), lambda b, s: (tile(b, s), cb // HG_HEADS))

    full = lambda a: pl.BlockSpec(a.shape, lambda b, s: (0,) * a.ndim)
    in_specs = [col(CB_Q), col(CB_FB if reverse else CB_FF), col(CB_I), full(lb),
                full(cum), full(bsum), full(lvl), full(tsm)]
    args = [z, z, z, lb, cum, bsum, lvl, tsm]
    if final:
        in_specs += [col(0), col(CB_G), full(gain)]
        args += [o_fwd, z, gain]
    return pl.pallas_call(
        functools.partial(_hgrn_body, n_levels=n_levels, final=final),
        grid=(nbatch, nt + 1),
        in_specs=in_specs,
        out_specs=col(0),
        out_shape=jax.ShapeDtypeStruct((na, width), BF16 if final else F32),
        scratch_shapes=[pltpu.VMEM((HG_HEADS, HG_DK, HG_DK), F32)],
        compiler_params=_cparams(("parallel", "arbitrary")),
        name="hgrn_bwd" if final else "hgrn_fwd",
    )(*args)


def _na_bias_table(rpb, rows):
    nrb = rows // NA_QROWS
    assert nrb >= 3 and rows >= NA_KROWS
    kr = min(NA_ROWS, rows)
    qc = np.arange(GRID_W)
    c0 = np.clip(qc - NA_COLS // 2, 0, GRID_W - NA_COLS)
    kc = np.arange(GRID_W)
    col_ok = (kc[None, :] >= c0[:, None]) & (kc[None, :] < c0[:, None] + NA_COLS)
    n_dr, n_dc = 2 * NA_ROWS - 1, 2 * NA_COLS - 1
    dc = np.where(col_ok, kc[None, :] - qc[:, None] + NA_COLS - 1, n_dc)
    sel_c = (dc[..., None] == np.arange(n_dc + 1)).astype(np.float32)
    sel_r = []
    for rb in (0, 1, nrb - 1):
        start = int(np.clip(NA_QROWS * rb - 4, 0, rows - NA_KROWS))
        r = NA_QROWS * rb + np.arange(NA_QROWS)
        r0 = np.clip(r - kr // 2, 0, rows - kr)
        keyrow = start + np.arange(NA_KROWS)
        row_ok = (keyrow[None, :] >= r0[:, None]) & (keyrow[None, :] < r0[:, None] + kr)
        dr = np.where(row_ok, keyrow[None, :] - r[:, None] + NA_ROWS - 1, n_dr)
        sel_r.append((dr[..., None] == np.arange(n_dr + 1)).astype(np.float32))
    masked = np.zeros_like(sel_r[0])
    masked[..., n_dr] = 1.0
    sel_r.append(masked)
    rpb_ext = jnp.pad(rpb.astype(F32), ((0, 0), (0, 1), (0, 1)), constant_values=NEG_BIG)
    bias = jnp.einsum("hij,paki,cdj->phackd", rpb_ext, jnp.asarray(np.stack(sel_r)), jnp.asarray(sel_c),
                      precision=HIGHEST)
    return bias.reshape(4, NA_HEADS, NA_QROWS * GRID_W, NA_KROWS * GRID_W)


def _na_body(q_ref, k_ref, v_ref, kc_ref, vc_ref, bias_ref, o_ref, *, rows):
    rb = pl.program_id(1)
    nk = NA_KROWS * GRID_W
    start_row = jnp.clip(NA_QROWS * rb - 4, 0, rows - NA_KROWS)
    start = pl.multiple_of(start_row * GRID_W, GRID_W)
    nq = q_ref.shape[0]
    lane = lax.broadcasted_iota(jnp.int32, (nq, 128), 1)
    scale = NA_HD ** -0.5
    pairs = [slice(128 * p, 128 * (p + 1)) for p in range(NA_HEADS // 2)]
    scores = []
    for p, sl in enumerate(pairs):
        qp = q_ref[:, sl] * scale
        zero = jnp.zeros_like(qp)
        q2 = jnp.concatenate([jnp.where(lane < NA_HD, qp, zero), jnp.where(lane >= NA_HD, qp, zero)], axis=0)
        s_loc = _dot_nt(q2, k_ref[pl.ds(start, nk), sl]) + bias_ref[0, 2 * p:2 * p + 2].reshape(2 * nq, nk)
        scores.append((s_loc, _dot_nt(q2, kc_ref[:, sl])))
    probs = []
    for s_loc, s_ctx in scores:
        m = jnp.maximum(jnp.max(s_loc, axis=-1, keepdims=True), jnp.max(s_ctx, axis=-1, keepdims=True))
        p_loc = jnp.exp(s_loc - m)
        p_ctx = jnp.exp(s_ctx - m)
        den = jnp.sum(p_loc, axis=-1, keepdims=True) + jnp.sum(p_ctx, axis=-1, keepdims=True)
        probs.append((p_loc.astype(BF16), p_ctx.astype(BF16), den))
    for sl, (p_loc, p_ctx, den) in zip(pairs, probs):
        o2 = (_dot(p_loc, v_ref[pl.ds(start, nk), sl]) + _dot(p_ctx, vc_ref[:, sl])) / den
        o_ref[:, sl] = jnp.where(lane < NA_HD, o2[:nq], o2[nq:]).astype(o_ref.dtype)


def _na_call(z, bias, nbatch, t_len, c_len, with_ctx):
    na = z.shape[0]
    rows = t_len // GRID_W
    nrb = rows // NA_QROWS
    nq = NA_QROWS * GRID_W
    assert nq == c_len
    ctx_base = nbatch * nrb
    steps = nrb + 1 if with_ctx else nrb

    def qtile(b, r):
        return jnp.where(r < nrb, b * nrb + r, ctx_base + b)

    def pattern(b, r):
        return jnp.where(r == 0, 0, jnp.where(r == nrb - 1, 2, jnp.where(r == nrb, 3, 1)))

    return pl.pallas_call(
        functools.partial(_na_body, rows=rows),
        grid=(nbatch, steps),
        in_specs=[pl.BlockSpec((nq, 512), lambda b, r: (qtile(b, r), CB_NQ // 4)),
                  pl.BlockSpec((t_len, 512), lambda b, r: (b, CB_NK // 4)),
                  pl.BlockSpec((t_len, 512), lambda b, r: (b, CB_NV // 4)),
                  pl.BlockSpec((c_len, 512), lambda b, r: (ctx_base + b, CB_NK // 4)),
                  pl.BlockSpec((c_len, 512), lambda b, r: (ctx_base + b, CB_NV // 4)),
                  pl.BlockSpec((1,) + bias.shape[1:], lambda b, r: (pattern(b, r), 0, 0, 0))],
        out_specs=pl.BlockSpec((nq, 512), lambda b, r: (qtile(b, r), 0)),
        out_shape=jax.ShapeDtypeStruct((na if with_ctx else nbatch * t_len, 512), BF16),
        compiler_params=_cparams(("parallel", "arbitrary")),
        name="natten",
    )(z, z, z, z, z, bias)


ROW_SLABS = 4


def _store_packed_rows(ref, x):
    m = x.shape[0]

    def bits(v):
        return lax.bitcast_convert_type(v.astype(BF16).astype(F32), jnp.uint32)

    for s in range(ROW_SLABS):
        lo = x[:, 128 * s:128 * (s + 1)]
        hi = x[:, 512 + 128 * s:512 + 128 * (s + 1)]
        ref[pl.ds(s, m, stride=ROW_SLABS), :] = (bits(hi) & jnp.uint32(0xFFFF0000)) | (bits(lo) >> 16)


def _load_packed_rows(ref):
    m = ref.shape[0] // ROW_SLABS
    los, his = [], []
    for s in range(ROW_SLABS):
        p = ref[pl.ds(s, m, stride=ROW_SLABS), :]
        los.append(lax.bitcast_convert_type(p << 16, F32))
        his.append(lax.bitcast_convert_type(p & jnp.uint32(0xFFFF0000), F32))
    return los + his


def _merge_body(xl_ref, xc_ref, zp_ref, za_ref, zn_ref, bc_ref, bp_ref, bn_ref, pcnt_ref, wp_ref, ps_ref,
                yb_ref, yc_ref, g0, g1, g2, g3, g4, g5, wa_ref, wb_ref, wc_ref, wo_ref,
                lng_ref, lnb_ref, mod_ref, wr2_ref, br_ref, tril_ref, x1_ref, h2_ref, route_ref, cnt_ref,
                cnt_scr, *, alpha, split, nt, n_lat_tiles):
    @pl.when(pl.program_id(0) == 0)
    def _():
        cnt_scr[...] = jnp.zeros_like(cnt_scr)

    gates = ((g0, g1), (g2, g3), (g4, g5))
    ya = _pooled_tile(zp_ref, za_ref, zn_ref, bc_ref, bp_ref, bn_ref, pcnt_ref, wp_ref, ps_ref,
                      nt=nt, n_lat_tiles=n_lat_tiles)
    ys = (ya, yb_ref[...], yc_ref[...])
    ws = (wa_ref, wb_ref, wc_ref)
    half = wa_ref.shape[1] // 2
    mix = None
    for n in range(2):
        m = None
        for kbr in range(3):
            pr = _dot(ys[kbr], ws[kbr][:, n * half:(n + 1) * half])
            term = _sigmoid_t(gates[kbr][n][...].astype(F32)) * pr
            m = term if m is None else m + term
        part = _dot(m.astype(BF16), wo_ref[n * half:(n + 1) * half, :])
        mix = part if mix is None else mix + part
    x = jnp.where(pl.program_id(0) < split, xl_ref[...], xc_ref[...])
    r = alpha * x + mod_ref[0, 2:3, :] * mix
    x1 = _ln(r) * lng_ref[...] + lnb_ref[...]
    x1_ref[...] = x1
    h2 = _ln(x1) * (1.0 + mod_ref[0, 4:5, :]) + mod_ref[0, 3:4, :]
    _store_packed_rows(h2_ref, h2)

    h2_hi = h2.astype(BF16)
    h2_lo = (h2 - h2_hi.astype(F32)).astype(BF16)
    hh = _dot(h2_hi, wr2_ref[...])
    logits = (hh[:, :128] + _dot(h2_lo, wr2_ref[:, :128]) + hh[:, 128:]) + br_ref[...]
    lane = lax.broadcasted_iota(jnp.int32, logits.shape, 1).astype(F32)
    is_grp = jnp.where(lane >= N_EXPERTS, jnp.where(lane < N_EXPERTS + N_GROUPS, 1.0, 0.0), 0.0) > 0.5
    lgm = jnp.where(is_grp, logits, NEG_BIG)
    mg = jnp.max(lgm, axis=-1, keepdims=True)
    p_grp = 1.0 / jnp.sum(jnp.exp(lgm - mg), axis=-1, keepdims=True)
    grp = jnp.min(jnp.where(lgm == mg, lane, 1e9), axis=-1, keepdims=True) - N_EXPERTS
    lo = grp * EXP_PER_GROUP
    in_grp = jnp.where(lane >= lo, jnp.where(lane < lo + EXP_PER_GROUP, 1.0, 0.0), 0.0) > 0.5
    lem = jnp.where(in_grp, logits, NEG_BIG)
    m1 = jnp.max(lem, axis=-1, keepdims=True)
    id1 = jnp.min(jnp.where(lem == m1, lane, 1e9), axis=-1, keepdims=True)
    lem2 = jnp.where(lane == id1, NEG_BIG, lem)
    m2 = jnp.max(lem2, axis=-1, keepdims=True)
    id2 = jnp.min(jnp.where(lem2 == m2, lane, 1e9), axis=-1, keepdims=True)
    u2 = jnp.exp(m2 - m1)
    w1 = p_grp / (1.0 + u2)
    w2 = p_grp * u2 / (1.0 + u2)
    oh1 = jnp.where(lane == id1, 1.0, 0.0)
    oh2 = jnp.where(lane == id2, 1.0, 0.0)
    oh = oh1 + oh2
    before = _dot(tril_ref[...], oh.astype(BF16)) + cnt_scr[...]
    rank1 = jnp.sum(before * oh1, axis=-1, keepdims=True)
    rank2 = jnp.sum(before * oh2, axis=-1, keepdims=True)
    cnt_scr[...] += jnp.sum(oh, axis=0, keepdims=True)
    cnt_ref[...] = jnp.broadcast_to(cnt_scr[...], cnt_ref.shape)
    route = jnp.zeros_like(logits)
    for ln, val in enumerate((w1, w2, id1, id2, rank1, rank2)):
        route = jnp.where(lane == ln, val, route)
    route_ref[...] = route


def _merge_call(tok, pool, yb, yc, z, mod, wa, wb, wc, wo, lng, lnb, wr, br, tm, modrow, alpha, t_len):
    na, d = yc.shape[0], tok[0].shape[1]
    split, tok_specs = _token_specs(tok, tm)
    assert pool[0][0].shape[2] == tm
    pool_specs, pool_args, n_lat_tiles = _pool_specs(z, *pool)
    row = lambda w: pl.BlockSpec((tm, w), lambda i: (i, 0))
    gate = lambda cb: pl.BlockSpec((tm, 512), lambda i: (i, cb))
    full = lambda a: pl.BlockSpec(a.shape, lambda i: (0,) * a.ndim)
    g0 = CB_GATE // 4
    tril = jnp.asarray(np.tril(np.ones((tm, tm), np.float32), -1), BF16)
    wr_hi = lax.reduce_precision(wr, exponent_bits=8, mantissa_bits=7)
    wr2 = jnp.concatenate([wr_hi, wr - wr_hi], axis=1).astype(BF16)
    return pl.pallas_call(
        functools.partial(_merge_body, alpha=alpha, split=split, nt=t_len // tm, n_lat_tiles=n_lat_tiles),
        grid=(na // tm,),
        in_specs=tok_specs + pool_specs + [row(512), row(512)] + [gate(g0 + j) for j in range(6)]
                 + [full(wa), full(wb), full(wc), full(wo), full(lng), full(lnb),
                    pl.BlockSpec((1, 6, d), lambda i: (modrow(i), 0, 0)), full(wr2), full(br), full(tril)],
        out_specs=[row(d), pl.BlockSpec((tm * ROW_SLABS, 128), lambda i: (i, 0)), row(128),
                   pl.BlockSpec((8, 128), lambda i: (0, 0))],
        out_shape=[jax.ShapeDtypeStruct((na, d), F32), jax.ShapeDtypeStruct((na * ROW_SLABS, 128), jnp.uint32),
                   jax.ShapeDtypeStruct((na, 128), F32), jax.ShapeDtypeStruct((8, 128), F32)],
        scratch_shapes=[pltpu.VMEM((1, 128), F32)],
        compiler_params=_cparams(("arbitrary",)),
        name="merge",
    )(tok[0], tok[1], *pool_args, yb, yc, z, z, z, z, z, z, wa, wb, wc, wo, lng, lnb, mod, wr2, br, tril)


ROW_DMA_UNROLL = 8


def _slab(ref, row):
    start = row * ROW_SLABS
    if not isinstance(row, int):
        start = pl.multiple_of(start, ROW_SLABS)
    return ref.at[pl.ds(start, ROW_SLABS), :]


def _slots_body(route_ref, seg_ref, o_ref):
    route = route_ref[...]
    lane = lax.broadcasted_iota(jnp.int32, route.shape, 1).astype(F32)
    seg = seg_ref[...]
    slots = []
    for k in range(2):
        eid = route[:, 2 + k:3 + k]
        start = jnp.sum(jnp.where(lane == eid, seg, 0.0), axis=-1, keepdims=True)
        slots.append(start + route[:, 4 + k:5 + k])
    both = jnp.where(lane == 0.0, slots[0], jnp.where(lane == 1.0, slots[1], 0.0))
    by_token = both.T[:8].astype(jnp.int32)
    tm = o_ref.shape[2]
    for c in range(o_ref.shape[0]):
        o_ref[c] = by_token[:, c * tm:(c + 1) * tm]


def _slots_call(route, seg, tm):
    na = route.shape[0]
    group = _row_tile(4, na // tm)
    return pl.pallas_call(
        _slots_body,
        grid=(na // (tm * group),),
        in_specs=[pl.BlockSpec((tm * group, 128), lambda i: (i, 0)), pl.BlockSpec((1, 128), lambda i: (0, 0))],
        out_specs=pl.BlockSpec((group, 8, tm), lambda i: (i, 0, 0)),
        out_shape=jax.ShapeDtypeStruct((na // tm, 8, tm), jnp.int32),
        compiler_params=_cparams(("parallel",)),
        name="slots",
    )(route, seg)


def _dispatch_body(slot_ref, h_ref, xs_in, xs_out, sem):
    del xs_in
    tm = h_ref.shape[0] // ROW_SLABS

    for t in range(tm):
        for k in range(2):
            pltpu.make_async_copy(_slab(h_ref, t), _slab(xs_out, slot_ref[0, k, t]), sem).start(priority=k)

    for k in range(2):
        pltpu.make_async_copy(h_ref, xs_out.at[pl.ds(0, tm * ROW_SLABS), :], sem).wait()


def _dispatch_call(h2p, slots, n_rows, tm, xs_prev=None):
    xs0 = jnp.zeros((n_rows * ROW_SLABS, 128), h2p.dtype) if xs_prev is None else xs_prev
    assert xs0.shape[0] >= n_rows * ROW_SLABS
    return pl.pallas_call(
        _dispatch_body,
        grid=(slots.shape[0],),
        in_specs=[pl.BlockSpec((1, 8, tm), lambda i: (i, 0, 0), memory_space=pltpu.SMEM),
                  pl.BlockSpec((tm * ROW_SLABS, 128), lambda i: (i, 0)),
                  pl.BlockSpec(memory_space=pl.ANY)],
        out_specs=pl.BlockSpec(memory_space=pl.ANY),
        out_shape=jax.ShapeDtypeStruct(xs0.shape, xs0.dtype),
        input_output_aliases={2: 0},
        scratch_shapes=[pltpu.SemaphoreType.DMA(())],
        compiler_params=_cparams(("arbitrary",)),
        name="dispatch",
    )(slots, h2p, xs0)


def _experts_body(te_ref, nu_ref, xs_ref, wg_ref, wu_ref, wd_ref, y_ref, wg_scr, wu_scr, wd_scr):
    j = pl.program_id(0)

    @pl.when(j >= nu_ref[0])
    def _():
        y_ref[...] = jnp.zeros_like(y_ref)

    @pl.when(jnp.logical_and(j < nu_ref[0], jnp.logical_or(j == 0, te_ref[j] != te_ref[jnp.maximum(j - 1, 0)])))
    def _():
        wg_scr[...] = wg_ref[0].astype(BF16)
        wu_scr[...] = wu_ref[0].astype(BF16)
        wd_scr[...] = wd_ref[0].astype(BF16)

    @pl.when(j < nu_ref[0])
    def _():
        x = jnp.concatenate([c.astype(BF16) for c in _load_packed_rows(xs_ref)], axis=1)
        gt = _dot(x, wg_scr[...])
        a = gt * _sigmoid_t(gt) * _dot(x, wu_scr[...])
        _store_packed_rows(y_ref, _dot(a.astype(BF16), wd_scr[...]))


def _experts_call(tile_expert, n_used, xs, wg, wu, wd, layer, tr):
    rows, w = xs.shape
    _, ne, d, de = wg.shape
    tr = tr * ROW_SLABS
    used = lambda j, te, nu: jnp.minimum(j, nu[0] - 1)
    return pl.pallas_call(
        _experts_body,
        grid_spec=pltpu.PrefetchScalarGridSpec(
            num_scalar_prefetch=2,
            grid=(rows // tr,),
            in_specs=[pl.BlockSpec((tr, w), lambda j, te, nu: (used(j, te, nu), 0)),
                      pl.BlockSpec((None, 1, d, de), lambda j, te, nu: (layer, te[used(j, te, nu)], 0, 0)),
                      pl.BlockSpec((None, 1, d, de), lambda j, te, nu: (layer, te[used(j, te, nu)], 0, 0)),
                      pl.BlockSpec((None, 1, de, d), lambda j, te, nu: (layer, te[used(j, te, nu)], 0, 0))],
            out_specs=pl.BlockSpec((tr, w), lambda j, te, nu: (j, 0)),
            scratch_shapes=[pltpu.VMEM((d, de), BF16), pltpu.VMEM((d, de), BF16), pltpu.VMEM((de, d), BF16)]),
        out_shape=jax.ShapeDtypeStruct((rows, w), jnp.uint32),
        compiler_params=_cparams(("arbitrary",)),
        name="experts",
    )(tile_expert, n_used, xs, wg, wu, wd)


def _combine_body(slot_ref, next_ref, y_hbm, x1_ref, route_ref, lng_ref, lnb_ref, mod_ref, o_ref,
                  buf_scr, sems, *, alpha):
    tm = x1_ref.shape[0]
    i = pl.program_id(0)
    cur = i % 2

    def gather(idx_ref, b):
        def issue(j, carry):
            for u in range(ROW_DMA_UNROLL):
                t = j * ROW_DMA_UNROLL + u
                for k in range(2):
                    pltpu.make_async_copy(_slab(y_hbm, idx_ref[0, k, t]), _slab(buf_scr.at[b, k], t),
                                          sems.at[b]).start(priority=k)
            return carry

        lax.fori_loop(0, tm // ROW_DMA_UNROLL, issue, 0)

    def wait_slot(b):
        for k in range(2):
            pltpu.make_async_copy(y_hbm.at[pl.ds(0, tm * ROW_SLABS), :], buf_scr.at[b, k], sems.at[b]).wait()

    @pl.when(i == 0)
    def _():
        gather(slot_ref, cur)

    wait_slot(cur)
    for t in range(tm):
        for k in range(2):
            pltpu.make_async_copy(_slab(y_hbm, next_ref[0, k, t]), _slab(buf_scr.at[1 - cur, k], t),
                                  sems.at[1 - cur]).start(priority=k)

    w1 = route_ref[:, 0:1]
    w2 = route_ref[:, 1:2]
    moe = jnp.concatenate([w1 * a1 + w2 * a2 for a1, a2 in zip(_load_packed_rows(buf_scr.at[cur, 0]),
                                                               _load_packed_rows(buf_scr.at[cur, 1]))], axis=1)
    r = alpha * x1_ref[...] + mod_ref[0, 5:6, :] * moe
    o_ref[...] = _ln(r) * lng_ref[...] + lnb_ref[...]

    @pl.when(i == pl.num_programs(0) - 1)
    def _():
        wait_slot(1 - cur)


def _combine_call(slots, y, x1, route, lng, lnb, mod, tm, modrow, alpha):
    na, d = x1.shape
    full = lambda a: pl.BlockSpec(a.shape, lambda i: (0,) * a.ndim)
    last = na // tm - 1
    return pl.pallas_call(
        functools.partial(_combine_body, alpha=alpha),
        grid=(na // tm,),
        in_specs=[pl.BlockSpec((1, 8, tm), lambda i: (i, 0, 0), memory_space=pltpu.SMEM),
                  pl.BlockSpec((1, 8, tm), lambda i: (jnp.minimum(i + 1, last), 0, 0), memory_space=pltpu.SMEM),
                  pl.BlockSpec(memory_space=pl.ANY),
                  pl.BlockSpec((tm, d), lambda i: (i, 0)),
                  pl.BlockSpec((tm, 128), lambda i: (i, 0)),
                  full(lng), full(lnb),
                  pl.BlockSpec((1, 6, d), lambda i: (modrow(i), 0, 0))],
        out_specs=pl.BlockSpec((tm, d), lambda i: (i, 0)),
        out_shape=jax.ShapeDtypeStruct((na, d), F32),
        scratch_shapes=[pltpu.VMEM((2, 2, tm * ROW_SLABS, 128), jnp.uint32), pltpu.SemaphoreType.DMA((2,))],
        compiler_params=_cparams(("arbitrary",)),
        name="combine",
    )(slots, slots, y, x1, route, lng, lnb, mod)


def _expert_buffer_tiles(n_tokens, tr):
    return -(-2 * n_tokens // tr) + N_EXPERTS


def _routing_tables(counts, n_tiles, tr):
    cnt = counts[0, :N_EXPERTS].astype(jnp.int32)
    ntile = (cnt + tr - 1) // tr
    tile_start = jnp.cumsum(ntile) - ntile
    n_used = jnp.sum(ntile)
    seg = jnp.zeros((1, 128), F32).at[0, :N_EXPERTS].set((tile_start * tr).astype(F32))
    tile_expert = jnp.sum(jnp.arange(n_tiles, dtype=jnp.int32)[:, None] >= tile_start[None, :], axis=1) - 1
    return seg, tile_expert.astype(jnp.int32), n_used.reshape(1).astype(jnp.int32)


def _lower_bounds(logits):
    p = jax.nn.softmax(logits.astype(F32), axis=0)
    return jnp.cumsum(p, axis=0) - p[:1]


def _row_tile(limit, *sizes):
    tm = limit
    while any(s % tm for s in sizes):
        tm //= 2
    return tm


def kernel(x, c, ctx, c_ctx, w_ada, b_ada, w_in, w_pool, pool_scale, lb_logits_fwd, lb_logits_bwd, hg_gain, rpb, w_br_a, w_br_b, w_br_c, w_out, ln1_g, ln1_b, w_rg, b_rg, w_re, b_re, w_gate, w_up, w_down, ln2_g, ln2_b):
    nbatch, t_len, d = x.shape
    c_len = ctx.shape[1]
    depth = w_ada.shape[0]
    assert c_len == SEQ_TILE and t_len % SEQ_TILE == 0 and t_len % GRID_W == 0
    alpha = (2.0 * depth) ** 0.25
    n_lat = nbatch * t_len
    nt = t_len // SEQ_TILE

    na = n_lat + nbatch * c_len
    tok = (x.reshape(n_lat, d), ctx.reshape(nbatch * c_len, d), 0)

    mod_rows = -(-(nbatch + 1) // 8) * 8
    cc = jnp.zeros((mod_rows, d), F32).at[:nbatch].set(c).at[nbatch].set(c_ctx)
    ada = _ada_call(cc, w_ada, b_ada)

    lb_f = _lower_bounds(lb_logits_fwd).reshape(depth, HG_HEADS, 1, HG_DK)
    lb_b = _lower_bounds(lb_logits_bwd).reshape(depth, HG_HEADS, 1, HG_DK)
    tm_big = _row_tile(1024, t_len, nbatch * c_len)
    tm_mid = _row_tile(512, t_len, nbatch * c_len)
    assert tm_mid % c_len == 0
    pool_consts = _pool_consts(tm_mid, c_len)
    hg_f = _hgrn_consts(False)
    hg_b = _hgrn_consts(True)

    def modrow_for(tm):
        return lambda i: jnp.where(i * tm < n_lat, (i * tm) // t_len, nbatch)

    n_tiles = _expert_buffer_tiles(na, EXPERT_TILE)
    xs = None
    for l in range(depth):
        last = l == depth - 1
        mod = ada[l].reshape(mod_rows, 6, d)
        z = _inproj_call(tok, na, mod, w_in[l].astype(BF16), tm_big, modrow_for(tm_big))
        pool = (pool_consts, w_pool[l].astype(BF16), pool_scale[l].reshape(1, -1), n_lat)
        o_f =_hgrn_call(z, lb_f[l], hg_f, nbatch, nt, reverse=False)
        yb = _hgrn_call(z, lb_b[l], hg_b, nbatch, nt, reverse=True, o_fwd=o_f,
                        gain=hg_gain[l].reshape(HG_HEADS, 1, HG_DK))
        bias = _na_bias_table(rpb[l], t_len // GRID_W)
        yc = _na_call(z, bias, nbatch, t_len, c_len, with_ctx=not last)
        wr = jnp.zeros((d, 128), F32).at[:, :N_EXPERTS].set(w_re[l]).at[:, N_EXPERTS:N_EXPERTS + N_GROUPS].set(w_rg[l])
        br = jnp.zeros((1, 128), F32).at[0, :N_EXPERTS].set(b_re[l]).at[0, N_EXPERTS:N_EXPERTS + N_GROUPS].set(b_rg[l])
        x1, h2p, route, counts = _merge_call(
            tok, pool, yb, yc, z, mod, w_br_a[l].astype(BF16), w_br_b[l].astype(BF16), w_br_c[l].astype(BF16),
            w_out[l].astype(BF16), ln1_g[l].reshape(1, d), ln1_b[l].reshape(1, d), wr, br, tm_mid,
            modrow_for(tm_mid), alpha, t_len)
        seg, tile_expert, n_used = _routing_tables(counts, n_tiles, EXPERT_TILE)
        slots = _slots_call(route, seg, tm_mid)
        xs = _dispatch_call(h2p, slots, n_tiles * EXPERT_TILE, tm_mid, xs)
        ys = _experts_call(tile_expert, n_used, xs, w_gate, w_up, w_down, l, EXPERT_TILE)
        xa = _combine_call(slots, ys, x1, route, ln2_g[l].reshape(1, d), ln2_b[l].reshape(1, d), mod,
                           tm_mid, modrow_for(tm_mid), alpha)
        tok = (xa, xa, n_lat)
    return xa.reshape(nbatch, t_len, d)
```

```python
import functools

import numpy as np
import jax
import jax.numpy as jnp
from jax import lax
from jax.experimental import pallas as pl
from jax.experimental.pallas import tpu as pltpu

F32 = jnp.float32
BF16 = jnp.bfloat16
HIGHEST = lax.Precision.HIGHEST

GRID_W = 64
POOL_WINDOWS = (2, 4, 8, 16)
POOL_GDIM = 128
HG_HEADS = 4
HG_DK = 128
HG_BLOCK = 16
NA_HEADS = 8
NA_HD = 64
NA_ROWS = 8
NA_COLS = 16
NA_QROWS = 4
NA_KROWS = 12
N_GROUPS = 4
EXP_PER_GROUP = 8
N_EXPERTS = N_GROUPS * EXP_PER_GROUP
LN_EPS = 1e-5
RMS_EPS = 1e-6
NEG_BIG = -1e30
SEQ_TILE = 256
EXPERT_TILE = 512
VMEM_LIMIT = 56 * 1024 * 1024

CB_A, CB_Q, CB_FF, CB_FB, CB_I, CB_G, CB_NQ, CB_NK, CB_NV, CB_GATE = 0, 4, 8, 12, 16, 20, 24, 28, 32, 36


def _cparams(sem):
    return pltpu.CompilerParams(dimension_semantics=sem, vmem_limit_bytes=VMEM_LIMIT)


def _ln(x):
    mu = jnp.mean(x, axis=-1, keepdims=True)
    xc = x - mu
    var = jnp.mean(xc * xc, axis=-1, keepdims=True)
    return xc * lax.rsqrt(var + LN_EPS)


def _sigmoid(x):
    return 1.0 / (1.0 + jnp.exp(-x))


def _sigmoid_t(x):
    return 0.5 * jnp.tanh(0.5 * x) + 0.5


def _dot(a, b):
    return jnp.dot(a, b, preferred_element_type=F32)


def _dot_nt(a, b):
    return lax.dot_general(a, b, (((1,), (1,)), ((), ())), preferred_element_type=F32)


def _dot_tn(a, b):
    return lax.dot_general(a, b, (((0,), (0,)), ((), ())), preferred_element_type=F32)


def _dot01(m01, x):
    x1 = x.astype(BF16)
    r1 = x - x1.astype(F32)
    x2 = r1.astype(BF16)
    x3 = (r1 - x2.astype(F32)).astype(BF16)
    return _dot(m01, x1) + _dot(m01, x2) + _dot(m01, x3)


def _ada_body(c_ref, w_ref, b_ref, o_ref):
    cs = c_ref[...]
    s = cs * _sigmoid(cs)
    o_ref[0] = jnp.dot(s, w_ref[0], preferred_element_type=F32, precision=HIGHEST) + b_ref[0]


def _ada_call(cc, w_ada, b_ada):
    depth, d, n6 = w_ada.shape
    rows = cc.shape[0]
    return pl.pallas_call(
        _ada_body,
        grid=(depth, n6 // d),
        in_specs=[pl.BlockSpec((rows, d), lambda l, j: (0, 0)),
                  pl.BlockSpec((1, d, d), lambda l, j: (l, 0, j)),
                  pl.BlockSpec((1, 1, d), lambda l, j: (l, 0, j))],
        out_specs=pl.BlockSpec((1, rows, d), lambda l, j: (l, 0, j)),
        out_shape=jax.ShapeDtypeStruct((depth, rows, n6), F32),
        compiler_params=_cparams(("parallel", "parallel")),
        name="ada",
    )(cc, w_ada, b_ada.reshape(depth, 1, n6))


def _token_specs(tok, tm):
    lat, ctx, ctx_row0 = tok
    split = (ctx_row0 if ctx is lat else lat.shape[0]) // tm
    off = ctx_row0 // tm
    d = lat.shape[1]
    lat_map = lambda i, *_: (jnp.minimum(i, split - 1), 0)
    ctx_map = lambda i, *_: (jnp.maximum(i - split, 0) + off, 0)
    return split, [pl.BlockSpec((tm, d), lat_map), pl.BlockSpec((tm, d), ctx_map)]


def _inproj_body(xl_ref, xc_ref, mod_ref, w_ref, z_ref, h_scr, *, split):
    def modulated(x_ref):
        h = _ln(x_ref[...]) * (1.0 + mod_ref[0, 1:2, :]) + mod_ref[0, 0:1, :]
        h_scr[...] = h.astype(BF16)

    first = pl.program_id(1) == 0
    pl.when(jnp.logical_and(first, pl.program_id(0) < split))(lambda: modulated(xl_ref))
    pl.when(jnp.logical_and(first, pl.program_id(0) >= split))(lambda: modulated(xc_ref))
    z_ref[...] = _dot(h_scr[...], w_ref[...]).astype(z_ref.dtype)


def _inproj_call(tok, na, mod, w_in, tm, modrow):
    d, d_in = w_in.shape
    tn = 2560
    assert d_in % tn == 0
    split, tok_specs = _token_specs(tok, tm)
    return pl.pallas_call(
        functools.partial(_inproj_body, split=split),
        grid=(na // tm, d_in // tn),
        in_specs=tok_specs + [pl.BlockSpec((1, 6, d), lambda i, j: (modrow(i), 0, 0)),
                              pl.BlockSpec((d, tn), lambda i, j: (0, j))],
        out_specs=pl.BlockSpec((tm, tn), lambda i, j: (i, j)),
        out_shape=jax.ShapeDtypeStruct((na, d_in), BF16),
        scratch_shapes=[pltpu.VMEM((tm, d), BF16)],
        compiler_params=_cparams(("parallel", "arbitrary")),
        name="inproj",
    )(tok[0], tok[1], mod, w_in)


def _pool_consts(n, c_len):
    t = np.arange(n)[:, None]
    bc = np.zeros((2, 4, n, n), np.float32)
    bp = np.zeros((4, n, 16), np.float32)
    bn = np.zeros((4, n, 16), np.float32)
    for g, win in enumerate(POOL_WINDOWS):
        lo, hi = t - win // 2, t + win // 2 - 1
        s = np.arange(n)[None, :]
        bc[0, g] = (s >= lo) & (s <= hi)
        bc[1, g] = bc[0, g] * (s // c_len == t // c_len)
        s = np.arange(16)[None, :] - 16
        bp[g] = (s >= lo) & (s <= hi)
        s = np.arange(16)[None, :] + n
        bn[g] = (s >= lo) & (s <= hi)
    cnt = np.stack([np.stack([bc[0].sum(-1), bp.sum(-1), bn.sum(-1)], axis=1),
                    np.stack([bc[1].sum(-1), 0 * bp.sum(-1), 0 * bn.sum(-1)], axis=1)])
    cnt = np.broadcast_to(cnt[..., None], (2, 4, 3, n, 128)).astype(np.float32)
    return (jnp.asarray(bc, BF16), jnp.asarray(bp, BF16), jnp.asarray(bn, BF16), jnp.asarray(cnt))


def _pooled_tile(prev_ref, cur_ref, next_ref, bc_ref, bp_ref, bn_ref, cnt_ref, wp_ref, ps_ref, *, nt, n_lat_tiles):
    i = pl.program_id(0)
    k = i % nt
    lat = i < n_lat_tiles
    has_prev = jnp.where(jnp.logical_and(lat, k != 0), 1.0, 0.0).astype(F32)
    has_next = jnp.where(jnp.logical_and(lat, k != nt - 1), 1.0, 0.0).astype(F32)
    groups = []
    for g in range(len(POOL_WINDOWS)):
        sl = slice(g * POOL_GDIM, (g + 1) * POOL_GDIM)
        u = cur_ref[:, sl]
        ssum = (_dot(bc_ref[0, g], u) + has_prev * _dot(bp_ref[g], prev_ref[:, sl])
                + has_next * _dot(bn_ref[g], next_ref[:, sl]))
        cnt = cnt_ref[0, g, 0] + has_prev * cnt_ref[0, g, 1] + has_next * cnt_ref[0, g, 2]
        dlt = ssum / cnt - u.astype(F32)
        groups.append((_dot(dlt.astype(BF16), wp_ref[g]) * ps_ref[:, sl]).astype(BF16))
    return jnp.concatenate(groups, axis=1)


def _pool_specs(z, consts, w_pool, pool_scale, n_lat):
    bc, bp, bn, cnt = consts
    n = bc.shape[2]
    hb = n // 16
    last16 = z.shape[0] // 16 - 1
    n_lat_tiles = n_lat // n
    full = lambda a: pl.BlockSpec(a.shape, lambda i: (0,) * a.ndim)
    variant = lambda a: pl.BlockSpec((1,) + a.shape[1:],
                                     lambda i: (jnp.where(i < n_lat_tiles, 0, 1),) + (0,) * (a.ndim - 1))
    specs = [pl.BlockSpec((16, 512), lambda i: (jnp.maximum(i * hb - 1, 0), CB_A // 4)),
             pl.BlockSpec((n, 512), lambda i: (i, CB_A // 4)),
             pl.BlockSpec((16, 512), lambda i: (jnp.minimum((i + 1) * hb, last16), CB_A // 4)),
             variant(bc), full(bp), full(bn), variant(cnt), full(w_pool), full(pool_scale)]
    return specs, [z, z, z, bc, bp, bn, cnt, w_pool, pool_scale], n_lat_tiles


def _hgrn_consts(reverse):
    n, bs = SEQ_TILE, HG_BLOCK
    nb = n // bs
    t = np.arange(n)
    o = (n - 1 - t) if reverse else t
    blk = t // bs
    jb = np.arange(nb)
    ob = (nb - 1 - jb) if reverse else jb
    cum = ((blk[:, None] == blk[None, :]) & (o[None, :] <= o[:, None])).astype(np.float32)
    bsum = (jb[:, None] == blk[None, :]).astype(np.float32)
    widths = [2 ** l for l in range(1, int(np.log2(nb)) + 1)]
    lvl = np.full((n, n), -1, np.int32)
    obt = ob[blk]
    same = blk[:, None] == blk[None, :]
    lvl[same & (o[None, :] <= o[:, None])] = 0
    for li, w in reversed(list(enumerate(widths, start=1))):
        m = (obt[:, None] // w == obt[None, :] // w) & (obt[None, :] < obt[:, None]) & ~same
        lvl[m] = li
    mats = []
    for w in widths:
        mid = (ob // w) * w + w // 2
        mats.append((mid[:, None] <= ob[None, :]) & (ob[None, :] < ob[:, None]))
    for w in widths:
        mid = (ob // w) * w + w // 2
        mats.append((ob[:, None] < ob[None, :]) & (ob[None, :] < mid[:, None]))
    mats.append(ob[None, :] < ob[:, None])
    mats.append(ob[None, :] > ob[:, None])
    mats.append(np.ones((nb, nb), bool))
    tsm = np.concatenate(mats, axis=0).astype(np.float32)
    return (jnp.asarray(cum, BF16), jnp.asarray(bsum, BF16), jnp.asarray(lvl, BF16), jnp.asarray(tsm, BF16),
            len(widths))


def _expand_blocks(c, n):
    nb, lanes = c.shape
    return jnp.concatenate([jnp.broadcast_to(c[j:j + 1, :], (n // nb, lanes)) for j in range(nb)], axis=0)


def _hgrn_body(*refs, n_levels, final):
    if final:
        (zq_ref, zf_ref, zi_ref, lb_ref, cum_ref, bsum_ref, lvl_ref, tsm_ref,
         of_ref, zg_ref, gain_ref, o_ref, st_scr) = refs
    else:
        zq_ref, zf_ref, zi_ref, lb_ref, cum_ref, bsum_ref, lvl_ref, tsm_ref, o_ref, st_scr = refs
    n = zq_ref.shape[0]
    nb = n // HG_BLOCK

    @pl.when(pl.program_id(1) == 0)
    def _():
        st_scr[...] = jnp.zeros_like(st_scr)

    zq = zq_ref[...].astype(F32)
    zf = zf_ref[...].astype(F32)
    lb = jnp.concatenate([lb_ref[h] for h in range(HG_HEADS)], axis=1)
    sig = _sigmoid(zf)
    lf = jnp.log(lb + (1.0 - lb) * sig)
    k_all = (1.0 - lb) * (1.0 - sig)
    q_all = zq * _sigmoid_t(zq)
    lf_hi = lf.astype(BF16)
    lf2 = jnp.concatenate([lf_hi, (lf - lf_hi.astype(F32)).astype(BF16)], axis=1)
    w = HG_HEADS * HG_DK
    b2 = _dot(cum_ref[...], lf2)
    b_all = b2[:, :w] + b2[:, w:]
    t2 = _dot(bsum_ref[...], lf2)
    tot_all = t2[:, :w] + t2[:, w:]
    coef_all = _dot01(tsm_ref[...], tot_all)
    qd_all = q_all * jnp.exp(b_all)
    kd_all = (k_all * jnp.exp(-b_all)).astype(BF16)
    ks_all = k_all * jnp.exp(_expand_blocks(tot_all, n) - b_all)

    ecoef = jnp.exp(coef_all)
    scale = lambda idx: _expand_blocks(ecoef[idx * nb:(idx + 1) * nb], n)
    q_lv = [qd_all.astype(BF16)] + [(qd_all * scale(li)).astype(BF16) for li in range(n_levels)]
    k_lv = [kd_all] + [(ks_all * scale(n_levels + li)).astype(BF16) for li in range(n_levels)]
    qs_all = (qd_all * scale(2 * n_levels)).astype(BF16)
    kn_all = (ks_all * scale(2 * n_levels + 1)).astype(BF16)
    dec_all = ecoef[(2 * n_levels + 2) * nb:(2 * n_levels + 2) * nb + 1]

    lvl = lvl_ref[...]
    heads = [slice(h * HG_DK, (h + 1) * HG_DK) for h in range(HG_HEADS)]
    a_h = [jnp.zeros((n, n), BF16) for _ in heads]
    for li in range(n_levels + 1):
        for h, sl in enumerate(heads):
            a_h[h] = jnp.where(lvl == float(li), _dot_nt(q_lv[li][:, sl], k_lv[li][:, sl]).astype(BF16), a_h[h])

    for h, sl in enumerate(heads):
        v = zi_ref[:, sl]
        o = _dot(a_h[h], v)

        st = st_scr[h]
        o = o + _dot_nt(qs_all[:, sl], st.astype(BF16))
        st_scr[h] = st * dec_all[:, sl] + _dot_tn(v, kn_all[:, sl])

        if final:
            o = o + of_ref[:, sl]
            o = o * lax.rsqrt(jnp.mean(o * o, axis=-1, keepdims=True) + RMS_EPS) * gain_ref[h]
            zg = zg_ref[:, sl].astype(F32)
            o_ref[:, sl] = (o * (zg * _sigmoid_t(zg))).astype(o_ref.dtype)
        else:
            o_ref[:, sl] = o


def _hgrn_call(z, lb, consts, nbatch, nt, reverse, o_fwd=None, gain=None):
    na = z.shape[0]
    n = SEQ_TILE
    cum, bsum, lvl, tsm, n_levels = consts
    ctx_base = nbatch * nt
    final = o_fwd is not None

    def tile(b, s):
        lat = (b * nt + nt - s) if reverse else (b * nt + s - 1)
        return jnp.where(s == 0, ctx_base + b, lat)

    width = HG_HEADS * HG_DK

    def col(cb):
        return pl.BlockSpec((n, width), lambda b, s: (tile(b, s), cb // HG_HEADS))

    full = lambda a: pl.BlockSpec(a.shape, lambda b, s: (0,) * a.ndim)
    in_specs = [col(CB_Q), col(CB_FB if reverse else CB_FF), col(CB_I), full(lb),
                full(cum), full(bsum), full(lvl), full(tsm)]
    args = [z, z, z, lb, cum, bsum, lvl, tsm]
    if final:
        in_specs += [col(0), col(CB_G), full(gain)]
        args += [o_fwd, z, gain]
    return pl.pallas_call(
        functools.partial(_hgrn_body, n_levels=n_levels, final=final),
        grid=(nbatch, nt + 1),
        in_specs=in_specs,
        out_specs=col(0),
        out_shape=jax.ShapeDtypeStruct((na, width), BF16 if final else F32),
        scratch_shapes=[pltpu.VMEM((HG_HEADS, HG_DK, HG_DK), F32)],
        compiler_params=_cparams(("parallel", "arbitrary")),
        name="hgrn_bwd" if final else "hgrn_fwd",
    )(*args)


def _na_bias_table(rpb, rows):
    nrb = rows // NA_QROWS
    assert nrb >= 3 and rows >= NA_KROWS
    kr = min(NA_ROWS, rows)
    qc = np.arange(GRID_W)
    c0 = np.clip(qc - NA_COLS // 2, 0, GRID_W - NA_COLS)
    kc = np.arange(GRID_W)
    col_ok = (kc[None, :] >= c0[:, None]) & (kc[None, :] < c0[:, None] + NA_COLS)
    n_dr, n_dc = 2 * NA_ROWS - 1, 2 * NA_COLS - 1
    dc = np.where(col_ok, kc[None, :] - qc[:, None] + NA_COLS - 1, n_dc)
    sel_c = (dc[..., None] == np.arange(n_dc + 1)).astype(np.float32)
    sel_r = []
    for rb in (0, 1, nrb - 1):
        start = int(np.clip(NA_QROWS * rb - 4, 0, rows - NA_KROWS))
        r = NA_QROWS * rb + np.arange(NA_QROWS)
        r0 = np.clip(r - kr // 2, 0, rows - kr)
        keyrow = start + np.arange(NA_KROWS)
        row_ok = (keyrow[None, :] >= r0[:, None]) & (keyrow[None, :] < r0[:, None] + kr)
        dr = np.where(row_ok, keyrow[None, :] - r[:, None] + NA_ROWS - 1, n_dr)
        sel_r.append((dr[..., None] == np.arange(n_dr + 1)).astype(np.float32))
    masked = np.zeros_like(sel_r[0])
    masked[..., n_dr] = 1.0
    sel_r.append(masked)
    rpb_ext = jnp.pad(rpb.astype(F32), ((0, 0), (0, 1), (0, 1)), constant_values=NEG_BIG)
    bias = jnp.einsum("hij,paki,cdj->phackd", rpb_ext, jnp.asarray(np.stack(sel_r)), jnp.asarray(sel_c),
                      precision=HIGHEST)
    return bias.reshape(4, NA_HEADS, NA_QROWS * GRID_W, NA_KROWS * GRID_W)


def _na_body(q_ref, k_ref, v_ref, kc_ref, vc_ref, bias_ref, o_ref, *, rows):
    rb = pl.program_id(1)
    nk = NA_KROWS * GRID_W
    start_row = jnp.clip(NA_QROWS * rb - 4, 0, rows - NA_KROWS)
    start = pl.multiple_of(start_row * GRID_W, GRID_W)
    nq = q_ref.shape[0]
    lane = lax.broadcasted_iota(jnp.int32, (nq, 128), 1)
    scale = NA_HD ** -0.5
    pairs = [slice(128 * p, 128 * (p + 1)) for p in range(NA_HEADS // 2)]
    scores = []
    for p, sl in enumerate(pairs):
        qp = q_ref[:, sl] * scale
        zero = jnp.zeros_like(qp)
        q2 = jnp.concatenate([jnp.where(lane < NA_HD, qp, zero), jnp.where(lane >= NA_HD, qp, zero)], axis=0)
        s_loc = _dot_nt(q2, k_ref[pl.ds(start, nk), sl]) + bias_ref[0, 2 * p:2 * p + 2].reshape(2 * nq, nk)
        scores.append((s_loc, _dot_nt(q2, kc_ref[:, sl])))
    probs = []
    for s_loc, s_ctx in scores:
        m = jnp.maximum(jnp.max(s_loc, axis=-1, keepdims=True), jnp.max(s_ctx, axis=-1, keepdims=True))
        p_loc = jnp.exp(s_loc - m)
        p_ctx = jnp.exp(s_ctx - m)
        den = jnp.sum(p_loc, axis=-1, keepdims=True) + jnp.sum(p_ctx, axis=-1, keepdims=True)
        probs.append((p_loc.astype(BF16), p_ctx.astype(BF16), den))
    for sl, (p_loc, p_ctx, den) in zip(pairs, probs):
        o2 = (_dot(p_loc, v_ref[pl.ds(start, nk), sl]) + _dot(p_ctx, vc_ref[:, sl])) / den
        o_ref[:, sl] = jnp.where(lane < NA_HD, o2[:nq], o2[nq:]).astype(o_ref.dtype)


def _na_call(z, bias, nbatch, t_len, c_len, with_ctx):
    na = z.shape[0]
    rows = t_len // GRID_W
    nrb = rows // NA_QROWS
    nq = NA_QROWS * GRID_W
    assert nq == c_len
    ctx_base = nbatch * nrb
    steps = nrb + 1 if with_ctx else nrb

    def qtile(b, r):
        return jnp.where(r < nrb, b * nrb + r, ctx_base + b)

    def pattern(b, r):
        return jnp.where(r == 0, 0, jnp.where(r == nrb - 1, 2, jnp.where(r == nrb, 3, 1)))

    return pl.pallas_call(
        functools.partial(_na_body, rows=rows),
        grid=(nbatch, steps),
        in_specs=[pl.BlockSpec((nq, 512), lambda b, r: (qtile(b, r), CB_NQ // 4)),
                  pl.BlockSpec((t_len, 512), lambda b, r: (b, CB_NK // 4)),
                  pl.BlockSpec((t_len, 512), lambda b, r: (b, CB_NV // 4)),
                  pl.BlockSpec((c_len, 512), lambda b, r: (ctx_base + b, CB_NK // 4)),
                  pl.BlockSpec((c_len, 512), lambda b, r: (ctx_base + b, CB_NV // 4)),
                  pl.BlockSpec((1,) + bias.shape[1:], lambda b, r: (pattern(b, r), 0, 0, 0))],
        out_specs=pl.BlockSpec((nq, 512), lambda b, r: (qtile(b, r), 0)),
        out_shape=jax.ShapeDtypeStruct((na if with_ctx else nbatch * t_len, 512), BF16),
        compiler_params=_cparams(("parallel", "arbitrary")),
        name="natten",
    )(z, z, z, z, z, bias)


ROW_SLABS = 4


def _store_packed_rows(ref, x):
    m = x.shape[0]

    def bits(v):
        return lax.bitcast_convert_type(v.astype(BF16).astype(F32), jnp.uint32)

    for s in range(ROW_SLABS):
        lo = x[:, 128 * s:128 * (s + 1)]
        hi = x[:, 512 + 128 * s:512 + 128 * (s + 1)]
        ref[pl.ds(s, m, stride=ROW_SLABS), :] = (bits(hi) & jnp.uint32(0xFFFF0000)) | (bits(lo) >> 16)


def _load_packed_rows(ref):
    m = ref.shape[0] // ROW_SLABS
    los, his = [], []
    for s in range(ROW_SLABS):
        p = ref[pl.ds(s, m, stride=ROW_SLABS), :]
        los.append(lax.bitcast_convert_type(p << 16, F32))
        his.append(lax.bitcast_convert_type(p & jnp.uint32(0xFFFF0000), F32))
    return los + his


def _merge_body(xl_ref, xc_ref, zp_ref, za_ref, zn_ref, bc_ref, bp_ref, bn_ref, pcnt_ref, wp_ref, ps_ref,
                yb_ref, yc_ref, g0, g1, g2, g3, g4, g5, wa_ref, wb_ref, wc_ref, wo_ref,
                lng_ref, lnb_ref, mod_ref, wr2_ref, br_ref, tril_ref, x1_ref, h2_ref, route_ref, cnt_ref,
                cnt_scr, *, alpha, split, nt, n_lat_tiles):
    @pl.when(pl.program_id(0) == 0)
    def _():
        cnt_scr[...] = jnp.zeros_like(cnt_scr)

    gates = ((g0, g1), (g2, g3), (g4, g5))
    ya = _pooled_tile(zp_ref, za_ref, zn_ref, bc_ref, bp_ref, bn_ref, pcnt_ref, wp_ref, ps_ref,
                      nt=nt, n_lat_tiles=n_lat_tiles)
    ys = (ya, yb_ref[...], yc_ref[...])
    ws = (wa_ref, wb_ref, wc_ref)
    half = wa_ref.shape[1] // 2
    mix = None
    for n in range(2):
        m = None
        for kbr in range(3):
            pr = _dot(ys[kbr], ws[kbr][:, n * half:(n + 1) * half])
            term = _sigmoid_t(gates[kbr][n][...].astype(F32)) * pr
            m = term if m is None else m + term
        part = _dot(m.astype(BF16), wo_ref[n * half:(n + 1) * half, :])
        mix = part if mix is None else mix + part
    x = jnp.where(pl.program_id(0) < split, xl_ref[...], xc_ref[...])
    r = alpha * x + mod_ref[0, 2:3, :] * mix
    x1 = _ln(r) * lng_ref[...] + lnb_ref[...]
    x1_ref[...] = x1
    h2 = _ln(x1) * (1.0 + mod_ref[0, 4:5, :]) + mod_ref[0, 3:4, :]
    _store_packed_rows(h2_ref, h2)

    h2_hi = h2.astype(BF16)
    h2_lo = (h2 - h2_hi.astype(F32)).astype(BF16)
    hh = _dot(h2_hi, wr2_ref[...])
    logits = (hh[:, :128] + _dot(h2_lo, wr2_ref[:, :128]) + hh[:, 128:]) + br_ref[...]
    lane = lax.broadcasted_iota(jnp.int32, logits.shape, 1).astype(F32)
    is_grp = jnp.where(lane >= N_EXPERTS, jnp.where(lane < N_EXPERTS + N_GROUPS, 1.0, 0.0), 0.0) > 0.5
    lgm = jnp.where(is_grp, logits, NEG_BIG)
    mg = jnp.max(lgm, axis=-1, keepdims=True)
    p_grp = 1.0 / jnp.sum(jnp.exp(lgm - mg), axis=-1, keepdims=True)
    grp = jnp.min(jnp.where(lgm == mg, lane, 1e9), axis=-1, keepdims=True) - N_EXPERTS
    lo = grp * EXP_PER_GROUP
    in_grp = jnp.where(lane >= lo, jnp.where(lane < lo + EXP_PER_GROUP, 1.0, 0.0), 0.0) > 0.5
    lem = jnp.where(in_grp, logits, NEG_BIG)
    m1 = jnp.max(lem, axis=-1, keepdims=True)
    id1 = jnp.min(jnp.where(lem == m1, lane, 1e9), axis=-1, keepdims=True)
    lem2 = jnp.where(lane == id1, NEG_BIG, lem)
    m2 = jnp.max(lem2, axis=-1, keepdims=True)
    id2 = jnp.min(jnp.where(lem2 == m2, lane, 1e9), axis=-1, keepdims=True)
    u2 = jnp.exp(m2 - m1)
    w1 = p_grp / (1.0 + u2)
    w2 = p_grp * u2 / (1.0 + u2)
    oh1 = jnp.where(lane == id1, 1.0, 0.0)
    oh2 = jnp.where(lane == id2, 1.0, 0.0)
    oh = oh1 + oh2
    before = _dot(tril_ref[...], oh.astype(BF16)) + cnt_scr[...]
    rank1 = jnp.sum(before * oh1, axis=-1, keepdims=True)
    rank2 = jnp.sum(before * oh2, axis=-1, keepdims=True)
    cnt_scr[...] += jnp.sum(oh, axis=0, keepdims=True)
    cnt_ref[...] = jnp.broadcast_to(cnt_scr[...], cnt_ref.shape)
    route = jnp.zeros_like(logits)
    for ln, val in enumerate((w1, w2, id1, id2, rank1, rank2)):
        route = jnp.where(lane == ln, val, route)
    route_ref[...] = route


def _merge_call(tok, pool, yb, yc, z, mod, wa, wb, wc, wo, lng, lnb, wr, br, tm, modrow, alpha, t_len):
    na, d = yc.shape[0], tok[0].shape[1]
    split, tok_specs = _token_specs(tok, tm)
    assert pool[0][0].shape[2] == tm
    pool_specs, pool_args, n_lat_tiles = _pool_specs(z, *pool)
    row = lambda w: pl.BlockSpec((tm, w), lambda i: (i, 0))
    gate = lambda cb: pl.BlockSpec((tm, 512), lambda i: (i, cb))
    full = lambda a: pl.BlockSpec(a.shape, lambda i: (0,) * a.ndim)
    g0 = CB_GATE // 4
    tril = jnp.asarray(np.tril(np.ones((tm, tm), np.float32), -1), BF16)
    wr_hi = lax.reduce_precision(wr, exponent_bits=8, mantissa_bits=7)
    wr2 = jnp.concatenate([wr_hi, wr - wr_hi], axis=1).astype(BF16)
    return pl.pallas_call(
        functools.partial(_merge_body, alpha=alpha, split=split, nt=t_len // tm, n_lat_tiles=n_lat_tiles),
        grid=(na // tm,),
        in_specs=tok_specs + pool_specs + [row(512), row(512)] + [gate(g0 + j) for j in range(6)]
                 + [full(wa), full(wb), full(wc), full(wo), full(lng), full(lnb),
                    pl.BlockSpec((1, 6, d), lambda i: (modrow(i), 0, 0)), full(wr2), full(br), full(tril)],
        out_specs=[row(d), pl.BlockSpec((tm * ROW_SLABS, 128), lambda i: (i, 0)), row(128),
                   pl.BlockSpec((8, 128), lambda i: (0, 0))],
        out_shape=[jax.ShapeDtypeStruct((na, d), F32), jax.ShapeDtypeStruct((na * ROW_SLABS, 128), jnp.uint32),
                   jax.ShapeDtypeStruct((na, 128), F32), jax.ShapeDtypeStruct((8, 128), F32)],
        scratch_shapes=[pltpu.VMEM((1, 128), F32)],
        compiler_params=_cparams(("arbitrary",)),
        name="merge",
    )(tok[0], tok[1], *pool_args, yb, yc, z, z, z, z, z, z, wa, wb, wc, wo, lng, lnb, mod, wr2, br, tril)


ROW_DMA_UNROLL = 8


def _slab(ref, row):
    start = row * ROW_SLABS
    if not isinstance(row, int):
        start = pl.multiple_of(start, ROW_SLABS)
    return ref.at[pl.ds(start, ROW_SLABS), :]


def _slots_body(route_ref, seg_ref, o_ref):
    route = route_ref[...]
    lane = lax.broadcasted_iota(jnp.int32, route.shape, 1).astype(F32)
    seg = seg_ref[...]
    slots = []
    for k in range(2):
        eid = route[:, 2 + k:3 + k]
        start = jnp.sum(jnp.where(lane == eid, seg, 0.0), axis=-1, keepdims=True)
        slots.append(start + route[:, 4 + k:5 + k])
    both = jnp.where(lane == 0.0, slots[0], jnp.where(lane == 1.0, slots[1], 0.0))
    by_token = both.T[:8].astype(jnp.int32)
    tm = o_ref.shape[2]
    for c in range(o_ref.shape[0]):
        o_ref[c] = by_token[:, c * tm:(c + 1) * tm]


def _slots_call(route, seg, tm):
    na = route.shape[0]
    group = _row_tile(4, na // tm)
    return pl.pallas_call(
        _slots_body,
        grid=(na // (tm * group),),
        in_specs=[pl.BlockSpec((tm * group, 128), lambda i: (i, 0)), pl.BlockSpec((1, 128), lambda i: (0, 0))],
        out_specs=pl.BlockSpec((group, 8, tm), lambda i: (i, 0, 0)),
        out_shape=jax.ShapeDtypeStruct((na // tm, 8, tm), jnp.int32),
        compiler_params=_cparams(("parallel",)),
        name="slots",
    )(route, seg)


def _dispatch_body(slot_ref, h_ref, xs_in, xs_out, sem):
    del xs_in
    tm = h_ref.shape[0] // ROW_SLABS

    for t in range(tm):
        for k in range(2):
            pltpu.make_async_copy(_slab(h_ref, t), _slab(xs_out, slot_ref[0, k, t]), sem).start(priority=k)

    for k in range(2):
        pltpu.make_async_copy(h_ref, xs_out.at[pl.ds(0, tm * ROW_SLABS), :], sem).wait()


def _dispatch_call(h2p, slots, n_rows, tm, xs_prev=None):
    xs0 = jnp.zeros((n_rows * ROW_SLABS, 128), h2p.dtype) if xs_prev is None else xs_prev
    assert xs0.shape[0] >= n_rows * ROW_SLABS
    return pl.pallas_call(
        _dispatch_body,
        grid=(slots.shape[0],),
        in_specs=[pl.BlockSpec((1, 8, tm), lambda i: (i, 0, 0), memory_space=pltpu.SMEM),
                  pl.BlockSpec((tm * ROW_SLABS, 128), lambda i: (i, 0)),
                  pl.BlockSpec(memory_space=pl.ANY)],
        out_specs=pl.BlockSpec(memory_space=pl.ANY),
        out_shape=jax.ShapeDtypeStruct(xs0.shape, xs0.dtype),
        input_output_aliases={2: 0},
        scratch_shapes=[pltpu.SemaphoreType.DMA(())],
        compiler_params=_cparams(("arbitrary",)),
        name="dispatch",
    )(slots, h2p, xs0)


def _experts_body(te_ref, nu_ref, xs_ref, wg_ref, wu_ref, wd_ref, y_ref, wg_scr, wu_scr, wd_scr):
    j = pl.program_id(0)

    @pl.when(j >= nu_ref[0])
    def _():
        y_ref[...] = jnp.zeros_like(y_ref)

    @pl.when(jnp.logical_and(j < nu_ref[0], jnp.logical_or(j == 0, te_ref[j] != te_ref[jnp.maximum(j - 1, 0)])))
    def _():
        wg_scr[...] = wg_ref[0].astype(BF16)
        wu_scr[...] = wu_ref[0].astype(BF16)
        wd_scr[...] = wd_ref[0].astype(BF16)

    @pl.when(j < nu_ref[0])
    def _():
        x = jnp.concatenate([c.astype(BF16) for c in _load_packed_rows(xs_ref)], axis=1)
        gt = _dot(x, wg_scr[...])
        a = gt * _sigmoid_t(gt) * _dot(x, wu_scr[...])
        _store_packed_rows(y_ref, _dot(a.astype(BF16), wd_scr[...]))


def _experts_call(tile_expert, n_used, xs, wg, wu, wd, layer, tr):
    rows, w = xs.shape
    _, ne, d, de = wg.shape
    tr = tr * ROW_SLABS
    used = lambda j, te, nu: jnp.minimum(j, nu[0] - 1)
    return pl.pallas_call(
        _experts_body,
        grid_spec=pltpu.PrefetchScalarGridSpec(
            num_scalar_prefetch=2,
            grid=(rows // tr,),
            in_specs=[pl.BlockSpec((tr, w), lambda j, te, nu: (used(j, te, nu), 0)),
                      pl.BlockSpec((None, 1, d, de), lambda j, te, nu: (layer, te[used(j, te, nu)], 0, 0)),
                      pl.BlockSpec((None, 1, d, de), lambda j, te, nu: (layer, te[used(j, te, nu)], 0, 0)),
                      pl.BlockSpec((None, 1, de, d), lambda j, te, nu: (layer, te[used(j, te, nu)], 0, 0))],
            out_specs=pl.BlockSpec((tr, w), lambda j, te, nu: (j, 0)),
            scratch_shapes=[pltpu.VMEM((d, de), BF16), pltpu.VMEM((d, de), BF16), pltpu.VMEM((de, d), BF16)]),
        out_shape=jax.ShapeDtypeStruct((rows, w), jnp.uint32),
        compiler_params=_cparams(("arbitrary",)),
        name="experts",
    )(tile_expert, n_used, xs, wg, wu, wd)


def _combine_body(slot_ref, next_ref, y_hbm, x1_ref, route_ref, lng_ref, lnb_ref, mod_ref, o_ref,
                  buf_scr, sems, *, alpha):
    tm = x1_ref.shape[0]
    i = pl.program_id(0)
    cur = i % 2

    def gather(idx_ref, b):
        def issue(j, carry):
            for u in range(ROW_DMA_UNROLL):
                t = j * ROW_DMA_UNROLL + u
                for k in range(2):
                    pltpu.make_async_copy(_slab(y_hbm, idx_ref[0, k, t]), _slab(buf_scr.at[b, k], t),
                                          sems.at[b]).start(priority=k)
            return carry

        lax.fori_loop(0, tm // ROW_DMA_UNROLL, issue, 0)

    def wait_slot(b):
        for k in range(2):
            pltpu.make_async_copy(y_hbm.at[pl.ds(0, tm * ROW_SLABS), :], buf_scr.at[b, k], sems.at[b]).wait()

    @pl.when(i == 0)
    def _():
        gather(slot_ref, cur)

    wait_slot(cur)
    w1 = route_ref[:, 0:1]
    w2 = route_ref[:, 1:2]
    moe = jnp.concatenate([w1 * a1 + w2 * a2 for a1, a2 in zip(_load_packed_rows(buf_scr.at[cur, 0]),
                                                               _load_packed_rows(buf_scr.at[cur, 1]))], axis=1)
    for t in range(tm):
        for k in range(2):
            pltpu.make_async_copy(_slab(y_hbm, next_ref[0, k, t]), _slab(buf_scr.at[1 - cur, k], t),
                                  sems.at[1 - cur]).start(priority=k)

    r = alpha * x1_ref[...] + mod_ref[0, 5:6, :] * moe
    o_ref[...] = _ln(r) * lng_ref[...] + lnb_ref[...]

    @pl.when(i == pl.num_programs(0) - 1)
    def _():
        wait_slot(1 - cur)


def _combine_call(slots, y, x1, route, lng, lnb, mod, tm, modrow, alpha):
    na, d = x1.shape
    full = lambda a: pl.BlockSpec(a.shape, lambda i: (0,) * a.ndim)
    last = na // tm - 1
    return pl.pallas_call(
        functools.partial(_combine_body, alpha=alpha),
        grid=(na // tm,),
        in_specs=[pl.BlockSpec((1, 8, tm), lambda i: (i, 0, 0), memory_space=pltpu.SMEM),
                  pl.BlockSpec((1, 8, tm), lambda i: (jnp.minimum(i + 1, last), 0, 0), memory_space=pltpu.SMEM),
                  pl.BlockSpec(memory_space=pl.ANY),
                  pl.BlockSpec((tm, d), lambda i: (i, 0)),
                  pl.BlockSpec((tm, 128), lambda i: (i, 0)),
                  full(lng), full(lnb),
                  pl.BlockSpec((1, 6, d), lambda i: (modrow(i), 0, 0))],
        out_specs=pl.BlockSpec((tm, d), lambda i: (i, 0)),
        out_shape=jax.ShapeDtypeStruct((na, d), F32),
        scratch_shapes=[pltpu.VMEM((2, 2, tm * ROW_SLABS, 128), jnp.uint32), pltpu.SemaphoreType.DMA((2,))],
        compiler_params=_cparams(("arbitrary",)),
        name="combine",
    )(slots, slots, y, x1, route, lng, lnb, mod)


def _expert_buffer_tiles(n_tokens, tr):
    return -(-2 * n_tokens // tr) + N_EXPERTS


def _routing_tables(counts, n_tiles, tr):
    cnt = counts[0, :N_EXPERTS].astype(jnp.int32)
    ntile = (cnt + tr - 1) // tr
    tile_start = jnp.cumsum(ntile) - ntile
    n_used = jnp.sum(ntile)
    seg = jnp.zeros((1, 128), F32).at[0, :N_EXPERTS].set((tile_start * tr).astype(F32))
    tile_expert = jnp.sum(jnp.arange(n_tiles, dtype=jnp.int32)[:, None] >= tile_start[None, :], axis=1) - 1
    return seg, tile_expert.astype(jnp.int32), n_used.reshape(1).astype(jnp.int32)


def _lower_bounds(logits):
    p = jax.nn.softmax(logits.astype(F32), axis=0)
    return jnp.cumsum(p, axis=0) - p[:1]


def _row_tile(limit, *sizes):
    tm = limit
    while any(s % tm for s in sizes):
        tm //= 2
    return tm


def kernel(x, c, ctx, c_ctx, w_ada, b_ada, w_in, w_pool, pool_scale, lb_logits_fwd, lb_logits_bwd, hg_gain, rpb, w_br_a, w_br_b, w_br_c, w_out, ln1_g, ln1_b, w_rg, b_rg, w_re, b_re, w_gate, w_up, w_down, ln2_g, ln2_b):
    nbatch, t_len, d = x.shape
    c_len = ctx.shape[1]
    depth = w_ada.shape[0]
    assert c_len == SEQ_TILE and t_len % SEQ_TILE == 0 and t_len % GRID_W == 0
    alpha = (2.0 * depth) ** 0.25
    n_lat = nbatch * t_len
    nt = t_len // SEQ_TILE

    na = n_lat + nbatch * c_len
    tok = (x.reshape(n_lat, d), ctx.reshape(nbatch * c_len, d), 0)

    mod_rows = -(-(nbatch + 1) // 8) * 8
    cc = jnp.zeros((mod_rows, d), F32).at[:nbatch].set(c).at[nbatch].set(c_ctx)
    ada = _ada_call(cc, w_ada, b_ada)

    lb_f = _lower_bounds(lb_logits_fwd).reshape(depth, HG_HEADS, 1, HG_DK)
    lb_b = _lower_bounds(lb_logits_bwd).reshape(depth, HG_HEADS, 1, HG_DK)
    tm_big = _row_tile(1024, t_len, nbatch * c_len)
    tm_mid = _row_tile(512, t_len, nbatch * c_len)
    assert tm_mid % c_len == 0
    pool_consts = _pool_consts(tm_mid, c_len)
    hg_f = _hgrn_consts(False)
    hg_b = _hgrn_consts(True)

    def modrow_for(tm):
        return lambda i: jnp.where(i * tm < n_lat, (i * tm) // t_len, nbatch)

    n_tiles = _expert_buffer_tiles(na, EXPERT_TILE)
    xs = None
    for l in range(depth):
        last = l == depth - 1
        mod = ada[l].reshape(mod_rows, 6, d)
        z = _inproj_call(tok, na, mod, w_in[l].astype(BF16), tm_big, modrow_for(tm_big))
        pool = (pool_consts, w_pool[l].astype(BF16), pool_scale[l].reshape(1, -1), n_lat)
        o_f =_hgrn_call(z, lb_f[l], hg_f, nbatch, nt, reverse=False)
        yb = _hgrn_call(z, lb_b[l], hg_b, nbatch, nt, reverse=True, o_fwd=o_f,
                        gain=hg_gain[l].reshape(HG_HEADS, 1, HG_DK))
        bias = _na_bias_table(rpb[l], t_len // GRID_W)
        yc = _na_call(z, bias, nbatch, t_len, c_len, with_ctx=not last)
        wr = jnp.zeros((d, 128), F32).at[:, :N_EXPERTS].set(w_re[l]).at[:, N_EXPERTS:N_EXPERTS + N_GROUPS].set(w_rg[l])
        br = jnp.zeros((1, 128), F32).at[0, :N_EXPERTS].set(b_re[l]).at[0, N_EXPERTS:N_EXPERTS + N_GROUPS].set(b_rg[l])
        x1, h2p, route, counts = _merge_call(
            tok, pool, yb, yc, z, mod, w_br_a[l].astype(BF16), w_br_b[l].astype(BF16), w_br_c[l].astype(BF16),
            w_out[l].astype(BF16), ln1_g[l].reshape(1, d), ln1_b[l].reshape(1, d), wr, br, tm_mid,
            modrow_for(tm_mid), alpha, t_len)
        seg, tile_expert, n_used = _routing_tables(counts, n_tiles, EXPERT_TILE)
        slots = _slots_call(route, seg, tm_mid)
        xs = _dispatch_call(h2p, slots, n_tiles * EXPERT_TILE, tm_mid, xs)
        ys = _experts_call(tile_expert, n_used, xs, w_gate, w_up, w_down, l, EXPERT_TILE)
        xa = _combine_call(slots, ys, x1, route, ln2_g[l].reshape(1, d), ln2_b[l].reshape(1, d), mod,
                           tm_mid, modrow_for(tm_mid), alpha)
        tok = (xa, xa, n_lat)
    return xa.reshape(nbatch, t_len, d)
```
